```python
import math
import jax, jax.numpy as jnp
from jax import lax
import numpy as np

D_MODEL = 2048
BATCH = 8
SEQ = 8192
DEPTH = 1

N_META = 16
BLOCK = 128
PAD = BLOCK - N_META
HG_HEADS = 8
HG_DK = 128
HG_DV = 128
HG_CHUNK = 64
SB_HEADS = 8
SB_DH = 128
D_FF = 5632
LN_EPS = 1e-5
RMS_EPS = 1e-6
DN_ALPHA = (2.0 * DEPTH) ** 0.25
DN_BETA = (8.0 * DEPTH) ** -0.25

HG_QK_W = HG_HEADS * HG_DK
HG_V_W = HG_HEADS * HG_DV
SB_W = SB_HEADS * SB_DH
SPLIT_IDX = (HG_QK_W, 2 * HG_QK_W, 2 * HG_QK_W + HG_V_W, 2 * HG_QK_W + 2 * HG_V_W,
             2 * HG_QK_W + 2 * HG_V_W + SB_W, 2 * HG_QK_W + 2 * HG_V_W + 2 * SB_W,
             2 * HG_QK_W + 2 * HG_V_W + 3 * SB_W)
IN_COLS = SPLIT_IDX[-1] + 2 * D_MODEL

kernel_name = "hybrid_hgrn2_stickbreaking_macaron_deepnorm"


def layer_norm(x, g, b):
    xf = x.astype(jnp.float32)
    mu = jnp.mean(xf, axis=-1, keepdims=True)
    var = jnp.mean(jnp.square(xf - mu), axis=-1, keepdims=True)
    return ((xf - mu) * lax.rsqrt(var + LN_EPS) * g.astype(jnp.float32) + b.astype(jnp.float32)).astype(x.dtype)


def swiglu(x, w_gate, w_up, w_down):
    return (jax.nn.silu(x @ w_gate) * (x @ w_up)) @ w_down


def hgrn2_chunked(q, k, v, log_f):
    B, L, H, DK = q.shape
    DV = v.shape[-1]
    n = L // HG_CHUNK

    def to_chunks(t):
        return t.astype(jnp.float32).reshape(B, n, HG_CHUNK, H, t.shape[-1]).transpose(1, 0, 3, 2, 4)

    qc, kc, vc, gc = to_chunks(q), to_chunks(k), to_chunks(v), to_chunks(log_f)
    causal = jnp.tril(jnp.ones((HG_CHUNK, HG_CHUNK), dtype=bool))[None, None, :, :, None]

    def step(S, inp):
        qi, ki, vi, gi = inp
        b = jnp.cumsum(gi, axis=2)
        o_inter = jnp.einsum('bhtk,bhkv->bhtv', qi * jnp.exp(b), S)
        rel = jnp.where(causal, b[:, :, :, None, :] - b[:, :, None, :, :], -jnp.inf)
        scores = jnp.einsum('bhtk,bhsk,bhtsk->bhts', qi, ki, jnp.exp(rel))
        o_intra = jnp.einsum('bhts,bhsv->bhtv', scores, vi)
        b_last = b[:, :, -1:, :]
        S_new = jnp.exp(b_last[:, :, 0, :])[..., None] * S + jnp.einsum(
            'bhsk,bhsv->bhkv', ki * jnp.exp(b_last - b), vi)
        return S_new, o_inter + o_intra

    S0 = jnp.zeros((B, H, DK, DV), jnp.float32)
    _, o = lax.scan(step, S0, (qc, kc, vc, gc))
    return o.transpose(1, 0, 3, 2, 4).reshape(B, L, H, DV)


def stick_breaking(q, k, v, key_valid):
    B, L, H, D = q.shape
    nb = L // BLOCK
    scale = 1.0 / math.sqrt(D)
    qb = q.astype(jnp.float32).reshape(B, nb, BLOCK, H, D).transpose(1, 0, 3, 2, 4)
    kf = k.astype(jnp.float32)
    vf = v.astype(jnp.float32)
    kpos = jnp.arange(L)

    def one_block(args):
        qi, start = args
        z = jnp.einsum('bhtd,bshd->bhts', qi, kf) * scale
        qpos = start + jnp.arange(BLOCK)
        mask = (kpos[None, :] < qpos[:, None]) & key_valid[None, :]
        log_beta = jax.nn.log_sigmoid(z)
        log_1mb = jnp.where(mask, jax.nn.log_sigmoid(-z), 0.0)
        suffix = jnp.flip(jnp.cumsum(jnp.flip(log_1mb, axis=-1), axis=-1), axis=-1)
        w = jnp.where(mask, jnp.exp(log_beta + suffix - log_1mb), 0.0)
        return jnp.einsum('bhts,bshd->bhtd', w, vf)

    starts = jnp.arange(nb) * BLOCK
    o = lax.map(one_block, (qb, starts))
    return o.transpose(1, 0, 3, 2, 4).reshape(B, L, H, D)


def gated_mixer(h, valid, w_in, b_gate, lb, hg_norm_g, w_proj_hg, w_proj_sb, w_out):
    B, L, _ = h.shape
    proj = h @ w_in
    hq, hf, hi, hog, sq, sk, sv, gates = jnp.split(proj, SPLIT_IDX, axis=-1)
    vmask = valid[:, None]

    f = lb + (1.0 - lb) * jax.nn.sigmoid(hf.astype(jnp.float32))
    log_f = jnp.where(vmask, jnp.log(f), 0.0)
    k_hg = jnp.where(vmask, 1.0 - f, 0.0)
    q_hg = jax.nn.silu(hq.astype(jnp.float32))
    o_hg = hgrn2_chunked(q_hg.reshape(B, L, HG_HEADS, HG_DK), k_hg.reshape(B, L, HG_HEADS, HG_DK),
                         hi.reshape(B, L, HG_HEADS, HG_DV), log_f.reshape(B, L, HG_HEADS, HG_DK))
    o_hg = o_hg * lax.rsqrt(jnp.mean(jnp.square(o_hg), axis=-1, keepdims=True) + RMS_EPS)
    o_hg = o_hg * hg_norm_g.astype(jnp.float32).reshape(HG_HEADS, HG_DV)
    o_hg = (o_hg.reshape(B, L, HG_V_W) * jax.nn.silu(hog.astype(jnp.float32))).astype(h.dtype)

    o_sb = stick_breaking(sq.reshape(B, L, SB_HEADS, SB_DH), sk.reshape(B, L, SB_HEADS, SB_DH),
                          sv.reshape(B, L, SB_HEADS, SB_DH), valid)
    o_sb = o_sb.reshape(B, L, SB_W).astype(h.dtype)

    g = jax.nn.sigmoid((gates + b_gate).astype(jnp.float32)).astype(h.dtype)
    g_hg, g_sb = jnp.split(g, 2, axis=-1)
    y = g_hg * (o_hg @ w_proj_hg) + g_sb * (o_sb @ w_proj_sb)
    return y @ w_out


def _fwd_setup_inputs(seed: int = 0) -> dict:
    key = jax.random.key(seed)
    ks = jax.random.split(key, 24)
    f32 = jnp.float32

    def nrm(k, shape, scale):
        return jax.random.normal(k, shape, f32) * scale

    def gain(k, shape):
        return 1.0 + 0.02 * jax.random.normal(k, shape, f32)

    d_inv = D_MODEL ** -0.5
    return {
        "x": nrm(ks[0], (BATCH, SEQ, D_MODEL), 1.0),
        "meta": nrm(ks[1], (N_META, D_MODEL), 1.0),
        "ln1_g": gain(ks[2], (DEPTH, D_MODEL)),
        "ln1_b": nrm(ks[3], (DEPTH, D_MODEL), 0.02),
        "ffn1_w_gate": nrm(ks[4], (DEPTH, D_MODEL, D_FF), d_inv),
        "ffn1_w_up": nrm(ks[5], (DEPTH, D_MODEL, D_FF), d_inv),
        "ffn1_w_down": nrm(ks[6], (DEPTH, D_FF, D_MODEL), D_FF ** -0.5 * DN_BETA),
        "w_in": nrm(ks[7], (DEPTH, D_MODEL, IN_COLS), d_inv),
        "b_gate": nrm(ks[8], (DEPTH, 2 * D_MODEL), 0.1),
        "hg_lb_logits": nrm(ks[9], (DEPTH + 1, HG_QK_W), 0.1),
        "hg_norm_g": gain(ks[10], (DEPTH, HG_V_W)),
        "w_proj_hg": nrm(ks[11], (DEPTH, HG_V_W, D_MODEL), HG_V_W ** -0.5),
        "w_proj_sb": nrm(ks[12], (DEPTH, SB_W, D_MODEL), SB_W ** -0.5),
        "w_out": nrm(ks[13], (DEPTH, D_MODEL, D_MODEL), d_inv * DN_BETA),
        "ln2_g": gain(ks[14], (DEPTH, D_MODEL)),
        "ln2_b": nrm(ks[15], (DEPTH, D_MODEL), 0.02),
        "ffn2_w_gate": nrm(ks[16], (DEPTH, D_MODEL, D_FF), d_inv),
        "ffn2_w_up": nrm(ks[17], (DEPTH, D_MODEL, D_FF), d_inv),
        "ffn2_w_down": nrm(ks[18], (DEPTH, D_FF, D_MODEL), D_FF ** -0.5 * DN_BETA),
        "ln3_g": gain(ks[19], (DEPTH, D_MODEL)),
        "ln3_b": nrm(ks[20], (DEPTH, D_MODEL), 0.02),
    }


def _fwd_reference(x, meta, ln1_g, ln1_b, ffn1_w_gate, ffn1_w_up, ffn1_w_down, w_in, b_gate, hg_lb_logits,
              hg_norm_g, w_proj_hg, w_proj_sb, w_out, ln2_g, ln2_b, ffn2_w_gate, ffn2_w_up, ffn2_w_down,
              ln3_g, ln3_b):
    B, S, D = x.shape
    pad = jnp.zeros((B, PAD, D), x.dtype)
    meta_b = jnp.broadcast_to(meta.astype(x.dtype)[None], (B, N_META, D))
    h = jnp.concatenate([pad, meta_b, x], axis=1)
    L = h.shape[1]
    valid = jnp.arange(L) >= PAD

    lb_all = jnp.cumsum(jax.nn.softmax(hg_lb_logits.astype(jnp.float32), axis=0), axis=0)

    for l in range(DEPTH):
        h = layer_norm(DN_ALPHA * h + 0.5 * swiglu(h, ffn1_w_gate[l], ffn1_w_up[l], ffn1_w_down[l]),
                       ln1_g[l], ln1_b[l])
        mix = gated_mixer(h, valid, w_in[l], b_gate[l], lb_all[l], hg_norm_g[l],
                          w_proj_hg[l], w_proj_sb[l], w_out[l])
        h = layer_norm(DN_ALPHA * h + mix, ln2_g[l], ln2_b[l])
        h = layer_norm(DN_ALPHA * h + 0.5 * swiglu(h, ffn2_w_gate[l], ffn2_w_up[l], ffn2_w_down[l]),
                       ln3_g[l], ln3_b[l])

    return h[:, PAD + N_META:]


import jax as _jax
import jax.numpy as _jnp

TWIN_FORMAT = 'train_step'
FWD_PARAMS = ['x', 'meta', 'ln1_g', 'ln1_b', 'ffn1_w_gate', 'ffn1_w_up', 'ffn1_w_down', 'w_in', 'b_gate', 'hg_lb_logits', 'hg_norm_g', 'w_proj_hg', 'w_proj_sb', 'w_out', 'ln2_g', 'ln2_b', 'ffn2_w_gate', 'ffn2_w_up', 'ffn2_w_down', 'ln3_g', 'ln3_b']
TWIN_WEIGHTS = ['meta', 'ln1_g', 'ln1_b', 'ffn1_w_gate', 'ffn1_w_up', 'ffn1_w_down', 'w_in', 'b_gate', 'hg_lb_logits', 'hg_norm_g', 'w_proj_hg', 'w_proj_sb', 'w_out', 'ln2_g', 'ln2_b', 'ffn2_w_gate', 'ffn2_w_up', 'ffn2_w_down', 'ln3_g', 'ln3_b']
TWIN_DIFF_INPUT = 'x'
TWIN_INPUTS = ['x', 'meta', 'ln1_g', 'ln1_b', 'ffn1_w_gate', 'ffn1_w_up', 'ffn1_w_down', 'w_in', 'b_gate', 'hg_lb_logits', 'hg_norm_g', 'w_proj_hg', 'w_proj_sb', 'w_out', 'ln2_g', 'ln2_b', 'ffn2_w_gate', 'ffn2_w_up', 'ffn2_w_down', 'ln3_g', 'ln3_b', 'loss_target', 'm_meta', 'm_ln1_g', 'm_ln1_b', 'm_ffn1_w_gate', 'm_ffn1_w_up', 'm_ffn1_w_down', 'm_w_in', 'm_b_gate', 'm_hg_lb_logits', 'm_hg_norm_g', 'm_w_proj_hg', 'm_w_proj_sb', 'm_w_out', 'm_ln2_g', 'm_ln2_b', 'm_ffn2_w_gate', 'm_ffn2_w_up', 'm_ffn2_w_down', 'm_ln3_g', 'm_ln3_b', 'v_meta', 'v_ln1_g', 'v_ln1_b', 'v_ffn1_w_gate', 'v_ffn1_w_up', 'v_ffn1_w_down', 'v_w_in', 'v_b_gate', 'v_hg_lb_logits', 'v_hg_norm_g', 'v_w_proj_hg', 'v_w_proj_sb', 'v_w_out', 'v_ln2_g', 'v_ln2_b', 'v_ffn2_w_gate', 'v_ffn2_w_up', 'v_ffn2_w_down', 'v_ln3_g', 'v_ln3_b']
TWIN_OUTPUTS = ['loss', 'grad_x', 'grad_meta', 'grad_ln1_g', 'grad_ln1_b', 'grad_ffn1_w_gate', 'grad_ffn1_w_up', 'grad_ffn1_w_down', 'grad_w_in', 'grad_b_gate', 'grad_hg_lb_logits', 'grad_hg_norm_g', 'grad_w_proj_hg', 'grad_w_proj_sb', 'grad_w_out', 'grad_ln2_g', 'grad_ln2_b', 'grad_ffn2_w_gate', 'grad_ffn2_w_up', 'grad_ffn2_w_down', 'grad_ln3_g', 'grad_ln3_b', 'delta_meta', 'delta_ln1_g', 'delta_ln1_b', 'delta_ffn1_w_gate', 'delta_ffn1_w_up', 'delta_ffn1_w_down', 'delta_w_in', 'delta_b_gate', 'delta_hg_lb_logits', 'delta_hg_norm_g', 'delta_w_proj_hg', 'delta_w_proj_sb', 'delta_w_out', 'delta_ln2_g', 'delta_ln2_b', 'delta_ffn2_w_gate', 'delta_ffn2_w_up', 'delta_ffn2_w_down', 'delta_ln3_g', 'delta_ln3_b', 'new_m_meta', 'new_m_ln1_g', 'new_m_ln1_b', 'new_m_ffn1_w_gate', 'new_m_ffn1_w_up', 'new_m_ffn1_w_down', 'new_m_w_in', 'new_m_b_gate', 'new_m_hg_lb_logits', 'new_m_hg_norm_g', 'new_m_w_proj_hg', 'new_m_w_proj_sb', 'new_m_w_out', 'new_m_ln2_g', 'new_m_ln2_b', 'new_m_ffn2_w_gate', 'new_m_ffn2_w_up', 'new_m_ffn2_w_down', 'new_m_ln3_g', 'new_m_ln3_b', 'new_v_meta', 'new_v_ln1_g', 'new_v_ln1_b', 'new_v_ffn1_w_gate', 'new_v_ffn1_w_up', 'new_v_ffn1_w_down', 'new_v_w_in', 'new_v_b_gate', 'new_v_hg_lb_logits', 'new_v_hg_norm_g', 'new_v_w_proj_hg', 'new_v_w_proj_sb', 'new_v_w_out', 'new_v_ln2_g', 'new_v_ln2_b', 'new_v_ffn2_w_gate', 'new_v_ffn2_w_up', 'new_v_ffn2_w_down', 'new_v_ln3_g', 'new_v_ln3_b']
TWIN_LEAF_KINDS = {'loss': 'loss', 'grad_x': 'grad_x', 'grad_meta': 'grad_w', 'grad_ln1_g': 'grad_w', 'grad_ln1_b': 'grad_w', 'grad_ffn1_w_gate': 'grad_w', 'grad_ffn1_w_up': 'grad_w', 'grad_ffn1_w_down': 'grad_w', 'grad_w_in': 'grad_w', 'grad_b_gate': 'grad_w', 'grad_hg_lb_logits': 'grad_w', 'grad_hg_norm_g': 'grad_w', 'grad_w_proj_hg': 'grad_w', 'grad_w_proj_sb': 'grad_w', 'grad_w_out': 'grad_w', 'grad_ln2_g': 'grad_w', 'grad_ln2_b': 'grad_w', 'grad_ffn2_w_gate': 'grad_w', 'grad_ffn2_w_up': 'grad_w', 'grad_ffn2_w_down': 'grad_w', 'grad_ln3_g': 'grad_w', 'grad_ln3_b': 'grad_w', 'delta_meta': 'delta_w', 'delta_ln1_g': 'delta_w', 'delta_ln1_b': 'delta_w', 'delta_ffn1_w_gate': 'delta_w', 'delta_ffn1_w_up': 'delta_w', 'delta_ffn1_w_down': 'delta_w', 'delta_w_in': 'delta_w', 'delta_b_gate': 'delta_w', 'delta_hg_lb_logits': 'delta_w', 'delta_hg_norm_g': 'delta_w', 'delta_w_proj_hg': 'delta_w', 'delta_w_proj_sb': 'delta_w', 'delta_w_out': 'delta_w', 'delta_ln2_g': 'delta_w', 'delta_ln2_b': 'delta_w', 'delta_ffn2_w_gate': 'delta_w', 'delta_ffn2_w_up': 'delta_w', 'delta_ffn2_w_down': 'delta_w', 'delta_ln3_g': 'delta_w', 'delta_ln3_b': 'delta_w', 'new_m_meta': 'new_m', 'new_m_ln1_g': 'new_m', 'new_m_ln1_b': 'new_m', 'new_m_ffn1_w_gate': 'new_m', 'new_m_ffn1_w_up': 'new_m', 'new_m_ffn1_w_down': 'new_m', 'new_m_w_in': 'new_m', 'new_m_b_gate': 'new_m', 'new_m_hg_lb_logits': 'new_m', 'new_m_hg_norm_g': 'new_m', 'new_m_w_proj_hg': 'new_m', 'new_m_w_proj_sb': 'new_m', 'new_m_w_out': 'new_m', 'new_m_ln2_g': 'new_m', 'new_m_ln2_b': 'new_m', 'new_m_ffn2_w_gate': 'new_m', 'new_m_ffn2_w_up': 'new_m', 'new_m_ffn2_w_down': 'new_m', 'new_m_ln3_g': 'new_m', 'new_m_ln3_b': 'new_m', 'new_v_meta': 'new_v', 'new_v_ln1_g': 'new_v', 'new_v_ln1_b': 'new_v', 'new_v_ffn1_w_gate': 'new_v', 'new_v_ffn1_w_up': 'new_v', 'new_v_ffn1_w_down': 'new_v', 'new_v_w_in': 'new_v', 'new_v_b_gate': 'new_v', 'new_v_hg_lb_logits': 'new_v', 'new_v_hg_norm_g': 'new_v', 'new_v_w_proj_hg': 'new_v', 'new_v_w_proj_sb': 'new_v', 'new_v_w_out': 'new_v', 'new_v_ln2_g': 'new_v', 'new_v_ln2_b': 'new_v', 'new_v_ffn2_w_gate': 'new_v', 'new_v_ffn2_w_up': 'new_v', 'new_v_ffn2_w_down': 'new_v', 'new_v_ln3_g': 'new_v', 'new_v_ln3_b': 'new_v'}


def _forward(args):
    return _fwd_reference(*[args[k] for k in FWD_PARAMS])


def _output_shape():
    def fwd():
        inp = _fwd_setup_inputs(0)
        return _fwd_reference(*[inp[k] for k in FWD_PARAMS])
    out = _jax.eval_shape(fwd)
    return out.shape, out.dtype

N_MICROBATCH = 1
ADAM_LR = 0.001
ADAM_B1 = 0.9
ADAM_B2 = 0.999
ADAM_EPS = 1e-08
ADAM_WD = 0.01
ADAM_STEP = 10
PER_EXAMPLE_BATCH_AXIS = {'x': 0, 'loss_target': 0}
SHARED_INPUTS = []
_WEIGHT_DTYPES = {'meta': _jnp.float32, 'ln1_g': _jnp.float32, 'ln1_b': _jnp.float32, 'ffn1_w_gate': _jnp.float32, 'ffn1_w_up': _jnp.float32, 'ffn1_w_down': _jnp.float32, 'w_in': _jnp.float32, 'b_gate': _jnp.float32, 'hg_lb_logits': _jnp.float32, 'hg_norm_g': _jnp.float32, 'w_proj_hg': _jnp.float32, 'w_proj_sb': _jnp.float32, 'w_out': _jnp.float32, 'ln2_g': _jnp.float32, 'ln2_b': _jnp.float32, 'ffn2_w_gate': _jnp.float32, 'ffn2_w_up': _jnp.float32, 'ffn2_w_down': _jnp.float32, 'ln3_g': _jnp.float32, 'ln3_b': _jnp.float32}
MOMENT_SCALE = {'meta': 5.651063e-04, 'ln1_g': 1.003720e+00, 'ln1_b': 4.824882e-01, 'ffn1_w_gate': 1.191425e-02, 'ffn1_w_up': 1.154010e-02, 'ffn1_w_down': 3.217321e-02, 'w_in': 1.729214e-02, 'b_gate': 8.099138e-03, 'hg_lb_logits': 2.551267e-03, 'hg_norm_g': 2.857893e-02, 'w_proj_hg': 1.999216e-02, 'w_proj_sb': 2.157486e-02, 'w_out': 4.950857e-02, 'ln2_g': 1.074106e+00, 'ln2_b': 4.906191e-01, 'ffn2_w_gate': 1.150227e-02, 'ffn2_w_up': 1.116550e-02, 'ffn2_w_down': 3.113356e-02, 'ln3_g': 3.201018e+01, 'ln3_b': 9.203026e-01}


def _to_microbatches(a, axis):
    t = _jnp.moveaxis(a, axis, 0)
    t = t.reshape((N_MICROBATCH, t.shape[0] // N_MICROBATCH) + t.shape[1:])
    return _jnp.moveaxis(t, 1, axis + 1)


def setup_inputs(seed: int = 0) -> dict:
    inp = _fwd_setup_inputs(seed)
    key = _jax.random.fold_in(_jax.random.key(seed), 7919)
    shape, _ = _output_shape()
    out = dict(inp)
    out["loss_target"] = _jax.random.normal(_jax.random.fold_in(key, 0), shape, _jnp.float32)
    for i, name in enumerate(TWIN_WEIGHTS):
        w = inp[name].astype(_jnp.float32)
        if MOMENT_SCALE is None:
            s = _jnp.sqrt(_jnp.mean(_jnp.square(w)) + 1e-30)
        else:
            s = MOMENT_SCALE[name]
        km, kv = _jax.random.split(_jax.random.fold_in(key, i + 1))
        out[name] = w
        out["m_" + name] = s * _jax.random.normal(km, w.shape, _jnp.float32)
        out["v_" + name] = (s * s) * _jax.random.uniform(kv, w.shape, _jnp.float32, 0.5, 1.5)
    if N_MICROBATCH > 1:
        for name, axis in PER_EXAMPLE_BATCH_AXIS.items():
            out[name] = _to_microbatches(out[name], axis)
    return {'x': out['x'], 'meta': out['meta'], 'ln1_g': out['ln1_g'], 'ln1_b': out['ln1_b'], 'ffn1_w_gate': out['ffn1_w_gate'], 'ffn1_w_up': out['ffn1_w_up'], 'ffn1_w_down': out['ffn1_w_down'], 'w_in': out['w_in'], 'b_gate': out['b_gate'], 'hg_lb_logits': out['hg_lb_logits'], 'hg_norm_g': out['hg_norm_g'], 'w_proj_hg': out['w_proj_hg'], 'w_proj_sb': out['w_proj_sb'], 'w_out': out['w_out'], 'ln2_g': out['ln2_g'], 'ln2_b': out['ln2_b'], 'ffn2_w_gate': out['ffn2_w_gate'], 'ffn2_w_up': out['ffn2_w_up'], 'ffn2_w_down': out['ffn2_w_down'], 'ln3_g': out['ln3_g'], 'ln3_b': out['ln3_b'], 'loss_target': out['loss_target'], 'm_meta': out['m_meta'], 'm_ln1_g': out['m_ln1_g'], 'm_ln1_b': out['m_ln1_b'], 'm_ffn1_w_gate': out['m_ffn1_w_gate'], 'm_ffn1_w_up': out['m_ffn1_w_up'], 'm_ffn1_w_down': out['m_ffn1_w_down'], 'm_w_in': out['m_w_in'], 'm_b_gate': out['m_b_gate'], 'm_hg_lb_logits': out['m_hg_lb_logits'], 'm_hg_norm_g': out['m_hg_norm_g'], 'm_w_proj_hg': out['m_w_proj_hg'], 'm_w_proj_sb': out['m_w_proj_sb'], 'm_w_out': out['m_w_out'], 'm_ln2_g': out['m_ln2_g'], 'm_ln2_b': out['m_ln2_b'], 'm_ffn2_w_gate': out['m_ffn2_w_gate'], 'm_ffn2_w_up': out['m_ffn2_w_up'], 'm_ffn2_w_down': out['m_ffn2_w_down'], 'm_ln3_g': out['m_ln3_g'], 'm_ln3_b': out['m_ln3_b'], 'v_meta': out['v_meta'], 'v_ln1_g': out['v_ln1_g'], 'v_ln1_b': out['v_ln1_b'], 'v_ffn1_w_gate': out['v_ffn1_w_gate'], 'v_ffn1_w_up': out['v_ffn1_w_up'], 'v_ffn1_w_down': out['v_ffn1_w_down'], 'v_w_in': out['v_w_in'], 'v_b_gate': out['v_b_gate'], 'v_hg_lb_logits': out['v_hg_lb_logits'], 'v_hg_norm_g': out['v_hg_norm_g'], 'v_w_proj_hg': out['v_w_proj_hg'], 'v_w_proj_sb': out['v_w_proj_sb'], 'v_w_out': out['v_w_out'], 'v_ln2_g': out['v_ln2_g'], 'v_ln2_b': out['v_ln2_b'], 'v_ffn2_w_gate': out['v_ffn2_w_gate'], 'v_ffn2_w_up': out['v_ffn2_w_up'], 'v_ffn2_w_down': out['v_ffn2_w_down'], 'v_ln3_g': out['v_ln3_g'], 'v_ln3_b': out['v_ln3_b']}


def _loss(weights, diff, rest, loss_target):
    with _jax.named_scope("forward"):
        args = {**rest, TWIN_DIFF_INPUT: diff, **{k: w.astype(_WEIGHT_DTYPES[k]) for k, w in weights.items()}}
        y = _forward(args)
    with _jax.named_scope("loss_head"):
        err = _jnp.square(y.astype(_jnp.float32) - loss_target)
        return 0.5 * _jnp.sum(_jnp.mean(err, axis=-1)) if err.ndim else 0.5 * err


def _adamw(w, g, m, v):
    m = ADAM_B1 * m + (1.0 - ADAM_B1) * g
    v = ADAM_B2 * v + (1.0 - ADAM_B2) * _jnp.square(g)
    m_hat = m / (1.0 - ADAM_B1 ** ADAM_STEP)
    v_hat = v / (1.0 - ADAM_B2 ** ADAM_STEP)
    delta = -ADAM_LR * (m_hat / (_jnp.sqrt(v_hat) + ADAM_EPS) + ADAM_WD * w)
    return delta, m, v


def reference(x, meta, ln1_g, ln1_b, ffn1_w_gate, ffn1_w_up, ffn1_w_down, w_in, b_gate, hg_lb_logits, hg_norm_g, w_proj_hg, w_proj_sb, w_out, ln2_g, ln2_b, ffn2_w_gate, ffn2_w_up, ffn2_w_down, ln3_g, ln3_b, loss_target, m_meta, m_ln1_g, m_ln1_b, m_ffn1_w_gate, m_ffn1_w_up, m_ffn1_w_down, m_w_in, m_b_gate, m_hg_lb_logits, m_hg_norm_g, m_w_proj_hg, m_w_proj_sb, m_w_out, m_ln2_g, m_ln2_b, m_ffn2_w_gate, m_ffn2_w_up, m_ffn2_w_down, m_ln3_g, m_ln3_b, v_meta, v_ln1_g, v_ln1_b, v_ffn1_w_gate, v_ffn1_w_up, v_ffn1_w_down, v_w_in, v_b_gate, v_hg_lb_logits, v_hg_norm_g, v_w_proj_hg, v_w_proj_sb, v_w_out, v_ln2_g, v_ln2_b, v_ffn2_w_gate, v_ffn2_w_up, v_ffn2_w_down, v_ln3_g, v_ln3_b):
    given = dict(x=x, meta=meta, ln1_g=ln1_g, ln1_b=ln1_b, ffn1_w_gate=ffn1_w_gate, ffn1_w_up=ffn1_w_up, ffn1_w_down=ffn1_w_down, w_in=w_in, b_gate=b_gate, hg_lb_logits=hg_lb_logits, hg_norm_g=hg_norm_g, w_proj_hg=w_proj_hg, w_proj_sb=w_proj_sb, w_out=w_out, ln2_g=ln2_g, ln2_b=ln2_b, ffn2_w_gate=ffn2_w_gate, ffn2_w_up=ffn2_w_up, ffn2_w_down=ffn2_w_down, ln3_g=ln3_g, ln3_b=ln3_b, loss_target=loss_target, m_meta=m_meta, m_ln1_g=m_ln1_g, m_ln1_b=m_ln1_b, m_ffn1_w_gate=m_ffn1_w_gate, m_ffn1_w_up=m_ffn1_w_up, m_ffn1_w_down=m_ffn1_w_down, m_w_in=m_w_in, m_b_gate=m_b_gate, m_hg_lb_logits=m_hg_lb_logits, m_hg_norm_g=m_hg_norm_g, m_w_proj_hg=m_w_proj_hg, m_w_proj_sb=m_w_proj_sb, m_w_out=m_w_out, m_ln2_g=m_ln2_g, m_ln2_b=m_ln2_b, m_ffn2_w_gate=m_ffn2_w_gate, m_ffn2_w_up=m_ffn2_w_up, m_ffn2_w_down=m_ffn2_w_down, m_ln3_g=m_ln3_g, m_ln3_b=m_ln3_b, v_meta=v_meta, v_ln1_g=v_ln1_g, v_ln1_b=v_ln1_b, v_ffn1_w_gate=v_ffn1_w_gate, v_ffn1_w_up=v_ffn1_w_up, v_ffn1_w_down=v_ffn1_w_down, v_w_in=v_w_in, v_b_gate=v_b_gate, v_hg_lb_logits=v_hg_lb_logits, v_hg_norm_g=v_hg_norm_g, v_w_proj_hg=v_w_proj_hg, v_w_proj_sb=v_w_proj_sb, v_w_out=v_w_out, v_ln2_g=v_ln2_g, v_ln2_b=v_ln2_b, v_ffn2_w_gate=v_ffn2_w_gate, v_ffn2_w_up=v_ffn2_w_up, v_ffn2_w_down=v_ffn2_w_down, v_ln3_g=v_ln3_g, v_ln3_b=v_ln3_b)
    weights = {n: given[n] for n in TWIN_WEIGHTS}
    shared = {n: given[n] for n in SHARED_INPUTS}
    per_example = {n: given[n] for n in ['x']}
    grad_fn = _jax.value_and_grad(_loss, argnums=(0, 1))

    def one_microbatch(ex, loss_target):
        ex = dict(ex)
        diff = ex.pop(TWIN_DIFF_INPUT)
        return grad_fn(weights, diff, {**shared, **ex}, loss_target)

    if N_MICROBATCH == 1:
        loss, (grad_w, grad_x) = one_microbatch(per_example, given["loss_target"])
    else:
        def body(carry, xs):
            loss_sum, grad_sum = carry
            l_k, (gw_k, gx_k) = one_microbatch(xs[0], xs[1])
            with _jax.named_scope("update"):
                return (loss_sum + l_k, _jax.tree.map(_jnp.add, grad_sum, gw_k)), gx_k

        init = (_jnp.zeros((), _jnp.float32), _jax.tree.map(_jnp.zeros_like, weights))
        (loss, grad_w), grad_x = _jax.lax.scan(body, init, (per_example, given["loss_target"]))
    with _jax.named_scope("update"):
        delta_w, new_m, new_v = {}, {}, {}
        for n in TWIN_WEIGHTS:
            delta_w[n], new_m[n], new_v[n] = _adamw(weights[n], grad_w[n], given["m_" + n], given["v_" + n])
    return (loss, grad_x, *[grad_w[n] for n in TWIN_WEIGHTS], *[delta_w[n] for n in TWIN_WEIGHTS],
            *[new_m[n] for n in TWIN_WEIGHTS], *[new_v[n] for n in TWIN_WEIGHTS])
```

```python
import functools
import math

import jax
import jax.numpy as jnp
from jax import lax
from jax.experimental import pallas as pl
from jax.experimental.pallas import tpu as pltpu

F32 = jnp.float32
BF16 = jnp.bfloat16
MESH = pl.DeviceIdType.MESH

N_DEV = 8
N_META = 16
BLOCK = 128
PAD = BLOCK - N_META
HEAD = 128
CHUNK = 16
LN_EPS = 1e-5
RMS_EPS = 1e-6
DN_ALPHA = 2.0 ** 0.25
ADAM_LR, ADAM_B1, ADAM_B2, ADAM_EPS, ADAM_WD, ADAM_STEP = 0.001, 0.9, 0.999, 1e-08, 0.01, 10

VMEM_LIMIT_V7X = 60 * 1024 * 1024
ROW_TILE = 640
LN_ROW_TILE = 320

NN = (((1,), (0,)), ((), ()))
NT = (((1,), (1,)), ((), ()))
TN = (((0,), (0,)), ((), ()))


def _tile(n, pref, mult=16):
    best = None
    for t in range(mult, min(n, pref) + 1, mult):
        if n % t == 0:
            best = t
    return n if best is None else best


def _params(n_axes):
    return pltpu.CompilerParams(dimension_semantics=("arbitrary",) * n_axes, vmem_limit_bytes=VMEM_LIMIT_V7X)


def _sigmoid(x):
    return 1.0 / (1.0 + jnp.exp(-x))


def _gemm(name, grid, pairs, acc_of, acc_shapes, dims, extras, outs, epilogue):
    n_pairs, n_extra, n_out = len(pairs), len(extras), len(outs)
    nk = grid[-1]
    k_axis = len(grid) - 1

    def body(*refs):
        pr = refs[:2 * n_pairs]
        er = refs[2 * n_pairs:2 * n_pairs + n_extra]
        orf = refs[2 * n_pairs + n_extra:2 * n_pairs + n_extra + n_out]
        accs = refs[2 * n_pairs + n_extra + n_out:]

        def part(p):
            return lax.dot_general(pr[2 * p][...].astype(BF16), pr[2 * p + 1][...].astype(BF16), dims,
                                   preferred_element_type=F32)

        if nk == 1:
            vals = [None] * len(acc_shapes)
            for p in range(n_pairs):
                d = part(p)
                vals[acc_of[p]] = d if vals[acc_of[p]] is None else vals[acc_of[p]] + d
            epilogue(vals, er, orf)
        else:
            k = pl.program_id(k_axis)

            @pl.when(k == 0)
            def _():
                for acc in accs:
                    acc[...] = jnp.zeros_like(acc)

            for p in range(n_pairs):
                accs[acc_of[p]][...] += part(p)

            @pl.when(k == nk - 1)
            def _():
                epilogue([acc[...] for acc in accs], er, orf)

    operands, in_specs = [], []
    for a, a_spec, b, b_spec in pairs:
        operands += [a, b]
        in_specs += [a_spec, b_spec]
    for e, e_spec in extras:
        operands.append(e)
        in_specs.append(e_spec)
    scratch = [] if nk == 1 else [pltpu.VMEM(s, F32) for s in acc_shapes]
    res = pl.pallas_call(
        body, name=name, grid=grid, in_specs=in_specs,
        out_specs=[s for _, s in outs], out_shape=[o for o, _ in outs],
        scratch_shapes=scratch, compiler_params=_params(len(grid)),
    )(*operands)
    return res


def _sds(shape, dtype):
    return jax.ShapeDtypeStruct(shape, dtype)


def _ln_rows(r, g, b):
    mu = jnp.mean(r, axis=-1, keepdims=True)
    xc = r - mu
    var = jnp.mean(xc * xc, axis=-1, keepdims=True)
    return xc * lax.rsqrt(var + LN_EPS) * g + b


def _ffn_up(name, hb, wg, wu):
    m, d = hb.shape
    nd, _, fs = wg.shape
    tm = _tile(m, ROW_TILE)

    def epi(acc, er, orf):
        a, b = acc
        orf[0][...] = a
        orf[1][...] = b
        orf[2][...] = (a * _sigmoid(a) * b).astype(BF16)

    h_spec = pl.BlockSpec((tm, d), lambda i, j, k: (i, 0))
    w_spec = pl.BlockSpec((None, d, fs), lambda i, j, k: (j, 0, 0))
    o_spec = pl.BlockSpec((None, tm, fs), lambda i, j, k: (j, i, 0))
    return _gemm(name,(m // tm, nd, 1), [(hb, h_spec, wg, w_spec), (hb, h_spec, wu, w_spec)], [0, 1],
                 [(tm, fs)] * 2, NN, [],
                 [(_sds((nd, m, fs), F32), o_spec), (_sds((nd, m, fs), F32), o_spec), (_sds((nd, m, fs), BF16), o_spec)], epi)


def _residual_ln(name, a, a_stacked, w, h_in, g, beta, scale):
    nk, tk, d = w.shape
    m = h_in.shape[0]
    tm = _tile(m, LN_ROW_TILE)

    def epi(acc, er, orf):
        r = DN_ALPHA * er[0][...] + scale * acc[0]
        h = _ln_rows(r, er[1][...], er[2][...])
        orf[0][...] = r
        orf[1][...] = h
        orf[2][...] = h.astype(BF16)

    if a_stacked:
        a_spec = pl.BlockSpec((None, tm, tk), lambda i, k: (k, i, 0))
    else:
        a_spec = pl.BlockSpec((tm, tk), lambda i, k: (i, k))
    w_spec = pl.BlockSpec((None, tk, d), lambda i, k: (k, 0, 0))
    row = pl.BlockSpec((tm, d), lambda i, k: (i, 0))
    vec = pl.BlockSpec((1, d), lambda i, k: (0, 0))
    return _gemm(name, (m // tm, nk), [(a, a_spec, w, w_spec)], [0], [(tm, d)], NN,
                 [(h_in, row), (g, vec), (beta, vec)],
                 [(_sds((m, d), F32), row), (_sds((m, d), F32), row), (_sds((m, d), BF16), row)], epi)


def _in_proj(hb, w_in):
    m, d = hb.shape
    nd, _, cs = w_in.shape
    tm = _tile(m, ROW_TILE)

    def epi(acc, er, orf):
        orf[0][...] = acc[0]

    return _gemm("in_proj", (m // tm, nd, 1),
                 [(hb, pl.BlockSpec((tm, d), lambda i, j, k: (i, 0)), w_in, pl.BlockSpec((None, d, cs), lambda i, j, k: (j, 0, 0)))],
                 [0], [(tm, cs)], NN, [], [(_sds((m, nd * cs), F32), pl.BlockSpec((tm, cs), lambda i, j, k: (i, j)))], epi)[0]


def _proj_merge(o_hg, o_sb, p_hg, p_sb, proj, b_gate, gate_col):
    m, w = o_hg.shape
    nd, _, ds = p_hg.shape
    d = nd * ds
    tm = _tile(m, ROW_TILE)
    c0 = gate_col // ds

    def epi(acc, er, orf):
        u_hg, u_sb = acc
        g_hg = _sigmoid(er[0][...] + er[2][...])
        g_sb = _sigmoid(er[1][...] + er[3][...])
        orf[0][...] = u_hg
        orf[1][...] = u_sb
        orf[2][...] = (g_hg * u_hg + g_sb * u_sb).astype(BF16)

    o_spec = pl.BlockSpec((tm, w), lambda i, j, k: (i, 0))
    p_spec = pl.BlockSpec((None, w, ds), lambda i, j, k: (j, 0, 0))
    out = pl.BlockSpec((tm, ds), lambda i, j, k: (i, j))
    return _gemm("proj_merge", (m // tm, nd, 1), [(o_hg, o_spec, p_hg, p_spec), (o_sb, o_spec, p_sb, p_spec)], [0, 1],
                 [(tm, ds)] * 2, NN,
                 [(proj, pl.BlockSpec((tm, ds), lambda i, j, k: (i, c0 + j))),
                  (proj, pl.BlockSpec((tm, ds), lambda i, j, k: (i, c0 + nd + j))),
                  (b_gate, pl.BlockSpec((1, ds), lambda i, j, k: (0, j))),
                  (b_gate, pl.BlockSpec((1, ds), lambda i, j, k: (0, nd + j)))],
                 [(_sds((m, d), F32), out), (_sds((m, d), F32), out), (_sds((m, d), BF16), out)], epi)


def _ln_bwd(name, r, g, out_scale, dy=None, beta=None, target=None, first_row=0):
    m, d = r.shape
    tm = _tile(m, LN_ROW_TILE if target is None else BLOCK)
    with_loss = target is not None
    skip = first_row // tm if with_loss else 0
    assert not with_loss or first_row % tm == 0

    def body(*refs):
        if with_loss:
            r_ref, g_ref, b_ref, t_ref, dr_ref, drb_ref, dg_ref, db_ref, loss_ref = refs
        else:
            r_ref, g_ref, dy_ref, dr_ref, drb_ref, dg_ref, db_ref = refs
        i = pl.program_id(0)
        x = r_ref[...]
        mu = jnp.mean(x, axis=-1, keepdims=True)
        xc = x - mu
        var = jnp.mean(xc * xc, axis=-1, keepdims=True)
        rstd = lax.rsqrt(var + LN_EPS)
        xhat = xc * rstd
        gv = g_ref[...]
        if with_loss:
            err = xhat * gv + b_ref[...] - t_ref[...]
            live = (i >= skip).astype(F32)
            dyv = err * (live / d)
            part = 0.5 * live * jnp.sum(jnp.sum(err * err, axis=-1, keepdims=True), axis=0, keepdims=True) / d
        else:
            dyv = dy_ref[...]
        dxh = dyv * gv
        m1 = jnp.mean(dxh, axis=-1, keepdims=True)
        m2 = jnp.mean(dxh * xhat, axis=-1, keepdims=True)
        dr = rstd * (dxh - m1 - xhat * m2)
        dr_ref[...] = dr
        drb_ref[...] = (out_scale * dr).astype(BF16)

        @pl.when(i == 0)
        def _():
            dg_ref[...] = jnp.zeros_like(dg_ref)
            db_ref[...] = jnp.zeros_like(db_ref)
            if with_loss:
                loss_ref[...] = jnp.zeros_like(loss_ref)

        dg_ref[...] += jnp.sum(dyv * xhat, axis=0, keepdims=True)
        db_ref[...] += jnp.sum(dyv, axis=0, keepdims=True)
        if with_loss:
            loss_ref[...] += jnp.broadcast_to(part, loss_ref.shape)

    row = pl.BlockSpec((tm, d), lambda i: (i, 0))
    vec = pl.BlockSpec((1, d), lambda i: (0, 0))
    out_shape = [_sds((m, d), F32), _sds((m, d), BF16), _sds((1, d), F32), _sds((1, d), F32)]
    out_specs = [row, row, vec, vec]
    if with_loss:
        operands = [r, g, beta, target]
        in_specs = [row, vec, vec, pl.BlockSpec((tm, d), lambda i: (jnp.maximum(i - skip, 0), 0))]
        out_shape.append(_sds((1, BLOCK), F32))
        out_specs.append(pl.BlockSpec((1, BLOCK), lambda i: (0, 0)))
    else:
        operands = [r, g, dy]
        in_specs = [row, vec, row]
    return pl.pallas_call(body, name=name, grid=(m // tm,), in_specs=in_specs, out_specs=out_specs, out_shape=out_shape,
                          compiler_params=_params(1))(*operands)


def _ffn_bwd(tag, drb, dr, hb, a, b, s, wg, wu, wd):
    m, d = drb.shape
    nd, _, fs = wg.shape
    tm = _tile(m, ROW_TILE)

    def epi_ds(acc, er, orf):
        ds = acc[0]
        av, bv = er[0][...], er[1][...]
        sg = _sigmoid(av)
        orf[0][...] = (ds * bv * sg * (1.0 + av * (1.0 - sg))).astype(BF16)
        orf[1][...] = (ds * av * sg).astype(BF16)

    st = pl.BlockSpec((None, tm, fs), lambda i, j, k: (j, i, 0))
    da, db = _gemm(tag + "_ds", (m // tm, nd, 1),
                   [(drb, pl.BlockSpec((tm, d), lambda i, j, k: (i, 0)), wd, pl.BlockSpec((None, fs, d), lambda i, j, k: (j, 0, 0)))],
                   [0], [(tm, fs)], NT, [(a, st), (b, st)],
                   [(_sds((nd, m, fs), BF16), st), (_sds((nd, m, fs), BF16), st)], epi_ds)

    def epi_w(acc, er, orf):
        for o, v in zip(orf, acc):
            o[...] = v.astype(BF16)

    nkm = m // tm
    dwd = _gemm(tag + "_dwd", (nd, nkm),
                [(s, pl.BlockSpec((None, tm, fs), lambda j, k: (j, k, 0)), drb, pl.BlockSpec((tm, d), lambda j, k: (k, 0)))],
                [0], [(fs, d)], TN, [], [(_sds((nd, fs, d), BF16), pl.BlockSpec((None, fs, d), lambda j, k: (j, 0, 0)))], epi_w)[0]
    h_spec = pl.BlockSpec((tm, d), lambda j, k: (k, 0))
    g_spec = pl.BlockSpec((None, tm, fs), lambda j, k: (j, k, 0))
    w_out = pl.BlockSpec((None, d, fs), lambda j, k: (j, 0, 0))
    dwg, dwu = _gemm(tag + "_dwgu", (nd, nkm), [(hb, h_spec, da, g_spec), (hb, h_spec, db, g_spec)], [0, 1],
                     [(d, fs)] * 2, TN, [], [(_sds((nd, d, fs), BF16), w_out), (_sds((nd, d, fs), BF16), w_out)], epi_w)

    def epi_dh(acc, er, orf):
        orf[0][...] = DN_ALPHA * er[0][...] + acc[0]

    gk = pl.BlockSpec((None, tm, fs), lambda i, k: (k, i, 0))
    wk = pl.BlockSpec((None, d, fs), lambda i, k: (k, 0, 0))
    row = pl.BlockSpec((tm, d), lambda i, k: (i, 0))
    dh = _gemm(tag + "_dh", (m // tm, nd), [(da, gk, wg, wk), (db, gk, wu, wk)], [0, 0], [(tm, d)], NT,
               [(dr, row)], [(_sds((m, d), F32), row)], epi_dh)[0]
    return dh, dwg, dwu, dwd


def _merge_bwd(dmixb, w_out2, proj, b_gate, u_hg, u_sb, gate_col, ds):
    m, d = dmixb.shape
    nd = d // ds
    tm = _tile(m, ROW_TILE)
    c0 = gate_col // ds

    def epi(acc, er, orf):
        i = pl.program_id(1)
        dy = acc[0]
        g_hg = _sigmoid(er[0][...] + er[2][...])
        g_sb = _sigmoid(er[1][...] + er[3][...])
        orf[0][...] = (dy * g_hg).astype(BF16)
        orf[1][...] = (dy * g_sb).astype(BF16)
        dz_hg = dy * er[4][...] * g_hg * (1.0 - g_hg)
        dz_sb = dy * er[5][...] * g_sb * (1.0 - g_sb)
        orf[2][...] = dz_hg.astype(BF16)
        orf[3][...] = dz_sb.astype(BF16)

        @pl.when(i == 0)
        def _():
            orf[4][...] = jnp.zeros_like(orf[4])
            orf[5][...] = jnp.zeros_like(orf[5])

        orf[4][...] += jnp.sum(dz_hg, axis=0, keepdims=True)
        orf[5][...] += jnp.sum(dz_sb, axis=0, keepdims=True)

    tile = pl.BlockSpec((tm, ds), lambda j, i, k: (i, j))
    vec = pl.BlockSpec((1, ds), lambda j, i, k: (0, j))
    du_hg, du_sb, dz_hg, dz_sb, db_hg, db_sb = _gemm(
        "merge_bwd", (nd, m // tm, 1),
        [(dmixb, pl.BlockSpec((tm, d), lambda j, i, k: (i, 0)), w_out2, pl.BlockSpec((ds, d), lambda j, i, k: (j, 0)))],
        [0], [(tm, ds)], NT,
        [(proj, pl.BlockSpec((tm, ds), lambda j, i, k: (i, c0 + j))),
         (proj, pl.BlockSpec((tm, ds), lambda j, i, k: (i, c0 + nd + j))),
         (b_gate, vec), (b_gate, pl.BlockSpec((1, ds), lambda j, i, k: (0, nd + j))),
         (u_hg, tile), (u_sb, tile)],
        [(_sds((m, d), BF16), tile), (_sds((m, d), BF16), tile), (_sds((m, d), BF16), tile), (_sds((m, d), BF16), tile),
         (_sds((1, d), F32), vec), (_sds((1, d), F32), vec)], epi)
    return du_hg, du_sb, dz_hg, dz_sb, jnp.concatenate([db_hg, db_sb], axis=1)


def _grad_w(name, x, dy, nd_out):
    m, kx = x.shape
    n = dy.shape[1]
    ns = n // nd_out
    tm = _tile(m, ROW_TILE)

    def epi(acc, er, orf):
        orf[0][...] = acc[0].astype(BF16)

    return _gemm(name, (nd_out, m // tm),
                 [(x, pl.BlockSpec((tm, kx), lambda j, k: (k, 0)), dy, pl.BlockSpec((tm, ns), lambda j, k: (k, j)))],
                 [0], [(kx, ns)], TN, [], [(_sds((nd_out, kx, ns), BF16), pl.BlockSpec((None, kx, ns), lambda j, k: (j, 0, 0)))], epi)[0]


def _grad_in(name, dy, w, add=None):
    m = dy.shape[0]
    nd, kx, ns = w.shape
    tm = _tile(m, ROW_TILE)

    def epi(acc, er, orf):
        orf[0][...] = acc[0] if add is None else DN_ALPHA * er[0][...] + acc[0]

    row = pl.BlockSpec((tm, kx), lambda i, k: (i, 0))
    return _gemm(name, (m // tm, nd),
                 [(dy, pl.BlockSpec((tm, ns), lambda i, k: (i, k)), w, pl.BlockSpec((None, kx, ns), lambda i, k: (k, 0, 0)))],
                 [0], [(tm, kx)], NT, [] if add is None else [(add, row)], [(_sds((m, kx), F32), row)], epi)[0]


def _tri(n, kind):
    r = lax.broadcasted_iota(jnp.int32, (n, n), 0)
    c = lax.broadcasted_iota(jnp.int32, (n, n), 1)
    return {"le": c <= r, "ge": c >= r, "gt": r > c, "lt": r < c}[kind]


def _dot_f32(a, b, dims=NN):
    return lax.dot_general(a, b, dims, preferred_element_type=F32, precision=lax.Precision.HIGHEST)


def _hgrn_gates(i, hq, hf, logits):
    lg = logits
    mx = jnp.maximum(lg[0:1], lg[1:2])
    e0 = jnp.exp(lg[0:1] - mx)
    lb = e0 / (e0 + jnp.exp(lg[1:2] - mx))
    sig = _sigmoid(hf)
    f = lb + (1.0 - lb) * sig
    valid = (i * BLOCK + lax.broadcasted_iota(jnp.int32, hf.shape, 0)) >= PAD
    g = jnp.where(valid, jnp.log(f), 0.0)
    k = jnp.where(valid, 1.0 - f, 0.0)
    sq = _sigmoid(hq)
    return hq * sq, k, g, sig, f, lb, valid, sq


def _hgrn_fwd(proj, logits, gn, n_heads):
    m = proj.shape[0]
    nb = m // BLOCK
    w = n_heads * HEAD
    cpb = BLOCK // CHUNK

    def body(hq_ref, hf_ref, hi_ref, hog_ref, lg_ref, gn_ref, o_ref, ohg_ref, st_all_ref, st_ref, q_s, k_s, v_s, b_s):
        i = pl.program_id(1)

        @pl.when(i == 0)
        def _():
            st_ref[...] = jnp.zeros_like(st_ref)

        q, k, g, _, _, _, _, _ = _hgrn_gates(i, hq_ref[...], hf_ref[...], lg_ref[...])
        q_s[...] = q
        k_s[...] = k
        v_s[...] = hi_ref[...]
        b_s[...] = _dot_f32(_tri(BLOCK, "le").astype(F32), g)
        trow = lax.broadcasted_iota(jnp.int32, (CHUNK, 1), 0)

        def chunk(c, carry):
            sl = pl.ds(pl.multiple_of(c * CHUNK, CHUNK), CHUNK)
            prev = b_s[pl.ds(pl.multiple_of(jnp.maximum(c - 1, 0) * CHUNK, CHUNK), CHUNK), :]
            base = prev[CHUNK - 1:CHUNK, :] * (c > 0).astype(F32)
            b = b_s[sl, :] - base
            qc, kc, vc = q_s[sl, :], k_s[sl, :], v_s[sl, :]
            st = st_ref[...]
            st_all_ref[c] = st.astype(BF16)
            o = lax.dot_general((qc * jnp.exp(b)).astype(BF16), st.astype(BF16), NT, preferred_element_type=F32)
            for s in range(CHUNK):
                e = jnp.exp(jnp.minimum(b - b[s:s + 1, :], 0.0))
                p = jnp.sum(qc * kc[s:s + 1, :] * e, axis=-1, keepdims=True)
                o = o + jnp.where(trow >= s, p, 0.0) * vc[s:s + 1, :]
            o_ref[sl, :] = o
            blast = b[CHUNK - 1:CHUNK, :]
            kd = kc * jnp.exp(blast - b)
            st_ref[...] = st * jnp.exp(blast) + lax.dot_general(vc.astype(BF16), kd.astype(BF16), TN, preferred_element_type=F32)
            return carry

        lax.fori_loop(0, cpb, chunk, 0)
        o = o_ref[...]
        n = o * lax.rsqrt(jnp.mean(o * o, axis=-1, keepdims=True) + RMS_EPS)
        hog = hog_ref[...]
        ohg_ref[...] = (n * gn_ref[...] * hog * _sigmoid(hog)).astype(BF16)

    def col(group):
        return pl.BlockSpec((BLOCK, HEAD), lambda h, i: (i, group * n_heads + h))

    vec = pl.BlockSpec((1, HEAD), lambda h, i: (0, h))
    tile = pl.BlockSpec((BLOCK, HEAD), lambda h, i: (i, h))
    return pl.pallas_call(
        body, name="hgrn_fwd", grid=(n_heads, nb),
        in_specs=[col(0), col(1), col(2), col(3), pl.BlockSpec((2, HEAD), lambda h, i: (0, h)), vec],
        out_specs=[tile, tile, pl.BlockSpec((None, cpb, HEAD, HEAD), lambda h, i: (h, i, 0, 0))],
        out_shape=[_sds((m, w), F32), _sds((m, w), BF16), _sds((n_heads, m // CHUNK, HEAD, HEAD), BF16)],
        scratch_shapes=[pltpu.VMEM((HEAD, HEAD), F32)] + [pltpu.VMEM((BLOCK, HEAD), F32)] * 4,
        compiler_params=_params(2),
    )(proj, proj, proj, proj, logits, gn)


def _hgrn_bwd(proj, logits, gn, o_raw, do_hg, states, n_heads):
    m = proj.shape[0]
    nb = m // BLOCK
    w = n_heads * HEAD
    cpb = BLOCK // CHUNK
    last_state = m // CHUNK - 1

    def body(hq_ref, hf_ref, hi_ref, hog_ref, lg_ref, gn_ref, o_ref, do_ref, st_all_ref, st_next_ref,
             dhq_ref, dhf_ref, dhi_ref, dhog_ref, dgn_ref, dlb_ref,
             dst_ref, q_s, k_s, v_s, b_s, do_s, dq_s, dk_s, dv_s, ex_s):
        step = pl.program_id(1)
        i = nb - 1 - step

        @pl.when(step == 0)
        def _():
            dst_ref[...] = jnp.zeros_like(dst_ref)
            dgn_ref[...] = jnp.zeros_like(dgn_ref)
            dlb_ref[...] = jnp.zeros_like(dlb_ref)

        hq = hq_ref[...]
        q, k, g, sig, f, lb, valid, sq = _hgrn_gates(i, hq, hf_ref[...], lg_ref[...])
        q_s[...] = q
        k_s[...] = k
        v_s[...] = hi_ref[...]
        b_s[...] = _dot_f32(_tri(BLOCK, "le").astype(F32), g)

        o = o_ref[...]
        rs = lax.rsqrt(jnp.mean(o * o, axis=-1, keepdims=True) + RMS_EPS)
        n = o * rs
        hog = hog_ref[...]
        sg = _sigmoid(hog)
        sil = hog * sg
        gnv = gn_ref[...]
        dh = do_ref[...]
        dhog_ref[...] = (dh * n * gnv * sg * (1.0 + hog * (1.0 - sg))).astype(BF16)
        dgn_ref[...] += jnp.sum(dh * n * sil, axis=0, keepdims=True)
        dn = dh * gnv * sil
        do_s[...] = rs * (dn - n * jnp.mean(dn * n, axis=-1, keepdims=True))
        trow = lax.broadcasted_iota(jnp.int32, (CHUNK, 1), 0)

        def chunk(t, st_end):
            c = cpb - 1 - t
            sl = pl.ds(pl.multiple_of(c * CHUNK, CHUNK), CHUNK)
            prev = b_s[pl.ds(pl.multiple_of(jnp.maximum(c - 1, 0) * CHUNK, CHUNK), CHUNK), :]
            base = prev[CHUNK - 1:CHUNK, :] * (c > 0).astype(F32)
            b = b_s[sl, :] - base
            qc, kc, vc, doc = q_s[sl, :], k_s[sl, :], v_s[sl, :], do_s[sl, :]
            eb = jnp.exp(b)
            blast = b[CHUNK - 1:CHUNK, :]
            ek = jnp.exp(blast - b)
            dst = dst_ref[...]
            dstb = dst.astype(BF16)
            docb = doc.astype(BF16)
            st = st_all_ref[c]
            ex_s[sl, :] = jnp.broadcast_to(jnp.sum(st_end.astype(F32) * dst, axis=0, keepdims=True), (CHUNK, HEAD))
            dq = lax.dot_general(docb, st, NN, preferred_element_type=F32) * eb
            dk = lax.dot_general(vc.astype(BF16), dstb, NN, preferred_element_type=F32) * ek
            dv = lax.dot_general((kc * ek).astype(BF16), dstb, NT, preferred_element_type=F32)
            for s in range(CHUNK):
                em = jnp.where(trow >= s, jnp.exp(jnp.minimum(b - b[s:s + 1, :], 0.0)), 0.0)
                ks, vs = kc[s:s + 1, :], vc[s:s + 1, :]
                dp = jnp.sum(doc * vs, axis=-1, keepdims=True)
                qe = qc * em
                p = jnp.sum(qe * ks, axis=-1, keepdims=True)
                dq = dq + dp * ks * em
                dk = dk + jnp.where(trow == s, jnp.sum(dp * qe, axis=0, keepdims=True), 0.0)
                dv = dv + jnp.where(trow == s, jnp.sum(p * doc, axis=0, keepdims=True), 0.0)
            dst_ref[...] = dst * jnp.exp(blast) + lax.dot_general(docb, (qc * eb).astype(BF16), TN, preferred_element_type=F32)
            dq_s[sl, :] = dq
            dk_s[sl, :] = dk
            dv_s[sl, :] = dv
            return st

        lax.fori_loop(0, cpb, chunk, st_next_ref[0])
        dq, dk = dq_s[...], dk_s[...]
        r_i = lax.broadcasted_iota(jnp.int32, (BLOCK, BLOCK), 0)
        c_i = lax.broadcasted_iota(jnp.int32, (BLOCK, BLOCK), 1)
        within = ((c_i >= r_i) & (c_i // CHUNK == r_i // CHUNK)).astype(F32)
        rc = _dot_f32(within, q * dq - k * dk) + ex_s[...]
        df =jnp.where(valid, rc / f - dk, 0.0)
        dhf_ref[...] = (df * (1.0 - lb) * sig * (1.0 - sig)).astype(BF16)
        dlb_ref[...] += jnp.sum(df * (1.0 - sig), axis=0, keepdims=True)
        dhq_ref[...] = (dq * sq * (1.0 + hq * (1.0 - sq))).astype(BF16)
        dhi_ref[...] = dv_s[...].astype(BF16)

    def col(group):
        return pl.BlockSpec((BLOCK, HEAD), lambda h, s: (nb - 1 - s, group * n_heads + h))

    vec = pl.BlockSpec((1, HEAD), lambda h, s: (0, h))
    tile = pl.BlockSpec((BLOCK, HEAD), lambda h, s: (nb - 1 - s, h))
    nxt = pl.BlockSpec((None, 1, HEAD, HEAD), lambda h, s: (h, jnp.minimum((nb - s) * cpb, last_state), 0, 0))
    return pl.pallas_call(
        body, name="hgrn_bwd", grid=(n_heads, nb),
        in_specs=[col(0), col(1), col(2), col(3), pl.BlockSpec((2, HEAD), lambda h, s: (0, h)), vec, tile, tile,
                  pl.BlockSpec((None, cpb, HEAD, HEAD), lambda h, s: (h, nb - 1 - s, 0, 0)), nxt],
        out_specs=[tile, tile, tile, tile, vec, vec],
        out_shape=[_sds((m, w), BF16)] * 4 + [_sds((1, w), F32)] * 2,
        scratch_shapes=[pltpu.VMEM((HEAD, HEAD), F32)] + [pltpu.VMEM((BLOCK, HEAD), F32)] * 9,
        compiler_params=_params(2),
    )(proj, proj, proj, proj, logits, gn, o_raw, do_hg, states, states)


def _split_dot(x, t):
    hi = x.astype(BF16)
    lo = (x - hi.astype(F32)).astype(BF16)
    return jnp.dot(hi, t, preferred_element_type=F32) + jnp.dot(lo, t, preferred_element_type=F32)


def _sb_scores(q, kj, i, j, scale):
    z = lax.dot_general(q, kj, NT, preferred_element_type=F32) * scale
    lp = jnp.log(1.0 + jnp.exp(-jnp.abs(z)))
    lbeta = jnp.minimum(z, 0.0) - lp
    qpos = i * BLOCK + lax.broadcasted_iota(jnp.int32, z.shape, 0)
    kpos = j * BLOCK + lax.broadcasted_iota(jnp.int32, z.shape, 1)
    mask = (kpos < qpos) & (kpos >= PAD)
    l1m = jnp.where(mask, lbeta - z, 0.0)
    return lbeta, l1m, mask


def _sb_fwd(proj, n_heads, group0):
    m = proj.shape[0]
    nb = m // BLOCK
    w = n_heads * HEAD
    scale = 1.0 / math.sqrt(HEAD)

    def body(q_ref, k_ref, v_ref, o_ref, tot_ref):
        i = pl.program_id(1)
        q = q_ref[...].astype(BF16)
        tsuf = _tri(BLOCK, "gt").astype(BF16)

        def step(t, carry):
            acc, run = carry
            j = i - t
            rows = pl.ds(pl.multiple_of(j * BLOCK, BLOCK), BLOCK)
            lbeta, l1m, mask = _sb_scores(q, k_ref[rows, :].astype(BF16), i, j, scale)
            wgt = jnp.where(mask, jnp.exp(lbeta + _split_dot(l1m, tsuf) + run), 0.0)
            acc = acc + jnp.dot(wgt.astype(BF16), v_ref[rows, :].astype(BF16), preferred_element_type=F32)
            return acc, run + jnp.sum(l1m, axis=-1, keepdims=True)

        acc, run = lax.fori_loop(0, i + 1, step, (jnp.zeros((BLOCK, HEAD), F32), jnp.zeros((BLOCK, 1), F32)))
        o_ref[...] = acc.astype(BF16)
        tot_ref[...] = jnp.broadcast_to(run, (BLOCK, HEAD))

    def whole(group):
        return pl.BlockSpec((m, HEAD), lambda h, i: (0, group * n_heads + h))

    return pl.pallas_call(
        body, name="sb_fwd", grid=(n_heads, nb),
        in_specs=[pl.BlockSpec((BLOCK, HEAD), lambda h, i: (i, group0 * n_heads + h)), whole(group0 + 1), whole(group0 + 2)],
        out_specs=[pl.BlockSpec((BLOCK, HEAD), lambda h, i: (i, h)), pl.BlockSpec((None, BLOCK, HEAD), lambda h, i: (h, i, 0))],
        out_shape=[_sds((m, w), BF16), _sds((n_heads, m, HEAD), F32)],
        compiler_params=_params(2),
    )(proj, proj, proj)


def _sb_bwd(proj, do, total, n_heads, group0):
    m = proj.shape[0]
    nb = m // BLOCK
    w = n_heads * HEAD
    scale = 1.0 / math.sqrt(HEAD)

    def body(q_ref, k_ref, v_ref, do_ref, tot_ref, dq_ref, dk_ref, dv_ref, dk_s, dv_s):
        i = pl.program_id(1)

        @pl.when(i == 0)
        def _():
            dk_s[...] = jnp.zeros_like(dk_s)
            dv_s[...] = jnp.zeros_like(dv_s)

        q = q_ref[...].astype(BF16)
        dob = do_ref[...].astype(BF16)
        tot = tot_ref[:, 0:1]
        tsuf = _tri(BLOCK, "gt").astype(BF16)
        tpre = _tri(BLOCK, "lt").astype(BF16)

        def step(j, carry):
            dq, seen, psum = carry
            rows = pl.ds(pl.multiple_of(j * BLOCK, BLOCK), BLOCK)
            kj = k_ref[rows, :].astype(BF16)
            vj = v_ref[rows, :].astype(BF16)
            lbeta, l1m, mask = _sb_scores(q, kj, i, j, scale)
            seen = seen + jnp.sum(l1m, axis=-1, keepdims=True)
            a = jnp.where(mask, jnp.exp(lbeta + _split_dot(l1m, tsuf) + (tot - seen)), 0.0)
            p = a * lax.dot_general(dob, vj, NT, preferred_element_type=F32)
            below = psum + _split_dot(p, tpre)
            beta = jnp.exp(lbeta)
            dz = (jnp.where(mask, p * (1.0 - beta) - below * beta, 0.0) * scale).astype(BF16)
            dq = dq + jnp.dot(dz, kj, preferred_element_type=F32)
            dk_s[rows, :] += lax.dot_general(dz, q, TN, preferred_element_type=F32)
            dv_s[rows, :] += lax.dot_general(a.astype(BF16), dob, TN, preferred_element_type=F32)
            return dq, seen, psum + jnp.sum(p, axis=-1, keepdims=True)

        zero = jnp.zeros((BLOCK, 1), F32)
        dq, _, _ = lax.fori_loop(0, i + 1, step, (jnp.zeros((BLOCK, HEAD), F32), zero, zero))
        dq_ref[...] = dq.astype(BF16)

        @pl.when(i == nb - 1)
        def _():
            dk_ref[...] = dk_s[...].astype(BF16)
            dv_ref[...] = dv_s[...].astype(BF16)

    def whole(group):
        return pl.BlockSpec((m, HEAD), lambda h, i: (0, group * n_heads + h))

    tile = pl.BlockSpec((BLOCK, HEAD), lambda h, i: (i, h))
    col = pl.BlockSpec((m, HEAD), lambda h, i: (0, h))
    return pl.pallas_call(
        body, name="sb_bwd", grid=(n_heads, nb),
        in_specs=[pl.BlockSpec((BLOCK, HEAD), lambda h, i: (i, group0 * n_heads + h)), whole(group0 + 1), whole(group0 + 2),
                  tile, pl.BlockSpec((None, BLOCK, HEAD), lambda h, i: (h, i, 0))],
        out_specs=[tile, col, col],
        out_shape=[_sds((m, w), BF16)] * 3,
        scratch_shapes=[pltpu.VMEM((m, HEAD), F32)] * 2,
        compiler_params=_params(2),
    )(proj, proj, proj, do, total)


def _grad_w_rows(name, x, dy, nd_out):
    m, kx = x.shape
    n = dy.shape[1]
    ks = kx // nd_out
    tm = _tile(m, ROW_TILE)

    def epi(acc, er, orf):
        orf[0][...] = acc[0].astype(BF16)

    return _gemm(name, (nd_out, m // tm),
                 [(x, pl.BlockSpec((tm, ks), lambda j, k: (k, j)), dy, pl.BlockSpec((tm, n), lambda j, k: (k, 0)))],
                 [0], [(ks, n)], TN, [], [(_sds((nd_out, ks, n), BF16), pl.BlockSpec((None, ks, n), lambda j, k: (j, 0, 0)))], epi)[0]


def _local_step(x, target, meta, vec, wts):
    d = x.shape[1]
    nd = wts["w_in"].shape[0]
    width = vec["hg_norm_g"].shape[1]
    n_heads = width // HEAD
    gate_col = 7 * width
    h0 = jnp.concatenate([jnp.zeros((PAD, d), F32), meta, x], axis=0)
    h0b = h0.astype(BF16)
    w_out = wts["w_out"]

    a1, b1, s1 = _ffn_up("ffn1_up", h0b,wts["ffn1_w_gate"], wts["ffn1_w_up"])
    r1, h1, h1b = _residual_ln("ffn1_down", s1, True, wts["ffn1_w_down"], h0, vec["ln1_g"], vec["ln1_b"], 0.5)
    proj = _in_proj(h1b, wts["w_in"])
    o_raw, o_hg, states = _hgrn_fwd(proj, vec["hg_lb_logits"], vec["hg_norm_g"], n_heads)
    o_sb, total = _sb_fwd(proj, n_heads, 4)
    u_hg, u_sb, y = _proj_merge(o_hg, o_sb, wts["w_proj_hg"], wts["w_proj_sb"], proj, vec["b_gate"], gate_col)
    r2, h2, h2b = _residual_ln("out_proj", y, False, w_out.reshape(2, d // 2, d), h1, vec["ln2_g"], vec["ln2_b"], 1.0)
    a2, b2, s2 = _ffn_up("ffn2_up", h2b,wts["ffn2_w_gate"], wts["ffn2_w_up"])
    r3, _, _ = _residual_ln("ffn2_down", s2, True, wts["ffn2_w_down"], h2, vec["ln3_g"], vec["ln3_b"], 0.5)

    dr3, dr3b, dg3, db3, loss = _ln_bwd("ln3_bwd", r3, vec["ln3_g"], 0.5, beta=vec["ln3_b"], target=target, first_row=BLOCK)
    dh2, dwg2, dwu2, dwd2 = _ffn_bwd("ffn2", dr3b, dr3, h2b, a2, b2, s2, wts["ffn2_w_gate"], wts["ffn2_w_up"], wts["ffn2_w_down"])
    dr2, dr2b, dg2, db2 = _ln_bwd("ln2_bwd", r2, vec["ln2_g"], 1.0, dy=dh2)
    du_hg, du_sb, dz_hg, dz_sb, dbg = _merge_bwd(dr2b, w_out.reshape(d, d), proj, vec["b_gate"], u_hg, u_sb, gate_col, d // nd)
    dw_out = _grad_w_rows("dw_out", y, dr2b, nd)
    dp_hg = _grad_w("dp_hg", o_hg, du_hg, nd)
    dp_sb = _grad_w("dp_sb", o_sb, du_sb, nd)
    do_hg = _grad_in("do_hg", du_hg, wts["w_proj_hg"])
    do_sb = _grad_in("do_sb", du_sb, wts["w_proj_sb"])
    dhq, dhf, dhi, dhog, dgn, dlb = _hgrn_bwd(proj, vec["hg_lb_logits"], vec["hg_norm_g"], o_raw, do_hg, states, n_heads)
    dsq, dsk, dsv = _sb_bwd(proj, do_sb, total, n_heads, 4)
    dproj = jnp.concatenate([dhq, dhf, dhi, dhog, dsq, dsk, dsv, dz_hg, dz_sb], axis=1)
    dw_in = _grad_w("dw_in", h1b, dproj, nd)
    dh1 = _grad_in("dh1", dproj, wts["w_in"], add=dr2)
    dr1, dr1b, dg1, db1 = _ln_bwd("ln1_bwd", r1, vec["ln1_g"], 0.5, dy=dh1)
    dh0, dwg1, dwu1, dwd1 = _ffn_bwd("ffn1", dr1b, dr1, h0b, a1, b1, s1, wts["ffn1_w_gate"], wts["ffn1_w_up"], wts["ffn1_w_down"])

    small = {"ln1_g": dg1, "ln1_b": db1, "ln2_g": dg2, "ln2_b": db2, "ln3_g": dg3, "ln3_b": db3,
             "b_gate": dbg, "hg_lb": dlb, "hg_norm_g": dgn}
    big = {"ffn1_w_gate": dwg1, "ffn1_w_up": dwu1, "ffn1_w_down": dwd1, "w_in": dw_in, "w_proj_hg": dp_hg,
           "w_proj_sb": dp_sb, "w_out": dw_out, "ffn2_w_gate": dwg2, "ffn2_w_up": dwu2, "ffn2_w_down": dwd2}
    return loss, dh0[BLOCK:], dh0[PAD:BLOCK], small, big


def _position():
    return lax.axis_index("x"), lax.axis_index("y"), lax.axis_index("c")


def _slot(px, py, pc):
    return 4 * px + 2 * py + pc


ANY = pl.BlockSpec(memory_space=pl.ANY)


def _all_gather(shards):
    n = len(shards)

    def body(*refs):
        ins, outs = refs[:n], refs[n:2 * n]
        send_sems, recv_sems, local_sems = refs[2 * n:]
        x, y, c = _position()
        me, sibling = (x, y, c), (x, y, 1 - c)
        chips = [(1 - x, y), (x, 1 - y), (1 - x, 1 - y)]

        def copy(a, k, block, to, src=None):
            dst = outs[a].at[_slot(*block)]
            return pltpu.make_async_remote_copy(src_ref=dst if src is None else src, dst_ref=dst,
                                                send_sem=send_sems.at[a, k], recv_sem=recv_sems.at[a, k],
                                                device_id=to, device_id_type=MESH)

        mine = [pltpu.make_async_copy(ins[a], outs[a].at[_slot(*me)], local_sems.at[a]) for a in range(n)]
        for cp in mine:
            cp.start()
        first = []
        for a in range(n):
            first.append(copy(a, 0, me, sibling, src=ins[a]))
            first += [copy(a, 1 + j, me, (*chip, c), src=ins[a]) for j, chip in enumerate(chips)]
        for cp in first:
            cp.start()
        passed = []
        for j, chip in enumerate(chips):
            for a in range(n):
                copy(a, 1 + j, (*chip, c), me).wait_recv()
                cp = copy(a, 4 + j, (*chip, c), sibling)
                cp.start()
                passed.append(cp)
        for a in range(n):
            copy(a, 0, sibling, me).wait_recv()
        for j, chip in enumerate(chips):
            for a in range(n):
                copy(a, 4 + j, (*chip, 1 - c), me).wait_recv()
        for cp in first + passed:
            cp.wait_send()
        for cp in mine:
            cp.wait()

    return pl.pallas_call(
        body, name="all_gather", out_shape=[_sds((N_DEV,) + s.shape, s.dtype) for s in shards],
        in_specs=[ANY] * n, out_specs=[ANY] * n,
        scratch_shapes=[pltpu.SemaphoreType.DMA((n, 7)), pltpu.SemaphoreType.DMA((n, 7)), pltpu.SemaphoreType.DMA((n,))],
    )(*shards)


def _exchange(grads):
    n = len(grads)

    def body(*refs):
        ins, outs = refs[:n], refs[n:2 * n]
        send_sems, recv_sems, local_sems = refs[2 * n:]
        x, y, c = _position()
        mine = _slot(x, y, c)
        peers = [(1 - x if k & 4 else x, 1 - y if k & 2 else y, 1 - c if k & 1 else c) for k in range(1, N_DEV)]

        def copy(a, k, peer):
            return pltpu.make_async_remote_copy(src_ref=ins[a].at[_slot(*peer)], dst_ref=outs[a].at[mine],
                                                send_sem=send_sems.at[a, k], recv_sem=recv_sems.at[a, k],
                                                device_id=peer, device_id_type=MESH)

        def landed(a, k, peer):
            return pltpu.make_async_remote_copy(src_ref=ins[a].at[mine], dst_ref=outs[a].at[_slot(*peer)],
                                                send_sem=send_sems.at[a, k], recv_sem=recv_sems.at[a, k],
                                                device_id=peer, device_id_type=MESH)

        own = [pltpu.make_async_copy(ins[a].at[mine], outs[a].at[mine], local_sems.at[a]) for a in range(n)]
        for cp in own:
            cp.start()
        sent = [copy(a, k, peer) for a in range(n) for k, peer in enumerate(peers)]
        for cp in sent:
            cp.start()
        for a in range(n):
            for k, peer in enumerate(peers):
                landed(a, k, peer).wait_recv()
        for cp in sent:
            cp.wait_send()
        for cp in own:
            cp.wait()

    return pl.pallas_call(
        body, name="grad_exchange", out_shape=[_sds(g.shape, g.dtype) for g in grads],
        in_specs=[ANY] * n, out_specs=[ANY] * n,
        scratch_shapes=[pltpu.SemaphoreType.DMA((n, 7)), pltpu.SemaphoreType.DMA((n, 7)), pltpu.SemaphoreType.DMA((n,))],
    )(*grads)


def _all_reduce_rows(v):
    rows = v.shape[0]

    def body(v_ref, out_ref, buf, send_sems, recv_sems):
        x, y, c = _position()
        me, sibling = (x, y, c), (x, y, 1 - c)
        chips = [(1 - x, y), (x, 1 - y), (1 - x, 1 - y)]

        def copy(k, block, to, src=None):
            dst = buf.at[_slot(*block)]
            return pltpu.make_async_remote_copy(src_ref=dst if src is None else src, dst_ref=dst,
                                                send_sem=send_sems.at[k], recv_sem=recv_sems.at[k],
                                                device_id=to, device_id_type=MESH)

        first = [copy(0, me, sibling, src=v_ref)] + [copy(1 + j, me, (*chip, c), src=v_ref) for j, chip in enumerate(chips)]
        for cp in first:
            cp.start()
        buf[_slot(*me)] = v_ref[...]
        passed = [copy(4 + j, (*chip, c), sibling) for j, chip in enumerate(chips)]
        for j, chip in enumerate(chips):
            copy(1 + j, (*chip, c), me).wait_recv()
            passed[j].start()
        copy(0, sibling, me).wait_recv()
        for j, chip in enumerate(chips):
            copy(4 + j, (*chip, 1 - c), me).wait_recv()
        for cp in first + passed:
            cp.wait_send()
        total = buf[0]
        for s in range(1, N_DEV):
            total = total + buf[s]
        out_ref[...] = total

    vmem = pl.BlockSpec(memory_space=pltpu.VMEM)
    return pl.pallas_call(
        body, name="small_all_reduce", out_shape=_sds(v.shape, F32), in_specs=[vmem], out_specs=vmem,
        scratch_shapes=[pltpu.VMEM((N_DEV, rows, 128), F32), pltpu.SemaphoreType.DMA((7,)), pltpu.SemaphoreType.DMA((7,))],
    )(v)


def _adamw(name, w, m, v, contrib):
    r, c = w.shape
    n = contrib.shape[0]
    tr = _tile(r, 256)

    def body(w_ref, m_ref, v_ref, c_ref, g_out, d_out, m_out, v_out):
        g = c_ref[0].astype(F32)
        for s in range(1, n):
            g = g + c_ref[s].astype(F32)
        m2 = ADAM_B1 * m_ref[...] + (1.0 - ADAM_B1) * g
        v2 = ADAM_B2 * v_ref[...] + (1.0 - ADAM_B2) * (g * g)
        m_hat = m2 / (1.0 - ADAM_B1 ** ADAM_STEP)
        v_hat = v2 / (1.0 - ADAM_B2 ** ADAM_STEP)
        g_out[...] = g
        d_out[...] = -ADAM_LR * (m_hat / (jnp.sqrt(v_hat) + ADAM_EPS) + ADAM_WD * w_ref[...])
        m_out[...] = m2
        v_out[...] = v2

    tile = pl.BlockSpec((tr, c), lambda i: (i, 0))
    return pl.pallas_call(
        body, name=name, grid=(r // tr,), in_specs=[tile, tile, tile, pl.BlockSpec((n, tr, c), lambda i: (0, i, 0))],
        out_specs=[tile] * 4, out_shape=[_sds((r, c), F32)] * 4, compiler_params=_params(1),
    )(w, m, v, contrib)


def _lb_logits_grad(logits, dlb):
    def body(lg_ref, d_ref, out_ref):
        lg = lg_ref[...]
        mx = jnp.maximum(lg[0:1], lg[1:2])
        e0 = jnp.exp(lg[0:1] - mx)
        p0 = e0 / (e0 + jnp.exp(lg[1:2] - mx))
        g0 = d_ref[...] * p0 * (1.0 - p0)
        out_ref[0:1, :] = g0
        out_ref[1:2, :] = -g0

    return pl.pallas_call(body, name="lb_logits_grad", out_shape=_sds(logits.shape, F32))(logits, dlb)


BIG = ("ffn1_w_gate", "ffn1_w_up", "ffn1_w_down", "w_in", "w_proj_hg", "w_proj_sb", "w_out",
       "ffn2_w_gate", "ffn2_w_up", "ffn2_w_down")
VECTORS = ("ln1_g", "ln1_b", "b_gate", "hg_lb_logits", "hg_norm_g", "ln2_g", "ln2_b", "ln3_g", "ln3_b")
WEIGHTS = ("meta", "ln1_g", "ln1_b", "ffn1_w_gate", "ffn1_w_up", "ffn1_w_down", "w_in", "b_gate", "hg_lb_logits",
           "hg_norm_g", "w_proj_hg", "w_proj_sb", "w_out", "ln2_g", "ln2_b", "ffn2_w_gate", "ffn2_w_up",
           "ffn2_w_down", "ln3_g", "ln3_b")


def kernel(x, meta, ln1_g, ln1_b, ffn1_w_gate, ffn1_w_up, ffn1_w_down, w_in, b_gate, hg_lb_logits, hg_norm_g, w_proj_hg, w_proj_sb, w_out, ln2_g, ln2_b, ffn2_w_gate, ffn2_w_up, ffn2_w_down, ln3_g, ln3_b, loss_target, m_meta, m_ln1_g, m_ln1_b, m_ffn1_w_gate, m_ffn1_w_up, m_ffn1_w_down, m_w_in, m_b_gate, m_hg_lb_logits, m_hg_norm_g, m_w_proj_hg, m_w_proj_sb, m_w_out, m_ln2_g, m_ln2_b, m_ffn2_w_gate, m_ffn2_w_up, m_ffn2_w_down, m_ln3_g, m_ln3_b, v_meta, v_ln1_g, v_ln1_b, v_ffn1_w_gate, v_ffn1_w_up, v_ffn1_w_down, v_w_in, v_b_gate, v_hg_lb_logits, v_hg_norm_g, v_w_proj_hg, v_w_proj_sb, v_w_out, v_ln2_g, v_ln2_b, v_ffn2_w_gate, v_ffn2_w_up, v_ffn2_w_down, v_ln3_g, v_ln3_b):
    given = dict(locals())
    d = x.shape[-1]
    ds = meta.shape[1]

    gathered = _all_gather([meta] + [given[k][0].astype(BF16) for k in BIG])
    meta_full = gathered[0].transpose(1, 0, 2).reshape(N_META, d)
    wts = dict(zip(BIG, gathered[1:]))
    vec = {k: given[k] for k in VECTORS}

    loss, grad_x, dmeta, small, big = _local_step(x[0], loss_target[0], meta_full, vec, wts)

    received = dict(zip(BIG, _exchange([big[k] for k in BIG])))
    order = ("ln1_g", "ln1_b", "ln2_g", "ln2_b", "ln3_g", "ln3_b", "b_gate", "hg_lb", "hg_norm_g")
    parts = [small[k].reshape(-1, 128) for k in order] + [dmeta.reshape(-1, 128), jnp.broadcast_to(loss, (8, 128))]
    total = _all_reduce_rows(jnp.concatenate(parts, axis=0))
    reduced, row = {}, 0
    for k, p in zip(order + ("meta", "loss"), parts):
        reduced[k] = total[row:row + p.shape[0]]
        row += p.shape[0]
    loss_out = reduced["loss"][0, 0]
    me = _slot(*_position())
    dmeta_mine = lax.dynamic_slice(reduced["meta"].reshape(N_META, d), (0, me * ds), (N_META, ds))
    dlogits = _lb_logits_grad(hg_lb_logits, reduced["hg_lb"].reshape(1, -1))

    grads, deltas, new_m, new_v = {}, {}, {}, {}
    for k in WEIGHTS:
        w = given[k]
        lead = w.shape[:-2]
        w2, m2, v2 = (a.reshape(a.shape[-2:]) for a in (w, given["m_" + k], given["v_" + k]))
        if k in BIG:
            contrib = received[k]
        elif k == "meta":
            contrib = dmeta_mine[None]
        elif k == "hg_lb_logits":
            contrib = dlogits[None]
        else:
            contrib = reduced[k].reshape((1,) + w2.shape)
        out = _adamw("adamw_" + k, w2, m2, v2, contrib)
        grads[k], deltas[k], new_m[k], new_v[k] = (o.reshape(lead + o.shape) for o in out)
    return (loss_out, grad_x[None], *[grads[k] for k in WEIGHTS], *[deltas[k] for k in WEIGHTS],
            *[new_m[k] for k in WEIGHTS], *[new_v[k] for k in WEIGHTS])
```

```python
import functools
import math

import jax
import jax.numpy as jnp
from jax import lax
from jax.experimental import pallas as pl
from jax.experimental.pallas import tpu as pltpu

F32 = jnp.float32
BF16 = jnp.bfloat16
MESH = pl.DeviceIdType.MESH

N_DEV = 8
N_META = 16
BLOCK = 128
PAD = BLOCK - N_META
HEAD = 128
CHUNK = 16
LN_EPS = 1e-5
RMS_EPS = 1e-6
DN_ALPHA = 2.0 ** 0.25
ADAM_LR, ADAM_B1, ADAM_B2, ADAM_EPS, ADAM_WD, ADAM_STEP = 0.001, 0.9, 0.999, 1e-08, 0.01, 10

VMEM_LIMIT_V7X = 60 * 1024 * 1024
ROW_TILE = 640
LN_ROW_TILE = 320

NN = (((1,), (0,)), ((), ()))
NT = (((1,), (1,)), ((), ()))
TN = (((0,), (0,)), ((), ()))


def _tile(n, pref, mult=16):
    best = None
    for t in range(mult, min(n, pref) + 1, mult):
        if n % t == 0:
            best = t
    return n if best is None else best


def _params(n_axes):
    return pltpu.CompilerParams(dimension_semantics=("arbitrary",) * n_axes, vmem_limit_bytes=VMEM_LIMIT_V7X)


def _sigmoid(x):
    return 1.0 / (1.0 + jnp.exp(-x))


def _gemm(name, grid, pairs, acc_of, acc_shapes, dims, extras, outs, epilogue):
    n_pairs, n_extra, n_out = len(pairs), len(extras), len(outs)
    nk = grid[-1]
    k_axis = len(grid) - 1

    def body(*refs):
        pr = refs[:2 * n_pairs]
        er = refs[2 * n_pairs:2 * n_pairs + n_extra]
        orf = refs[2 * n_pairs + n_extra:2 * n_pairs + n_extra + n_out]
        accs = refs[2 * n_pairs + n_extra + n_out:]

        def part(p):
            return lax.dot_general(pr[2 * p][...].astype(BF16), pr[2 * p + 1][...].astype(BF16), dims,
                                   preferred_element_type=F32)

        if nk == 1:
            vals = [None] * len(acc_shapes)
            for p in range(n_pairs):
                d = part(p)
                vals[acc_of[p]] = d if vals[acc_of[p]] is None else vals[acc_of[p]] + d
            epilogue(vals, er, orf)
        else:
            k = pl.program_id(k_axis)

            @pl.when(k == 0)
            def _():
                for acc in accs:
                    acc[...] = jnp.zeros_like(acc)

            for p in range(n_pairs):
                accs[acc_of[p]][...] += part(p)

            @pl.when(k == nk - 1)
            def _():
                epilogue([acc[...] for acc in accs], er, orf)

    operands, in_specs = [], []
    for a, a_spec, b, b_spec in pairs:
        operands += [a, b]
        in_specs += [a_spec, b_spec]
    for e, e_spec in extras:
        operands.append(e)
        in_specs.append(e_spec)
    scratch = [] if nk == 1 else [pltpu.VMEM(s, F32) for s in acc_shapes]
    res = pl.pallas_call(
        body, name=name, grid=grid, in_specs=in_specs,
        out_specs=[s for _, s in outs], out_shape=[o for o, _ in outs],
        scratch_shapes=scratch, compiler_params=_params(len(grid)),
    )(*operands)
    return res


def _sds(shape, dtype):
    return jax.ShapeDtypeStruct(shape, dtype)


def _ln_rows(r, g, b):
    mu = jnp.mean(r, axis=-1, keepdims=True)
    xc = r - mu
    var = jnp.mean(xc * xc, axis=-1, keepdims=True)
    return xc * lax.rsqrt(var + LN_EPS) * g + b


def _ffn_up(name, hb, wg, wu):
    m, d = hb.shape
    nd, _, fs = wg.shape
    tm = _tile(m, ROW_TILE)

    def epi(acc, er, orf):
        a, b = acc
        orf[0][...] = a
        orf[1][...] = b
        orf[2][...] = (a * _sigmoid(a) * b).astype(BF16)

    h_spec = pl.BlockSpec((tm, d), lambda i, j, k: (i, 0))
    w_spec = pl.BlockSpec((None, d, fs), lambda i, j, k: (j, 0, 0))
    o_spec = pl.BlockSpec((None, tm, fs), lambda i, j, k: (j, i, 0))
    return _gemm(name,(m // tm, nd, 1), [(hb, h_spec, wg, w_spec), (hb, h_spec, wu, w_spec)], [0, 1],
                 [(tm, fs)] * 2, NN, [],
                 [(_sds((nd, m, fs), F32), o_spec), (_sds((nd, m, fs), F32), o_spec), (_sds((nd, m, fs), BF16), o_spec)], epi)


def _residual_ln(name, a, a_stacked, w, h_in, g, beta, scale):
    nk, tk, d = w.shape
    m = h_in.shape[0]
    tm = _tile(m, LN_ROW_TILE)

    def epi(acc, er, orf):
        r = DN_ALPHA * er[0][...] + scale * acc[0]
        h = _ln_rows(r, er[1][...], er[2][...])
        orf[0][...] = r
        orf[1][...] = h
        orf[2][...] = h.astype(BF16)

    if a_stacked:
        a_spec = pl.BlockSpec((None, tm, tk), lambda i, k: (k, i, 0))
    else:
        a_spec = pl.BlockSpec((tm, tk), lambda i, k: (i, k))
    w_spec = pl.BlockSpec((None, tk, d), lambda i, k: (k, 0, 0))
    row = pl.BlockSpec((tm, d), lambda i, k: (i, 0))
    vec = pl.BlockSpec((1, d), lambda i, k: (0, 0))
    return _gemm(name, (m // tm, nk), [(a, a_spec, w, w_spec)], [0], [(tm, d)], NN,
                 [(h_in, row), (g, vec), (beta, vec)],
                 [(_sds((m, d), F32), row), (_sds((m, d), F32), row), (_sds((m, d), BF16), row)], epi)


def _in_proj(hb, w_in):
    m, d = hb.shape
    nd, _, cs = w_in.shape
    tm = _tile(m, ROW_TILE)

    def epi(acc, er, orf):
        orf[0][...] = acc[0]

    return _gemm("in_proj", (m // tm, nd, 1),
                 [(hb, pl.BlockSpec((tm, d), lambda i, j, k: (i, 0)), w_in, pl.BlockSpec((None, d, cs), lambda i, j, k: (j, 0, 0)))],
                 [0], [(tm, cs)], NN, [], [(_sds((m, nd * cs), F32), pl.BlockSpec((tm, cs), lambda i, j, k: (i, j)))], epi)[0]


def _proj_merge(o_hg, o_sb, p_hg, p_sb, proj, b_gate, gate_col):
    m, w = o_hg.shape
    nd, _, ds = p_hg.shape
    d = nd * ds
    tm = _tile(m, ROW_TILE)
    c0 = gate_col // ds

    def epi(acc, er, orf):
        u_hg, u_sb = acc
        g_hg = _sigmoid(er[0][...] + er[2][...])
        g_sb = _sigmoid(er[1][...] + er[3][...])
        orf[0][...] = u_hg
        orf[1][...] = u_sb
        orf[2][...] = (g_hg * u_hg + g_sb * u_sb).astype(BF16)

    o_spec = pl.BlockSpec((tm, w), lambda i, j, k: (i, 0))
    p_spec = pl.BlockSpec((None, w, ds), lambda i, j, k: (j, 0, 0))
    out = pl.BlockSpec((tm, ds), lambda i, j, k: (i, j))
    return _gemm("proj_merge", (m // tm, nd, 1), [(o_hg, o_spec, p_hg, p_spec), (o_sb, o_spec, p_sb, p_spec)], [0, 1],
                 [(tm, ds)] * 2, NN,
                 [(proj, pl.BlockSpec((tm, ds), lambda i, j, k: (i, c0 + j))),
                  (proj, pl.BlockSpec((tm, ds), lambda i, j, k: (i, c0 + nd + j))),
                  (b_gate, pl.BlockSpec((1, ds), lambda i, j, k: (0, j))),
                  (b_gate, pl.BlockSpec((1, ds), lambda i, j, k: (0, nd + j)))],
                 [(_sds((m, d), F32), out), (_sds((m, d), F32), out), (_sds((m, d), BF16), out)], epi)


def _ln_bwd(name, r, g, out_scale, dy=None, beta=None, target=None, first_row=0):
    m, d = r.shape
    tm = _tile(m, LN_ROW_TILE if target is None else BLOCK)
    with_loss = target is not None
    skip = first_row // tm if with_loss else 0
    assert not with_loss or first_row % tm == 0

    def body(*refs):
        if with_loss:
            r_ref, g_ref, b_ref, t_ref, dr_ref, drb_ref, dg_ref, db_ref, loss_ref = refs
        else:
            r_ref, g_ref, dy_ref, dr_ref, drb_ref, dg_ref, db_ref = refs
        i = pl.program_id(0)
        x = r_ref[...]
        mu = jnp.mean(x, axis=-1, keepdims=True)
        xc = x - mu
        var = jnp.mean(xc * xc, axis=-1, keepdims=True)
        rstd = lax.rsqrt(var + LN_EPS)
        xhat = xc * rstd
        gv = g_ref[...]
        if with_loss:
            err = xhat * gv + b_ref[...] - t_ref[...]
            live = (i >= skip).astype(F32)
            dyv = err * (live / d)
            part = 0.5 * live * jnp.sum(jnp.sum(err * err, axis=-1, keepdims=True), axis=0, keepdims=True) / d
        else:
            dyv = dy_ref[...]
        dxh = dyv * gv
        m1 = jnp.mean(dxh, axis=-1, keepdims=True)
        m2 = jnp.mean(dxh * xhat, axis=-1, keepdims=True)
        dr = rstd * (dxh - m1 - xhat * m2)
        dr_ref[...] = dr
        drb_ref[...] = (out_scale * dr).astype(BF16)

        @pl.when(i == 0)
        def _():
            dg_ref[...] = jnp.zeros_like(dg_ref)
            db_ref[...] = jnp.zeros_like(db_ref)
            if with_loss:
                loss_ref[...] = jnp.zeros_like(loss_ref)

        dg_ref[...] += jnp.sum(dyv * xhat, axis=0, keepdims=True)
        db_ref[...] += jnp.sum(dyv, axis=0, keepdims=True)
        if with_loss:
            loss_ref[...] += jnp.broadcast_to(part, loss_ref.shape)

    row = pl.BlockSpec((tm, d), lambda i: (i, 0))
    vec = pl.BlockSpec((1, d), lambda i: (0, 0))
    out_shape = [_sds((m, d), F32), _sds((m, d), BF16), _sds((1, d), F32), _sds((1, d), F32)]
    out_specs = [row, row, vec, vec]
    if with_loss:
        operands = [r, g, beta, target]
        in_specs = [row, vec, vec, pl.BlockSpec((tm, d), lambda i: (jnp.maximum(i - skip, 0), 0))]
        out_shape.append(_sds((1, BLOCK), F32))
        out_specs.append(pl.BlockSpec((1, BLOCK), lambda i: (0, 0)))
    else:
        operands = [r, g, dy]
        in_specs = [row, vec, row]
    return pl.pallas_call(body, name=name, grid=(m // tm,), in_specs=in_specs, out_specs=out_specs, out_shape=out_shape,
                          compiler_params=_params(1))(*operands)


def _ffn_bwd(tag, drb, dr, hb, a, b, s, wg, wu, wd):
    m, d = drb.shape
    nd, _, fs = wg.shape
    tm = _tile(m, ROW_TILE)

    def epi_ds(acc, er, orf):
        ds = acc[0]
        av, bv = er[0][...], er[1][...]
        sg = _sigmoid(av)
        orf[0][...] = (ds * bv * sg * (1.0 + av * (1.0 - sg))).astype(BF16)
        orf[1][...] = (ds * av * sg).astype(BF16)

    st = pl.BlockSpec((None, tm, fs), lambda i, j, k: (j, i, 0))
    da, db = _gemm(tag + "_ds", (m // tm, nd, 1),
                   [(drb, pl.BlockSpec((tm, d), lambda i, j, k: (i, 0)), wd, pl.BlockSpec((None, fs, d), lambda i, j, k: (j, 0, 0)))],
                   [0], [(tm, fs)], NT, [(a, st), (b, st)],
                   [(_sds((nd, m, fs), BF16), st), (_sds((nd, m, fs), BF16), st)], epi_ds)

    def epi_w(acc, er, orf):
        for o, v in zip(orf, acc):
            o[...] = v.astype(BF16)

    nkm = m // tm
    dwd = _gemm(tag + "_dwd", (nd, nkm),
                [(s, pl.BlockSpec((None, tm, fs), lambda j, k: (j, k, 0)), drb, pl.BlockSpec((tm, d), lambda j, k: (k, 0)))],
                [0], [(fs, d)], TN, [], [(_sds((nd, fs, d), BF16), pl.BlockSpec((None, fs, d), lambda j, k: (j, 0, 0)))], epi_w)[0]
    h_spec = pl.BlockSpec((tm, d), lambda j, k: (k, 0))
    g_spec = pl.BlockSpec((None, tm, fs), lambda j, k: (j, k, 0))
    w_out = pl.BlockSpec((None, d, fs), lambda j, k: (j, 0, 0))
    dwg, dwu = _gemm(tag + "_dwgu", (nd, nkm), [(hb, h_spec, da, g_spec), (hb, h_spec, db, g_spec)], [0, 1],
                     [(d, fs)] * 2, TN, [], [(_sds((nd, d, fs), BF16), w_out), (_sds((nd, d, fs), BF16), w_out)], epi_w)

    def epi_dh(acc, er, orf):
        orf[0][...] = DN_ALPHA * er[0][...] + acc[0]

    gk = pl.BlockSpec((None, tm, fs), lambda i, k: (k, i, 0))
    wk = pl.BlockSpec((None, d, fs), lambda i, k: (k, 0, 0))
    row = pl.BlockSpec((tm, d), lambda i, k: (i, 0))
    dh = _gemm(tag + "_dh", (m // tm, nd), [(da, gk, wg, wk), (db, gk, wu, wk)], [0, 0], [(tm, d)], NT,
               [(dr, row)], [(_sds((m, d), F32), row)], epi_dh)[0]
    return dh, dwg, dwu, dwd


def _merge_bwd(dmixb, w_out2, proj, b_gate, u_hg, u_sb, gate_col, ds):
    m, d = dmixb.shape
    nd = d // ds
    tm = _tile(m, ROW_TILE)
    c0 = gate_col // ds

    def epi(acc, er, orf):
        i = pl.program_id(1)
        dy = acc[0]
        g_hg = _sigmoid(er[0][...] + er[2][...])
        g_sb = _sigmoid(er[1][...] + er[3][...])
        orf[0][...] = (dy * g_hg).astype(BF16)
        orf[1][...] = (dy * g_sb).astype(BF16)
        dz_hg = dy * er[4][...] * g_hg * (1.0 - g_hg)
        dz_sb = dy * er[5][...] * g_sb * (1.0 - g_sb)
        orf[2][...] = dz_hg.astype(BF16)
        orf[3][...] = dz_sb.astype(BF16)

        @pl.when(i == 0)
        def _():
            orf[4][...] = jnp.zeros_like(orf[4])
            orf[5][...] = jnp.zeros_like(orf[5])

        orf[4][...] += jnp.sum(dz_hg, axis=0, keepdims=True)
        orf[5][...] += jnp.sum(dz_sb, axis=0, keepdims=True)

    tile = pl.BlockSpec((tm, ds), lambda j, i, k: (i, j))
    vec = pl.BlockSpec((1, ds), lambda j, i, k: (0, j))
    du_hg, du_sb, dz_hg, dz_sb, db_hg, db_sb = _gemm(
        "merge_bwd", (nd, m // tm, 1),
        [(dmixb, pl.BlockSpec((tm, d), lambda j, i, k: (i, 0)), w_out2, pl.BlockSpec((ds, d), lambda j, i, k: (j, 0)))],
        [0], [(tm, ds)], NT,
        [(proj, pl.BlockSpec((tm, ds), lambda j, i, k: (i, c0 + j))),
         (proj, pl.BlockSpec((tm, ds), lambda j, i, k: (i, c0 + nd + j))),
         (b_gate, vec), (b_gate, pl.BlockSpec((1, ds), lambda j, i, k: (0, nd + j))),
         (u_hg, tile), (u_sb, tile)],
        [(_sds((m, d), BF16), tile), (_sds((m, d), BF16), tile), (_sds((m, d), BF16), tile), (_sds((m, d), BF16), tile),
         (_sds((1, d), F32), vec), (_sds((1, d), F32), vec)], epi)
    return du_hg, du_sb, dz_hg, dz_sb, jnp.concatenate([db_hg, db_sb], axis=1)


def _grad_w(name, x, dy, nd_out):
    m, kx = x.shape
    n = dy.shape[1]
    ns = n // nd_out
    tm = _tile(m, ROW_TILE)

    def epi(acc, er, orf):
        orf[0][...] = acc[0].astype(BF16)

    return _gemm(name, (nd_out, m // tm),
                 [(x, pl.BlockSpec((tm, kx), lambda j, k: (k, 0)), dy, pl.BlockSpec((tm, ns), lambda j, k: (k, j)))],
                 [0], [(kx, ns)], TN, [], [(_sds((nd_out, kx, ns), BF16), pl.BlockSpec((None, kx, ns), lambda j, k: (j, 0, 0)))], epi)[0]


def _grad_in(name, dy, w, add=None):
    m = dy.shape[0]
    nd, kx, ns = w.shape
    tm = _tile(m, ROW_TILE)

    def epi(acc, er, orf):
        orf[0][...] = acc[0] if add is None else DN_ALPHA * er[0][...] + acc[0]

    row = pl.BlockSpec((tm, kx), lambda i, k: (i, 0))
    return _gemm(name, (m // tm, nd),
                 [(dy, pl.BlockSpec((tm, ns), lambda i, k: (i, k)), w, pl.BlockSpec((None, kx, ns), lambda i, k: (k, 0, 0)))],
                 [0], [(tm, kx)], NT, [] if add is None else [(add, row)], [(_sds((m, kx), F32), row)], epi)[0]


def _tri(n, kind):
    r = lax.broadcasted_iota(jnp.int32, (n, n), 0)
    c = lax.broadcasted_iota(jnp.int32, (n, n), 1)
    return {"le": c <= r, "ge": c >= r, "gt": r > c, "lt": r < c}[kind]


def _dot_f32(a, b, dims=NN):
    return lax.dot_general(a, b, dims, preferred_element_type=F32, precision=lax.Precision.HIGHEST)


def _hgrn_gates(i, hq, hf, logits):
    lg = logits
    mx = jnp.maximum(lg[0:1], lg[1:2])
    e0 = jnp.exp(lg[0:1] - mx)
    lb = e0 / (e0 + jnp.exp(lg[1:2] - mx))
    sig = _sigmoid(hf)
    f = lb + (1.0 - lb) * sig
    valid = (i * BLOCK + lax.broadcasted_iota(jnp.int32, hf.shape, 0)) >= PAD
    g = jnp.where(valid, jnp.log(f), 0.0)
    k = jnp.where(valid, 1.0 - f, 0.0)
    sq = _sigmoid(hq)
    return hq * sq, k, g, sig, f, lb, valid, sq


def _hgrn_fwd(proj, logits, gn, n_heads):
    m = proj.shape[0]
    nb = m // BLOCK
    w = n_heads * HEAD
    cpb = BLOCK // CHUNK

    def body(hq_ref, hf_ref, hi_ref, hog_ref, lg_ref, gn_ref, o_ref, ohg_ref, st_all_ref, st_ref, q_s, k_s, v_s, b_s):
        i = pl.program_id(1)

        @pl.when(i == 0)
        def _():
            st_ref[...] = jnp.zeros_like(st_ref)

        q, k, g, _, _, _, _, _ = _hgrn_gates(i, hq_ref[...], hf_ref[...], lg_ref[...])
        q_s[...] = q
        k_s[...] = k
        v_s[...] = hi_ref[...]
        b_s[...] = _dot_f32(_tri(BLOCK, "le").astype(F32), g)
        trow = lax.broadcasted_iota(jnp.int32, (CHUNK, 1), 0)

        def chunk(c, carry):
            sl = pl.ds(pl.multiple_of(c * CHUNK, CHUNK), CHUNK)
            prev = b_s[pl.ds(pl.multiple_of(jnp.maximum(c - 1, 0) * CHUNK, CHUNK), CHUNK), :]
            base = prev[CHUNK - 1:CHUNK, :] * (c > 0).astype(F32)
            b = b_s[sl, :] - base
            qc, kc, vc = q_s[sl, :], k_s[sl, :], v_s[sl, :]
            st = st_ref[...]
            st_all_ref[c] = st.astype(BF16)
            o = lax.dot_general((qc * jnp.exp(b)).astype(BF16), st.astype(BF16), NT, preferred_element_type=F32)
            for s in range(CHUNK):
                e = jnp.exp(jnp.minimum(b - b[s:s + 1, :], 0.0))
                p = jnp.sum(qc * kc[s:s + 1, :] * e, axis=-1, keepdims=True)
                o = o + jnp.where(trow >= s, p, 0.0) * vc[s:s + 1, :]
            o_ref[sl, :] = o
            blast = b[CHUNK - 1:CHUNK, :]
            kd = kc * jnp.exp(blast - b)
            st_ref[...] = st * jnp.exp(blast) + lax.dot_general(vc.astype(BF16), kd.astype(BF16), TN, preferred_element_type=F32)
            return carry

        lax.fori_loop(0, cpb, chunk, 0)
        o = o_ref[...]
        n = o * lax.rsqrt(jnp.mean(o * o, axis=-1, keepdims=True) + RMS_EPS)
        hog = hog_ref[...]
        ohg_ref[...] = (n * gn_ref[...] * hog * _sigmoid(hog)).astype(BF16)

    def col(group):
        return pl.BlockSpec((BLOCK, HEAD), lambda h, i: (i, group * n_heads + h))

    vec = pl.BlockSpec((1, HEAD), lambda h, i: (0, h))
    tile = pl.BlockSpec((BLOCK, HEAD), lambda h, i: (i, h))
    return pl.pallas_call(
        body, name="hgrn_fwd", grid=(n_heads, nb),
        in_specs=[col(0), col(1), col(2), col(3), pl.BlockSpec((2, HEAD), lambda h, i: (0, h)), vec],
        out_specs=[tile, tile, pl.BlockSpec((None, cpb, HEAD, HEAD), lambda h, i: (h, i, 0, 0))],
        out_shape=[_sds((m, w), F32), _sds((m, w), BF16), _sds((n_heads, m // CHUNK, HEAD, HEAD), BF16)],
        scratch_shapes=[pltpu.VMEM((HEAD, HEAD), F32)] + [pltpu.VMEM((BLOCK, HEAD), F32)] * 4,
        compiler_params=_params(2),
    )(proj, proj, proj, proj, logits, gn)


def _hgrn_bwd(proj, logits, gn, o_raw, do_hg, states, n_heads):
    m = proj.shape[0]
    nb = m // BLOCK
    w = n_heads * HEAD
    cpb = BLOCK // CHUNK
    last_state = m // CHUNK - 1

    def body(hq_ref, hf_ref, hi_ref, hog_ref, lg_ref, gn_ref, o_ref, do_ref, st_all_ref, st_next_ref,
             dhq_ref, dhf_ref, dhi_ref, dhog_ref, dgn_ref, dlb_ref,
             dst_ref, q_s, k_s, v_s, b_s, do_s, dq_s, dk_s, dv_s, ex_s):
        step = pl.program_id(1)
        i = nb - 1 - step

        @pl.when(step == 0)
        def _():
            dst_ref[...] = jnp.zeros_like(dst_ref)
            dgn_ref[...] = jnp.zeros_like(dgn_ref)
            dlb_ref[...] = jnp.zeros_like(dlb_ref)

        hq = hq_ref[...]
        q, k, g, sig, f, lb, valid, sq = _hgrn_gates(i, hq, hf_ref[...], lg_ref[...])
        q_s[...] = q
        k_s[...] = k
        v_s[...] = hi_ref[...]
        b_s[...] = _dot_f32(_tri(BLOCK, "le").astype(F32), g)

        o = o_ref[...]
        rs = lax.rsqrt(jnp.mean(o * o, axis=-1, keepdims=True) + RMS_EPS)
        n = o * rs
        hog = hog_ref[...]
        sg = _sigmoid(hog)
        sil = hog * sg
        gnv = gn_ref[...]
        dh = do_ref[...]
        dhog_ref[...] = (dh * n * gnv * sg * (1.0 + hog * (1.0 - sg))).astype(BF16)
        dgn_ref[...] += jnp.sum(dh * n * sil, axis=0, keepdims=True)
        dn = dh * gnv * sil
        do_s[...] = rs * (dn - n * jnp.mean(dn * n, axis=-1, keepdims=True))
        trow = lax.broadcasted_iota(jnp.int32, (CHUNK, 1), 0)

        def chunk(t, st_end):
            c = cpb - 1 - t
            sl = pl.ds(pl.multiple_of(c * CHUNK, CHUNK), CHUNK)
            prev = b_s[pl.ds(pl.multiple_of(jnp.maximum(c - 1, 0) * CHUNK, CHUNK), CHUNK), :]
            base = prev[CHUNK - 1:CHUNK, :] * (c > 0).astype(F32)
            b = b_s[sl, :] - base
            qc, kc, vc, doc = q_s[sl, :], k_s[sl, :], v_s[sl, :], do_s[sl, :]
            eb = jnp.exp(b)
            blast = b[CHUNK - 1:CHUNK, :]
            ek = jnp.exp(blast - b)
            dst = dst_ref[...]
            dstb = dst.astype(BF16)
            docb = doc.astype(BF16)
            st = st_all_ref[c]
            ex_s[sl, :] = jnp.broadcast_to(jnp.sum(st_end.astype(F32) * dst, axis=0, keepdims=True), (CHUNK, HEAD))
            dq = lax.dot_general(docb, st, NN, preferred_element_type=F32) * eb
            dk = lax.dot_general(vc.astype(BF16), dstb, NN, preferred_element_type=F32) * ek
            dv = lax.dot_general((kc * ek).astype(BF16), dstb, NT, preferred_element_type=F32)
            for s in range(CHUNK):
                em = jnp.where(trow >= s, jnp.exp(jnp.minimum(b - b[s:s + 1, :], 0.0)), 0.0)
                ks, vs = kc[s:s + 1, :], vc[s:s + 1, :]
                dp = jnp.sum(doc * vs, axis=-1, keepdims=True)
                qe = qc * em
                p = jnp.sum(qe * ks, axis=-1, keepdims=True)
                dq = dq + dp * ks * em
                dk = dk + jnp.where(trow == s, jnp.sum(dp * qe, axis=0, keepdims=True), 0.0)
                dv = dv + jnp.where(trow == s, jnp.sum(p * doc, axis=0, keepdims=True), 0.0)
            dst_ref[...] = dst * jnp.exp(blast) + lax.dot_general(docb, (qc * eb).astype(BF16), TN, preferred_element_type=F32)
            dq_s[sl, :] = dq
            dk_s[sl, :] = dk
            dv_s[sl, :] = dv
            return st

        lax.fori_loop(0, cpb, chunk, st_next_ref[0])
        dq, dk = dq_s[...], dk_s[...]
        r_i = lax.broadcasted_iota(jnp.int32, (BLOCK, BLOCK), 0)
        c_i = lax.broadcasted_iota(jnp.int32, (BLOCK, BLOCK), 1)
        within = ((c_i >= r_i) & (c_i // CHUNK == r_i // CHUNK)).astype(F32)
        rc = _dot_f32(within, q * dq - k * dk) + ex_s[...]
        df =jnp.where(valid, rc / f - dk, 0.0)
        dhf_ref[...] = (df * (1.0 - lb) * sig * (1.0 - sig)).astype(BF16)
        dlb_ref[...] += jnp.sum(df * (1.0 - sig), axis=0, keepdims=True)
        dhq_ref[...] = (dq * sq * (1.0 + hq * (1.0 - sq))).astype(BF16)
        dhi_ref[...] = dv_s[...].astype(BF16)

    def col(group):
        return pl.BlockSpec((BLOCK, HEAD), lambda h, s: (nb - 1 - s, group * n_heads + h))

    vec = pl.BlockSpec((1, HEAD), lambda h, s: (0, h))
    tile = pl.BlockSpec((BLOCK, HEAD), lambda h, s: (nb - 1 - s, h))
    nxt = pl.BlockSpec((None, 1, HEAD, HEAD), lambda h, s: (h, jnp.minimum((nb - s) * cpb, last_state), 0, 0))
    return pl.pallas_call(
        body, name="hgrn_bwd", grid=(n_heads, nb),
        in_specs=[col(0), col(1), col(2), col(3), pl.BlockSpec((2, HEAD), lambda h, s: (0, h)), vec, tile, tile,
                  pl.BlockSpec((None, cpb, HEAD, HEAD), lambda h, s: (h, nb - 1 - s, 0, 0)), nxt],
        out_specs=[tile, tile, tile, tile, vec, vec],
        out_shape=[_sds((m, w), BF16)] * 4 + [_sds((1, w), F32)] * 2,
        scratch_shapes=[pltpu.VMEM((HEAD, HEAD), F32)] + [pltpu.VMEM((BLOCK, HEAD), F32)] * 9,
        compiler_params=_params(2),
    )(proj, proj, proj, proj, logits, gn, o_raw, do_hg, states, states)


def _split_dot(x, t):
    hi = x.astype(BF16)
    lo = (x - hi.astype(F32)).astype(BF16)
    return jnp.dot(hi, t, preferred_element_type=F32) + jnp.dot(lo, t, preferred_element_type=F32)


def _sb_scores(q, kj, i, j, scale):
    z = lax.dot_general(q, kj, NT, preferred_element_type=F32) * scale
    lp = jnp.log(1.0 + jnp.exp(-jnp.abs(z)))
    lbeta = jnp.minimum(z, 0.0) - lp
    qpos = i * BLOCK + lax.broadcasted_iota(jnp.int32, z.shape, 0)
    kpos = j * BLOCK + lax.broadcasted_iota(jnp.int32, z.shape, 1)
    mask = (kpos < qpos) & (kpos >= PAD)
    l1m = jnp.where(mask, lbeta - z, 0.0)
    return lbeta, l1m, mask


SB_DEAD = -104.0


def _sb_fwd(proj, n_heads, group0):
    m = proj.shape[0]
    nb = m // BLOCK
    w = n_heads * HEAD
    scale = 1.0 / math.sqrt(HEAD)

    def body(q_ref, k_ref, v_ref, o_ref, start_ref, count_ref):
        h, i = pl.program_id(0), pl.program_id(1)
        q = q_ref[...].astype(BF16)
        tsuf = _tri(BLOCK, "gt").astype(BF16)

        def live(carry):
            t, _, run = carry
            return (t <= i) & (jnp.max(run) > SB_DEAD)

        def step(carry):
            t, acc, run = carry
            j = i - t
            rows = pl.ds(pl.multiple_of(j * BLOCK, BLOCK), BLOCK)
            start_ref[...] = jnp.broadcast_to(run, (BLOCK, HEAD))
            lbeta, l1m, mask = _sb_scores(q, k_ref[rows, :].astype(BF16), i, j, scale)
            wgt = jnp.where(mask, jnp.exp(lbeta + _split_dot(l1m, tsuf) + run), 0.0)
            acc = acc + jnp.dot(wgt.astype(BF16), v_ref[rows, :].astype(BF16), preferred_element_type=F32)
            return t + 1, acc, run + jnp.sum(l1m, axis=-1, keepdims=True)

        t, acc, _ = lax.while_loop(live, step, (jnp.int32(0), jnp.zeros((BLOCK, HEAD), F32), jnp.zeros((BLOCK, 1), F32)))
        o_ref[...] = acc.astype(BF16)
        count_ref[h, i] = t.astype(F32)

    def whole(group):
        return pl.BlockSpec((m, HEAD), lambda h, i: (0, group * n_heads + h))

    return pl.pallas_call(
        body, name="sb_fwd", grid=(n_heads, nb),
        in_specs=[pl.BlockSpec((BLOCK, HEAD), lambda h, i: (i, group0 * n_heads + h)), whole(group0 + 1), whole(group0 + 2)],
        out_specs=[pl.BlockSpec((BLOCK, HEAD), lambda h, i: (i, h)), pl.BlockSpec((None, BLOCK, HEAD), lambda h, i: (h, i, 0)),
                   pl.BlockSpec(memory_space=pltpu.SMEM)],
        out_shape=[_sds((m, w), BF16), _sds((n_heads, m, HEAD), F32), _sds((n_heads, nb), F32)],
        compiler_params=_params(2),
    )(proj, proj, proj)


def _sb_bwd(proj, do, start, count, n_heads, group0):
    m = proj.shape[0]
    nb = m // BLOCK
    w = n_heads * HEAD
    scale = 1.0 / math.sqrt(HEAD)

    def body(q_ref, k_ref, v_ref, do_ref, start_ref, count_ref, dq_ref, dk_ref, dv_ref, dk_s, dv_s):
        h, i = pl.program_id(0), pl.program_id(1)

        @pl.when(i == 0)
        def _():
            dk_s[...] = jnp.zeros_like(dk_s)
            dv_s[...] = jnp.zeros_like(dv_s)

        q = q_ref[...].astype(BF16)
        dob = do_ref[...].astype(BF16)
        first = i + 1 - count_ref[h, i].astype(jnp.int32)
        tsuf = _tri(BLOCK, "gt").astype(BF16)
        tpre = _tri(BLOCK, "lt").astype(BF16)

        def step(j, carry):
            dq, right, psum = carry
            rows = pl.ds(pl.multiple_of(j * BLOCK, BLOCK), BLOCK)
            kj = k_ref[rows, :].astype(BF16)
            vj = v_ref[rows, :].astype(BF16)
            lbeta, l1m, mask = _sb_scores(q, kj, i, j, scale)
            right = jnp.where(j == first, right, right - jnp.sum(l1m, axis=-1, keepdims=True))
            a = jnp.where(mask, jnp.exp(lbeta + _split_dot(l1m, tsuf) + right), 0.0)
            p = a * lax.dot_general(dob, vj, NT, preferred_element_type=F32)
            below = psum + _split_dot(p, tpre)
            beta = jnp.exp(lbeta)
            dz = (jnp.where(mask, p * (1.0 - beta) - below * beta, 0.0) * scale).astype(BF16)
            dq = dq + jnp.dot(dz, kj, preferred_element_type=F32)
            dk_s[rows, :] += lax.dot_general(dz, q, TN, preferred_element_type=F32)
            dv_s[rows, :] += lax.dot_general(a.astype(BF16), dob, TN, preferred_element_type=F32)
            return dq, right, psum + jnp.sum(p, axis=-1, keepdims=True)

        dq, _, _ = lax.fori_loop(first, i + 1, step,
                                 (jnp.zeros((BLOCK, HEAD), F32), start_ref[:, 0:1], jnp.zeros((BLOCK, 1), F32)))
        dq_ref[...] = dq.astype(BF16)

        @pl.when(i == nb - 1)
        def _():
            dk_ref[...] = dk_s[...].astype(BF16)
            dv_ref[...] = dv_s[...].astype(BF16)

    def whole(group):
        return pl.BlockSpec((m, HEAD), lambda h, i: (0, group * n_heads + h))

    tile = pl.BlockSpec((BLOCK, HEAD), lambda h, i: (i, h))
    col = pl.BlockSpec((m, HEAD), lambda h, i: (0, h))
    return pl.pallas_call(
        body, name="sb_bwd", grid=(n_heads, nb),
        in_specs=[pl.BlockSpec((BLOCK, HEAD), lambda h, i: (i, group0 * n_heads + h)), whole(group0 + 1), whole(group0 + 2),
                  tile, pl.BlockSpec((None, BLOCK, HEAD), lambda h, i: (h, i, 0)), pl.BlockSpec(memory_space=pltpu.SMEM)],
        out_specs=[tile, col, col],
        out_shape=[_sds((m, w), BF16)] * 3,
        scratch_shapes=[pltpu.VMEM((m, HEAD), F32)] * 2,
        compiler_params=_params(2),
    )(proj, proj, proj, do, start, count)


def _grad_w_rows(name, x, dy, nd_out):
    m, kx = x.shape
    n = dy.shape[1]
    ks = kx // nd_out
    tm = _tile(m, ROW_TILE)

    def epi(acc, er, orf):
        orf[0][...] = acc[0].astype(BF16)

    return _gemm(name, (nd_out, m // tm),
                 [(x, pl.BlockSpec((tm, ks), lambda j, k: (k, j)), dy, pl.BlockSpec((tm, n), lambda j, k: (k, 0)))],
                 [0], [(ks, n)], TN, [], [(_sds((nd_out, ks, n), BF16), pl.BlockSpec((None, ks, n), lambda j, k: (j, 0, 0)))], epi)[0]


def _local_step(x, target, meta, vec, wts):
    d = x.shape[1]
    nd = wts["w_in"].shape[0]
    width = vec["hg_norm_g"].shape[1]
    n_heads = width // HEAD
    gate_col = 7 * width
    h0 = jnp.concatenate([jnp.zeros((PAD, d), F32), meta, x], axis=0)
    h0b = h0.astype(BF16)
    w_out = wts["w_out"]

    a1, b1, s1 = _ffn_up("ffn1_up", h0b,wts["ffn1_w_gate"], wts["ffn1_w_up"])
    r1, h1, h1b = _residual_ln("ffn1_down", s1, True, wts["ffn1_w_down"], h0, vec["ln1_g"], vec["ln1_b"], 0.5)
    proj = _in_proj(h1b, wts["w_in"])
    o_raw, o_hg, states = _hgrn_fwd(proj, vec["hg_lb_logits"], vec["hg_norm_g"], n_heads)
    o_sb, sb_start, sb_count = _sb_fwd(proj, n_heads, 4)
    u_hg, u_sb, y = _proj_merge(o_hg, o_sb, wts["w_proj_hg"], wts["w_proj_sb"], proj, vec["b_gate"], gate_col)
    r2, h2, h2b = _residual_ln("out_proj", y, False, w_out.reshape(2, d // 2, d), h1, vec["ln2_g"], vec["ln2_b"], 1.0)
    a2, b2, s2 = _ffn_up("ffn2_up", h2b,wts["ffn2_w_gate"], wts["ffn2_w_up"])
    r3, _, _ = _residual_ln("ffn2_down", s2, True, wts["ffn2_w_down"], h2, vec["ln3_g"], vec["ln3_b"], 0.5)

    dr3, dr3b, dg3, db3, loss = _ln_bwd("ln3_bwd", r3, vec["ln3_g"], 0.5, beta=vec["ln3_b"], target=target, first_row=BLOCK)
    dh2, dwg2, dwu2, dwd2 = _ffn_bwd("ffn2", dr3b, dr3, h2b, a2, b2, s2, wts["ffn2_w_gate"], wts["ffn2_w_up"], wts["ffn2_w_down"])
    dr2, dr2b, dg2, db2 = _ln_bwd("ln2_bwd", r2, vec["ln2_g"], 1.0, dy=dh2)
    du_hg, du_sb, dz_hg, dz_sb, dbg = _merge_bwd(dr2b, w_out.reshape(d, d), proj, vec["b_gate"], u_hg, u_sb, gate_col, d // nd)
    dw_out = _grad_w_rows("dw_out", y, dr2b, nd)
    dp_hg = _grad_w("dp_hg", o_hg, du_hg, nd)
    dp_sb = _grad_w("dp_sb", o_sb, du_sb, nd)
    do_hg = _grad_in("do_hg", du_hg, wts["w_proj_hg"])
    do_sb = _grad_in("do_sb", du_sb, wts["w_proj_sb"])
    dhq, dhf, dhi, dhog, dgn, dlb = _hgrn_bwd(proj, vec["hg_lb_logits"], vec["hg_norm_g"], o_raw, do_hg, states, n_heads)
    dsq, dsk, dsv = _sb_bwd(proj, do_sb, sb_start, sb_count, n_heads, 4)
    dproj = jnp.concatenate([dhq, dhf, dhi, dhog, dsq, dsk, dsv, dz_hg, dz_sb], axis=1)
    dw_in = _grad_w("dw_in", h1b, dproj, nd)
    dh1 = _grad_in("dh1", dproj, wts["w_in"], add=dr2)
    dr1, dr1b, dg1, db1 = _ln_bwd("ln1_bwd", r1, vec["ln1_g"], 0.5, dy=dh1)
    dh0, dwg1, dwu1, dwd1 = _ffn_bwd("ffn1", dr1b, dr1, h0b, a1, b1, s1, wts["ffn1_w_gate"], wts["ffn1_w_up"], wts["ffn1_w_down"])

    small = {"ln1_g": dg1, "ln1_b": db1, "ln2_g": dg2, "ln2_b": db2, "ln3_g": dg3, "ln3_b": db3,
             "b_gate": dbg, "hg_lb": dlb, "hg_norm_g": dgn}
    big = {"ffn1_w_gate": dwg1, "ffn1_w_up": dwu1, "ffn1_w_down": dwd1, "w_in": dw_in, "w_proj_hg": dp_hg,
           "w_proj_sb": dp_sb, "w_out": dw_out, "ffn2_w_gate": dwg2, "ffn2_w_up": dwu2, "ffn2_w_down": dwd2}
    return loss, dh0[BLOCK:], dh0[PAD:BLOCK], small, big


def _position():
    return lax.axis_index("x"), lax.axis_index("y"), lax.axis_index("c")


def _slot(px, py, pc):
    return 4 * px + 2 * py + pc


ANY = pl.BlockSpec(memory_space=pl.ANY)


def _all_gather(shards):
    n = len(shards)

    def body(*refs):
        ins, outs = refs[:n], refs[n:2 * n]
        send_sems, recv_sems, local_sems = refs[2 * n:]
        x, y, c = _position()
        me, sibling = (x, y, c), (x, y, 1 - c)
        chips = [(1 - x, y), (x, 1 - y), (1 - x, 1 - y)]

        def copy(a, k, block, to, src=None):
            dst = outs[a].at[_slot(*block)]
            return pltpu.make_async_remote_copy(src_ref=dst if src is None else src, dst_ref=dst,
                                                send_sem=send_sems.at[a, k], recv_sem=recv_sems.at[a, k],
                                                device_id=to, device_id_type=MESH)

        mine = [pltpu.make_async_copy(ins[a], outs[a].at[_slot(*me)], local_sems.at[a]) for a in range(n)]
        for cp in mine:
            cp.start()
        first = []
        for a in range(n):
            first.append(copy(a, 0, me, sibling, src=ins[a]))
            first += [copy(a, 1 + j, me, (*chip, c), src=ins[a]) for j, chip in enumerate(chips)]
        for cp in first:
            cp.start()
        passed = []
        for j, chip in enumerate(chips):
            for a in range(n):
                copy(a, 1 + j, (*chip, c), me).wait_recv()
                cp = copy(a, 4 + j, (*chip, c), sibling)
                cp.start()
                passed.append(cp)
        for a in range(n):
            copy(a, 0, sibling, me).wait_recv()
        for j, chip in enumerate(chips):
            for a in range(n):
                copy(a, 4 + j, (*chip, 1 - c), me).wait_recv()
        for cp in first + passed:
            cp.wait_send()
        for cp in mine:
            cp.wait()

    return pl.pallas_call(
        body, name="all_gather", out_shape=[_sds((N_DEV,) + s.shape, s.dtype) for s in shards],
        in_specs=[ANY] * n, out_specs=[ANY] * n,
        scratch_shapes=[pltpu.SemaphoreType.DMA((n, 7)), pltpu.SemaphoreType.DMA((n, 7)), pltpu.SemaphoreType.DMA((n,))],
    )(*shards)


def _exchange(grads):
    n = len(grads)

    def body(*refs):
        ins, outs = refs[:n], refs[n:2 * n]
        send_sems, recv_sems, local_sems = refs[2 * n:]
        x, y, c = _position()
        mine = _slot(x, y, c)
        peers = [(1 - x if k & 4 else x, 1 - y if k & 2 else y, 1 - c if k & 1 else c) for k in range(1, N_DEV)]

        def copy(a, k, peer):
            return pltpu.make_async_remote_copy(src_ref=ins[a].at[_slot(*peer)], dst_ref=outs[a].at[mine],
                                                send_sem=send_sems.at[a, k], recv_sem=recv_sems.at[a, k],
                                                device_id=peer, device_id_type=MESH)

        def landed(a, k, peer):
            return pltpu.make_async_remote_copy(src_ref=ins[a].at[mine], dst_ref=outs[a].at[_slot(*peer)],
                                                send_sem=send_sems.at[a, k], recv_sem=recv_sems.at[a, k],
                                                device_id=peer, device_id_type=MESH)

        own = [pltpu.make_async_copy(ins[a].at[mine], outs[a].at[mine], local_sems.at[a]) for a in range(n)]
        for cp in own:
            cp.start()
        sent = [copy(a, k, peer) for a in range(n) for k, peer in enumerate(peers)]
        for cp in sent:
            cp.start()
        for a in range(n):
            for k, peer in enumerate(peers):
                landed(a, k, peer).wait_recv()
        for cp in sent:
            cp.wait_send()
        for cp in own:
            cp.wait()

    return pl.pallas_call(
        body, name="grad_exchange", out_shape=[_sds(g.shape, g.dtype) for g in grads],
        in_specs=[ANY] * n, out_specs=[ANY] * n,
        scratch_shapes=[pltpu.SemaphoreType.DMA((n, 7)), pltpu.SemaphoreType.DMA((n, 7)), pltpu.SemaphoreType.DMA((n,))],
    )(*grads)


def _all_reduce_rows(v):
    rows = v.shape[0]

    def body(v_ref, out_ref, buf, send_sems, recv_sems):
        x, y, c = _position()
        me, sibling = (x, y, c), (x, y, 1 - c)
        chips = [(1 - x, y), (x, 1 - y), (1 - x, 1 - y)]

        def copy(k, block, to, src=None):
            dst = buf.at[_slot(*block)]
            return pltpu.make_async_remote_copy(src_ref=dst if src is None else src, dst_ref=dst,
                                                send_sem=send_sems.at[k], recv_sem=recv_sems.at[k],
                                                device_id=to, device_id_type=MESH)

        first = [copy(0, me, sibling, src=v_ref)] + [copy(1 + j, me, (*chip, c), src=v_ref) for j, chip in enumerate(chips)]
        for cp in first:
            cp.start()
        buf[_slot(*me)] = v_ref[...]
        passed = [copy(4 + j, (*chip, c), sibling) for j, chip in enumerate(chips)]
        for j, chip in enumerate(chips):
            copy(1 + j, (*chip, c), me).wait_recv()
            passed[j].start()
        copy(0, sibling, me).wait_recv()
        for j, chip in enumerate(chips):
            copy(4 + j, (*chip, 1 - c), me).wait_recv()
        for cp in first + passed:
            cp.wait_send()
        total = buf[0]
        for s in range(1, N_DEV):
            total = total + buf[s]
        out_ref[...] = total

    vmem = pl.BlockSpec(memory_space=pltpu.VMEM)
    return pl.pallas_call(
        body, name="small_all_reduce", out_shape=_sds(v.shape, F32), in_specs=[vmem], out_specs=vmem,
        scratch_shapes=[pltpu.VMEM((N_DEV, rows, 128), F32), pltpu.SemaphoreType.DMA((7,)), pltpu.SemaphoreType.DMA((7,))],
    )(v)


def _adamw(name, w, m, v, contrib):
    r, c = w.shape
    n = contrib.shape[0]
    tr = _tile(r, 256)

    def body(w_ref, m_ref, v_ref, c_ref, g_out, d_out, m_out, v_out):
        g = c_ref[0].astype(F32)
        for s in range(1, n):
            g = g + c_ref[s].astype(F32)
        m2 = ADAM_B1 * m_ref[...] + (1.0 - ADAM_B1) * g
        v2 = ADAM_B2 * v_ref[...] + (1.0 - ADAM_B2) * (g * g)
        m_hat = m2 / (1.0 - ADAM_B1 ** ADAM_STEP)
        v_hat = v2 / (1.0 - ADAM_B2 ** ADAM_STEP)
        g_out[...] = g
        d_out[...] = -ADAM_LR * (m_hat / (jnp.sqrt(v_hat) + ADAM_EPS) + ADAM_WD * w_ref[...])
        m_out[...] = m2
        v_out[...] = v2

    tile = pl.BlockSpec((tr, c), lambda i: (i, 0))
    return pl.pallas_call(
        body, name=name, grid=(r // tr,), in_specs=[tile, tile, tile, pl.BlockSpec((n, tr, c), lambda i: (0, i, 0))],
        out_specs=[tile] * 4, out_shape=[_sds((r, c), F32)] * 4, compiler_params=_params(1),
    )(w, m, v, contrib)


def _lb_logits_grad(logits, dlb):
    def body(lg_ref, d_ref, out_ref):
        lg = lg_ref[...]
        mx = jnp.maximum(lg[0:1], lg[1:2])
        e0 = jnp.exp(lg[0:1] - mx)
        p0 = e0 / (e0 + jnp.exp(lg[1:2] - mx))
        g0 = d_ref[...] * p0 * (1.0 - p0)
        out_ref[0:1, :] = g0
        out_ref[1:2, :] = -g0

    return pl.pallas_call(body, name="lb_logits_grad", out_shape=_sds(logits.shape, F32))(logits, dlb)


BIG = ("ffn1_w_gate", "ffn1_w_up", "ffn1_w_down", "w_in", "w_proj_hg", "w_proj_sb", "w_out",
       "ffn2_w_gate", "ffn2_w_up", "ffn2_w_down")
VECTORS = ("ln1_g", "ln1_b", "b_gate", "hg_lb_logits", "hg_norm_g", "ln2_g", "ln2_b", "ln3_g", "ln3_b")
WEIGHTS = ("meta", "ln1_g", "ln1_b", "ffn1_w_gate", "ffn1_w_up", "ffn1_w_down", "w_in", "b_gate", "hg_lb_logits",
           "hg_norm_g", "w_proj_hg", "w_proj_sb", "w_out", "ln2_g", "ln2_b", "ffn2_w_gate", "ffn2_w_up",
           "ffn2_w_down", "ln3_g", "ln3_b")


def kernel(x, meta, ln1_g, ln1_b, ffn1_w_gate, ffn1_w_up, ffn1_w_down, w_in, b_gate, hg_lb_logits, hg_norm_g, w_proj_hg, w_proj_sb, w_out, ln2_g, ln2_b, ffn2_w_gate, ffn2_w_up, ffn2_w_down, ln3_g, ln3_b, loss_target, m_meta, m_ln1_g, m_ln1_b, m_ffn1_w_gate, m_ffn1_w_up, m_ffn1_w_down, m_w_in, m_b_gate, m_hg_lb_logits, m_hg_norm_g, m_w_proj_hg, m_w_proj_sb, m_w_out, m_ln2_g, m_ln2_b, m_ffn2_w_gate, m_ffn2_w_up, m_ffn2_w_down, m_ln3_g, m_ln3_b, v_meta, v_ln1_g, v_ln1_b, v_ffn1_w_gate, v_ffn1_w_up, v_ffn1_w_down, v_w_in, v_b_gate, v_hg_lb_logits, v_hg_norm_g, v_w_proj_hg, v_w_proj_sb, v_w_out, v_ln2_g, v_ln2_b, v_ffn2_w_gate, v_ffn2_w_up, v_ffn2_w_down, v_ln3_g, v_ln3_b):
    given = dict(locals())
    d = x.shape[-1]
    ds = meta.shape[1]

    gathered = _all_gather([meta] + [given[k][0].astype(BF16) for k in BIG])
    meta_full = gathered[0].transpose(1, 0, 2).reshape(N_META, d)
    wts = dict(zip(BIG, gathered[1:]))
    vec = {k: given[k] for k in VECTORS}

    loss, grad_x, dmeta, small, big = _local_step(x[0], loss_target[0], meta_full, vec, wts)

    received = dict(zip(BIG, _exchange([big[k] for k in BIG])))
    order = ("ln1_g", "ln1_b", "ln2_g", "ln2_b", "ln3_g", "ln3_b", "b_gate", "hg_lb", "hg_norm_g")
    parts = [small[k].reshape(-1, 128) for k in order] + [dmeta.reshape(-1, 128), jnp.broadcast_to(loss, (8, 128))]
    total = _all_reduce_rows(jnp.concatenate(parts, axis=0))
    reduced, row = {}, 0
    for k, p in zip(order + ("meta", "loss"), parts):
        reduced[k] = total[row:row + p.shape[0]]
        row += p.shape[0]
    loss_out = reduced["loss"][0, 0]
    me = _slot(*_position())
    dmeta_mine = lax.dynamic_slice(reduced["meta"].reshape(N_META, d), (0, me * ds), (N_META, ds))
    dlogits = _lb_logits_grad(hg_lb_logits, reduced["hg_lb"].reshape(1, -1))

    grads, deltas, new_m, new_v = {}, {}, {}, {}
    for k in WEIGHTS:
        w = given[k]
        lead = w.shape[:-2]
        w2, m2, v2 = (a.reshape(a.shape[-2:]) for a in (w, given["m_" + k], given["v_" + k]))
        if k in BIG:
            contrib = received[k]
        elif k == "meta":
            contrib = dmeta_mine[None]
        elif k == "hg_lb_logits":
            contrib = dlogits[None]
        else:
            contrib = reduced[k].reshape((1,) + w2.shape)
        out = _adamw("adamw_" + k, w2, m2, v2, contrib)
        grads[k], deltas[k], new_m[k], new_v[k] = (o.reshape(lead + o.shape) for o in out)
    return (loss_out, grad_x[None], *[grads[k] for k in WEIGHTS], *[deltas[k] for k in WEIGHTS],
            *[new_m[k] for k in WEIGHTS], *[new_v[k] for k in WEIGHTS])
```

```python
import functools
import math

import jax
import jax.numpy as jnp
from jax import lax
from jax.experimental import pallas as pl
from jax.experimental.pallas import tpu as pltpu

F32 = jnp.float32
BF16 = jnp.bfloat16
MESH = pl.DeviceIdType.MESH

N_DEV = 8
N_META = 16
BLOCK = 128
PAD = BLOCK - N_META
HEAD = 128
CHUNK = 16
LN_EPS = 1e-5
RMS_EPS = 1e-6
DN_ALPHA = 2.0 ** 0.25
ADAM_LR, ADAM_B1, ADAM_B2, ADAM_EPS, ADAM_WD, ADAM_STEP = 0.001, 0.9, 0.999, 1e-08, 0.01, 10

VMEM_LIMIT_V7X = 60 * 1024 * 1024
ROW_TILE = 640
LN_ROW_TILE = 320

NN = (((1,), (0,)), ((), ()))
NT = (((1,), (1,)), ((), ()))
TN = (((0,), (0,)), ((), ()))


def _tile(n, pref, mult=16):
    best = None
    for t in range(mult, min(n, pref) + 1, mult):
        if n % t == 0:
            best = t
    return n if best is None else best


def _params(n_axes):
    return pltpu.CompilerParams(dimension_semantics=("arbitrary",) * n_axes, vmem_limit_bytes=VMEM_LIMIT_V7X)


def _sigmoid(x):
    return 1.0 / (1.0 + jnp.exp(-x))


class _Carried:
    def __init__(self, ins, outs, sems, start, finish):
        self.ins, self.outs, self.sems, self.start, self.finish = ins, outs, sems, start, finish


ANY = pl.BlockSpec(memory_space=pl.ANY)


def _pallas(name, body, grid, in_specs, out_specs, out_shape, scratch, operands, carried=None):
    if carried is None:
        return pl.pallas_call(body, name=name, grid=grid, in_specs=in_specs, out_specs=out_specs, out_shape=out_shape,
                              scratch_shapes=scratch, compiler_params=_params(len(grid)))(*operands)
    n_in, n_out, n_scr = len(in_specs), len(out_specs), len(scratch)
    c_in, c_out = len(carried.ins), len(carried.outs)

    def wrapped(*refs):
        ins, rest = refs[:n_in], refs[n_in:]
        c_ins, rest = rest[:c_in], rest[c_in:]
        outs, rest = rest[:n_out], rest[n_out:]
        c_outs, rest = rest[:c_out], rest[c_out:]
        scr, c_sems = rest[:n_scr], rest[n_scr:]
        first = last = None
        for axis, size in enumerate(grid):
            at0, at_end = pl.program_id(axis) == 0, pl.program_id(axis) == size - 1
            first = at0 if first is None else first & at0
            last = at_end if last is None else last & at_end

        @pl.when(first)
        def _():
            carried.start(c_ins, c_outs, c_sems)

        body(*ins, *outs, *scr)

        @pl.when(last)
        def _():
            carried.finish(c_ins, c_outs, c_sems)

    res = pl.pallas_call(
        wrapped, name=name, grid=grid, in_specs=list(in_specs) + [ANY] * c_in, out_specs=list(out_specs) + [ANY] * c_out,
        out_shape=list(out_shape) + list(carried.outs), scratch_shapes=list(scratch) + list(carried.sems),
        compiler_params=_params(len(grid)),
    )(*operands, *carried.ins)
    return res[:n_out], res[n_out:]


def _gemm(name, grid, pairs, acc_of, acc_shapes, dims, extras, outs, epilogue, carried=None):
    n_pairs, n_extra, n_out = len(pairs), len(extras), len(outs)
    nk = grid[-1]
    k_axis = len(grid) - 1

    def body(*refs):
        pr = refs[:2 * n_pairs]
        er = refs[2 * n_pairs:2 * n_pairs + n_extra]
        orf = refs[2 * n_pairs + n_extra:2 * n_pairs + n_extra + n_out]
        accs = refs[2 * n_pairs + n_extra + n_out:]

        def part(p):
            return lax.dot_general(pr[2 * p][...].astype(BF16), pr[2 * p + 1][...].astype(BF16), dims,
                                   preferred_element_type=F32)

        if nk == 1:
            vals = [None] * len(acc_shapes)
            for p in range(n_pairs):
                d = part(p)
                vals[acc_of[p]] = d if vals[acc_of[p]] is None else vals[acc_of[p]] + d
            epilogue(vals, er, orf)
        else:
            k = pl.program_id(k_axis)

            @pl.when(k == 0)
            def _():
                for acc in accs:
                    acc[...] = jnp.zeros_like(acc)

            for p in range(n_pairs):
                accs[acc_of[p]][...] += part(p)

            @pl.when(k == nk - 1)
            def _():
                epilogue([acc[...] for acc in accs], er, orf)

    operands, in_specs = [], []
    for a, a_spec, b, b_spec in pairs:
        operands += [a, b]
        in_specs += [a_spec, b_spec]
    for e, e_spec in extras:
        operands.append(e)
        in_specs.append(e_spec)
    scratch = [] if nk == 1 else [pltpu.VMEM(s, F32) for s in acc_shapes]
    return _pallas(name, body, grid, in_specs, [s for _, s in outs], [o for o, _ in outs], scratch, operands, carried)


def _sds(shape, dtype):
    return jax.ShapeDtypeStruct(shape, dtype)


def _ln_rows(r, g, b):
    mu = jnp.mean(r, axis=-1, keepdims=True)
    xc = r - mu
    var = jnp.mean(xc * xc, axis=-1, keepdims=True)
    return xc * lax.rsqrt(var + LN_EPS) * g + b


def _ffn_up(name, hb, wg, wu, carried=None):
    m, d = hb.shape
    nd, _, fs = wg.shape
    tm = _tile(m, ROW_TILE)

    def epi(acc, er, orf):
        a, b = acc
        orf[0][...] = a
        orf[1][...] = b
        orf[2][...] = (a * _sigmoid(a) * b).astype(BF16)

    h_spec = pl.BlockSpec((tm, d), lambda i, j, k: (i, 0))
    w_spec = pl.BlockSpec((None, d, fs), lambda i, j, k: (j, 0, 0))
    o_spec = pl.BlockSpec((None, tm, fs), lambda i, j, k: (j, i, 0))
    return _gemm(name, (m // tm, nd, 1), [(hb, h_spec, wg, w_spec), (hb, h_spec, wu, w_spec)], [0, 1],
                 [(tm, fs)] * 2, NN, [],
                 [(_sds((nd, m, fs), F32), o_spec), (_sds((nd, m, fs), F32), o_spec), (_sds((nd, m, fs), BF16), o_spec)], epi,
                 carried)


def _residual_ln(name, a, a_stacked, w, h_in, g, beta, scale, carried=None):
    nk, tk, d = w.shape
    m = h_in.shape[0]
    tm = _tile(m, LN_ROW_TILE)

    def epi(acc, er, orf):
        r = DN_ALPHA * er[0][...] + scale * acc[0]
        h = _ln_rows(r, er[1][...], er[2][...])
        orf[0][...] = r
        orf[1][...] = h
        orf[2][...] = h.astype(BF16)

    if a_stacked:
        a_spec = pl.BlockSpec((None, tm, tk), lambda i, k: (k, i, 0))
    else:
        a_spec = pl.BlockSpec((tm, tk), lambda i, k: (i, k))
    w_spec = pl.BlockSpec((None, tk, d), lambda i, k: (k, 0, 0))
    row = pl.BlockSpec((tm, d), lambda i, k: (i, 0))
    vec = pl.BlockSpec((1, d), lambda i, k: (0, 0))
    return _gemm(name, (m // tm, nk), [(a, a_spec, w, w_spec)], [0], [(tm, d)], NN,
                 [(h_in, row), (g, vec), (beta, vec)],
                 [(_sds((m, d), F32), row), (_sds((m, d), F32), row), (_sds((m, d), BF16), row)], epi, carried)


def _in_proj(hb, w_in):
    m, d = hb.shape
    nd, _, cs = w_in.shape
    tm = _tile(m, ROW_TILE)

    def epi(acc, er, orf):
        orf[0][...] = acc[0]

    return _gemm("in_proj", (m // tm, nd, 1),
                 [(hb, pl.BlockSpec((tm, d), lambda i, j, k: (i, 0)), w_in, pl.BlockSpec((None, d, cs), lambda i, j, k: (j, 0, 0)))],
                 [0], [(tm, cs)], NN, [], [(_sds((m, nd * cs), F32), pl.BlockSpec((tm, cs), lambda i, j, k: (i, j)))], epi)[0]


def _proj_merge(o_hg, o_sb, p_hg, p_sb, proj, b_gate, gate_col):
    m, w = o_hg.shape
    nd, _, ds = p_hg.shape
    d = nd * ds
    tm = _tile(m, ROW_TILE)
    c0 = gate_col // ds

    def epi(acc, er, orf):
        u_hg, u_sb = acc
        g_hg = _sigmoid(er[0][...] + er[2][...])
        g_sb = _sigmoid(er[1][...] + er[3][...])
        orf[0][...] = u_hg
        orf[1][...] = u_sb
        orf[2][...] = (g_hg * u_hg + g_sb * u_sb).astype(BF16)

    o_spec = pl.BlockSpec((tm, w), lambda i, j, k: (i, 0))
    p_spec = pl.BlockSpec((None, w, ds), lambda i, j, k: (j, 0, 0))
    out = pl.BlockSpec((tm, ds), lambda i, j, k: (i, j))
    return _gemm("proj_merge", (m // tm, nd, 1), [(o_hg, o_spec, p_hg, p_spec), (o_sb, o_spec, p_sb, p_spec)], [0, 1],
                 [(tm, ds)] * 2, NN,
                 [(proj, pl.BlockSpec((tm, ds), lambda i, j, k: (i, c0 + j))),
                  (proj, pl.BlockSpec((tm, ds), lambda i, j, k: (i, c0 + nd + j))),
                  (b_gate, pl.BlockSpec((1, ds), lambda i, j, k: (0, j))),
                  (b_gate, pl.BlockSpec((1, ds), lambda i, j, k: (0, nd + j)))],
                 [(_sds((m, d), F32), out), (_sds((m, d), F32), out), (_sds((m, d), BF16), out)], epi)


def _ln_bwd(name, r, g, out_scale, dy=None, beta=None, target=None, first_row=0):
    m, d = r.shape
    tm = _tile(m, LN_ROW_TILE if target is None else BLOCK)
    with_loss = target is not None
    skip = first_row // tm if with_loss else 0
    assert not with_loss or first_row % tm == 0

    def body(*refs):
        if with_loss:
            r_ref, g_ref, b_ref, t_ref, dr_ref, drb_ref, dg_ref, db_ref, loss_ref = refs
        else:
            r_ref, g_ref, dy_ref, dr_ref, drb_ref, dg_ref, db_ref = refs
        i = pl.program_id(0)
        x = r_ref[...]
        mu = jnp.mean(x, axis=-1, keepdims=True)
        xc = x - mu
        var = jnp.mean(xc * xc, axis=-1, keepdims=True)
        rstd = lax.rsqrt(var + LN_EPS)
        xhat = xc * rstd
        gv = g_ref[...]
        if with_loss:
            err = xhat * gv + b_ref[...] - t_ref[...]
            live = (i >= skip).astype(F32)
            dyv = err * (live / d)
            part = 0.5 * live * jnp.sum(jnp.sum(err * err, axis=-1, keepdims=True), axis=0, keepdims=True) / d
        else:
            dyv = dy_ref[...]
        dxh = dyv * gv
        m1 = jnp.mean(dxh, axis=-1, keepdims=True)
        m2 = jnp.mean(dxh * xhat, axis=-1, keepdims=True)
        dr = rstd * (dxh - m1 - xhat * m2)
        dr_ref[...] = dr
        drb_ref[...] = (out_scale * dr).astype(BF16)

        @pl.when(i == 0)
        def _():
            dg_ref[...] = jnp.zeros_like(dg_ref)
            db_ref[...] = jnp.zeros_like(db_ref)
            if with_loss:
                loss_ref[...] = jnp.zeros_like(loss_ref)

        dg_ref[...] += jnp.sum(dyv * xhat, axis=0, keepdims=True)
        db_ref[...] += jnp.sum(dyv, axis=0, keepdims=True)
        if with_loss:
            loss_ref[...] += jnp.broadcast_to(part, loss_ref.shape)

    row = pl.BlockSpec((tm, d), lambda i: (i, 0))
    vec = pl.BlockSpec((1, d), lambda i: (0, 0))
    out_shape = [_sds((m, d), F32), _sds((m, d), BF16), _sds((1, d), F32), _sds((1, d), F32)]
    out_specs = [row, row, vec, vec]
    if with_loss:
        operands = [r, g, beta, target]
        in_specs = [row, vec, vec, pl.BlockSpec((tm, d), lambda i: (jnp.maximum(i - skip, 0), 0))]
        out_shape.append(_sds((1, BLOCK), F32))
        out_specs.append(pl.BlockSpec((1, BLOCK), lambda i: (0, 0)))
    else:
        operands = [r, g, dy]
        in_specs = [row, vec, row]
    return pl.pallas_call(body, name=name, grid=(m // tm,), in_specs=in_specs, out_specs=out_specs, out_shape=out_shape,
                          compiler_params=_params(1))(*operands)


def _ffn_bwd(tag, drb, dr, hb, a, b, s, wg, wu, wd, exchange=None):
    m, d = drb.shape
    nd, _, fs = wg.shape
    tm = _tile(m, ROW_TILE)

    def epi_ds(acc, er, orf):
        ds = acc[0]
        av, bv = er[0][...], er[1][...]
        sg = _sigmoid(av)
        orf[0][...] = (ds * bv * sg * (1.0 + av * (1.0 - sg))).astype(BF16)
        orf[1][...] = (ds * av * sg).astype(BF16)

    st = pl.BlockSpec((None, tm, fs), lambda i, j, k: (j, i, 0))
    da, db = _gemm(tag + "_ds", (m // tm, nd, 1),
                   [(drb, pl.BlockSpec((tm, d), lambda i, j, k: (i, 0)), wd, pl.BlockSpec((None, fs, d), lambda i, j, k: (j, 0, 0)))],
                   [0], [(tm, fs)], NT, [(a, st), (b, st)],
                   [(_sds((nd, m, fs), BF16), st), (_sds((nd, m, fs), BF16), st)], epi_ds)

    def epi_w(acc, er, orf):
        for o, v in zip(orf, acc):
            o[...] = v.astype(BF16)

    nkm = m // tm
    dwd = _gemm(tag + "_dwd", (nd, nkm),
                [(s, pl.BlockSpec((None, tm, fs), lambda j, k: (j, k, 0)), drb, pl.BlockSpec((tm, d), lambda j, k: (k, 0)))],
                [0], [(fs, d)], TN, [], [(_sds((nd, fs, d), BF16), pl.BlockSpec((None, fs, d), lambda j, k: (j, 0, 0)))], epi_w)[0]
    h_spec = pl.BlockSpec((tm, d), lambda j, k: (k, 0))
    g_spec = pl.BlockSpec((None, tm, fs), lambda j, k: (j, k, 0))
    w_out = pl.BlockSpec((None, d, fs), lambda j, k: (j, 0, 0))
    dwgu = _gemm(tag + "_dwgu", (nd, nkm), [(hb, h_spec, da, g_spec), (hb, h_spec, db, g_spec)], [0, 1],
                 [(d, fs)] * 2, TN, [], [(_sds((nd, d, fs), BF16), w_out), (_sds((nd, d, fs), BF16), w_out)], epi_w,
                 exchange([dwd]) if exchange else None)
    if exchange:
        (dwg, dwu), (dwd,) = dwgu
    else:
        dwg, dwu = dwgu

    def epi_dh(acc, er, orf):
        orf[0][...] = DN_ALPHA * er[0][...] + acc[0]

    gk = pl.BlockSpec((None, tm, fs), lambda i, k: (k, i, 0))
    wk = pl.BlockSpec((None, d, fs), lambda i, k: (k, 0, 0))
    row = pl.BlockSpec((tm, d), lambda i, k: (i, 0))
    dh = _gemm(tag + "_dh", (m // tm, nd), [(da, gk, wg, wk), (db, gk, wu, wk)], [0, 0], [(tm, d)], NT,
               [(dr, row)], [(_sds((m, d), F32), row)], epi_dh, exchange([dwg, dwu]) if exchange else None)
    if exchange:
        (dh,), (dwg, dwu) = dh
    else:
        dh = dh[0]
    return dh, dwg, dwu, dwd


def _merge_bwd(dmixb, w_out2, proj, b_gate, u_hg, u_sb, gate_col, ds):
    m, d = dmixb.shape
    nd = d // ds
    tm = _tile(m, ROW_TILE)
    c0 = gate_col // ds

    def epi(acc, er, orf):
        i = pl.program_id(1)
        dy = acc[0]
        g_hg = _sigmoid(er[0][...] + er[2][...])
        g_sb = _sigmoid(er[1][...] + er[3][...])
        orf[0][...] = (dy * g_hg).astype(BF16)
        orf[1][...] = (dy * g_sb).astype(BF16)
        dz_hg = dy * er[4][...] * g_hg * (1.0 - g_hg)
        dz_sb = dy * er[5][...] * g_sb * (1.0 - g_sb)
        orf[2][...] = dz_hg.astype(BF16)
        orf[3][...] = dz_sb.astype(BF16)

        @pl.when(i == 0)
        def _():
            orf[4][...] = jnp.zeros_like(orf[4])
            orf[5][...] = jnp.zeros_like(orf[5])

        orf[4][...] += jnp.sum(dz_hg, axis=0, keepdims=True)
        orf[5][...] += jnp.sum(dz_sb, axis=0, keepdims=True)

    tile = pl.BlockSpec((tm, ds), lambda j, i, k: (i, j))
    vec = pl.BlockSpec((1, ds), lambda j, i, k: (0, j))
    du_hg, du_sb, dz_hg, dz_sb, db_hg, db_sb = _gemm(
        "merge_bwd", (nd, m // tm, 1),
        [(dmixb, pl.BlockSpec((tm, d), lambda j, i, k: (i, 0)), w_out2, pl.BlockSpec((ds, d), lambda j, i, k: (j, 0)))],
        [0], [(tm, ds)], NT,
        [(proj, pl.BlockSpec((tm, ds), lambda j, i, k: (i, c0 + j))),
         (proj, pl.BlockSpec((tm, ds), lambda j, i, k: (i, c0 + nd + j))),
         (b_gate, vec), (b_gate, pl.BlockSpec((1, ds), lambda j, i, k: (0, nd + j))),
         (u_hg, tile), (u_sb, tile)],
        [(_sds((m, d), BF16), tile), (_sds((m, d), BF16), tile), (_sds((m, d), BF16), tile), (_sds((m, d), BF16), tile),
         (_sds((1, d), F32), vec), (_sds((1, d), F32), vec)], epi)
    return du_hg, du_sb, dz_hg, dz_sb, jnp.concatenate([db_hg, db_sb], axis=1)


def _grad_w(name, x, dy, nd_out):
    m, kx = x.shape
    n = dy.shape[1]
    ns = n // nd_out
    tm = _tile(m, ROW_TILE)

    def epi(acc, er, orf):
        orf[0][...] = acc[0].astype(BF16)

    return _gemm(name, (nd_out, m // tm),
                 [(x, pl.BlockSpec((tm, kx), lambda j, k: (k, 0)), dy, pl.BlockSpec((tm, ns), lambda j, k: (k, j)))],
                 [0], [(kx, ns)], TN, [], [(_sds((nd_out, kx, ns), BF16), pl.BlockSpec((None, kx, ns), lambda j, k: (j, 0, 0)))], epi)[0]


def _grad_in(name, dy, w, add=None, carried=None):
    m = dy.shape[0]
    nd, kx, ns = w.shape
    tm = _tile(m, ROW_TILE)

    def epi(acc, er, orf):
        orf[0][...] = acc[0] if add is None else DN_ALPHA * er[0][...] + acc[0]

    row = pl.BlockSpec((tm, kx), lambda i, k: (i, 0))
    res = _gemm(name, (m // tm, nd),
                [(dy, pl.BlockSpec((tm, ns), lambda i, k: (i, k)), w, pl.BlockSpec((None, kx, ns), lambda i, k: (k, 0, 0)))],
                [0], [(tm, kx)], NT, [] if add is None else [(add, row)], [(_sds((m, kx), F32), row)], epi, carried)
    return res[0] if carried is None else (res[0][0], res[1])


def _tri(n, kind):
    r = lax.broadcasted_iota(jnp.int32, (n, n), 0)
    c = lax.broadcasted_iota(jnp.int32, (n, n), 1)
    return {"le": c <= r, "ge": c >= r, "gt": r > c, "lt": r < c}[kind]


def _dot_f32(a, b, dims=NN):
    return lax.dot_general(a, b, dims, preferred_element_type=F32, precision=lax.Precision.HIGHEST)


def _hgrn_gates(i, hq, hf, logits):
    lg = logits
    mx = jnp.maximum(lg[0:1], lg[1:2])
    e0 = jnp.exp(lg[0:1] - mx)
    lb = e0 / (e0 + jnp.exp(lg[1:2] - mx))
    sig = _sigmoid(hf)
    f = lb + (1.0 - lb) * sig
    valid = (i * BLOCK + lax.broadcasted_iota(jnp.int32, hf.shape, 0)) >= PAD
    g = jnp.where(valid, jnp.log(f), 0.0)
    k = jnp.where(valid, 1.0 - f, 0.0)
    sq = _sigmoid(hq)
    return hq * sq, k, g, sig, f, lb, valid, sq


def _hgrn_fwd(proj, logits, gn, n_heads, carried=None):
    m = proj.shape[0]
    nb = m // BLOCK
    w = n_heads * HEAD
    cpb = BLOCK // CHUNK

    def body(hq_ref, hf_ref, hi_ref, hog_ref, lg_ref, gn_ref, o_ref, ohg_ref, st_all_ref, st_ref, q_s, k_s, v_s, b_s):
        i = pl.program_id(1)

        @pl.when(i == 0)
        def _():
            st_ref[...] = jnp.zeros_like(st_ref)

        q, k, g, _, _, _, _, _ = _hgrn_gates(i, hq_ref[...], hf_ref[...], lg_ref[...])
        q_s[...] = q
        k_s[...] = k
        v_s[...] = hi_ref[...]
        b_s[...] = _dot_f32(_tri(BLOCK, "le").astype(F32), g)
        trow = lax.broadcasted_iota(jnp.int32, (CHUNK, 1), 0)

        def chunk(c, carry):
            sl = pl.ds(pl.multiple_of(c * CHUNK, CHUNK), CHUNK)
            prev = b_s[pl.ds(pl.multiple_of(jnp.maximum(c - 1, 0) * CHUNK, CHUNK), CHUNK), :]
            base = prev[CHUNK - 1:CHUNK, :] * (c > 0).astype(F32)
            b = b_s[sl, :] - base
            qc, kc, vc = q_s[sl, :], k_s[sl, :], v_s[sl, :]
            st = st_ref[...]
            st_all_ref[c] = st.astype(BF16)
            o = lax.dot_general((qc * jnp.exp(b)).astype(BF16), st.astype(BF16), NT, preferred_element_type=F32)
            for s in range(CHUNK):
                e = jnp.exp(jnp.minimum(b - b[s:s + 1, :], 0.0))
                p = jnp.sum(qc * kc[s:s + 1, :] * e, axis=-1, keepdims=True)
                o = o + jnp.where(trow >= s, p, 0.0) * vc[s:s + 1, :]
            o_ref[sl, :] = o
            blast = b[CHUNK - 1:CHUNK, :]
            kd = kc * jnp.exp(blast - b)
            st_ref[...] = st * jnp.exp(blast) + lax.dot_general(vc.astype(BF16), kd.astype(BF16), TN, preferred_element_type=F32)
            return carry

        lax.fori_loop(0, cpb, chunk, 0)
        o = o_ref[...]
        n = o * lax.rsqrt(jnp.mean(o * o, axis=-1, keepdims=True) + RMS_EPS)
        hog = hog_ref[...]
        ohg_ref[...] = (n * gn_ref[...] * hog * _sigmoid(hog)).astype(BF16)

    def col(group):
        return pl.BlockSpec((BLOCK, HEAD), lambda h, i: (i, group * n_heads + h))

    vec = pl.BlockSpec((1, HEAD), lambda h, i: (0, h))
    tile = pl.BlockSpec((BLOCK, HEAD), lambda h, i: (i, h))
    return _pallas(
        "hgrn_fwd", body, (n_heads, nb),
        [col(0), col(1), col(2), col(3), pl.BlockSpec((2, HEAD), lambda h, i: (0, h)), vec],
        [tile, tile, pl.BlockSpec((None, cpb, HEAD, HEAD), lambda h, i: (h, i, 0, 0))],
        [_sds((m, w), F32), _sds((m, w), BF16), _sds((n_heads, m // CHUNK, HEAD, HEAD), BF16)],
        [pltpu.VMEM((HEAD, HEAD), F32)] + [pltpu.VMEM((BLOCK, HEAD), F32)] * 4,
        (proj, proj, proj, proj, logits, gn), carried)


def _hgrn_bwd(proj, logits, gn, o_raw, do_hg, states, n_heads, carried=None):
    m = proj.shape[0]
    nb = m // BLOCK
    w = n_heads * HEAD
    cpb = BLOCK // CHUNK
    last_state = m // CHUNK - 1

    def body(hq_ref, hf_ref, hi_ref, hog_ref, lg_ref, gn_ref, o_ref, do_ref, st_all_ref, st_next_ref,
             dhq_ref, dhf_ref, dhi_ref, dhog_ref, dgn_ref, dlb_ref,
             dst_ref, q_s, k_s, v_s, b_s, do_s, dq_s, dk_s, dv_s, ex_s):
        step = pl.program_id(1)
        i = nb - 1 - step

        @pl.when(step == 0)
        def _():
            dst_ref[...] = jnp.zeros_like(dst_ref)
            dgn_ref[...] = jnp.zeros_like(dgn_ref)
            dlb_ref[...] = jnp.zeros_like(dlb_ref)

        hq = hq_ref[...]
        q, k, g, sig, f, lb, valid, sq = _hgrn_gates(i, hq, hf_ref[...], lg_ref[...])
        q_s[...] = q
        k_s[...] = k
        v_s[...] = hi_ref[...]
        b_s[...] = _dot_f32(_tri(BLOCK, "le").astype(F32), g)

        o = o_ref[...]
        rs = lax.rsqrt(jnp.mean(o * o, axis=-1, keepdims=True) + RMS_EPS)
        n = o * rs
        hog = hog_ref[...]
        sg = _sigmoid(hog)
        sil = hog * sg
        gnv = gn_ref[...]
        dh = do_ref[...]
        dhog_ref[...] = (dh * n * gnv * sg * (1.0 + hog * (1.0 - sg))).astype(BF16)
        dgn_ref[...] += jnp.sum(dh * n * sil, axis=0, keepdims=True)
        dn = dh * gnv * sil
        do_s[...] = rs * (dn - n * jnp.mean(dn * n, axis=-1, keepdims=True))
        trow = lax.broadcasted_iota(jnp.int32, (CHUNK, 1), 0)

        def chunk(t, st_end):
            c = cpb - 1 - t
            sl = pl.ds(pl.multiple_of(c * CHUNK, CHUNK), CHUNK)
            prev = b_s[pl.ds(pl.multiple_of(jnp.maximum(c - 1, 0) * CHUNK, CHUNK), CHUNK), :]
            base = prev[CHUNK - 1:CHUNK, :] * (c > 0).astype(F32)
            b = b_s[sl, :] - base
            qc, kc, vc, doc = q_s[sl, :], k_s[sl, :], v_s[sl, :], do_s[sl, :]
            eb = jnp.exp(b)
            blast = b[CHUNK - 1:CHUNK, :]
            ek = jnp.exp(blast - b)
            dst = dst_ref[...]
            dstb = dst.astype(BF16)
            docb = doc.astype(BF16)
            st = st_all_ref[c]
            ex_s[sl, :] = jnp.broadcast_to(jnp.sum(st_end.astype(F32) * dst, axis=0, keepdims=True), (CHUNK, HEAD))
            dq = lax.dot_general(docb, st, NN, preferred_element_type=F32) * eb
            dk = lax.dot_general(vc.astype(BF16), dstb, NN, preferred_element_type=F32) * ek
            dv = lax.dot_general((kc * ek).astype(BF16), dstb, NT, preferred_element_type=F32)
            for s in range(CHUNK):
                em = jnp.where(trow >= s, jnp.exp(jnp.minimum(b - b[s:s + 1, :], 0.0)), 0.0)
                ks, vs = kc[s:s + 1, :], vc[s:s + 1, :]
                dp = jnp.sum(doc * vs, axis=-1, keepdims=True)
                qe = qc * em
                p = jnp.sum(qe * ks, axis=-1, keepdims=True)
                dq = dq + dp * ks * em
                dk = dk + jnp.where(trow == s, jnp.sum(dp * qe, axis=0, keepdims=True), 0.0)
                dv = dv + jnp.where(trow == s, jnp.sum(p * doc, axis=0, keepdims=True), 0.0)
            dst_ref[...] = dst * jnp.exp(blast) + lax.dot_general(docb, (qc * eb).astype(BF16), TN, preferred_element_type=F32)
            dq_s[sl, :] = dq
            dk_s[sl, :] = dk
            dv_s[sl, :] = dv
            return st

        lax.fori_loop(0, cpb, chunk, st_next_ref[0])
        dq, dk = dq_s[...], dk_s[...]
        r_i = lax.broadcasted_iota(jnp.int32, (BLOCK, BLOCK), 0)
        c_i = lax.broadcasted_iota(jnp.int32, (BLOCK, BLOCK), 1)
        within = ((c_i >= r_i) & (c_i // CHUNK == r_i // CHUNK)).astype(F32)
        rc = _dot_f32(within, q * dq - k * dk) + ex_s[...]
        df =jnp.where(valid, rc / f - dk, 0.0)
        dhf_ref[...] = (df * (1.0 - lb) * sig * (1.0 - sig)).astype(BF16)
        dlb_ref[...] += jnp.sum(df * (1.0 - sig), axis=0, keepdims=True)
        dhq_ref[...] = (dq * sq * (1.0 + hq * (1.0 - sq))).astype(BF16)
        dhi_ref[...] = dv_s[...].astype(BF16)

    def col(group):
        return pl.BlockSpec((BLOCK, HEAD), lambda h, s: (nb - 1 - s, group * n_heads + h))

    vec = pl.BlockSpec((1, HEAD), lambda h, s: (0, h))
    tile = pl.BlockSpec((BLOCK, HEAD), lambda h, s: (nb - 1 - s, h))
    nxt = pl.BlockSpec((None, 1, HEAD, HEAD), lambda h, s: (h, jnp.minimum((nb - s) * cpb, last_state), 0, 0))
    return _pallas(
        "hgrn_bwd", body, (n_heads, nb),
        [col(0), col(1), col(2), col(3), pl.BlockSpec((2, HEAD), lambda h, s: (0, h)), vec, tile, tile,
         pl.BlockSpec((None, cpb, HEAD, HEAD), lambda h, s: (h, nb - 1 - s, 0, 0)), nxt],
        [tile, tile, tile, tile, vec, vec],
        [_sds((m, w), BF16)] * 4 + [_sds((1, w), F32)] * 2,
        [pltpu.VMEM((HEAD, HEAD), F32)] + [pltpu.VMEM((BLOCK, HEAD), F32)] * 9,
        (proj, proj, proj, proj, logits, gn, o_raw, do_hg, states, states), carried)


def _split_dot(x, t):
    hi = x.astype(BF16)
    lo = (x - hi.astype(F32)).astype(BF16)
    return jnp.dot(hi, t, preferred_element_type=F32) + jnp.dot(lo, t, preferred_element_type=F32)


def _sb_scores(q, kj, i, j, scale):
    z = lax.dot_general(q, kj, NT, preferred_element_type=F32) * scale
    lp = jnp.log(1.0 + jnp.exp(-jnp.abs(z)))
    lbeta = jnp.minimum(z, 0.0) - lp
    qpos = i * BLOCK + lax.broadcasted_iota(jnp.int32, z.shape, 0)
    kpos = j * BLOCK + lax.broadcasted_iota(jnp.int32, z.shape, 1)
    mask = (kpos < qpos) & (kpos >= PAD)
    l1m = jnp.where(mask, lbeta - z, 0.0)
    return lbeta, l1m, mask


SB_DEAD = -104.0


def _sb_fwd(proj, n_heads, group0):
    m = proj.shape[0]
    nb = m // BLOCK
    w = n_heads * HEAD
    scale = 1.0 / math.sqrt(HEAD)

    def body(q_ref, k_ref, v_ref, o_ref, start_ref, count_ref):
        h, i = pl.program_id(0), pl.program_id(1)
        q = q_ref[...].astype(BF16)
        tsuf = _tri(BLOCK, "gt").astype(BF16)

        def live(carry):
            t, _, run = carry
            return (t <= i) & (jnp.max(run) > SB_DEAD)

        def step(carry):
            t, acc, run = carry
            j = i - t
            rows = pl.ds(pl.multiple_of(j * BLOCK, BLOCK), BLOCK)
            start_ref[...] = jnp.broadcast_to(run, (BLOCK, HEAD))
            lbeta, l1m, mask = _sb_scores(q, k_ref[rows, :].astype(BF16), i, j, scale)
            wgt = jnp.where(mask, jnp.exp(lbeta + _split_dot(l1m, tsuf) + run), 0.0)
            acc = acc + jnp.dot(wgt.astype(BF16), v_ref[rows, :].astype(BF16), preferred_element_type=F32)
            return t + 1, acc, run + jnp.sum(l1m, axis=-1, keepdims=True)

        t, acc, _ = lax.while_loop(live, step, (jnp.int32(0), jnp.zeros((BLOCK, HEAD), F32), jnp.zeros((BLOCK, 1), F32)))
        o_ref[...] = acc.astype(BF16)
        count_ref[h, i] = t.astype(F32)

    def whole(group):
        return pl.BlockSpec((m, HEAD), lambda h, i: (0, group * n_heads + h))

    return pl.pallas_call(
        body, name="sb_fwd", grid=(n_heads, nb),
        in_specs=[pl.BlockSpec((BLOCK, HEAD), lambda h, i: (i, group0 * n_heads + h)), whole(group0 + 1), whole(group0 + 2)],
        out_specs=[pl.BlockSpec((BLOCK, HEAD), lambda h, i: (i, h)), pl.BlockSpec((None, BLOCK, HEAD), lambda h, i: (h, i, 0)),
                   pl.BlockSpec(memory_space=pltpu.SMEM)],
        out_shape=[_sds((m, w), BF16), _sds((n_heads, m, HEAD), F32), _sds((n_heads, nb), F32)],
        compiler_params=_params(2),
    )(proj, proj, proj)


def _sb_bwd(proj, do, start, count, n_heads, group0):
    m = proj.shape[0]
    nb = m // BLOCK
    w = n_heads * HEAD
    scale = 1.0 / math.sqrt(HEAD)

    def body(q_ref, k_ref, v_ref, do_ref, start_ref, count_ref, dq_ref, dk_ref, dv_ref, dk_s, dv_s):
        h, i = pl.program_id(0), pl.program_id(1)

        @pl.when(i == 0)
        def _():
            dk_s[...] = jnp.zeros_like(dk_s)
            dv_s[...] = jnp.zeros_like(dv_s)

        q = q_ref[...].astype(BF16)
        dob = do_ref[...].astype(BF16)
        first = i + 1 - count_ref[h, i].astype(jnp.int32)
        tsuf = _tri(BLOCK, "gt").astype(BF16)
        tpre = _tri(BLOCK, "lt").astype(BF16)

        def step(j, carry):
            dq, right, psum = carry
            rows = pl.ds(pl.multiple_of(j * BLOCK, BLOCK), BLOCK)
            kj = k_ref[rows, :].astype(BF16)
            vj = v_ref[rows, :].astype(BF16)
            lbeta, l1m, mask = _sb_scores(q, kj, i, j, scale)
            right = jnp.where(j == first, right, right - jnp.sum(l1m, axis=-1, keepdims=True))
            a = jnp.where(mask, jnp.exp(lbeta + _split_dot(l1m, tsuf) + right), 0.0)
            p = a * lax.dot_general(dob, vj, NT, preferred_element_type=F32)
            below = psum + _split_dot(p, tpre)
            beta = jnp.exp(lbeta)
            dz = (jnp.where(mask, p * (1.0 - beta) - below * beta, 0.0) * scale).astype(BF16)
            dq = dq + jnp.dot(dz, kj, preferred_element_type=F32)
            dk_s[rows, :] += lax.dot_general(dz, q, TN, preferred_element_type=F32)
            dv_s[rows, :] += lax.dot_general(a.astype(BF16), dob, TN, preferred_element_type=F32)
            return dq, right, psum + jnp.sum(p, axis=-1, keepdims=True)

        dq, _, _ = lax.fori_loop(first, i + 1, step,
                                 (jnp.zeros((BLOCK, HEAD), F32), start_ref[:, 0:1], jnp.zeros((BLOCK, 1), F32)))
        dq_ref[...] = dq.astype(BF16)

        @pl.when(i == nb - 1)
        def _():
            dk_ref[...] = dk_s[...].astype(BF16)
            dv_ref[...] = dv_s[...].astype(BF16)

    def whole(group):
        return pl.BlockSpec((m, HEAD), lambda h, i: (0, group * n_heads + h))

    tile = pl.BlockSpec((BLOCK, HEAD), lambda h, i: (i, h))
    col = pl.BlockSpec((m, HEAD), lambda h, i: (0, h))
    return pl.pallas_call(
        body, name="sb_bwd", grid=(n_heads, nb),
        in_specs=[pl.BlockSpec((BLOCK, HEAD), lambda h, i: (i, group0 * n_heads + h)), whole(group0 + 1), whole(group0 + 2),
                  tile, pl.BlockSpec((None, BLOCK, HEAD), lambda h, i: (h, i, 0)), pl.BlockSpec(memory_space=pltpu.SMEM)],
        out_specs=[tile, col, col],
        out_shape=[_sds((m, w), BF16)] * 3,
        scratch_shapes=[pltpu.VMEM((m, HEAD), F32)] * 2,
        compiler_params=_params(2),
    )(proj, proj, proj, do, start, count)


def _grad_w_rows(name, x, dy, nd_out):
    m, kx = x.shape
    n = dy.shape[1]
    ks = kx // nd_out
    tm = _tile(m, ROW_TILE)

    def epi(acc, er, orf):
        orf[0][...] = acc[0].astype(BF16)

    return _gemm(name, (nd_out, m // tm),
                 [(x, pl.BlockSpec((tm, ks), lambda j, k: (k, j)), dy, pl.BlockSpec((tm, n), lambda j, k: (k, 0)))],
                 [0], [(ks, n)], TN, [], [(_sds((nd_out, ks, n), BF16), pl.BlockSpec((None, ks, n), lambda j, k: (j, 0, 0)))], epi)[0]


LATE = ("w_proj_hg", "w_proj_sb", "w_out", "ffn2_w_gate", "ffn2_w_up", "ffn2_w_down")


def _local_step(x, target, meta, vec, wts, shards=None):
    d = x.shape[1]
    width = vec["hg_norm_g"].shape[1]
    n_heads = width // HEAD
    gate_col = 7 * width
    h0 = jnp.concatenate([jnp.zeros((PAD, d), F32), meta, x], axis=0)
    h0b = h0.astype(BF16)
    wts = dict(wts)
    exchange = None if shards is None else _exchange_carried

    if shards is None:
        a1, b1, s1 = _ffn_up("ffn1_up", h0b, wts["ffn1_w_gate"], wts["ffn1_w_up"])
        r1, h1, h1b = _residual_ln("ffn1_down", s1, True, wts["ffn1_w_down"], h0, vec["ln1_g"], vec["ln1_b"], 0.5)
    else:
        (a1, b1, s1), (wts["ffn1_w_down"],) = _ffn_up("ffn1_up", h0b, wts["ffn1_w_gate"], wts["ffn1_w_up"],
                                                     _gather_carried([shards["ffn1_w_down"]]))
        (r1, h1, h1b), (wts["w_in"],) = _residual_ln("ffn1_down", s1, True, wts["ffn1_w_down"], h0, vec["ln1_g"],
                                                     vec["ln1_b"], 0.5, _gather_carried([shards["w_in"]]))
    nd = wts["w_in"].shape[0]
    proj = _in_proj(h1b, wts["w_in"])
    if shards is None:
        o_raw, o_hg, states = _hgrn_fwd(proj, vec["hg_lb_logits"], vec["hg_norm_g"], n_heads)
    else:
        (o_raw, o_hg, states), late = _hgrn_fwd(proj, vec["hg_lb_logits"], vec["hg_norm_g"], n_heads,
                                                _gather_carried([shards[k] for k in LATE]))
        wts.update(zip(LATE, late))
    w_out = wts["w_out"]
    o_sb, sb_start, sb_count = _sb_fwd(proj, n_heads, 4)
    u_hg, u_sb, y = _proj_merge(o_hg, o_sb, wts["w_proj_hg"], wts["w_proj_sb"], proj, vec["b_gate"], gate_col)
    r2, h2, h2b = _residual_ln("out_proj", y, False, w_out.reshape(2, d // 2, d), h1, vec["ln2_g"], vec["ln2_b"], 1.0)
    a2, b2, s2 = _ffn_up("ffn2_up", h2b,wts["ffn2_w_gate"], wts["ffn2_w_up"])
    r3, _, _ = _residual_ln("ffn2_down", s2, True, wts["ffn2_w_down"], h2, vec["ln3_g"], vec["ln3_b"], 0.5)

    dr3, dr3b, dg3, db3, loss = _ln_bwd("ln3_bwd", r3, vec["ln3_g"], 0.5, beta=vec["ln3_b"], target=target, first_row=BLOCK)
    dh2, dwg2, dwu2, dwd2 = _ffn_bwd("ffn2", dr3b, dr3, h2b, a2, b2, s2, wts["ffn2_w_gate"], wts["ffn2_w_up"],
                                     wts["ffn2_w_down"], exchange)
    dr2, dr2b, dg2, db2 = _ln_bwd("ln2_bwd", r2, vec["ln2_g"], 1.0, dy=dh2)
    du_hg, du_sb, dz_hg, dz_sb, dbg = _merge_bwd(dr2b, w_out.reshape(d, d), proj, vec["b_gate"], u_hg, u_sb, gate_col, d // nd)
    dw_out = _grad_w_rows("dw_out", y, dr2b, nd)
    dp_hg = _grad_w("dp_hg", o_hg, du_hg, nd)
    dp_sb = _grad_w("dp_sb", o_sb, du_sb, nd)
    do_hg = _grad_in("do_hg", du_hg, wts["w_proj_hg"])
    do_sb = _grad_in("do_sb", du_sb, wts["w_proj_sb"])
    hg = _hgrn_bwd(proj, vec["hg_lb_logits"], vec["hg_norm_g"], o_raw, do_hg, states, n_heads,
                   exchange([dw_out, dp_hg, dp_sb]) if exchange else None)
    if exchange:
        hg, (dw_out, dp_hg, dp_sb) = hg
    dhq, dhf, dhi, dhog, dgn, dlb = hg
    dsq, dsk, dsv = _sb_bwd(proj, do_sb, sb_start, sb_count, n_heads, 4)
    dproj = jnp.concatenate([dhq, dhf, dhi, dhog, dsq, dsk, dsv, dz_hg, dz_sb], axis=1)
    dw_in = _grad_w("dw_in", h1b, dproj, nd)
    dh1 = _grad_in("dh1", dproj, wts["w_in"], add=dr2, carried=exchange([dw_in]) if exchange else None)
    if exchange:
        dh1, (dw_in,) = dh1
    dr1, dr1b, dg1, db1 = _ln_bwd("ln1_bwd", r1, vec["ln1_g"], 0.5, dy=dh1)
    dh0, dwg1, dwu1, dwd1 = _ffn_bwd("ffn1", dr1b, dr1, h0b, a1, b1, s1, wts["ffn1_w_gate"], wts["ffn1_w_up"],
                                     wts["ffn1_w_down"], exchange)

    small = {"ln1_g": dg1, "ln1_b": db1, "ln2_g": dg2, "ln2_b": db2, "ln3_g": dg3, "ln3_b": db3,
             "b_gate": dbg, "hg_lb": dlb, "hg_norm_g": dgn}
    big = {"ffn1_w_gate": dwg1, "ffn1_w_up": dwu1, "ffn1_w_down": dwd1, "w_in": dw_in, "w_proj_hg": dp_hg,
           "w_proj_sb": dp_sb, "w_out": dw_out, "ffn2_w_gate": dwg2, "ffn2_w_up": dwu2, "ffn2_w_down": dwd2}
    return loss, dh0[BLOCK:], dh0[PAD:BLOCK], small, big


def _position():
    return lax.axis_index("x"), lax.axis_index("y"), lax.axis_index("c")


def _slot(px, py, pc):
    return 4 * px + 2 * py + pc


def _all_gather(shards):
    n = len(shards)

    def body(*refs):
        ins, outs = refs[:n], refs[n:2 * n]
        send_sems, recv_sems, local_sems = refs[2 * n:]
        x, y, c = _position()
        me, sibling = (x, y, c), (x, y, 1 - c)
        chips = [(1 - x, y), (x, 1 - y), (1 - x, 1 - y)]

        def copy(a, k, block, to, src=None):
            dst = outs[a].at[_slot(*block)]
            return pltpu.make_async_remote_copy(src_ref=dst if src is None else src, dst_ref=dst,
                                                send_sem=send_sems.at[a, k], recv_sem=recv_sems.at[a, k],
                                                device_id=to, device_id_type=MESH)

        mine = [pltpu.make_async_copy(ins[a], outs[a].at[_slot(*me)], local_sems.at[a]) for a in range(n)]
        for cp in mine:
            cp.start()
        first = []
        for a in range(n):
            first.append(copy(a, 0, me, sibling, src=ins[a]))
            first += [copy(a, 1 + j, me, (*chip, c), src=ins[a]) for j, chip in enumerate(chips)]
        for cp in first:
            cp.start()
        passed = []
        for j, chip in enumerate(chips):
            for a in range(n):
                copy(a, 1 + j, (*chip, c), me).wait_recv()
                cp = copy(a, 4 + j, (*chip, c), sibling)
                cp.start()
                passed.append(cp)
        for a in range(n):
            copy(a, 0, sibling, me).wait_recv()
        for j, chip in enumerate(chips):
            for a in range(n):
                copy(a, 4 + j, (*chip, 1 - c), me).wait_recv()
        for cp in first + passed:
            cp.wait_send()
        for cp in mine:
            cp.wait()

    return pl.pallas_call(
        body, name="all_gather", out_shape=[_sds((N_DEV,) + s.shape, s.dtype) for s in shards],
        in_specs=[ANY] * n, out_specs=[ANY] * n,
        scratch_shapes=[pltpu.SemaphoreType.DMA((n, 7)), pltpu.SemaphoreType.DMA((n, 7)), pltpu.SemaphoreType.DMA((n,))],
    )(*shards)


def _exchange_carried(grads):
    return _direct_copies(grads, [_sds(g.shape, g.dtype) for g in grads], lambda ref, slot: ref.at[slot])


def _gather_carried(shards):
    return _direct_copies(shards, [_sds((N_DEV,) + s.shape, s.dtype) for s in shards], lambda ref, slot: ref)


def _direct_copies(arrays, outs, block_for):
    n = len(arrays)

    def plan(ins, results, sems):
        send_sems, recv_sems, local_sems = sems
        x, y, c = _position()
        mine = _slot(x, y, c)
        peers = [(1 - x if k & 4 else x, 1 - y if k & 2 else y, 1 - c if k & 1 else c) for k in range(1, N_DEV)]
        own = [pltpu.make_async_copy(block_for(ins[a], mine), results[a].at[mine], local_sems.at[a]) for a in range(n)]
        sent = [pltpu.make_async_remote_copy(src_ref=block_for(ins[a], _slot(*peer)), dst_ref=results[a].at[mine],
                                             send_sem=send_sems.at[a, k], recv_sem=recv_sems.at[a, k],
                                             device_id=peer, device_id_type=MESH)
                for a in range(n) for k, peer in enumerate(peers)]
        landed = [pltpu.make_async_remote_copy(src_ref=block_for(ins[a], mine), dst_ref=results[a].at[_slot(*peer)],
                                               send_sem=send_sems.at[a, k], recv_sem=recv_sems.at[a, k],
                                               device_id=peer, device_id_type=MESH)
                  for a in range(n) for k, peer in enumerate(peers)]
        return own, sent, landed

    def start(ins, results, sems):
        own, sent, _ = plan(ins, results, sems)
        for cp in own + sent:
            cp.start()

    def finish(ins, results, sems):
        own, sent, landed = plan(ins, results, sems)
        for cp in landed:
            cp.wait_recv()
        for cp in sent:
            cp.wait_send()
        for cp in own:
            cp.wait()

    sems = [pltpu.SemaphoreType.DMA((n, 7)), pltpu.SemaphoreType.DMA((n, 7)), pltpu.SemaphoreType.DMA((n,))]
    return _Carried(list(arrays), outs, sems, start, finish)


def _all_reduce_rows(v):
    rows = v.shape[0]

    def body(v_ref, out_ref, buf, send_sems, recv_sems):
        x, y, c = _position()
        me, sibling = (x, y, c), (x, y, 1 - c)
        chips = [(1 - x, y), (x, 1 - y), (1 - x, 1 - y)]

        def copy(k, block, to, src=None):
            dst = buf.at[_slot(*block)]
            return pltpu.make_async_remote_copy(src_ref=dst if src is None else src, dst_ref=dst,
                                                send_sem=send_sems.at[k], recv_sem=recv_sems.at[k],
                                                device_id=to, device_id_type=MESH)

        first = [copy(0, me, sibling, src=v_ref)] + [copy(1 + j, me, (*chip, c), src=v_ref) for j, chip in enumerate(chips)]
        for cp in first:
            cp.start()
        buf[_slot(*me)] = v_ref[...]
        passed = [copy(4 + j, (*chip, c), sibling) for j, chip in enumerate(chips)]
        for j, chip in enumerate(chips):
            copy(1 + j, (*chip, c), me).wait_recv()
            passed[j].start()
        copy(0, sibling, me).wait_recv()
        for j, chip in enumerate(chips):
            copy(4 + j, (*chip, 1 - c), me).wait_recv()
        for cp in first + passed:
            cp.wait_send()
        total = buf[0]
        for s in range(1, N_DEV):
            total = total + buf[s]
        out_ref[...] = total

    vmem = pl.BlockSpec(memory_space=pltpu.VMEM)
    return pl.pallas_call(
        body, name="small_all_reduce", out_shape=_sds(v.shape, F32), in_specs=[vmem], out_specs=vmem,
        scratch_shapes=[pltpu.VMEM((N_DEV, rows, 128), F32), pltpu.SemaphoreType.DMA((7,)), pltpu.SemaphoreType.DMA((7,))],
    )(v)


def _adamw(name, w, m, v, contrib):
    r, c = w.shape
    n = contrib.shape[0]
    tr = _tile(r, 256)

    def body(w_ref, m_ref, v_ref, c_ref, g_out, d_out, m_out, v_out):
        g = c_ref[0].astype(F32)
        for s in range(1, n):
            g = g + c_ref[s].astype(F32)
        m2 = ADAM_B1 * m_ref[...] + (1.0 - ADAM_B1) * g
        v2 = ADAM_B2 * v_ref[...] + (1.0 - ADAM_B2) * (g * g)
        m_hat = m2 / (1.0 - ADAM_B1 ** ADAM_STEP)
        v_hat = v2 / (1.0 - ADAM_B2 ** ADAM_STEP)
        g_out[...] = g
        d_out[...] = -ADAM_LR * (m_hat / (jnp.sqrt(v_hat) + ADAM_EPS) + ADAM_WD * w_ref[...])
        m_out[...] = m2
        v_out[...] = v2

    tile = pl.BlockSpec((tr, c), lambda i: (i, 0))
    return pl.pallas_call(
        body, name=name, grid=(r // tr,), in_specs=[tile, tile, tile, pl.BlockSpec((n, tr, c), lambda i: (0, i, 0))],
        out_specs=[tile] * 4, out_shape=[_sds((r, c), F32)] * 4, compiler_params=_params(1),
    )(w, m, v, contrib)


def _lb_logits_grad(logits, dlb):
    def body(lg_ref, d_ref, out_ref):
        lg = lg_ref[...]
        mx = jnp.maximum(lg[0:1], lg[1:2])
        e0 = jnp.exp(lg[0:1] - mx)
        p0 = e0 / (e0 + jnp.exp(lg[1:2] - mx))
        g0 = d_ref[...] * p0 * (1.0 - p0)
        out_ref[0:1, :] = g0
        out_ref[1:2, :] = -g0

    return pl.pallas_call(body, name="lb_logits_grad", out_shape=_sds(logits.shape, F32))(logits, dlb)


BIG = ("ffn1_w_gate", "ffn1_w_up", "ffn1_w_down", "w_in", "w_proj_hg", "w_proj_sb", "w_out",
       "ffn2_w_gate", "ffn2_w_up", "ffn2_w_down")
VECTORS = ("ln1_g", "ln1_b", "b_gate", "hg_lb_logits", "hg_norm_g", "ln2_g", "ln2_b", "ln3_g", "ln3_b")
WEIGHTS = ("meta", "ln1_g", "ln1_b", "ffn1_w_gate", "ffn1_w_up", "ffn1_w_down", "w_in", "b_gate", "hg_lb_logits",
           "hg_norm_g", "w_proj_hg", "w_proj_sb", "w_out", "ln2_g", "ln2_b", "ffn2_w_gate", "ffn2_w_up",
           "ffn2_w_down", "ln3_g", "ln3_b")


def kernel(x, meta, ln1_g, ln1_b, ffn1_w_gate, ffn1_w_up, ffn1_w_down, w_in, b_gate, hg_lb_logits, hg_norm_g, w_proj_hg, w_proj_sb, w_out, ln2_g, ln2_b, ffn2_w_gate, ffn2_w_up, ffn2_w_down, ln3_g, ln3_b, loss_target, m_meta, m_ln1_g, m_ln1_b, m_ffn1_w_gate, m_ffn1_w_up, m_ffn1_w_down, m_w_in, m_b_gate, m_hg_lb_logits, m_hg_norm_g, m_w_proj_hg, m_w_proj_sb, m_w_out, m_ln2_g, m_ln2_b, m_ffn2_w_gate, m_ffn2_w_up, m_ffn2_w_down, m_ln3_g, m_ln3_b, v_meta, v_ln1_g, v_ln1_b, v_ffn1_w_gate, v_ffn1_w_up, v_ffn1_w_down, v_w_in, v_b_gate, v_hg_lb_logits, v_hg_norm_g, v_w_proj_hg, v_w_proj_sb, v_w_out, v_ln2_g, v_ln2_b, v_ffn2_w_gate, v_ffn2_w_up, v_ffn2_w_down, v_ln3_g, v_ln3_b):
    given = dict(locals())
    d = x.shape[-1]
    ds = meta.shape[1]

    shards = {k: given[k][0].astype(BF16) for k in BIG}
    first = ("ffn1_w_gate", "ffn1_w_up")
    gathered = _all_gather([meta] + [shards.pop(k) for k in first])
    meta_full = gathered[0].transpose(1, 0, 2).reshape(N_META, d)
    vec = {k: given[k] for k in VECTORS}
    loss, grad_x, dmeta, small, received = _local_step(x[0], loss_target[0], meta_full, vec, dict(zip(first, gathered[1:])),
                                                       shards)

    order =("ln1_g", "ln1_b", "ln2_g", "ln2_b", "ln3_g", "ln3_b", "b_gate", "hg_lb", "hg_norm_g")
    parts = [small[k].reshape(-1, 128) for k in order] + [dmeta.reshape(-1, 128), jnp.broadcast_to(loss, (8, 128))]
    total = _all_reduce_rows(jnp.concatenate(parts, axis=0))
    reduced, row = {}, 0
    for k, p in zip(order + ("meta", "loss"), parts):
        reduced[k] = total[row:row + p.shape[0]]
        row += p.shape[0]
    loss_out = reduced["loss"][0, 0]
    me = _slot(*_position())
    dmeta_mine = lax.dynamic_slice(reduced["meta"].reshape(N_META, d), (0, me * ds), (N_META, ds))
    dlogits = _lb_logits_grad(hg_lb_logits, reduced["hg_lb"].reshape(1, -1))

    grads, deltas, new_m, new_v = {}, {}, {}, {}
    for k in WEIGHTS:
        w = given[k]
        lead = w.shape[:-2]
        w2, m2, v2 = (a.reshape(a.shape[-2:]) for a in (w, given["m_" + k], given["v_" + k]))
        if k in BIG:
            contrib = received[k]
        elif k == "meta":
            contrib = dmeta_mine[None]
        elif k == "hg_lb_logits":
            contrib = dlogits[None]
        else:
            contrib = reduced[k].reshape((1,) + w2.shape)
        out = _adamw("adamw_" + k, w2, m2, v2, contrib)
        grads[k], deltas[k], new_m[k], new_v[k] = (o.reshape(lead + o.shape) for o in out)
    return (loss_out, grad_x[None], *[grads[k] for k in WEIGHTS], *[deltas[k] for k in WEIGHTS],
            *[new_m[k] for k in WEIGHTS], *[new_v[k] for k in WEIGHTS])
```

```python
import functools
import math

import jax
import jax.numpy as jnp
from jax import lax
from jax.experimental import pallas as pl
from jax.experimental.pallas import tpu as pltpu

F32 = jnp.float32
BF16 = jnp.bfloat16
MESH = pl.DeviceIdType.MESH

N_DEV = 8
N_META = 16
BLOCK = 128
PAD = BLOCK - N_META
HEAD = 128
CHUNK = 16
LN_EPS = 1e-5
RMS_EPS = 1e-6
DN_ALPHA = 2.0 ** 0.25
ADAM_LR, ADAM_B1, ADAM_B2, ADAM_EPS, ADAM_WD, ADAM_STEP = 0.001, 0.9, 0.999, 1e-08, 0.01, 10

VMEM_LIMIT_V7X = 60 * 1024 * 1024
ROW_TILE = 640
LN_ROW_TILE = 320
GRAD_ROW_TILE = 640

NN = (((1,), (0,)), ((), ()))
NT = (((1,), (1,)), ((), ()))
TN = (((0,), (0,)), ((), ()))


def _tile(n, pref, mult=16):
    best = None
    for t in range(mult, min(n, pref) + 1, mult):
        if n % t == 0:
            best = t
    return n if best is None else best


def _params(n_axes):
    return pltpu.CompilerParams(dimension_semantics=("arbitrary",) * n_axes, vmem_limit_bytes=VMEM_LIMIT_V7X)


def _sigmoid(x):
    return 1.0 / (1.0 + jnp.exp(-x))


class _Carried:
    def __init__(self, ins, outs, sems, start, finish):
        self.ins, self.outs, self.sems, self.start, self.finish = ins, outs, sems, start, finish


ANY = pl.BlockSpec(memory_space=pl.ANY)


def _pallas(name, body, grid, in_specs, out_specs, out_shape, scratch, operands, carried=None):
    if carried is None:
        return pl.pallas_call(body, name=name, grid=grid, in_specs=in_specs, out_specs=out_specs, out_shape=out_shape,
                              scratch_shapes=scratch, compiler_params=_params(len(grid)))(*operands)
    n_in, n_out, n_scr = len(in_specs), len(out_specs), len(scratch)
    c_in, c_out = len(carried.ins), len(carried.outs)

    def wrapped(*refs):
        ins, rest = refs[:n_in], refs[n_in:]
        c_ins, rest = rest[:c_in], rest[c_in:]
        outs, rest = rest[:n_out], rest[n_out:]
        c_outs, rest = rest[:c_out], rest[c_out:]
        scr, c_sems = rest[:n_scr], rest[n_scr:]
        first = last = None
        for axis, size in enumerate(grid):
            at0, at_end = pl.program_id(axis) == 0, pl.program_id(axis) == size - 1
            first = at0 if first is None else first & at0
            last = at_end if last is None else last & at_end

        @pl.when(first)
        def _():
            carried.start(c_ins, c_outs, c_sems)

        body(*ins, *outs, *scr)

        @pl.when(last)
        def _():
            carried.finish(c_ins, c_outs, c_sems)

    res = pl.pallas_call(
        wrapped, name=name, grid=grid, in_specs=list(in_specs) + [ANY] * c_in, out_specs=list(out_specs) + [ANY] * c_out,
        out_shape=list(out_shape) + list(carried.outs), scratch_shapes=list(scratch) + list(carried.sems),
        compiler_params=_params(len(grid)),
    )(*operands, *carried.ins)
    return res[:n_out], res[n_out:]


def _gemm(name, grid, pairs, acc_of, acc_shapes, dims, extras, outs, epilogue, carried=None):
    n_extra, n_out = len(extras), len(outs)
    nk = grid[-1]
    k_axis = len(grid) - 1
    operands, in_specs, where = [], [], {}
    for a, a_spec, b, b_spec in pairs:
        for arr, spec in ((a, a_spec), (b, b_spec)):
            if (id(arr), id(spec)) not in where:
                where[(id(arr), id(spec))] = len(operands)
                operands.append(arr)
                in_specs.append(spec)
    n_mat = len(operands)
    slots = [(where[(id(a), id(a_spec))], where[(id(b), id(b_spec))]) for a, a_spec, b, b_spec in pairs]

    def body(*refs):
        er = refs[n_mat:n_mat + n_extra]
        orf = refs[n_mat + n_extra:n_mat + n_extra + n_out]
        accs = refs[n_mat + n_extra + n_out:]

        def part(p):
            a_ref, b_ref = refs[slots[p][0]], refs[slots[p][1]]
            if len(a_ref.shape) == 2:
                return lax.dot_general(a_ref[...].astype(BF16), b_ref[...].astype(BF16), dims, preferred_element_type=F32)
            total = None
            for s in range(a_ref.shape[0]):
                d = lax.dot_general(a_ref[s].astype(BF16), b_ref[s].astype(BF16), dims, preferred_element_type=F32)
                total = d if total is None else total + d
            return total

        if nk == 1:
            vals = [None] * len(acc_shapes)
            for p in range(len(pairs)):
                d = part(p)
                vals[acc_of[p]] = d if vals[acc_of[p]] is None else vals[acc_of[p]] + d
            epilogue(vals, er, orf)
        else:
            k = pl.program_id(k_axis)

            @pl.when(k == 0)
            def _():
                for acc in accs:
                    acc[...] = jnp.zeros_like(acc)

            for p in range(len(pairs)):
                accs[acc_of[p]][...] += part(p)

            @pl.when(k == nk - 1)
            def _():
                epilogue([acc[...] for acc in accs], er, orf)

    for e, e_spec in extras:
        operands.append(e)
        in_specs.append(e_spec)
    scratch = [] if nk == 1 else [pltpu.VMEM(s, F32) for s in acc_shapes]
    return _pallas(name, body, grid, in_specs, [s for _, s in outs], [o for o, _ in outs], scratch, operands, carried)


def _sds(shape, dtype):
    return jax.ShapeDtypeStruct(shape, dtype)


def _ln_rows(r, g, b):
    mu = jnp.mean(r, axis=-1, keepdims=True)
    xc = r - mu
    var = jnp.mean(xc * xc, axis=-1, keepdims=True)
    return xc * lax.rsqrt(var + LN_EPS) * g + b


def _ffn_up(name, hb, wg, wu, carried=None):
    m, d = hb.shape
    nd, _, fs = wg.shape
    tm = _tile(m, ROW_TILE)

    def epi(acc, er, orf):
        a, b = acc
        orf[0][...] = a
        orf[1][...] = b
        orf[2][...] = (a * _sigmoid(a) * b).astype(BF16)

    h_spec = pl.BlockSpec((tm, d), lambda i, j, k: (i, 0))
    w_spec = pl.BlockSpec((None, d, fs), lambda i, j, k: (j, 0, 0))
    o_spec = pl.BlockSpec((None, tm, fs), lambda i, j, k: (j, i, 0))
    return _gemm(name, (m // tm, nd, 1), [(hb, h_spec, wg, w_spec), (hb, h_spec, wu, w_spec)], [0, 1],
                 [(tm, fs)] * 2, NN, [],
                 [(_sds((nd, m, fs), F32), o_spec), (_sds((nd, m, fs), F32), o_spec), (_sds((nd, m, fs), BF16), o_spec)], epi,
                 carried)


def _residual_ln(name, a, a_stacked, w, h_in, g, beta, scale, carried=None):
    d = w.shape[-1]
    m = h_in.shape[0]
    tm = _tile(m, LN_ROW_TILE)

    def epi(acc, er, orf):
        r = DN_ALPHA * er[0][...] + scale * acc[0]
        h = _ln_rows(r, er[1][...], er[2][...])
        orf[0][...] = r
        orf[1][...] = h
        orf[2][...] = h.astype(BF16)

    once = pl.Buffered(1)
    if a_stacked:
        a_spec = pl.BlockSpec((a.shape[0], tm, a.shape[2]), lambda i, k: (0, i, 0))
        w_spec = pl.BlockSpec(w.shape, lambda i, k: (0, 0, 0), pipeline_mode=once)
    else:
        a_spec = pl.BlockSpec((tm, a.shape[1]), lambda i, k: (i, 0))
        w_spec = pl.BlockSpec(w.shape, lambda i, k: (0, 0), pipeline_mode=once)
    row = pl.BlockSpec((tm, d), lambda i, k: (i, 0))
    vec = pl.BlockSpec((1, d), lambda i, k: (0, 0))
    return _gemm(name, (m // tm, 1), [(a, a_spec, w, w_spec)], [0], [(tm, d)], NN,
                 [(h_in, row), (g, vec), (beta, vec)],
                 [(_sds((m, d), F32), row), (_sds((m, d), F32), row), (_sds((m, d), BF16), row)], epi, carried)


def _in_proj(hb, w_in):
    m, d = hb.shape
    nd, _, cs = w_in.shape
    tm = _tile(m, ROW_TILE)

    def epi(acc, er, orf):
        orf[0][...] = acc[0]

    return _gemm("in_proj", (m // tm, nd, 1),
                 [(hb, pl.BlockSpec((tm, d), lambda i, j, k: (i, 0)), w_in, pl.BlockSpec((None, d, cs), lambda i, j, k: (j, 0, 0)))],
                 [0], [(tm, cs)], NN, [], [(_sds((m, nd * cs), F32), pl.BlockSpec((tm, cs), lambda i, j, k: (i, j)))], epi)[0]


def _proj_merge(o_hg, o_sb, p_hg, p_sb, proj, b_gate, gate_col):
    m, w = o_hg.shape
    nd, _, ds = p_hg.shape
    d = nd * ds
    tm = _tile(m, ROW_TILE)
    c0 = gate_col // ds

    def epi(acc, er, orf):
        u_hg, u_sb = acc
        g_hg = _sigmoid(er[0][...] + er[2][...])
        g_sb = _sigmoid(er[1][...] + er[3][...])
        orf[0][...] = u_hg
        orf[1][...] = u_sb
        orf[2][...] = (g_hg * u_hg + g_sb * u_sb).astype(BF16)

    o_spec = pl.BlockSpec((tm, w), lambda i, j, k: (i, 0))
    p_spec = pl.BlockSpec((None, w, ds), lambda i, j, k: (j, 0, 0))
    out = pl.BlockSpec((tm, ds), lambda i, j, k: (i, j))
    return _gemm("proj_merge", (m // tm, nd, 1), [(o_hg, o_spec, p_hg, p_spec), (o_sb, o_spec, p_sb, p_spec)], [0, 1],
                 [(tm, ds)] * 2, NN,
                 [(proj, pl.BlockSpec((tm, ds), lambda i, j, k: (i, c0 + j))),
                  (proj, pl.BlockSpec((tm, ds), lambda i, j, k: (i, c0 + nd + j))),
                  (b_gate, pl.BlockSpec((1, ds), lambda i, j, k: (0, j))),
                  (b_gate, pl.BlockSpec((1, ds), lambda i, j, k: (0, nd + j)))],
                 [(_sds((m, d), F32), out), (_sds((m, d), F32), out), (_sds((m, d), BF16), out)], epi)


def _ln_bwd(name, r, g, out_scale, dy=None, beta=None, target=None, first_row=0):
    m, d = r.shape
    tm = _tile(m, LN_ROW_TILE if target is None else BLOCK)
    with_loss = target is not None
    skip = first_row // tm if with_loss else 0
    assert not with_loss or first_row % tm == 0

    def body(*refs):
        if with_loss:
            r_ref, g_ref, b_ref, t_ref, dr_ref, drb_ref, dg_ref, db_ref, loss_ref = refs
        else:
            r_ref, g_ref, dy_ref, dr_ref, drb_ref, dg_ref, db_ref = refs
        i = pl.program_id(0)
        x = r_ref[...]
        mu = jnp.mean(x, axis=-1, keepdims=True)
        xc = x - mu
        var = jnp.mean(xc * xc, axis=-1, keepdims=True)
        rstd = lax.rsqrt(var + LN_EPS)
        xhat = xc * rstd
        gv = g_ref[...]
        if with_loss:
            err = xhat * gv + b_ref[...] - t_ref[...]
            live = (i >= skip).astype(F32)
            dyv = err * (live / d)
            part = 0.5 * live * jnp.sum(jnp.sum(err * err, axis=-1, keepdims=True), axis=0, keepdims=True) / d
        else:
            dyv = dy_ref[...]
        dxh = dyv * gv
        m1 = jnp.mean(dxh, axis=-1, keepdims=True)
        m2 = jnp.mean(dxh * xhat, axis=-1, keepdims=True)
        dr = rstd * (dxh - m1 - xhat * m2)
        dr_ref[...] = dr
        drb_ref[...] = (out_scale * dr).astype(BF16)

        @pl.when(i == 0)
        def _():
            dg_ref[...] = jnp.zeros_like(dg_ref)
            db_ref[...] = jnp.zeros_like(db_ref)
            if with_loss:
                loss_ref[...] = jnp.zeros_like(loss_ref)

        dg_ref[...] += jnp.sum(dyv * xhat, axis=0, keepdims=True)
        db_ref[...] += jnp.sum(dyv, axis=0, keepdims=True)
        if with_loss:
            loss_ref[...] += jnp.broadcast_to(part, loss_ref.shape)

    row = pl.BlockSpec((tm, d), lambda i: (i, 0))
    vec = pl.BlockSpec((1, d), lambda i: (0, 0))
    out_shape = [_sds((m, d), F32), _sds((m, d), BF16), _sds((1, d), F32), _sds((1, d), F32)]
    out_specs = [row, row, vec, vec]
    if with_loss:
        operands = [r, g, beta, target]
        in_specs = [row, vec, vec, pl.BlockSpec((tm, d), lambda i: (jnp.maximum(i - skip, 0), 0))]
        out_shape.append(_sds((1, BLOCK), F32))
        out_specs.append(pl.BlockSpec((1, BLOCK), lambda i: (0, 0)))
    else:
        operands = [r, g, dy]
        in_specs = [row, vec, row]
    return pl.pallas_call(body, name=name, grid=(m // tm,), in_specs=in_specs, out_specs=out_specs, out_shape=out_shape,
                          compiler_params=_params(1))(*operands)


def _ffn_bwd(tag, drb, dr, hb, a, b, s, wg, wu, wd, exchange=None):
    m, d = drb.shape
    nd, _, fs = wg.shape
    tm = _tile(m, ROW_TILE)

    def epi_ds(acc, er, orf):
        ds = acc[0]
        av, bv = er[0][...], er[1][...]
        sg = _sigmoid(av)
        orf[0][...] = (ds * bv * sg * (1.0 + av * (1.0 - sg))).astype(BF16)
        orf[1][...] = (ds * av * sg).astype(BF16)

    st = pl.BlockSpec((None, tm, fs), lambda i, j, k: (j, i, 0))
    da, db = _gemm(tag + "_ds", (m // tm, nd, 1),
                   [(drb, pl.BlockSpec((tm, d), lambda i, j, k: (i, 0)), wd, pl.BlockSpec((None, fs, d), lambda i, j, k: (j, 0, 0)))],
                   [0], [(tm, fs)], NT, [(a, st), (b, st)],
                   [(_sds((nd, m, fs), BF16), st), (_sds((nd, m, fs), BF16), st)], epi_ds)

    def epi_w(acc, er, orf):
        for o, v in zip(orf, acc):
            o[...] = v.astype(BF16)

    tr = _tile(m, GRAD_ROW_TILE)
    nkm = m // tr
    dwd = _gemm(tag + "_dwd", (nd, nkm),
                [(s, pl.BlockSpec((None, tr, fs), lambda j, k: (j, k, 0)), drb, pl.BlockSpec((tr, d), lambda j, k: (k, 0)))],
                [0], [(fs, d)], TN, [], [(_sds((nd, fs, d), BF16), pl.BlockSpec((None, fs, d), lambda j, k: (j, 0, 0)))], epi_w)[0]
    h_spec = pl.BlockSpec((tr, d), lambda j, k: (k, 0))
    g_spec = pl.BlockSpec((None, tr, fs), lambda j, k: (j, k, 0))
    w_out = pl.BlockSpec((None, d, fs), lambda j, k: (j, 0, 0))
    dwgu = _gemm(tag + "_dwgu", (nd, nkm), [(hb, h_spec, da, g_spec), (hb, h_spec, db, g_spec)], [0, 1],
                 [(d, fs)] * 2, TN, [], [(_sds((nd, d, fs), BF16), w_out), (_sds((nd, d, fs), BF16), w_out)], epi_w,
                 exchange([dwd]) if exchange else None)
    if exchange:
        (dwg, dwu), (dwd,) = dwgu
    else:
        dwg, dwu = dwgu

    def epi_dh(acc, er, orf):
        orf[0][...] = DN_ALPHA * er[0][...] + acc[0]

    gk = pl.BlockSpec((None, tm, fs), lambda i, k: (k, i, 0))
    wk = pl.BlockSpec((None, d, fs), lambda i, k: (k, 0, 0))
    row = pl.BlockSpec((tm, d), lambda i, k: (i, 0))
    dh = _gemm(tag + "_dh", (m // tm, nd), [(da, gk, wg, wk), (db, gk, wu, wk)], [0, 0], [(tm, d)], NT,
               [(dr, row)], [(_sds((m, d), F32), row)], epi_dh, exchange([dwg, dwu]) if exchange else None)
    if exchange:
        (dh,), (dwg, dwu) = dh
    else:
        dh = dh[0]
    return dh, dwg, dwu, dwd


def _merge_bwd(dmixb, w_out2, proj, b_gate, u_hg, u_sb, gate_col, ds):
    m, d = dmixb.shape
    nd = d // ds
    tm = _tile(m, ROW_TILE)
    c0 = gate_col // ds

    def epi(acc, er, orf):
        i = pl.program_id(1)
        dy = acc[0]
        g_hg = _sigmoid(er[0][...] + er[2][...])
        g_sb = _sigmoid(er[1][...] + er[3][...])
        orf[0][...] = (dy * g_hg).astype(BF16)
        orf[1][...] = (dy * g_sb).astype(BF16)
        dz_hg = dy * er[4][...] * g_hg * (1.0 - g_hg)
        dz_sb = dy * er[5][...] * g_sb * (1.0 - g_sb)
        orf[2][...] = dz_hg.astype(BF16)
        orf[3][...] = dz_sb.astype(BF16)

        @pl.when(i == 0)
        def _():
            orf[4][...] = jnp.zeros_like(orf[4])
            orf[5][...] = jnp.zeros_like(orf[5])

        orf[4][...] += jnp.sum(dz_hg, axis=0, keepdims=True)
        orf[5][...] += jnp.sum(dz_sb, axis=0, keepdims=True)

    tile = pl.BlockSpec((tm, ds), lambda j, i, k: (i, j))
    vec = pl.BlockSpec((1, ds), lambda j, i, k: (0, j))
    du_hg, du_sb, dz_hg, dz_sb, db_hg, db_sb = _gemm(
        "merge_bwd", (nd, m // tm, 1),
        [(dmixb, pl.BlockSpec((tm, d), lambda j, i, k: (i, 0)), w_out2, pl.BlockSpec((ds, d), lambda j, i, k: (j, 0)))],
        [0], [(tm, ds)], NT,
        [(proj, pl.BlockSpec((tm, ds), lambda j, i, k: (i, c0 + j))),
         (proj, pl.BlockSpec((tm, ds), lambda j, i, k: (i, c0 + nd + j))),
         (b_gate, vec), (b_gate, pl.BlockSpec((1, ds), lambda j, i, k: (0, nd + j))),
         (u_hg, tile), (u_sb, tile)],
        [(_sds((m, d), BF16), tile), (_sds((m, d), BF16), tile), (_sds((m, d), BF16), tile), (_sds((m, d), BF16), tile),
         (_sds((1, d), F32), vec), (_sds((1, d), F32), vec)], epi)
    return du_hg, du_sb, dz_hg, dz_sb, jnp.concatenate([db_hg, db_sb], axis=1)


def _grad_w(name, x, dy, nd_out):
    m, kx = x.shape
    n = dy.shape[1]
    ns = n // nd_out
    tm = _tile(m, GRAD_ROW_TILE)

    def epi(acc, er, orf):
        orf[0][...] = acc[0].astype(BF16)

    return _gemm(name, (nd_out, m // tm),
                 [(x, pl.BlockSpec((tm, kx), lambda j, k: (k, 0)), dy, pl.BlockSpec((tm, ns), lambda j, k: (k, j)))],
                 [0], [(kx, ns)], TN, [], [(_sds((nd_out, kx, ns), BF16), pl.BlockSpec((None, kx, ns), lambda j, k: (j, 0, 0)))], epi)[0]


def _grad_in(name, dy, w, add=None, carried=None):
    m = dy.shape[0]
    nd, kx, ns = w.shape
    tm = _tile(m, ROW_TILE)

    def epi(acc, er, orf):
        orf[0][...] = acc[0] if add is None else DN_ALPHA * er[0][...] + acc[0]

    row = pl.BlockSpec((tm, kx), lambda i, k: (i, 0))
    res = _gemm(name, (m // tm, nd),
                [(dy, pl.BlockSpec((tm, ns), lambda i, k: (i, k)), w, pl.BlockSpec((None, kx, ns), lambda i, k: (k, 0, 0)))],
                [0], [(tm, kx)], NT, [] if add is None else [(add, row)], [(_sds((m, kx), F32), row)], epi, carried)
    return res[0] if carried is None else (res[0][0], res[1])


def _tri(n, kind):
    r = lax.broadcasted_iota(jnp.int32, (n, n), 0)
    c = lax.broadcasted_iota(jnp.int32, (n, n), 1)
    return {"le": c <= r, "ge": c >= r, "gt": r > c, "lt": r < c}[kind]


def _dot_f32(a, b, dims=NN):
    return lax.dot_general(a, b, dims, preferred_element_type=F32, precision=lax.Precision.HIGHEST)


def _hgrn_gates(i, hq, hf, logits):
    lg = logits
    mx = jnp.maximum(lg[0:1], lg[1:2])
    e0 = jnp.exp(lg[0:1] - mx)
    lb = e0 / (e0 + jnp.exp(lg[1:2] - mx))
    sig = _sigmoid(hf)
    f = lb + (1.0 - lb) * sig
    valid = (i * BLOCK + lax.broadcasted_iota(jnp.int32, hf.shape, 0)) >= PAD
    g = jnp.where(valid, jnp.log(f), 0.0)
    k = jnp.where(valid, 1.0 - f, 0.0)
    sq = _sigmoid(hq)
    return hq * sq, k, g, sig, f, lb, valid, sq


def _pair_mask():
    s_i = lax.broadcasted_iota(jnp.int32, (CHUNK, CHUNK, 1), 0)
    t_i = lax.broadcasted_iota(jnp.int32, (CHUNK, CHUNK, 1), 1)
    return t_i >= s_i


def _heads_per_step(n_heads, want):
    return max(h for h in range(1, want + 1) if n_heads % h == 0)


def _hgrn_fwd(proj, logits, gn, n_heads, carried=None):
    m = proj.shape[0]
    nb = m // BLOCK
    w = n_heads * HEAD
    cpb = BLOCK // CHUNK
    hps = _heads_per_step(n_heads, 4)
    wide = hps * HEAD

    def body(hq_ref, hf_ref, hi_ref, hog_ref, lg_ref, gn_ref, o_ref, ohg_ref, st_all_ref, st_ref, q_s, k_s, v_s, b_s):
        i = pl.program_id(1)

        @pl.when(i == 0)
        def _():
            st_ref[...] = jnp.zeros_like(st_ref)

        q, k, g, _, _, _, _, _ = _hgrn_gates(i, hq_ref[...], hf_ref[...], lg_ref[...])
        q_s[...] = q
        k_s[...] = k
        v_s[...] = hi_ref[...]
        b_s[...] = _dot_f32(_tri(BLOCK, "le").astype(F32), g)
        causal = _pair_mask()

        def chunk(c, carry):
            sl = pl.ds(pl.multiple_of(c * CHUNK, CHUNK), CHUNK)
            prev = pl.ds(pl.multiple_of(jnp.maximum(c - 1, 0) * CHUNK, CHUNK), CHUNK)
            first = (c > 0).astype(F32)
            for hd in range(hps):
                cols = slice(hd * HEAD, (hd + 1) * HEAD)
                b = b_s[sl, cols] - b_s[prev, cols][CHUNK - 1:CHUNK, :] * first
                qc, kc, vc = q_s[sl, cols], k_s[sl, cols], v_s[sl, cols]
                st = st_ref[hd]
                st_all_ref[hd, c] = st.astype(BF16)
                o = lax.dot_general((qc * jnp.exp(b)).astype(BF16), st.astype(BF16), NT, preferred_element_type=F32)
                e = jnp.exp(jnp.minimum(b[None, :, :] - b[:, None, :], 0.0))
                p = jnp.sum(qc[None, :, :] * e * kc[:, None, :], axis=-1, keepdims=True)
                o_ref[sl, cols] = o + jnp.sum(jnp.where(causal, p, 0.0) * vc[:, None, :], axis=0)
                blast = b[CHUNK - 1:CHUNK, :]
                kd = kc * jnp.exp(blast - b)
                st_ref[hd] = st * jnp.exp(blast) + lax.dot_general(vc.astype(BF16), kd.astype(BF16), TN,
                                                                   preferred_element_type=F32)
            return carry

        lax.fori_loop(0, cpb, chunk, 0)
        for hd in range(hps):
            cols = slice(hd * HEAD, (hd + 1) * HEAD)
            o = o_ref[:, cols]
            n = o * lax.rsqrt(jnp.mean(o * o, axis=-1, keepdims=True) + RMS_EPS)
            hog = hog_ref[:, cols]
            ohg_ref[:, cols] = (n * gn_ref[:, cols] * hog * _sigmoid(hog)).astype(BF16)

    def col(group):
        return pl.BlockSpec((BLOCK, wide), lambda h, i: (i, group * (n_heads // hps) + h))

    vec = pl.BlockSpec((1, wide), lambda h, i: (0, h))
    tile = pl.BlockSpec((BLOCK, wide), lambda h, i: (i, h))
    return _pallas(
        "hgrn_fwd", body, (n_heads // hps, nb),
        [col(0), col(1), col(2), col(3), pl.BlockSpec((2, wide), lambda h, i: (0, h)), vec],
        [tile, tile, pl.BlockSpec((hps, cpb, HEAD, HEAD), lambda h, i: (h, i, 0, 0))],
        [_sds((m, w), F32), _sds((m, w), BF16), _sds((n_heads, m // CHUNK, HEAD, HEAD), BF16)],
        [pltpu.VMEM((hps, HEAD, HEAD), F32)] + [pltpu.VMEM((BLOCK, wide), F32)] * 4,
        (proj, proj, proj, proj, logits, gn), carried)


def _hgrn_bwd(proj, logits, gn, o_raw, do_hg, states, n_heads, carried=None):
    m = proj.shape[0]
    nb = m // BLOCK
    w = n_heads * HEAD
    cpb = BLOCK // CHUNK
    last_state = m // CHUNK - 1
    hps = _heads_per_step(n_heads, 2)
    wide = hps * HEAD

    def body(hq_ref, hf_ref, hi_ref, hog_ref, lg_ref, gn_ref, o_ref, do_ref, st_all_ref, st_next_ref,
             dhq_ref, dhf_ref, dhi_ref, dhog_ref, dgn_ref, dlb_ref,
             dst_ref, q_s, k_s, v_s, b_s, do_s, dq_s, dk_s, dv_s, ex_s):
        step = pl.program_id(1)
        i = nb - 1 - step

        @pl.when(step == 0)
        def _():
            dst_ref[...] = jnp.zeros_like(dst_ref)
            dgn_ref[...] = jnp.zeros_like(dgn_ref)
            dlb_ref[...] = jnp.zeros_like(dlb_ref)

        hq = hq_ref[...]
        q, k, g, sig, f, lb, valid, sq = _hgrn_gates(i, hq, hf_ref[...], lg_ref[...])
        q_s[...] = q
        k_s[...] = k
        v_s[...] = hi_ref[...]
        b_s[...] = _dot_f32(_tri(BLOCK, "le").astype(F32), g)

        hog = hog_ref[...]
        sg = _sigmoid(hog)
        sil = hog * sg
        gnv = gn_ref[...]
        dh = do_ref[...]
        dn = dh * gnv * sil
        for hd in range(hps):
            cols = slice(hd * HEAD, (hd + 1) * HEAD)
            o = o_ref[:, cols]
            rs = lax.rsqrt(jnp.mean(o * o, axis=-1, keepdims=True) + RMS_EPS)
            n = o * rs
            ex_s[:, cols] = n
            do_s[:, cols] = rs * (dn[:, cols] - n * jnp.mean(dn[:, cols] * n, axis=-1, keepdims=True))
        n = ex_s[...]
        dhog_ref[...] = (dh * n * gnv * sg * (1.0 + hog * (1.0 - sg))).astype(BF16)
        dgn_ref[...] += jnp.sum(dh * n * sil, axis=0, keepdims=True)
        causal = _pair_mask()

        def chunk(t, st_ends):
            c = cpb - 1 - t
            sl = pl.ds(pl.multiple_of(c * CHUNK, CHUNK), CHUNK)
            prev = pl.ds(pl.multiple_of(jnp.maximum(c - 1, 0) * CHUNK, CHUNK), CHUNK)
            first = (c > 0).astype(F32)
            starts = []
            for hd in range(hps):
                cols = slice(hd * HEAD, (hd + 1) * HEAD)
                b = b_s[sl, cols] - b_s[prev, cols][CHUNK - 1:CHUNK, :] * first
                qc, kc, vc, doc = q_s[sl, cols], k_s[sl, cols], v_s[sl, cols], do_s[sl, cols]
                eb = jnp.exp(b)
                blast = b[CHUNK - 1:CHUNK, :]
                ek = jnp.exp(blast - b)
                dst = dst_ref[hd]
                dstb = dst.astype(BF16)
                docb = doc.astype(BF16)
                st = st_all_ref[hd, c]
                starts.append(st)
                ex_s[sl, cols] = jnp.broadcast_to(jnp.sum(st_ends[hd].astype(F32) * dst, axis=0, keepdims=True),
                                                  (CHUNK, HEAD))
                dq = lax.dot_general(docb, st, NN, preferred_element_type=F32) * eb
                dk = lax.dot_general(vc.astype(BF16), dstb, NN, preferred_element_type=F32) * ek
                dv = lax.dot_general((kc * ek).astype(BF16), dstb, NT, preferred_element_type=F32)
                em = jnp.where(causal, jnp.exp(jnp.minimum(b[None, :, :] - b[:, None, :], 0.0)), 0.0)
                dp = jnp.sum(doc[None, :, :] * vc[:, None, :], axis=-1, keepdims=True)
                qe = qc[None, :, :] * em
                p = jnp.sum(qe * kc[:, None, :], axis=-1, keepdims=True)
                dq_s[sl, cols] = dq + jnp.sum(dp * em * kc[:, None, :], axis=0)
                dk_s[sl, cols] = dk + jnp.sum(dp * qe, axis=1)
                dv_s[sl, cols] = dv + jnp.sum(p * doc[None, :, :], axis=1)
                dst_ref[hd] = dst * jnp.exp(blast) + lax.dot_general(docb, (qc * eb).astype(BF16), TN,
                                                                     preferred_element_type=F32)
            return tuple(starts)

        lax.fori_loop(0, cpb, chunk, tuple(st_next_ref[hd, 0] for hd in range(hps)))
        dq, dk = dq_s[...], dk_s[...]
        r_i = lax.broadcasted_iota(jnp.int32, (BLOCK, BLOCK), 0)
        c_i = lax.broadcasted_iota(jnp.int32, (BLOCK, BLOCK), 1)
        within = ((c_i >= r_i) & (c_i // CHUNK == r_i // CHUNK)).astype(F32)
        rc = _dot_f32(within, q * dq - k * dk) + ex_s[...]
        df =jnp.where(valid, rc / f - dk, 0.0)
        dhf_ref[...] = (df * (1.0 - lb) * sig * (1.0 - sig)).astype(BF16)
        dlb_ref[...] += jnp.sum(df * (1.0 - sig), axis=0, keepdims=True)
        dhq_ref[...] = (dq * sq * (1.0 + hq * (1.0 - sq))).astype(BF16)
        dhi_ref[...] = dv_s[...].astype(BF16)

    def col(group):
        return pl.BlockSpec((BLOCK, wide), lambda h, s: (nb - 1 - s, group * (n_heads // hps) + h))

    vec = pl.BlockSpec((1, wide), lambda h, s: (0, h))
    tile = pl.BlockSpec((BLOCK, wide), lambda h, s: (nb - 1 - s, h))
    nxt = pl.BlockSpec((hps, 1, HEAD, HEAD), lambda h, s: (h, jnp.minimum((nb - s) * cpb, last_state), 0, 0))
    return _pallas(
        "hgrn_bwd", body, (n_heads // hps, nb),
        [col(0), col(1), col(2), col(3), pl.BlockSpec((2, wide), lambda h, s: (0, h)), vec, tile, tile,
         pl.BlockSpec((hps, cpb, HEAD, HEAD), lambda h, s: (h, nb - 1 - s, 0, 0)), nxt],
        [tile, tile, tile, tile, vec, vec],
        [_sds((m, w), BF16)] * 4 + [_sds((1, w), F32)] * 2,
        [pltpu.VMEM((hps, HEAD, HEAD), F32)] + [pltpu.VMEM((BLOCK, wide), F32)] * 9,
        (proj, proj, proj, proj, logits, gn, o_raw, do_hg, states, states), carried)


def _split_dot(x, t):
    hi = x.astype(BF16)
    lo = (x - hi.astype(F32)).astype(BF16)
    return jnp.dot(hi, t, preferred_element_type=F32) + jnp.dot(lo, t, preferred_element_type=F32)


def _sb_scores(q, kj, i, j, scale):
    z = lax.dot_general(q, kj, NT, preferred_element_type=F32) * scale
    lp = jnp.log(1.0 + jnp.exp(-jnp.abs(z)))
    lbeta = jnp.minimum(z, 0.0) - lp
    qpos = i * BLOCK + lax.broadcasted_iota(jnp.int32, z.shape, 0)
    kpos = j * BLOCK + lax.broadcasted_iota(jnp.int32, z.shape, 1)
    mask = (kpos < qpos) & (kpos >= PAD)
    l1m = jnp.where(mask, lbeta - z, 0.0)
    return lbeta, l1m, mask


SB_DEAD = -104.0
SB_UNROLL = 3


def _sb_fwd(proj, n_heads, group0):
    m = proj.shape[0]
    nb = m // BLOCK
    w = n_heads * HEAD
    scale = 1.0 / math.sqrt(HEAD)

    def body(q_ref, k_ref, v_ref, o_ref, start_ref, count_ref):
        h, i = pl.program_id(0), pl.program_id(1)
        q = q_ref[...].astype(BF16)
        tsuf = _tri(BLOCK, "gt").astype(BF16)

        def live(carry):
            t, _, run = carry
            return (t <= i) & (jnp.max(run) > SB_DEAD)

        def step(carry):
            t, acc, run = carry
            for u in range(SB_UNROLL):
                j = i - t - u
                rows = pl.ds(pl.multiple_of(jnp.maximum(j, 0) * BLOCK, BLOCK), BLOCK)
                if u == SB_UNROLL - 1:
                    start_ref[...] = jnp.broadcast_to(run, (BLOCK, HEAD))
                lbeta, l1m, mask = _sb_scores(q, k_ref[rows, :].astype(BF16), i, j, scale)
                wgt = jnp.where(mask, jnp.exp(lbeta + _split_dot(l1m, tsuf) + run), 0.0)
                acc = acc + jnp.dot(wgt.astype(BF16), v_ref[rows, :].astype(BF16), preferred_element_type=F32)
                run = run + jnp.sum(l1m, axis=-1, keepdims=True)
            return t + SB_UNROLL, acc, run

        t, acc, _ = lax.while_loop(live, step, (jnp.int32(0), jnp.zeros((BLOCK, HEAD), F32), jnp.zeros((BLOCK, 1), F32)))
        o_ref[...] = acc.astype(BF16)
        count_ref[h, i] = t.astype(F32)

    def whole(group):
        return pl.BlockSpec((m, HEAD), lambda h, i: (0, group * n_heads + h))

    return pl.pallas_call(
        body, name="sb_fwd", grid=(n_heads, nb),
        in_specs=[pl.BlockSpec((BLOCK, HEAD), lambda h, i: (i, group0 * n_heads + h)), whole(group0 + 1), whole(group0 + 2)],
        out_specs=[pl.BlockSpec((BLOCK, HEAD), lambda h, i: (i, h)), pl.BlockSpec((None, BLOCK, HEAD), lambda h, i: (h, i, 0)),
                   pl.BlockSpec(memory_space=pltpu.SMEM)],
        out_shape=[_sds((m, w), BF16), _sds((n_heads, m, HEAD), F32), _sds((n_heads, nb), F32)],
        compiler_params=_params(2),
    )(proj, proj, proj)


def _sb_bwd(proj, do, start, count, n_heads, group0):
    m = proj.shape[0]
    nb = m // BLOCK
    w = n_heads * HEAD
    scale = 1.0 / math.sqrt(HEAD)

    def body(q_ref, k_ref, v_ref, do_ref, start_ref, count_ref, dq_ref, dk_ref, dv_ref, dk_s, dv_s):
        h, i = pl.program_id(0), pl.program_id(1)

        @pl.when(i == 0)
        def _():
            dk_s[...] = jnp.zeros_like(dk_s)
            dv_s[...] = jnp.zeros_like(dv_s)

        q = q_ref[...].astype(BF16)
        dob = do_ref[...].astype(BF16)
        count = count_ref[h, i].astype(jnp.int32)
        first = i + 1 - count
        tsuf = _tri(BLOCK, "gt").astype(BF16)
        tpre = _tri(BLOCK, "lt").astype(BF16)

        def step(t, carry):
            dq, right, psum = carry
            for u in range(SB_UNROLL):
                j = first + t * SB_UNROLL + u
                rows = pl.ds(pl.multiple_of(jnp.maximum(j, 0) * BLOCK, BLOCK), BLOCK)
                kj = k_ref[rows, :].astype(BF16)
                vj = v_ref[rows, :].astype(BF16)
                lbeta, l1m, mask = _sb_scores(q, kj, i, j, scale)
                moved = right - jnp.sum(l1m, axis=-1, keepdims=True)
                right = jnp.where(t == 0, right, moved) if u == 0 else moved
                a = jnp.where(mask, jnp.exp(lbeta + _split_dot(l1m, tsuf) + right), 0.0)
                p = a * lax.dot_general(dob, vj, NT, preferred_element_type=F32)
                below = psum + _split_dot(p, tpre)
                beta = jnp.exp(lbeta)
                dz = (jnp.where(mask, p * (1.0 - beta) - below * beta, 0.0) * scale).astype(BF16)
                dq = dq + jnp.dot(dz, kj, preferred_element_type=F32)
                dk_s[rows, :] += lax.dot_general(dz, q, TN, preferred_element_type=F32)
                dv_s[rows, :] += lax.dot_general(a.astype(BF16), dob, TN, preferred_element_type=F32)
                psum = psum + jnp.sum(p, axis=-1, keepdims=True)
            return dq, right, psum

        dq, _, _ = lax.fori_loop(0, count // SB_UNROLL, step,
                                 (jnp.zeros((BLOCK, HEAD), F32), start_ref[:, 0:1], jnp.zeros((BLOCK, 1), F32)))
        dq_ref[...] = dq.astype(BF16)

        @pl.when(i == nb - 1)
        def _():
            dk_ref[...] = dk_s[...].astype(BF16)
            dv_ref[...] = dv_s[...].astype(BF16)

    def whole(group):
        return pl.BlockSpec((m, HEAD), lambda h, i: (0, group * n_heads + h))

    tile = pl.BlockSpec((BLOCK, HEAD), lambda h, i: (i, h))
    col = pl.BlockSpec((m, HEAD), lambda h, i: (0, h))
    return pl.pallas_call(
        body, name="sb_bwd", grid=(n_heads, nb),
        in_specs=[pl.BlockSpec((BLOCK, HEAD), lambda h, i: (i, group0 * n_heads + h)), whole(group0 + 1), whole(group0 + 2),
                  tile, pl.BlockSpec((None, BLOCK, HEAD), lambda h, i: (h, i, 0)), pl.BlockSpec(memory_space=pltpu.SMEM)],
        out_specs=[tile, col, col],
        out_shape=[_sds((m, w), BF16)] * 3,
        scratch_shapes=[pltpu.VMEM((m, HEAD), F32)] * 2,
        compiler_params=_params(2),
    )(proj, proj, proj, do, start, count)


def _grad_w_rows(name, x, dy, nd_out):
    m, kx = x.shape
    n = dy.shape[1]
    ks = kx // nd_out
    tm = _tile(m, GRAD_ROW_TILE)

    def epi(acc, er, orf):
        orf[0][...] = acc[0].astype(BF16)

    return _gemm(name, (nd_out, m // tm),
                 [(x, pl.BlockSpec((tm, ks), lambda j, k: (k, j)), dy, pl.BlockSpec((tm, n), lambda j, k: (k, 0)))],
                 [0], [(ks, n)], TN, [], [(_sds((nd_out, ks, n), BF16), pl.BlockSpec((None, ks, n), lambda j, k: (j, 0, 0)))], epi)[0]


LATE = ("w_proj_hg", "w_proj_sb", "w_out", "ffn2_w_gate", "ffn2_w_up", "ffn2_w_down")


def _local_step(x, target, meta, vec, wts, shards=None):
    d = x.shape[1]
    width = vec["hg_norm_g"].shape[1]
    n_heads = width // HEAD
    gate_col = 7 * width
    h0 = jnp.concatenate([jnp.zeros((PAD, d), F32), meta, x], axis=0)
    h0b = h0.astype(BF16)
    wts = dict(wts)
    exchange = None if shards is None else _exchange_carried

    if shards is None:
        a1, b1, s1 = _ffn_up("ffn1_up", h0b, wts["ffn1_w_gate"], wts["ffn1_w_up"])
        r1, h1, h1b = _residual_ln("ffn1_down", s1, True, wts["ffn1_w_down"], h0, vec["ln1_g"], vec["ln1_b"], 0.5)
    else:
        (a1, b1, s1), (wts["ffn1_w_down"],) = _ffn_up("ffn1_up", h0b, wts["ffn1_w_gate"], wts["ffn1_w_up"],
                                                     _gather_carried([shards["ffn1_w_down"]]))
        (r1, h1, h1b), (wts["w_in"],) = _residual_ln("ffn1_down", s1, True, wts["ffn1_w_down"], h0, vec["ln1_g"],
                                                     vec["ln1_b"], 0.5, _gather_carried([shards["w_in"]]))
    nd = wts["w_in"].shape[0]
    proj = _in_proj(h1b, wts["w_in"])
    if shards is None:
        o_raw, o_hg, states = _hgrn_fwd(proj, vec["hg_lb_logits"], vec["hg_norm_g"], n_heads)
    else:
        (o_raw, o_hg, states), late = _hgrn_fwd(proj, vec["hg_lb_logits"], vec["hg_norm_g"], n_heads,
                                                _gather_carried([shards[k] for k in LATE]))
        wts.update(zip(LATE, late))
    w_out = wts["w_out"]
    o_sb, sb_start, sb_count = _sb_fwd(proj, n_heads, 4)
    u_hg, u_sb, y = _proj_merge(o_hg, o_sb, wts["w_proj_hg"], wts["w_proj_sb"], proj, vec["b_gate"], gate_col)
    r2, h2, h2b = _residual_ln("out_proj", y, False, w_out.reshape(d, d), h1, vec["ln2_g"], vec["ln2_b"], 1.0)
    a2, b2, s2 = _ffn_up("ffn2_up", h2b,wts["ffn2_w_gate"], wts["ffn2_w_up"])
    r3, _, _ = _residual_ln("ffn2_down", s2, True, wts["ffn2_w_down"], h2, vec["ln3_g"], vec["ln3_b"], 0.5)

    dr3, dr3b, dg3, db3, loss = _ln_bwd("ln3_bwd", r3, vec["ln3_g"], 0.5, beta=vec["ln3_b"], target=target, first_row=BLOCK)
    dh2, dwg2, dwu2, dwd2 = _ffn_bwd("ffn2", dr3b, dr3, h2b, a2, b2, s2, wts["ffn2_w_gate"], wts["ffn2_w_up"],
                                     wts["ffn2_w_down"], exchange)
    dr2, dr2b, dg2, db2 = _ln_bwd("ln2_bwd", r2, vec["ln2_g"], 1.0, dy=dh2)
    du_hg, du_sb, dz_hg, dz_sb, dbg = _merge_bwd(dr2b, w_out.reshape(d, d), proj, vec["b_gate"], u_hg, u_sb, gate_col, d // nd)
    dw_out = _grad_w_rows("dw_out", y, dr2b, nd)
    dp_hg = _grad_w("dp_hg", o_hg, du_hg, nd)
    dp_sb = _grad_w("dp_sb", o_sb, du_sb, nd)
    do_hg = _grad_in("do_hg", du_hg, wts["w_proj_hg"])
    do_sb = _grad_in("do_sb", du_sb, wts["w_proj_sb"])
    hg = _hgrn_bwd(proj, vec["hg_lb_logits"], vec["hg_norm_g"], o_raw, do_hg, states, n_heads,
                   exchange([dw_out, dp_hg, dp_sb]) if exchange else None)
    if exchange:
        hg, (dw_out, dp_hg, dp_sb) = hg
    dhq, dhf, dhi, dhog, dgn, dlb = hg
    dsq, dsk, dsv = _sb_bwd(proj, do_sb, sb_start, sb_count, n_heads, 4)
    dproj = jnp.concatenate([dhq, dhf, dhi, dhog, dsq, dsk, dsv, dz_hg, dz_sb], axis=1)
    dw_in = _grad_w("dw_in", h1b, dproj, nd)
    dh1 = _grad_in("dh1", dproj, wts["w_in"], add=dr2, carried=exchange([dw_in]) if exchange else None)
    if exchange:
        dh1, (dw_in,) = dh1
    dr1, dr1b, dg1, db1 = _ln_bwd("ln1_bwd", r1, vec["ln1_g"], 0.5, dy=dh1)
    dh0, dwg1, dwu1, dwd1 = _ffn_bwd("ffn1", dr1b, dr1, h0b, a1, b1, s1, wts["ffn1_w_gate"], wts["ffn1_w_up"],
                                     wts["ffn1_w_down"], exchange)

    small = {"ln1_g": dg1, "ln1_b": db1, "ln2_g": dg2, "ln2_b": db2, "ln3_g": dg3, "ln3_b": db3,
             "b_gate": dbg, "hg_lb": dlb, "hg_norm_g": dgn}
    big = {"ffn1_w_gate": dwg1, "ffn1_w_up": dwu1, "ffn1_w_down": dwd1, "w_in": dw_in, "w_proj_hg": dp_hg,
           "w_proj_sb": dp_sb, "w_out": dw_out, "ffn2_w_gate": dwg2, "ffn2_w_up": dwu2, "ffn2_w_down": dwd2}
    return loss, dh0[BLOCK:], dh0[PAD:BLOCK], small, big


def _position():
    return lax.axis_index("x"), lax.axis_index("y"), lax.axis_index("c")


def _slot(px, py, pc):
    return 4 * px + 2 * py + pc


def _all_gather(shards):
    n = len(shards)

    def body(*refs):
        ins, outs = refs[:n], refs[n:2 * n]
        send_sems, recv_sems, local_sems = refs[2 * n:]
        x, y, c = _position()
        me, sibling = (x, y, c), (x, y, 1 - c)
        chips = [(1 - x, y), (x, 1 - y), (1 - x, 1 - y)]

        def copy(a, k, block, to, src=None):
            dst = outs[a].at[_slot(*block)]
            return pltpu.make_async_remote_copy(src_ref=dst if src is None else src, dst_ref=dst,
                                                send_sem=send_sems.at[a, k], recv_sem=recv_sems.at[a, k],
                                                device_id=to, device_id_type=MESH)

        mine = [pltpu.make_async_copy(ins[a], outs[a].at[_slot(*me)], local_sems.at[a]) for a in range(n)]
        for cp in mine:
            cp.start()
        first = []
        for a in range(n):
            first.append(copy(a, 0, me, sibling, src=ins[a]))
            first += [copy(a, 1 + j, me, (*chip, c), src=ins[a]) for j, chip in enumerate(chips)]
        for cp in first:
            cp.start()
        passed = []
        for j, chip in enumerate(chips):
            for a in range(n):
                copy(a, 1 + j, (*chip, c), me).wait_recv()
                cp = copy(a, 4 + j, (*chip, c), sibling)
                cp.start()
                passed.append(cp)
        for a in range(n):
            copy(a, 0, sibling, me).wait_recv()
        for j, chip in enumerate(chips):
            for a in range(n):
                copy(a, 4 + j, (*chip, 1 - c), me).wait_recv()
        for cp in first + passed:
            cp.wait_send()
        for cp in mine:
            cp.wait()

    return pl.pallas_call(
        body, name="all_gather", out_shape=[_sds((N_DEV,) + s.shape, s.dtype) for s in shards],
        in_specs=[ANY] * n, out_specs=[ANY] * n,
        scratch_shapes=[pltpu.SemaphoreType.DMA((n, 7)), pltpu.SemaphoreType.DMA((n, 7)), pltpu.SemaphoreType.DMA((n,))],
    )(*shards)


def _exchange_carried(grads):
    return _direct_copies(grads, [_sds(g.shape, g.dtype) for g in grads], lambda ref, slot: ref.at[slot])


def _gather_carried(shards):
    return _direct_copies(shards, [_sds((N_DEV,) + s.shape, s.dtype) for s in shards], lambda ref, slot: ref)


def _direct_copies(arrays, outs, block_for):
    n = len(arrays)

    def plan(ins, results, sems, arriving):
        send_sems, recv_sems, local_sems = sems
        x, y, c = _position()
        mine = _slot(x, y, c)
        peers = [(1 - x if k & 4 else x, 1 - y if k & 2 else y, 1 - c if k & 1 else c) for k in range(1, N_DEV)]
        own = [pltpu.make_async_copy(block_for(ins[a], mine), results[a].at[mine], local_sems.at[a]) for a in range(n)]
        remote = [pltpu.make_async_remote_copy(
            src_ref=block_for(ins[a], mine if arriving else _slot(*peer)),
            dst_ref=results[a].at[_slot(*peer) if arriving else mine],
            send_sem=send_sems.at[a, k], recv_sem=recv_sems.at[a, k], device_id=peer, device_id_type=MESH)
            for a in range(n) for k, peer in enumerate(peers)]
        return own, remote

    def start(ins, results, sems):
        own, sent = plan(ins, results, sems, False)
        for cp in own + sent:
            cp.start()

    def finish(ins, results, sems):
        _, landed = plan(ins, results, sems, True)
        for cp in landed:
            cp.wait_recv()
        own, sent = plan(ins, results, sems, False)
        for cp in sent:
            cp.wait_send()
        for cp in own:
            cp.wait()

    sems = [pltpu.SemaphoreType.DMA((n, 7)), pltpu.SemaphoreType.DMA((n, 7)), pltpu.SemaphoreType.DMA((n,))]
    return _Carried(list(arrays), outs, sems, start, finish)


def _all_reduce_rows(v):
    rows = v.shape[0]

    def body(v_ref, out_ref, buf, send_sems, recv_sems):
        x, y, c = _position()
        me, sibling = (x, y, c), (x, y, 1 - c)
        chips = [(1 - x, y), (x, 1 - y), (1 - x, 1 - y)]

        def copy(k, block, to, src=None):
            dst = buf.at[_slot(*block)]
            return pltpu.make_async_remote_copy(src_ref=dst if src is None else src, dst_ref=dst,
                                                send_sem=send_sems.at[k], recv_sem=recv_sems.at[k],
                                                device_id=to, device_id_type=MESH)

        first = [copy(0, me, sibling, src=v_ref)] + [copy(1 + j, me, (*chip, c), src=v_ref) for j, chip in enumerate(chips)]
        for cp in first:
            cp.start()
        buf[_slot(*me)] = v_ref[...]
        passed = [copy(4 + j, (*chip, c), sibling) for j, chip in enumerate(chips)]
        for j, chip in enumerate(chips):
            copy(1 + j, (*chip, c), me).wait_recv()
            passed[j].start()
        copy(0, sibling, me).wait_recv()
        for j, chip in enumerate(chips):
            copy(4 + j, (*chip, 1 - c), me).wait_recv()
        for cp in first + passed:
            cp.wait_send()
        total = buf[0]
        for s in range(1, N_DEV):
            total = total + buf[s]
        out_ref[...] = total

    vmem = pl.BlockSpec(memory_space=pltpu.VMEM)
    return pl.pallas_call(
        body, name="small_all_reduce", out_shape=_sds(v.shape, F32), in_specs=[vmem], out_specs=vmem,
        scratch_shapes=[pltpu.VMEM((N_DEV, rows, 128), F32), pltpu.SemaphoreType.DMA((7,)), pltpu.SemaphoreType.DMA((7,))],
    )(v)


def _adamw(name, w, m, v, contrib):
    r, c = w.shape
    n = contrib.shape[0]
    tr = _tile(r, 256)

    def body(w_ref, m_ref, v_ref, c_ref, g_out, d_out, m_out, v_out):
        g = c_ref[0].astype(F32)
        for s in range(1, n):
            g = g + c_ref[s].astype(F32)
        m2 = ADAM_B1 * m_ref[...] + (1.0 - ADAM_B1) * g
        v2 = ADAM_B2 * v_ref[...] + (1.0 - ADAM_B2) * (g * g)
        m_hat = m2 / (1.0 - ADAM_B1 ** ADAM_STEP)
        v_hat = v2 / (1.0 - ADAM_B2 ** ADAM_STEP)
        g_out[...] = g
        d_out[...] = -ADAM_LR * (m_hat / (jnp.sqrt(v_hat) + ADAM_EPS) + ADAM_WD * w_ref[...])
        m_out[...] = m2
        v_out[...] = v2

    tile = pl.BlockSpec((tr, c), lambda i: (i, 0))
    return pl.pallas_call(
        body, name=name, grid=(r // tr,), in_specs=[tile, tile, tile, pl.BlockSpec((n, tr, c), lambda i: (0, i, 0))],
        out_specs=[tile] * 4, out_shape=[_sds((r, c), F32)] * 4, compiler_params=_params(1),
    )(w, m, v, contrib)


def _lb_logits_grad(logits, dlb):
    def body(lg_ref, d_ref, out_ref):
        lg = lg_ref[...]
        mx = jnp.maximum(lg[0:1], lg[1:2])
        e0 = jnp.exp(lg[0:1] - mx)
        p0 = e0 / (e0 + jnp.exp(lg[1:2] - mx))
        g0 = d_ref[...] * p0 * (1.0 - p0)
        out_ref[0:1, :] = g0
        out_ref[1:2, :] = -g0

    return pl.pallas_call(body, name="lb_logits_grad", out_shape=_sds(logits.shape, F32))(logits, dlb)


BIG = ("ffn1_w_gate", "ffn1_w_up", "ffn1_w_down", "w_in", "w_proj_hg", "w_proj_sb", "w_out",
       "ffn2_w_gate", "ffn2_w_up", "ffn2_w_down")
VECTORS = ("ln1_g", "ln1_b", "b_gate", "hg_lb_logits", "hg_norm_g", "ln2_g", "ln2_b", "ln3_g", "ln3_b")
WEIGHTS = ("meta", "ln1_g", "ln1_b", "ffn1_w_gate", "ffn1_w_up", "ffn1_w_down", "w_in", "b_gate", "hg_lb_logits",
           "hg_norm_g", "w_proj_hg", "w_proj_sb", "w_out", "ln2_g", "ln2_b", "ffn2_w_gate", "ffn2_w_up",
           "ffn2_w_down", "ln3_g", "ln3_b")


def kernel(x, meta, ln1_g, ln1_b, ffn1_w_gate, ffn1_w_up, ffn1_w_down, w_in, b_gate, hg_lb_logits, hg_norm_g, w_proj_hg, w_proj_sb, w_out, ln2_g, ln2_b, ffn2_w_gate, ffn2_w_up, ffn2_w_down, ln3_g, ln3_b, loss_target, m_meta, m_ln1_g, m_ln1_b, m_ffn1_w_gate, m_ffn1_w_up, m_ffn1_w_down, m_w_in, m_b_gate, m_hg_lb_logits, m_hg_norm_g, m_w_proj_hg, m_w_proj_sb, m_w_out, m_ln2_g, m_ln2_b, m_ffn2_w_gate, m_ffn2_w_up, m_ffn2_w_down, m_ln3_g, m_ln3_b, v_meta, v_ln1_g, v_ln1_b, v_ffn1_w_gate, v_ffn1_w_up, v_ffn1_w_down, v_w_in, v_b_gate, v_hg_lb_logits, v_hg_norm_g, v_w_proj_hg, v_w_proj_sb, v_w_out, v_ln2_g, v_ln2_b, v_ffn2_w_gate, v_ffn2_w_up, v_ffn2_w_down, v_ln3_g, v_ln3_b):
    given = dict(locals())
    d = x.shape[-1]
    ds = meta.shape[1]

    shards = {k: given[k][0].astype(BF16) for k in BIG}
    first = ("ffn1_w_gate", "ffn1_w_up")
    gathered = _all_gather([meta] + [shards.pop(k) for k in first])
    meta_full = gathered[0].transpose(1, 0, 2).reshape(N_META, d)
    vec = {k: given[k] for k in VECTORS}
    loss, grad_x, dmeta, small, received = _local_step(x[0], loss_target[0], meta_full, vec, dict(zip(first, gathered[1:])),
                                                       shards)

    order =("ln1_g", "ln1_b", "ln2_g", "ln2_b", "ln3_g", "ln3_b", "b_gate", "hg_lb", "hg_norm_g")
    parts = [small[k].reshape(-1, 128) for k in order] + [dmeta.reshape(-1, 128), jnp.broadcast_to(loss, (8, 128))]
    total = _all_reduce_rows(jnp.concatenate(parts, axis=0))
    reduced, row = {}, 0
    for k, p in zip(order + ("meta", "loss"), parts):
        reduced[k] = total[row:row + p.shape[0]]
        row += p.shape[0]
    loss_out = reduced["loss"][0, 0]
    me = _slot(*_position())
    dmeta_mine = lax.dynamic_slice(reduced["meta"].reshape(N_META, d), (0, me * ds), (N_META, ds))
    dlogits = _lb_logits_grad(hg_lb_logits, reduced["hg_lb"].reshape(1, -1))

    grads, deltas, new_m, new_v = {}, {}, {}, {}
    for k in WEIGHTS:
        w = given[k]
        lead = w.shape[:-2]
        w2, m2, v2 = (a.reshape(a.shape[-2:]) for a in (w, given["m_" + k], given["v_" + k]))
        if k in BIG:
            contrib = received[k]
        elif k == "meta":
            contrib = dmeta_mine[None]
        elif k == "hg_lb_logits":
            contrib = dlogits[None]
        else:
            contrib = reduced[k].reshape((1,) + w2.shape)
        out = _adamw("adamw_" + k, w2, m2, v2, contrib)
        grads[k], deltas[k], new_m[k], new_v[k] = (o.reshape(lead + o.shape) for o in out)
    return (loss_out, grad_x[None], *[grads[k] for k in WEIGHTS], *[deltas[k] for k in WEIGHTS],
            *[new_m[k] for k in WEIGHTS], *[new_v[k] for k in WEIGHTS])
```

```python
import functools
import math

import jax
import jax.numpy as jnp
from jax import lax
from jax.experimental import pallas as pl
from jax.experimental.pallas import tpu as pltpu

F32 = jnp.float32
BF16 = jnp.bfloat16
MESH = pl.DeviceIdType.MESH

N_DEV = 8
N_META = 16
BLOCK = 128
PAD = BLOCK - N_META
HEAD = 128
CHUNK = 16
LN_EPS = 1e-5
RMS_EPS = 1e-6
DN_ALPHA = 2.0 ** 0.25
ADAM_LR, ADAM_B1, ADAM_B2, ADAM_EPS, ADAM_WD, ADAM_STEP = 0.001, 0.9, 0.999, 1e-08, 0.01, 10

VMEM_LIMIT_V7X = 60 * 1024 * 1024
ROW_TILE = 640
LN_ROW_TILE = 320
GRAD_ROW_TILE = 640

NN = (((1,), (0,)), ((), ()))
NT = (((1,), (1,)), ((), ()))
TN = (((0,), (0,)), ((), ()))


def _tile(n, pref, mult=16):
    best = None
    for t in range(mult, min(n, pref) + 1, mult):
        if n % t == 0:
            best = t
    return n if best is None else best


def _params(n_axes):
    return pltpu.CompilerParams(dimension_semantics=("arbitrary",) * n_axes, vmem_limit_bytes=VMEM_LIMIT_V7X)


def _sigmoid(x):
    return 1.0 / (1.0 + jnp.exp(-x))


class _Carried:
    def __init__(self, ins, outs, sems, start, finish):
        self.ins, self.outs, self.sems, self.start, self.finish = ins, outs, sems, start, finish


ANY = pl.BlockSpec(memory_space=pl.ANY)


def _pallas(name, body, grid, in_specs, out_specs, out_shape, scratch, operands, carried=None):
    if carried is None:
        return pl.pallas_call(body, name=name, grid=grid, in_specs=in_specs, out_specs=out_specs, out_shape=out_shape,
                              scratch_shapes=scratch, compiler_params=_params(len(grid)))(*operands)
    n_in, n_out, n_scr = len(in_specs), len(out_specs), len(scratch)
    c_in, c_out = len(carried.ins), len(carried.outs)

    def wrapped(*refs):
        ins, rest = refs[:n_in], refs[n_in:]
        c_ins, rest = rest[:c_in], rest[c_in:]
        outs, rest = rest[:n_out], rest[n_out:]
        c_outs, rest = rest[:c_out], rest[c_out:]
        scr, c_sems = rest[:n_scr], rest[n_scr:]
        first = last = None
        for axis, size in enumerate(grid):
            at0, at_end = pl.program_id(axis) == 0, pl.program_id(axis) == size - 1
            first = at0 if first is None else first & at0
            last = at_end if last is None else last & at_end

        @pl.when(first)
        def _():
            carried.start(c_ins, c_outs, c_sems)

        body(*ins, *outs, *scr)

        @pl.when(last)
        def _():
            carried.finish(c_ins, c_outs, c_sems)

    res = pl.pallas_call(
        wrapped, name=name, grid=grid, in_specs=list(in_specs) + [ANY] * c_in, out_specs=list(out_specs) + [ANY] * c_out,
        out_shape=list(out_shape) + list(carried.outs), scratch_shapes=list(scratch) + list(carried.sems),
        compiler_params=_params(len(grid)),
    )(*operands, *carried.ins)
    return res[:n_out], res[n_out:]


def _gemm(name, grid, pairs, acc_of, acc_shapes, dims, extras, outs, epilogue, carried=None):
    n_extra, n_out = len(extras), len(outs)
    nk = grid[-1]
    k_axis = len(grid) - 1
    operands, in_specs, where = [], [], {}
    for a, a_spec, b, b_spec in pairs:
        for arr, spec in ((a, a_spec), (b, b_spec)):
            if (id(arr), id(spec)) not in where:
                where[(id(arr), id(spec))] = len(operands)
                operands.append(arr)
                in_specs.append(spec)
    n_mat = len(operands)
    slots = [(where[(id(a), id(a_spec))], where[(id(b), id(b_spec))]) for a, a_spec, b, b_spec in pairs]

    def body(*refs):
        er = refs[n_mat:n_mat + n_extra]
        orf = refs[n_mat + n_extra:n_mat + n_extra + n_out]
        accs = refs[n_mat + n_extra + n_out:]

        def part(p):
            a_ref, b_ref = refs[slots[p][0]], refs[slots[p][1]]
            if len(a_ref.shape) == 2:
                return lax.dot_general(a_ref[...].astype(BF16), b_ref[...].astype(BF16), dims, preferred_element_type=F32)
            total = None
            for s in range(a_ref.shape[0]):
                d = lax.dot_general(a_ref[s].astype(BF16), b_ref[s].astype(BF16), dims, preferred_element_type=F32)
                total = d if total is None else total + d
            return total

        if nk == 1:
            vals = [None] * len(acc_shapes)
            for p in range(len(pairs)):
                d = part(p)
                vals[acc_of[p]] = d if vals[acc_of[p]] is None else vals[acc_of[p]] + d
            epilogue(vals, er, orf)
        else:
            k = pl.program_id(k_axis)

            @pl.when(k == 0)
            def _():
                for acc in accs:
                    acc[...] = jnp.zeros_like(acc)

            for p in range(len(pairs)):
                accs[acc_of[p]][...] += part(p)

            @pl.when(k == nk - 1)
            def _():
                epilogue([acc[...] for acc in accs], er, orf)

    for e, e_spec in extras:
        operands.append(e)
        in_specs.append(e_spec)
    scratch = [] if nk == 1 else [pltpu.VMEM(s, F32) for s in acc_shapes]
    return _pallas(name, body, grid, in_specs, [s for _, s in outs], [o for o, _ in outs], scratch, operands, carried)


def _sds(shape, dtype):
    return jax.ShapeDtypeStruct(shape, dtype)


def _ln_rows(r, g, b):
    mu = jnp.mean(r, axis=-1, keepdims=True)
    xc = r - mu
    var = jnp.mean(xc * xc, axis=-1, keepdims=True)
    return xc * lax.rsqrt(var + LN_EPS) * g + b


def _ffn_up(name, hb, wg, wu, carried=None):
    m, d = hb.shape
    nd, _, fs = wg.shape
    tm = _tile(m, ROW_TILE)

    def epi(acc, er, orf):
        a, b = acc
        orf[0][...] = a
        orf[1][...] = b
        orf[2][...] = (a * _sigmoid(a) * b).astype(BF16)

    h_spec = pl.BlockSpec((tm, d), lambda i, j, k: (i, 0))
    w_spec = pl.BlockSpec((None, d, fs), lambda i, j, k: (j, 0, 0))
    o_spec = pl.BlockSpec((None, tm, fs), lambda i, j, k: (j, i, 0))
    return _gemm(name, (m // tm, nd, 1), [(hb, h_spec, wg, w_spec), (hb, h_spec, wu, w_spec)], [0, 1],
                 [(tm, fs)] * 2, NN, [],
                 [(_sds((nd, m, fs), F32), o_spec), (_sds((nd, m, fs), F32), o_spec), (_sds((nd, m, fs), BF16), o_spec)], epi,
                 carried)


def _residual_ln(name, a, a_stacked, w, h_in, g, beta, scale, carried=None):
    d = w.shape[-1]
    m = h_in.shape[0]
    tm = _tile(m, LN_ROW_TILE)

    def epi(acc, er, orf):
        r = DN_ALPHA * er[0][...] + scale * acc[0]
        h = _ln_rows(r, er[1][...], er[2][...])
        orf[0][...] = r
        orf[1][...] = h
        orf[2][...] = h.astype(BF16)

    once = pl.Buffered(1)
    if a_stacked:
        a_spec = pl.BlockSpec((a.shape[0], tm, a.shape[2]), lambda i, k: (0, i, 0))
        w_spec = pl.BlockSpec(w.shape, lambda i, k: (0, 0, 0), pipeline_mode=once)
    else:
        a_spec = pl.BlockSpec((tm, a.shape[1]), lambda i, k: (i, 0))
        w_spec = pl.BlockSpec(w.shape, lambda i, k: (0, 0), pipeline_mode=once)
    row = pl.BlockSpec((tm, d), lambda i, k: (i, 0))
    vec = pl.BlockSpec((1, d), lambda i, k: (0, 0))
    return _gemm(name, (m // tm, 1), [(a, a_spec, w, w_spec)], [0], [(tm, d)], NN,
                 [(h_in, row), (g, vec), (beta, vec)],
                 [(_sds((m, d), F32), row), (_sds((m, d), F32), row), (_sds((m, d), BF16), row)], epi, carried)


def _in_proj(hb, w_in, carried=None):
    m, d = hb.shape
    nd, _, cs = w_in.shape
    tm = _tile(m, ROW_TILE)

    def epi(acc, er, orf):
        orf[0][...] = acc[0]

    res = _gemm("in_proj", (m // tm, nd, 1),
                [(hb, pl.BlockSpec((tm, d), lambda i, j, k: (i, 0)), w_in, pl.BlockSpec((None, d, cs), lambda i, j, k: (j, 0, 0)))],
                [0], [(tm, cs)], NN, [], [(_sds((m, nd * cs), F32), pl.BlockSpec((tm, cs), lambda i, j, k: (i, j)))], epi,
                carried)
    return res[0] if carried is None else (res[0][0], res[1])


def _proj_merge(o_hg, o_sb, p_hg, p_sb, proj, b_gate, gate_col):
    m, w = o_hg.shape
    nd, _, ds = p_hg.shape
    d = nd * ds
    tm = _tile(m, ROW_TILE)
    c0 = gate_col // ds

    def epi(acc, er, orf):
        u_hg, u_sb = acc
        g_hg = _sigmoid(er[0][...] + er[2][...])
        g_sb = _sigmoid(er[1][...] + er[3][...])
        orf[0][...] = u_hg
        orf[1][...] = u_sb
        orf[2][...] = (g_hg * u_hg + g_sb * u_sb).astype(BF16)

    o_spec = pl.BlockSpec((tm, w), lambda i, j, k: (i, 0))
    p_spec = pl.BlockSpec((None, w, ds), lambda i, j, k: (j, 0, 0))
    out = pl.BlockSpec((tm, ds), lambda i, j, k: (i, j))
    return _gemm("proj_merge", (m // tm, nd, 1), [(o_hg, o_spec, p_hg, p_spec), (o_sb, o_spec, p_sb, p_spec)], [0, 1],
                 [(tm, ds)] * 2, NN,
                 [(proj, pl.BlockSpec((tm, ds), lambda i, j, k: (i, c0 + j))),
                  (proj, pl.BlockSpec((tm, ds), lambda i, j, k: (i, c0 + nd + j))),
                  (b_gate, pl.BlockSpec((1, ds), lambda i, j, k: (0, j))),
                  (b_gate, pl.BlockSpec((1, ds), lambda i, j, k: (0, nd + j)))],
                 [(_sds((m, d), F32), out), (_sds((m, d), F32), out), (_sds((m, d), BF16), out)], epi)


def _ln_bwd(name, r, g, out_scale, dy=None, beta=None, target=None, first_row=0):
    m, d = r.shape
    tm = _tile(m, LN_ROW_TILE if target is None else BLOCK)
    with_loss = target is not None
    skip = first_row // tm if with_loss else 0
    assert not with_loss or first_row % tm == 0

    def body(*refs):
        if with_loss:
            r_ref, g_ref, b_ref, t_ref, dr_ref, drb_ref, dg_ref, db_ref, loss_ref = refs
        else:
            r_ref, g_ref, dy_ref, dr_ref, drb_ref, dg_ref, db_ref = refs
        i = pl.program_id(0)
        x = r_ref[...]
        mu = jnp.mean(x, axis=-1, keepdims=True)
        xc = x - mu
        var = jnp.mean(xc * xc, axis=-1, keepdims=True)
        rstd = lax.rsqrt(var + LN_EPS)
        xhat = xc * rstd
        gv = g_ref[...]
        if with_loss:
            err = xhat * gv + b_ref[...] - t_ref[...]
            live = (i >= skip).astype(F32)
            dyv = err * (live / d)
            part = 0.5 * live * jnp.sum(jnp.sum(err * err, axis=-1, keepdims=True), axis=0, keepdims=True) / d
        else:
            dyv = dy_ref[...]
        dxh = dyv * gv
        m1 = jnp.mean(dxh, axis=-1, keepdims=True)
        m2 = jnp.mean(dxh * xhat, axis=-1, keepdims=True)
        dr = rstd * (dxh - m1 - xhat * m2)
        dr_ref[...] = dr
        drb_ref[...] = (out_scale * dr).astype(BF16)

        @pl.when(i == 0)
        def _():
            dg_ref[...] = jnp.zeros_like(dg_ref)
            db_ref[...] = jnp.zeros_like(db_ref)
            if with_loss:
                loss_ref[...] = jnp.zeros_like(loss_ref)

        dg_ref[...] += jnp.sum(dyv * xhat, axis=0, keepdims=True)
        db_ref[...] += jnp.sum(dyv, axis=0, keepdims=True)
        if with_loss:
            loss_ref[...] += jnp.broadcast_to(part, loss_ref.shape)

    row = pl.BlockSpec((tm, d), lambda i: (i, 0))
    vec = pl.BlockSpec((1, d), lambda i: (0, 0))
    out_shape = [_sds((m, d), F32), _sds((m, d), BF16), _sds((1, d), F32), _sds((1, d), F32)]
    out_specs = [row, row, vec, vec]
    if with_loss:
        operands = [r, g, beta, target]
        in_specs = [row, vec, vec, pl.BlockSpec((tm, d), lambda i: (jnp.maximum(i - skip, 0), 0))]
        out_shape.append(_sds((1, BLOCK), F32))
        out_specs.append(pl.BlockSpec((1, BLOCK), lambda i: (0, 0)))
    else:
        operands = [r, g, dy]
        in_specs = [row, vec, row]
    return pl.pallas_call(body, name=name, grid=(m // tm,), in_specs=in_specs, out_specs=out_specs, out_shape=out_shape,
                          compiler_params=_params(1))(*operands)


def _ffn_bwd(tag, drb, dr, hb, a, b, s, wg, wu, wd, exchange=None):
    m, d = drb.shape
    nd, _, fs = wg.shape
    tm = _tile(m, ROW_TILE)

    def epi_ds(acc, er, orf):
        ds = acc[0]
        av, bv = er[0][...], er[1][...]
        sg = _sigmoid(av)
        orf[0][...] = (ds * bv * sg * (1.0 + av * (1.0 - sg))).astype(BF16)
        orf[1][...] = (ds * av * sg).astype(BF16)

    st = pl.BlockSpec((None, tm, fs), lambda i, j, k: (j, i, 0))
    da, db = _gemm(tag + "_ds", (m // tm, nd, 1),
                   [(drb, pl.BlockSpec((tm, d), lambda i, j, k: (i, 0)), wd, pl.BlockSpec((None, fs, d), lambda i, j, k: (j, 0, 0)))],
                   [0], [(tm, fs)], NT, [(a, st), (b, st)],
                   [(_sds((nd, m, fs), BF16), st), (_sds((nd, m, fs), BF16), st)], epi_ds)

    def epi_w(acc, er, orf):
        for o, v in zip(orf, acc):
            o[...] = v.astype(BF16)

    tr = _tile(m, GRAD_ROW_TILE)
    nkm = m // tr
    dwd = _gemm(tag + "_dwd", (nd, nkm),
                [(s, pl.BlockSpec((None, tr, fs), lambda j, k: (j, k, 0)), drb, pl.BlockSpec((tr, d), lambda j, k: (k, 0)))],
                [0], [(fs, d)], TN, [], [(_sds((nd, fs, d), BF16), pl.BlockSpec((None, fs, d), lambda j, k: (j, 0, 0)))], epi_w)[0]
    h_spec = pl.BlockSpec((tr, d), lambda j, k: (k, 0))
    g_spec = pl.BlockSpec((None, tr, fs), lambda j, k: (j, k, 0))
    w_out = pl.BlockSpec((None, d, fs), lambda j, k: (j, 0, 0))
    dwgu = _gemm(tag + "_dwgu", (nd, nkm), [(hb, h_spec, da, g_spec), (hb, h_spec, db, g_spec)], [0, 1],
                 [(d, fs)] * 2, TN, [], [(_sds((nd, d, fs), BF16), w_out), (_sds((nd, d, fs), BF16), w_out)], epi_w,
                 exchange([dwd]) if exchange else None)
    if exchange:
        (dwg, dwu), (dwd,) = dwgu
    else:
        dwg, dwu = dwgu

    def epi_dh(acc, er, orf):
        orf[0][...] = DN_ALPHA * er[0][...] + acc[0]

    gk = pl.BlockSpec((None, tm, fs), lambda i, k: (k, i, 0))
    wk = pl.BlockSpec((None, d, fs), lambda i, k: (k, 0, 0))
    row = pl.BlockSpec((tm, d), lambda i, k: (i, 0))
    dh = _gemm(tag + "_dh", (m // tm, nd), [(da, gk, wg, wk), (db, gk, wu, wk)], [0, 0], [(tm, d)], NT,
               [(dr, row)], [(_sds((m, d), F32), row)], epi_dh, exchange([dwg, dwu]) if exchange else None)
    if exchange:
        (dh,), (dwg, dwu) = dh
    else:
        dh = dh[0]
    return dh, dwg, dwu, dwd


def _merge_bwd(dmixb, w_out2, proj, b_gate, u_hg, u_sb, gate_col, ds):
    m, d = dmixb.shape
    nd = d // ds
    tm = _tile(m, ROW_TILE)
    c0 = gate_col // ds

    def epi(acc, er, orf):
        i = pl.program_id(1)
        dy = acc[0]
        g_hg = _sigmoid(er[0][...] + er[2][...])
        g_sb = _sigmoid(er[1][...] + er[3][...])
        orf[0][...] = (dy * g_hg).astype(BF16)
        orf[1][...] = (dy * g_sb).astype(BF16)
        dz_hg = dy * er[4][...] * g_hg * (1.0 - g_hg)
        dz_sb = dy * er[5][...] * g_sb * (1.0 - g_sb)
        orf[2][...] = dz_hg.astype(BF16)
        orf[3][...] = dz_sb.astype(BF16)

        @pl.when(i == 0)
        def _():
            orf[4][...] = jnp.zeros_like(orf[4])
            orf[5][...] = jnp.zeros_like(orf[5])

        orf[4][...] += jnp.sum(dz_hg, axis=0, keepdims=True)
        orf[5][...] += jnp.sum(dz_sb, axis=0, keepdims=True)

    tile = pl.BlockSpec((tm, ds), lambda j, i, k: (i, j))
    vec = pl.BlockSpec((1, ds), lambda j, i, k: (0, j))
    du_hg, du_sb, dz_hg, dz_sb, db_hg, db_sb = _gemm(
        "merge_bwd", (nd, m // tm, 1),
        [(dmixb, pl.BlockSpec((tm, d), lambda j, i, k: (i, 0)), w_out2, pl.BlockSpec((ds, d), lambda j, i, k: (j, 0)))],
        [0], [(tm, ds)], NT,
        [(proj, pl.BlockSpec((tm, ds), lambda j, i, k: (i, c0 + j))),
         (proj, pl.BlockSpec((tm, ds), lambda j, i, k: (i, c0 + nd + j))),
         (b_gate, vec), (b_gate, pl.BlockSpec((1, ds), lambda j, i, k: (0, nd + j))),
         (u_hg, tile), (u_sb, tile)],
        [(_sds((m, d), BF16), tile), (_sds((m, d), BF16), tile), (_sds((m, d), BF16), tile), (_sds((m, d), BF16), tile),
         (_sds((1, d), F32), vec), (_sds((1, d), F32), vec)], epi)
    return du_hg, du_sb, dz_hg, dz_sb, jnp.concatenate([db_hg, db_sb], axis=1)


def _grad_w(name, x, dy, nd_out):
    m, kx = x.shape
    n = dy.shape[1]
    ns = n // nd_out
    tm = _tile(m, GRAD_ROW_TILE)

    def epi(acc, er, orf):
        orf[0][...] = acc[0].astype(BF16)

    return _gemm(name, (nd_out, m // tm),
                 [(x, pl.BlockSpec((tm, kx), lambda j, k: (k, 0)), dy, pl.BlockSpec((tm, ns), lambda j, k: (k, j)))],
                 [0], [(kx, ns)], TN, [], [(_sds((nd_out, kx, ns), BF16), pl.BlockSpec((None, kx, ns), lambda j, k: (j, 0, 0)))], epi)[0]


def _grad_in(name, dy, w, add=None, carried=None):
    m = dy.shape[0]
    nd, kx, ns = w.shape
    tm = _tile(m, ROW_TILE)

    def epi(acc, er, orf):
        orf[0][...] = acc[0] if add is None else DN_ALPHA * er[0][...] + acc[0]

    row = pl.BlockSpec((tm, kx), lambda i, k: (i, 0))
    res = _gemm(name, (m // tm, nd),
                [(dy, pl.BlockSpec((tm, ns), lambda i, k: (i, k)), w, pl.BlockSpec((None, kx, ns), lambda i, k: (k, 0, 0)))],
                [0], [(tm, kx)], NT, [] if add is None else [(add, row)], [(_sds((m, kx), F32), row)], epi, carried)
    return res[0] if carried is None else (res[0][0], res[1])


def _tri(n, kind):
    r = lax.broadcasted_iota(jnp.int32, (n, n), 0)
    c = lax.broadcasted_iota(jnp.int32, (n, n), 1)
    return {"le": c <= r, "ge": c >= r, "gt": r > c, "lt": r < c}[kind]


def _dot_f32(a, b, dims=NN):
    return lax.dot_general(a, b, dims, preferred_element_type=F32, precision=lax.Precision.HIGHEST)


def _hgrn_gates(i, hq, hf, logits):
    lg = logits
    mx = jnp.maximum(lg[0:1], lg[1:2])
    e0 = jnp.exp(lg[0:1] - mx)
    lb = e0 / (e0 + jnp.exp(lg[1:2] - mx))
    sig = _sigmoid(hf)
    f = lb + (1.0 - lb) * sig
    valid = (i * BLOCK + lax.broadcasted_iota(jnp.int32, hf.shape, 0)) >= PAD
    g = jnp.where(valid, jnp.log(f), 0.0)
    k = jnp.where(valid, 1.0 - f, 0.0)
    sq = _sigmoid(hq)
    return hq * sq, k, g, sig, f, lb, valid, sq


def _pair_mask():
    s_i = lax.broadcasted_iota(jnp.int32, (CHUNK, CHUNK, 1), 0)
    t_i = lax.broadcasted_iota(jnp.int32, (CHUNK, CHUNK, 1), 1)
    return t_i >= s_i


def _heads_per_step(n_heads, want):
    return max(h for h in range(1, want + 1) if n_heads % h == 0)


def _hgrn_fwd(proj, logits, gn, n_heads, carried=None):
    m = proj.shape[0]
    nb = m // BLOCK
    w = n_heads * HEAD
    cpb = BLOCK // CHUNK
    hps = _heads_per_step(n_heads, 4)
    wide = hps * HEAD

    def body(hq_ref, hf_ref, hi_ref, hog_ref, lg_ref, gn_ref, o_ref, ohg_ref, st_all_ref, st_ref, q_s, k_s, v_s, b_s):
        i = pl.program_id(1)

        @pl.when(i == 0)
        def _():
            st_ref[...] = jnp.zeros_like(st_ref)

        q, k, g, _, _, _, _, _ = _hgrn_gates(i, hq_ref[...], hf_ref[...], lg_ref[...])
        q_s[...] = q
        k_s[...] = k
        v_s[...] = hi_ref[...]
        b_s[...] = _dot_f32(_tri(BLOCK, "le").astype(F32), g)
        causal = _pair_mask()

        def chunk(c, carry):
            sl = pl.ds(pl.multiple_of(c * CHUNK, CHUNK), CHUNK)
            prev = pl.ds(pl.multiple_of(jnp.maximum(c - 1, 0) * CHUNK, CHUNK), CHUNK)
            first = (c > 0).astype(F32)
            for hd in range(hps):
                cols = slice(hd * HEAD, (hd + 1) * HEAD)
                b = b_s[sl, cols] - b_s[prev, cols][CHUNK - 1:CHUNK, :] * first
                qc, kc, vc = q_s[sl, cols], k_s[sl, cols], v_s[sl, cols]
                st = st_ref[hd]
                st_all_ref[hd, c] = st.astype(BF16)
                o = lax.dot_general((qc * jnp.exp(b)).astype(BF16), st.astype(BF16), NT, preferred_element_type=F32)
                e = jnp.exp(jnp.minimum(b[None, :, :] - b[:, None, :], 0.0))
                p = jnp.sum(qc[None, :, :] * e * kc[:, None, :], axis=-1, keepdims=True)
                o_ref[sl, cols] = o + jnp.sum(jnp.where(causal, p, 0.0) * vc[:, None, :], axis=0)
                blast = b[CHUNK - 1:CHUNK, :]
                kd = kc * jnp.exp(blast - b)
                st_ref[hd] = st * jnp.exp(blast) + lax.dot_general(vc.astype(BF16), kd.astype(BF16), TN,
                                                                   preferred_element_type=F32)
            return carry

        lax.fori_loop(0, cpb, chunk, 0)
        for hd in range(hps):
            cols = slice(hd * HEAD, (hd + 1) * HEAD)
            o = o_ref[:, cols]
            n = o * lax.rsqrt(jnp.mean(o * o, axis=-1, keepdims=True) + RMS_EPS)
            hog = hog_ref[:, cols]
            ohg_ref[:, cols] = (n * gn_ref[:, cols] * hog * _sigmoid(hog)).astype(BF16)

    def col(group):
        return pl.BlockSpec((BLOCK, wide), lambda h, i: (i, group * (n_heads // hps) + h))

    vec = pl.BlockSpec((1, wide), lambda h, i: (0, h))
    tile = pl.BlockSpec((BLOCK, wide), lambda h, i: (i, h))
    return _pallas(
        "hgrn_fwd", body, (n_heads // hps, nb),
        [col(0), col(1), col(2), col(3), pl.BlockSpec((2, wide), lambda h, i: (0, h)), vec],
        [tile, tile, pl.BlockSpec((hps, cpb, HEAD, HEAD), lambda h, i: (h, i, 0, 0))],
        [_sds((m, w), F32), _sds((m, w), BF16), _sds((n_heads, m // CHUNK, HEAD, HEAD), BF16)],
        [pltpu.VMEM((hps, HEAD, HEAD), F32)] + [pltpu.VMEM((BLOCK, wide), F32)] * 4,
        (proj, proj, proj, proj, logits, gn), carried)


def _hgrn_bwd(proj, logits, gn, o_raw, do_hg, states, n_heads, carried=None):
    m = proj.shape[0]
    nb = m // BLOCK
    w = n_heads * HEAD
    cpb = BLOCK // CHUNK
    last_state = m // CHUNK - 1
    hps = _heads_per_step(n_heads, 2)
    wide = hps * HEAD

    def body(hq_ref, hf_ref, hi_ref, hog_ref, lg_ref, gn_ref, o_ref, do_ref, st_all_ref, st_next_ref,
             dhq_ref, dhf_ref, dhi_ref, dhog_ref, dgn_ref, dlb_ref,
             dst_ref, q_s, k_s, v_s, b_s, do_s, dq_s, dk_s, dv_s, ex_s):
        step = pl.program_id(1)
        i = nb - 1 - step

        @pl.when(step == 0)
        def _():
            dst_ref[...] = jnp.zeros_like(dst_ref)
            dgn_ref[...] = jnp.zeros_like(dgn_ref)
            dlb_ref[...] = jnp.zeros_like(dlb_ref)

        hq = hq_ref[...]
        q, k, g, sig, f, lb, valid, sq = _hgrn_gates(i, hq, hf_ref[...], lg_ref[...])
        q_s[...] = q
        k_s[...] = k
        v_s[...] = hi_ref[...]
        b_s[...] = _dot_f32(_tri(BLOCK, "le").astype(F32), g)

        hog = hog_ref[...]
        sg = _sigmoid(hog)
        sil = hog * sg
        gnv = gn_ref[...]
        dh = do_ref[...]
        dn = dh * gnv * sil
        for hd in range(hps):
            cols = slice(hd * HEAD, (hd + 1) * HEAD)
            o = o_ref[:, cols]
            rs = lax.rsqrt(jnp.mean(o * o, axis=-1, keepdims=True) + RMS_EPS)
            n = o * rs
            ex_s[:, cols] = n
            do_s[:, cols] = rs * (dn[:, cols] - n * jnp.mean(dn[:, cols] * n, axis=-1, keepdims=True))
        n = ex_s[...]
        dhog_ref[...] = (dh * n * gnv * sg * (1.0 + hog * (1.0 - sg))).astype(BF16)
        dgn_ref[...] += jnp.sum(dh * n * sil, axis=0, keepdims=True)
        causal = _pair_mask()

        def chunk(t, st_ends):
            c = cpb - 1 - t
            sl = pl.ds(pl.multiple_of(c * CHUNK, CHUNK), CHUNK)
            prev = pl.ds(pl.multiple_of(jnp.maximum(c - 1, 0) * CHUNK, CHUNK), CHUNK)
            first = (c > 0).astype(F32)
            starts = []
            for hd in range(hps):
                cols = slice(hd * HEAD, (hd + 1) * HEAD)
                b = b_s[sl, cols] - b_s[prev, cols][CHUNK - 1:CHUNK, :] * first
                qc, kc, vc, doc = q_s[sl, cols], k_s[sl, cols], v_s[sl, cols], do_s[sl, cols]
                eb = jnp.exp(b)
                blast = b[CHUNK - 1:CHUNK, :]
                ek = jnp.exp(blast - b)
                dst = dst_ref[hd]
                dstb = dst.astype(BF16)
                docb = doc.astype(BF16)
                st = st_all_ref[hd, c]
                starts.append(st)
                ex_s[sl, cols] = jnp.broadcast_to(jnp.sum(st_ends[hd].astype(F32) * dst, axis=0, keepdims=True),
                                                  (CHUNK, HEAD))
                dq = lax.dot_general(docb, st, NN, preferred_element_type=F32) * eb
                dk = lax.dot_general(vc.astype(BF16), dstb, NN, preferred_element_type=F32) * ek
                dv = lax.dot_general((kc * ek).astype(BF16), dstb, NT, preferred_element_type=F32)
                em = jnp.where(causal, jnp.exp(jnp.minimum(b[None, :, :] - b[:, None, :], 0.0)), 0.0)
                dp = jnp.sum(doc[None, :, :] * vc[:, None, :], axis=-1, keepdims=True)
                qe = qc[None, :, :] * em
                p = jnp.sum(qe * kc[:, None, :], axis=-1, keepdims=True)
                dq_s[sl, cols] = dq + jnp.sum(dp * em * kc[:, None, :], axis=0)
                dk_s[sl, cols] = dk + jnp.sum(dp * qe, axis=1)
                dv_s[sl, cols] = dv + jnp.sum(p * doc[None, :, :], axis=1)
                dst_ref[hd] = dst * jnp.exp(blast) + lax.dot_general(docb, (qc * eb).astype(BF16), TN,
                                                                     preferred_element_type=F32)
            return tuple(starts)

        lax.fori_loop(0, cpb, chunk, tuple(st_next_ref[hd, 0] for hd in range(hps)))
        dq, dk = dq_s[...], dk_s[...]
        r_i = lax.broadcasted_iota(jnp.int32, (BLOCK, BLOCK), 0)
        c_i = lax.broadcasted_iota(jnp.int32, (BLOCK, BLOCK), 1)
        within = ((c_i >= r_i) & (c_i // CHUNK == r_i // CHUNK)).astype(F32)
        rc = _dot_f32(within, q * dq - k * dk) + ex_s[...]
        df =jnp.where(valid, rc / f - dk, 0.0)
        dhf_ref[...] = (df * (1.0 - lb) * sig * (1.0 - sig)).astype(BF16)
        dlb_ref[...] += jnp.sum(df * (1.0 - sig), axis=0, keepdims=True)
        dhq_ref[...] = (dq * sq * (1.0 + hq * (1.0 - sq))).astype(BF16)
        dhi_ref[...] = dv_s[...].astype(BF16)

    def col(group):
        return pl.BlockSpec((BLOCK, wide), lambda h, s: (nb - 1 - s, group * (n_heads // hps) + h))

    vec = pl.BlockSpec((1, wide), lambda h, s: (0, h))
    tile = pl.BlockSpec((BLOCK, wide), lambda h, s: (nb - 1 - s, h))
    nxt = pl.BlockSpec((hps, 1, HEAD, HEAD), lambda h, s: (h, jnp.minimum((nb - s) * cpb, last_state), 0, 0))
    return _pallas(
        "hgrn_bwd", body, (n_heads // hps, nb),
        [col(0), col(1), col(2), col(3), pl.BlockSpec((2, wide), lambda h, s: (0, h)), vec, tile, tile,
         pl.BlockSpec((hps, cpb, HEAD, HEAD), lambda h, s: (h, nb - 1 - s, 0, 0)), nxt],
        [tile, tile, tile, tile, vec, vec],
        [_sds((m, w), BF16)] * 4 + [_sds((1, w), F32)] * 2,
        [pltpu.VMEM((hps, HEAD, HEAD), F32)] + [pltpu.VMEM((BLOCK, wide), F32)] * 9,
        (proj, proj, proj, proj, logits, gn, o_raw, do_hg, states, states), carried)


def _split_dot(x, t):
    hi = x.astype(BF16)
    lo = (x - hi.astype(F32)).astype(BF16)
    return jnp.dot(hi, t, preferred_element_type=F32) + jnp.dot(lo, t, preferred_element_type=F32)


def _sb_window(ref, j_left):
    parts = [ref[pl.ds(pl.multiple_of(jnp.maximum(j_left + u, 0) * BLOCK, BLOCK), BLOCK), :].astype(BF16)
             for u in range(SB_UNROLL)]
    return jnp.concatenate(parts, axis=0)


def _sb_scores(q, kw, i, j_left, scale):
    z = lax.dot_general(q, kw, NT, preferred_element_type=F32) * scale
    lp = jnp.log(1.0 + jnp.exp(-jnp.abs(z)))
    lbeta = jnp.minimum(z, 0.0) - lp
    qpos = i * BLOCK + lax.broadcasted_iota(jnp.int32, z.shape, 0)
    kpos = j_left * BLOCK + lax.broadcasted_iota(jnp.int32, z.shape, 1)
    mask = (kpos < qpos) & (kpos >= PAD)
    l1m = jnp.where(mask, lbeta - z, 0.0)
    return lbeta, l1m, mask


SB_DEAD = -104.0
SB_UNROLL = 3


def _sb_fwd(proj, n_heads, group0, carried=None):
    m = proj.shape[0]
    nb = m // BLOCK
    w = n_heads * HEAD
    scale = 1.0 / math.sqrt(HEAD)

    def body(q_ref, k_ref, v_ref, o_ref, start_ref, count_ref):
        h, i = pl.program_id(0), pl.program_id(1)
        q = q_ref[...].astype(BF16)
        tsuf = _tri(SB_UNROLL * BLOCK, "gt").astype(BF16)

        def live(carry):
            t, _, run = carry
            return (t <= i) & (jnp.max(run) > SB_DEAD)

        def step(carry):
            t, acc, run = carry
            j_left = i - t - (SB_UNROLL - 1)
            start_ref[...] = jnp.broadcast_to(run, (BLOCK, HEAD))
            lbeta, l1m, mask = _sb_scores(q, _sb_window(k_ref, j_left), i, j_left, scale)
            wgt = jnp.where(mask, jnp.exp(lbeta + _split_dot(l1m, tsuf) + run), 0.0)
            acc = acc + jnp.dot(wgt.astype(BF16), _sb_window(v_ref, j_left), preferred_element_type=F32)
            return t + SB_UNROLL, acc, run + jnp.sum(l1m, axis=-1, keepdims=True)

        t, acc, _ = lax.while_loop(live, step, (jnp.int32(0), jnp.zeros((BLOCK, HEAD), F32), jnp.zeros((BLOCK, 1), F32)))
        o_ref[...] = acc.astype(BF16)
        count_ref[h, i] = t.astype(F32)

    def whole(group):
        return pl.BlockSpec((m, HEAD), lambda h, i: (0, group * n_heads + h))

    return _pallas(
        "sb_fwd", body, (n_heads, nb),
        [pl.BlockSpec((BLOCK, HEAD), lambda h, i: (i, group0 * n_heads + h)), whole(group0 + 1), whole(group0 + 2)],
        [pl.BlockSpec((BLOCK, HEAD), lambda h, i: (i, h)), pl.BlockSpec((None, BLOCK, HEAD), lambda h, i: (h, i, 0)),
         pl.BlockSpec(memory_space=pltpu.SMEM)],
        [_sds((m, w), BF16), _sds((n_heads, m, HEAD), F32), _sds((n_heads, nb), F32)],
        [], (proj, proj, proj), carried)


def _sb_bwd(proj, do, start, count, n_heads, group0):
    m = proj.shape[0]
    nb = m // BLOCK
    w = n_heads * HEAD
    scale = 1.0 / math.sqrt(HEAD)

    def body(q_ref, k_ref, v_ref, do_ref, start_ref, count_ref, dq_ref, dk_ref, dv_ref, dk_s, dv_s):
        h, i = pl.program_id(0), pl.program_id(1)

        @pl.when(i == 0)
        def _():
            dk_s[...] = jnp.zeros_like(dk_s)
            dv_s[...] = jnp.zeros_like(dv_s)

        q = q_ref[...].astype(BF16)
        dob = do_ref[...].astype(BF16)
        count = count_ref[h, i].astype(jnp.int32)
        first = i + 1 - count
        tsuf = _tri(SB_UNROLL * BLOCK, "gt").astype(BF16)
        tpre = _tri(SB_UNROLL * BLOCK, "lt").astype(BF16)

        def step(t, carry):
            dq, right, psum = carry
            j_left = first + t * SB_UNROLL
            kw = _sb_window(k_ref, j_left)
            vw = _sb_window(v_ref, j_left)
            lbeta, l1m, mask = _sb_scores(q, kw, i, j_left, scale)
            right = jnp.where(t == 0, right, right - jnp.sum(l1m, axis=-1, keepdims=True))
            a = jnp.where(mask, jnp.exp(lbeta + _split_dot(l1m, tsuf) + right), 0.0)
            p = a * lax.dot_general(dob, vw, NT, preferred_element_type=F32)
            below = psum + _split_dot(p, tpre)
            beta = jnp.exp(lbeta)
            dz = (jnp.where(mask, p * (1.0 - beta) - below * beta, 0.0) * scale).astype(BF16)
            dq = dq + jnp.dot(dz, kw, preferred_element_type=F32)
            dkw = lax.dot_general(dz, q, TN, preferred_element_type=F32)
            dvw = lax.dot_general(a.astype(BF16), dob, TN, preferred_element_type=F32)
            for u in range(SB_UNROLL):
                rows = pl.ds(pl.multiple_of(jnp.maximum(j_left + u, 0) * BLOCK, BLOCK), BLOCK)
                dk_s[rows, :] += dkw[u * BLOCK:(u + 1) * BLOCK, :]
                dv_s[rows, :] += dvw[u * BLOCK:(u + 1) * BLOCK, :]
            return dq, right, psum + jnp.sum(p, axis=-1, keepdims=True)

        dq, _, _ = lax.fori_loop(0, count // SB_UNROLL, step,
                                 (jnp.zeros((BLOCK, HEAD), F32), start_ref[:, 0:1], jnp.zeros((BLOCK, 1), F32)))
        dq_ref[...] = dq.astype(BF16)

        @pl.when(i == nb - 1)
        def _():
            dk_ref[...] = dk_s[...].astype(BF16)
            dv_ref[...] = dv_s[...].astype(BF16)

    def whole(group):
        return pl.BlockSpec((m, HEAD), lambda h, i: (0, group * n_heads + h))

    tile = pl.BlockSpec((BLOCK, HEAD), lambda h, i: (i, h))
    col = pl.BlockSpec((m, HEAD), lambda h, i: (0, h))
    return pl.pallas_call(
        body, name="sb_bwd", grid=(n_heads, nb),
        in_specs=[pl.BlockSpec((BLOCK, HEAD), lambda h, i: (i, group0 * n_heads + h)), whole(group0 + 1), whole(group0 + 2),
                  tile, pl.BlockSpec((None, BLOCK, HEAD), lambda h, i: (h, i, 0)), pl.BlockSpec(memory_space=pltpu.SMEM)],
        out_specs=[tile, col, col],
        out_shape=[_sds((m, w), BF16)] * 3,
        scratch_shapes=[pltpu.VMEM((m, HEAD), F32)] * 2,
        compiler_params=_params(2),
    )(proj, proj, proj, do, start, count)


def _grad_w_rows(name, x, dy, nd_out):
    m, kx = x.shape
    n = dy.shape[1]
    ks = kx // nd_out
    tm = _tile(m, GRAD_ROW_TILE)

    def epi(acc, er, orf):
        orf[0][...] = acc[0].astype(BF16)

    return _gemm(name, (nd_out, m // tm),
                 [(x, pl.BlockSpec((tm, ks), lambda j, k: (k, j)), dy, pl.BlockSpec((tm, n), lambda j, k: (k, 0)))],
                 [0], [(ks, n)], TN, [], [(_sds((nd_out, ks, n), BF16), pl.BlockSpec((None, ks, n), lambda j, k: (j, 0, 0)))], epi)[0]


def _local_step(x, target, meta, vec, wts, shards=None):
    d = x.shape[1]
    width = vec["hg_norm_g"].shape[1]
    n_heads = width // HEAD
    gate_col = 7 * width
    h0 = jnp.concatenate([jnp.zeros((PAD, d), F32), meta, x], axis=0)
    h0b = h0.astype(BF16)
    wts = dict(wts)
    exchange = None if shards is None else _exchange_carried

    if shards is None:
        a1, b1, s1 = _ffn_up("ffn1_up", h0b, wts["ffn1_w_gate"], wts["ffn1_w_up"])
        r1, h1, h1b = _residual_ln("ffn1_down", s1, True, wts["ffn1_w_down"], h0, vec["ln1_g"], vec["ln1_b"], 0.5)
        proj = _in_proj(h1b, wts["w_in"])
        o_raw, o_hg, states = _hgrn_fwd(proj, vec["hg_lb_logits"], vec["hg_norm_g"], n_heads)
        o_sb, sb_start, sb_count = _sb_fwd(proj, n_heads, 4)
    else:
        half = shards["w_in"].shape[0] // 2
        (a1, b1, s1), (wts["ffn1_w_down"], w_in_top) = _ffn_up(
            "ffn1_up", h0b, wts["ffn1_w_gate"], wts["ffn1_w_up"],
            _gather_carried([shards["ffn1_w_down"], shards["w_in"][:half]]))
        (r1, h1, h1b), (w_in_bottom,) = _residual_ln("ffn1_down", s1, True, wts["ffn1_w_down"], h0, vec["ln1_g"],
                                                     vec["ln1_b"], 0.5, _gather_carried([shards["w_in"][half:]]))
        wts["w_in"] = jnp.concatenate([w_in_top, w_in_bottom], axis=1)
        with_proj = ("w_proj_hg", "w_proj_sb", "w_out", "ffn2_w_gate")
        proj, got = _in_proj(h1b, wts["w_in"], _gather_carried([shards[k] for k in with_proj]))
        wts.update(zip(with_proj, got))
        (o_raw, o_hg, states), (wts["ffn2_w_up"],) = _hgrn_fwd(proj, vec["hg_lb_logits"], vec["hg_norm_g"], n_heads,
                                                               _gather_carried([shards["ffn2_w_up"]]))
        (o_sb, sb_start, sb_count), (wts["ffn2_w_down"],) = _sb_fwd(proj, n_heads, 4,
                                                                    _gather_carried([shards["ffn2_w_down"]]))
    nd = wts["w_in"].shape[0]
    w_out = wts["w_out"]
    u_hg, u_sb, y = _proj_merge(o_hg, o_sb, wts["w_proj_hg"], wts["w_proj_sb"], proj, vec["b_gate"], gate_col)
    r2, h2, h2b = _residual_ln("out_proj", y, False, w_out.reshape(d, d), h1, vec["ln2_g"], vec["ln2_b"], 1.0)
    a2, b2, s2 = _ffn_up("ffn2_up", h2b,wts["ffn2_w_gate"], wts["ffn2_w_up"])
    r3, _, _ = _residual_ln("ffn2_down", s2, True, wts["ffn2_w_down"], h2, vec["ln3_g"], vec["ln3_b"], 0.5)

    dr3, dr3b, dg3, db3, loss = _ln_bwd("ln3_bwd", r3, vec["ln3_g"], 0.5, beta=vec["ln3_b"], target=target, first_row=BLOCK)
    dh2, dwg2, dwu2, dwd2 = _ffn_bwd("ffn2", dr3b, dr3, h2b, a2, b2, s2, wts["ffn2_w_gate"], wts["ffn2_w_up"],
                                     wts["ffn2_w_down"], exchange)
    dr2, dr2b, dg2, db2 = _ln_bwd("ln2_bwd", r2, vec["ln2_g"], 1.0, dy=dh2)
    du_hg, du_sb, dz_hg, dz_sb, dbg = _merge_bwd(dr2b, w_out.reshape(d, d), proj, vec["b_gate"], u_hg, u_sb, gate_col, d // nd)
    dw_out = _grad_w_rows("dw_out", y, dr2b, nd)
    dp_hg = _grad_w("dp_hg", o_hg, du_hg, nd)
    dp_sb = _grad_w("dp_sb", o_sb, du_sb, nd)
    do_hg = _grad_in("do_hg", du_hg, wts["w_proj_hg"])
    do_sb = _grad_in("do_sb", du_sb, wts["w_proj_sb"])
    hg = _hgrn_bwd(proj, vec["hg_lb_logits"], vec["hg_norm_g"], o_raw, do_hg, states, n_heads,
                   exchange([dw_out, dp_hg, dp_sb]) if exchange else None)
    if exchange:
        hg, (dw_out, dp_hg, dp_sb) = hg
    dhq, dhf, dhi, dhog, dgn, dlb = hg
    dsq, dsk, dsv = _sb_bwd(proj, do_sb, sb_start, sb_count, n_heads, 4)
    dproj = jnp.concatenate([dhq, dhf, dhi, dhog, dsq, dsk, dsv, dz_hg, dz_sb], axis=1)
    dw_in = _grad_w("dw_in", h1b, dproj, nd)
    dh1 = _grad_in("dh1", dproj, wts["w_in"], add=dr2, carried=exchange([dw_in]) if exchange else None)
    if exchange:
        dh1, (dw_in,) = dh1
    dr1, dr1b, dg1, db1 = _ln_bwd("ln1_bwd", r1, vec["ln1_g"], 0.5, dy=dh1)
    dh0, dwg1, dwu1, dwd1 = _ffn_bwd("ffn1", dr1b, dr1, h0b, a1, b1, s1, wts["ffn1_w_gate"], wts["ffn1_w_up"],
                                     wts["ffn1_w_down"], exchange)

    small = {"ln1_g": dg1, "ln1_b": db1, "ln2_g": dg2, "ln2_b": db2, "ln3_g": dg3, "ln3_b": db3,
             "b_gate": dbg, "hg_lb": dlb, "hg_norm_g": dgn}
    big = {"ffn1_w_gate": dwg1, "ffn1_w_up": dwu1, "ffn1_w_down": dwd1, "w_in": dw_in, "w_proj_hg": dp_hg,
           "w_proj_sb": dp_sb, "w_out": dw_out, "ffn2_w_gate": dwg2, "ffn2_w_up": dwu2, "ffn2_w_down": dwd2}
    return loss, dh0[BLOCK:], dh0[PAD:BLOCK], small, big


def _position():
    return lax.axis_index("x"), lax.axis_index("y"), lax.axis_index("c")


def _slot(px, py, pc):
    return 4 * px + 2 * py + pc


def _all_gather(shards):
    n = len(shards)

    def body(*refs):
        ins, outs = refs[:n], refs[n:2 * n]
        send_sems, recv_sems, local_sems = refs[2 * n:]
        x, y, c = _position()
        me, sibling = (x, y, c), (x, y, 1 - c)
        chips = [(1 - x, y), (x, 1 - y), (1 - x, 1 - y)]

        def copy(a, k, block, to, src=None):
            dst = outs[a].at[_slot(*block)]
            return pltpu.make_async_remote_copy(src_ref=dst if src is None else src, dst_ref=dst,
                                                send_sem=send_sems.at[a, k], recv_sem=recv_sems.at[a, k],
                                                device_id=to, device_id_type=MESH)

        mine = [pltpu.make_async_copy(ins[a], outs[a].at[_slot(*me)], local_sems.at[a]) for a in range(n)]
        for cp in mine:
            cp.start()
        first = []
        for a in range(n):
            first.append(copy(a, 0, me, sibling, src=ins[a]))
            first += [copy(a, 1 + j, me, (*chip, c), src=ins[a]) for j, chip in enumerate(chips)]
        for cp in first:
            cp.start()
        passed = []
        for j, chip in enumerate(chips):
            for a in range(n):
                copy(a, 1 + j, (*chip, c), me).wait_recv()
                cp = copy(a, 4 + j, (*chip, c), sibling)
                cp.start()
                passed.append(cp)
        for a in range(n):
            copy(a, 0, sibling, me).wait_recv()
        for j, chip in enumerate(chips):
            for a in range(n):
                copy(a, 4 + j, (*chip, 1 - c), me).wait_recv()
        for cp in first + passed:
            cp.wait_send()
        for cp in mine:
            cp.wait()

    return pl.pallas_call(
        body, name="all_gather", out_shape=[_sds((N_DEV,) + s.shape, s.dtype) for s in shards],
        in_specs=[ANY] * n, out_specs=[ANY] * n,
        scratch_shapes=[pltpu.SemaphoreType.DMA((n, 7)), pltpu.SemaphoreType.DMA((n, 7)), pltpu.SemaphoreType.DMA((n,))],
    )(*shards)


def _exchange_carried(grads):
    return _direct_copies(grads, [_sds(g.shape, g.dtype) for g in grads], lambda ref, slot: ref.at[slot])


def _gather_carried(shards):
    return _direct_copies(shards, [_sds((N_DEV,) + s.shape, s.dtype) for s in shards], lambda ref, slot: ref)


def _direct_copies(arrays, outs, block_for):
    n = len(arrays)

    def plan(ins, results, sems, arriving):
        send_sems, recv_sems, local_sems = sems
        x, y, c = _position()
        mine = _slot(x, y, c)
        peers = [(1 - x if k & 4 else x, 1 - y if k & 2 else y, 1 - c if k & 1 else c) for k in range(1, N_DEV)]
        own = [pltpu.make_async_copy(block_for(ins[a], mine), results[a].at[mine], local_sems.at[a]) for a in range(n)]
        remote = [pltpu.make_async_remote_copy(
            src_ref=block_for(ins[a], mine if arriving else _slot(*peer)),
            dst_ref=results[a].at[_slot(*peer) if arriving else mine],
            send_sem=send_sems.at[a, k], recv_sem=recv_sems.at[a, k], device_id=peer, device_id_type=MESH)
            for a in range(n) for k, peer in enumerate(peers)]
        return own, remote

    def start(ins, results, sems):
        own, sent = plan(ins, results, sems, False)
        for cp in own + sent:
            cp.start()

    def finish(ins, results, sems):
        _, landed = plan(ins, results, sems, True)
        for cp in landed:
            cp.wait_recv()
        own, sent = plan(ins, results, sems, False)
        for cp in sent:
            cp.wait_send()
        for cp in own:
            cp.wait()

    sems = [pltpu.SemaphoreType.DMA((n, 7)), pltpu.SemaphoreType.DMA((n, 7)), pltpu.SemaphoreType.DMA((n,))]
    return _Carried(list(arrays), outs, sems, start, finish)


def _all_reduce_rows(v):
    rows = v.shape[0]

    def body(v_ref, out_ref, buf, send_sems, recv_sems):
        x, y, c = _position()
        me, sibling = (x, y, c), (x, y, 1 - c)
        chips = [(1 - x, y), (x, 1 - y), (1 - x, 1 - y)]

        def copy(k, block, to, src=None):
            dst = buf.at[_slot(*block)]
            return pltpu.make_async_remote_copy(src_ref=dst if src is None else src, dst_ref=dst,
                                                send_sem=send_sems.at[k], recv_sem=recv_sems.at[k],
                                                device_id=to, device_id_type=MESH)

        first = [copy(0, me, sibling, src=v_ref)] + [copy(1 + j, me, (*chip, c), src=v_ref) for j, chip in enumerate(chips)]
        for cp in first:
            cp.start()
        buf[_slot(*me)] = v_ref[...]
        passed = [copy(4 + j, (*chip, c), sibling) for j, chip in enumerate(chips)]
        for j, chip in enumerate(chips):
            copy(1 + j, (*chip, c), me).wait_recv()
            passed[j].start()
        copy(0, sibling, me).wait_recv()
        for j, chip in enumerate(chips):
            copy(4 + j, (*chip, 1 - c), me).wait_recv()
        for cp in first + passed:
            cp.wait_send()
        total = buf[0]
        for s in range(1, N_DEV):
            total = total + buf[s]
        out_ref[...] = total

    vmem = pl.BlockSpec(memory_space=pltpu.VMEM)
    return pl.pallas_call(
        body, name="small_all_reduce", out_shape=_sds(v.shape, F32), in_specs=[vmem], out_specs=vmem,
        scratch_shapes=[pltpu.VMEM((N_DEV, rows, 128), F32), pltpu.SemaphoreType.DMA((7,)), pltpu.SemaphoreType.DMA((7,))],
    )(v)


def _adamw(name, w, m, v, contrib):
    r, c = w.shape
    n = contrib.shape[0]
    tr = _tile(r, 256)

    def body(w_ref, m_ref, v_ref, c_ref, g_out, d_out, m_out, v_out):
        g = c_ref[0].astype(F32)
        for s in range(1, n):
            g = g + c_ref[s].astype(F32)
        m2 = ADAM_B1 * m_ref[...] + (1.0 - ADAM_B1) * g
        v2 = ADAM_B2 * v_ref[...] + (1.0 - ADAM_B2) * (g * g)
        m_hat = m2 / (1.0 - ADAM_B1 ** ADAM_STEP)
        v_hat = v2 / (1.0 - ADAM_B2 ** ADAM_STEP)
        g_out[...] = g
        d_out[...] = -ADAM_LR * (m_hat / (jnp.sqrt(v_hat) + ADAM_EPS) + ADAM_WD * w_ref[...])
        m_out[...] = m2
        v_out[...] = v2

    tile = pl.BlockSpec((tr, c), lambda i: (i, 0))
    return pl.pallas_call(
        body, name=name, grid=(r // tr,), in_specs=[tile, tile, tile, pl.BlockSpec((n, tr, c), lambda i: (0, i, 0))],
        out_specs=[tile] * 4, out_shape=[_sds((r, c), F32)] * 4, compiler_params=_params(1),
    )(w, m, v, contrib)


def _lb_logits_grad(logits, dlb):
    def body(lg_ref, d_ref, out_ref):
        lg = lg_ref[...]
        mx = jnp.maximum(lg[0:1], lg[1:2])
        e0 = jnp.exp(lg[0:1] - mx)
        p0 = e0 / (e0 + jnp.exp(lg[1:2] - mx))
        g0 = d_ref[...] * p0 * (1.0 - p0)
        out_ref[0:1, :] = g0
        out_ref[1:2, :] = -g0

    return pl.pallas_call(body, name="lb_logits_grad", out_shape=_sds(logits.shape, F32))(logits, dlb)


BIG = ("ffn1_w_gate", "ffn1_w_up", "ffn1_w_down", "w_in", "w_proj_hg", "w_proj_sb", "w_out",
       "ffn2_w_gate", "ffn2_w_up", "ffn2_w_down")
VECTORS = ("ln1_g", "ln1_b", "b_gate", "hg_lb_logits", "hg_norm_g", "ln2_g", "ln2_b", "ln3_g", "ln3_b")
WEIGHTS = ("meta", "ln1_g", "ln1_b", "ffn1_w_gate", "ffn1_w_up", "ffn1_w_down", "w_in", "b_gate", "hg_lb_logits",
           "hg_norm_g", "w_proj_hg", "w_proj_sb", "w_out", "ln2_g", "ln2_b", "ffn2_w_gate", "ffn2_w_up",
           "ffn2_w_down", "ln3_g", "ln3_b")


def kernel(x, meta, ln1_g, ln1_b, ffn1_w_gate, ffn1_w_up, ffn1_w_down, w_in, b_gate, hg_lb_logits, hg_norm_g, w_proj_hg, w_proj_sb, w_out, ln2_g, ln2_b, ffn2_w_gate, ffn2_w_up, ffn2_w_down, ln3_g, ln3_b, loss_target, m_meta, m_ln1_g, m_ln1_b, m_ffn1_w_gate, m_ffn1_w_up, m_ffn1_w_down, m_w_in, m_b_gate, m_hg_lb_logits, m_hg_norm_g, m_w_proj_hg, m_w_proj_sb, m_w_out, m_ln2_g, m_ln2_b, m_ffn2_w_gate, m_ffn2_w_up, m_ffn2_w_down, m_ln3_g, m_ln3_b, v_meta, v_ln1_g, v_ln1_b, v_ffn1_w_gate, v_ffn1_w_up, v_ffn1_w_down, v_w_in, v_b_gate, v_hg_lb_logits, v_hg_norm_g, v_w_proj_hg, v_w_proj_sb, v_w_out, v_ln2_g, v_ln2_b, v_ffn2_w_gate, v_ffn2_w_up, v_ffn2_w_down, v_ln3_g, v_ln3_b):
    given = dict(locals())
    d = x.shape[-1]
    ds = meta.shape[1]

    shards = {k: given[k][0].astype(BF16) for k in BIG}
    first = ("ffn1_w_gate", "ffn1_w_up")
    gathered = _all_gather([meta] + [shards.pop(k) for k in first])
    meta_full = gathered[0].transpose(1, 0, 2).reshape(N_META, d)
    vec = {k: given[k] for k in VECTORS}
    loss, grad_x, dmeta, small, received = _local_step(x[0], loss_target[0], meta_full, vec, dict(zip(first, gathered[1:])),
                                                       shards)

    order =("ln1_g", "ln1_b", "ln2_g", "ln2_b", "ln3_g", "ln3_b", "b_gate", "hg_lb", "hg_norm_g")
    parts = [small[k].reshape(-1, 128) for k in order] + [dmeta.reshape(-1, 128), jnp.broadcast_to(loss, (8, 128))]
    total = _all_reduce_rows(jnp.concatenate(parts, axis=0))
    reduced, row = {}, 0
    for k, p in zip(order + ("meta", "loss"), parts):
        reduced[k] = total[row:row + p.shape[0]]
        row += p.shape[0]
    loss_out = reduced["loss"][0, 0]
    me = _slot(*_position())
    dmeta_mine = lax.dynamic_slice(reduced["meta"].reshape(N_META, d), (0, me * ds), (N_META, ds))
    dlogits = _lb_logits_grad(hg_lb_logits, reduced["hg_lb"].reshape(1, -1))

    grads, deltas, new_m, new_v = {}, {}, {}, {}
    for k in WEIGHTS:
        w = given[k]
        lead = w.shape[:-2]
        w2, m2, v2 = (a.reshape(a.shape[-2:]) for a in (w, given["m_" + k], given["v_" + k]))
        if k in BIG:
            contrib = received[k]
        elif k == "meta":
            contrib = dmeta_mine[None]
        elif k == "hg_lb_logits":
            contrib = dlogits[None]
        else:
            contrib = reduced[k].reshape((1,) + w2.shape)
        out = _adamw("adamw_" + k, w2, m2, v2, contrib)
        grads[k], deltas[k], new_m[k], new_v[k] = (o.reshape(lead + o.shape) for o in out)
    return (loss_out, grad_x[None], *[grads[k] for k in WEIGHTS], *[deltas[k] for k in WEIGHTS],
            *[new_m[k] for k in WEIGHTS], *[new_v[k] for k in WEIGHTS])
```

```python
import functools
import math

import jax
import jax.numpy as jnp
from jax import lax
from jax.experimental import pallas as pl
from jax.experimental.pallas import tpu as pltpu

F32 = jnp.float32
BF16 = jnp.bfloat16
MESH = pl.DeviceIdType.MESH

N_DEV = 8
N_META = 16
BLOCK = 128
PAD = BLOCK - N_META
HEAD = 128
CHUNK = 16
LN_EPS = 1e-5
RMS_EPS = 1e-6
DN_ALPHA = 2.0 ** 0.25
ADAM_LR, ADAM_B1, ADAM_B2, ADAM_EPS, ADAM_WD, ADAM_STEP = 0.001, 0.9, 0.999, 1e-08, 0.01, 10

VMEM_LIMIT_V7X = 60 * 1024 * 1024
ROW_TILE = 640
LN_ROW_TILE = 320
COL_TILE = 512
GRAD_ROW_TILE = 640

NN = (((1,), (0,)), ((), ()))
NT = (((1,), (1,)), ((), ()))
TN = (((0,), (0,)), ((), ()))


def _tile(n, pref, mult=16):
    best = None
    for t in range(mult, min(n, pref) + 1, mult):
        if n % t == 0:
            best = t
    return n if best is None else best


def _params(n_axes):
    return pltpu.CompilerParams(dimension_semantics=("arbitrary",) * n_axes, vmem_limit_bytes=VMEM_LIMIT_V7X)


def _sigmoid(x):
    return 1.0 / (1.0 + jnp.exp(-x))


class _Carried:
    def __init__(self, ins, outs, sems, start, finish):
        self.ins, self.outs, self.sems, self.start, self.finish = ins, outs, sems, start, finish


ANY = pl.BlockSpec(memory_space=pl.ANY)


def _pallas(name, body, grid, in_specs, out_specs, out_shape, scratch, operands, carried=None):
    if carried is None:
        return pl.pallas_call(body, name=name, grid=grid, in_specs=in_specs, out_specs=out_specs, out_shape=out_shape,
                              scratch_shapes=scratch, compiler_params=_params(len(grid)))(*operands)
    n_in, n_out, n_scr = len(in_specs), len(out_specs), len(scratch)
    c_in, c_out = len(carried.ins), len(carried.outs)

    def wrapped(*refs):
        ins, rest = refs[:n_in], refs[n_in:]
        c_ins, rest = rest[:c_in], rest[c_in:]
        outs, rest = rest[:n_out], rest[n_out:]
        c_outs, rest = rest[:c_out], rest[c_out:]
        scr, c_sems = rest[:n_scr], rest[n_scr:]
        first = last = None
        for axis, size in enumerate(grid):
            at0, at_end = pl.program_id(axis) == 0, pl.program_id(axis) == size - 1
            first = at0 if first is None else first & at0
            last = at_end if last is None else last & at_end

        @pl.when(first)
        def _():
            carried.start(c_ins, c_outs, c_sems)

        body(*ins, *outs, *scr)

        @pl.when(last)
        def _():
            carried.finish(c_ins, c_outs, c_sems)

    res = pl.pallas_call(
        wrapped, name=name, grid=grid, in_specs=list(in_specs) + [ANY] * c_in, out_specs=list(out_specs) + [ANY] * c_out,
        out_shape=list(out_shape) + list(carried.outs), scratch_shapes=list(scratch) + list(carried.sems),
        compiler_params=_params(len(grid)),
    )(*operands, *carried.ins)
    return res[:n_out], res[n_out:]


def _gemm(name, grid, pairs, acc_of, acc_shapes, dims, extras, outs, epilogue, carried=None):
    n_extra, n_out = len(extras), len(outs)
    nk = grid[-1]
    k_axis = len(grid) - 1
    operands, in_specs, where = [], [], {}
    for a, a_spec, b, b_spec in pairs:
        for arr, spec in ((a, a_spec), (b, b_spec)):
            if (id(arr), id(spec)) not in where:
                where[(id(arr), id(spec))] = len(operands)
                operands.append(arr)
                in_specs.append(spec)
    n_mat = len(operands)
    slots = [(where[(id(a), id(a_spec))], where[(id(b), id(b_spec))]) for a, a_spec, b, b_spec in pairs]

    def body(*refs):
        er = refs[n_mat:n_mat + n_extra]
        orf = refs[n_mat + n_extra:n_mat + n_extra + n_out]
        accs = refs[n_mat + n_extra + n_out:]

        def part(p):
            a_ref, b_ref = refs[slots[p][0]], refs[slots[p][1]]
            if len(a_ref.shape) == 2:
                return lax.dot_general(a_ref[...].astype(BF16), b_ref[...].astype(BF16), dims, preferred_element_type=F32)
            total = None
            for s in range(a_ref.shape[0]):
                d = lax.dot_general(a_ref[s].astype(BF16), b_ref[s].astype(BF16), dims, preferred_element_type=F32)
                total = d if total is None else total + d
            return total

        if nk == 1:
            vals = [None] * len(acc_shapes)
            for p in range(len(pairs)):
                d = part(p)
                vals[acc_of[p]] = d if vals[acc_of[p]] is None else vals[acc_of[p]] + d
            epilogue(vals, er, orf)
        else:
            k = pl.program_id(k_axis)

            @pl.when(k == 0)
            def _():
                for acc in accs:
                    acc[...] = jnp.zeros_like(acc)

            for p in range(len(pairs)):
                accs[acc_of[p]][...] += part(p)

            @pl.when(k == nk - 1)
            def _():
                epilogue([acc[...] for acc in accs], er, orf)

    for e, e_spec in extras:
        operands.append(e)
        in_specs.append(e_spec)
    scratch = [] if nk == 1 else [pltpu.VMEM(s, F32) for s in acc_shapes]
    return _pallas(name, body, grid, in_specs, [s for _, s in outs], [o for o, _ in outs], scratch, operands, carried)


def _sds(shape, dtype):
    return jax.ShapeDtypeStruct(shape, dtype)


def _ln_rows(r, g, b):
    mu = jnp.mean(r, axis=-1, keepdims=True)
    xc = r - mu
    var = jnp.mean(xc * xc, axis=-1, keepdims=True)
    return xc * lax.rsqrt(var + LN_EPS) * g + b


def _ffn_up(name, hb, wg, wu, carried=None):
    m, d = hb.shape
    nd, _, fs = wg.shape
    tm = _tile(m, ROW_TILE)

    def epi(acc, er, orf):
        a, b = acc
        orf[0][...] = a
        orf[1][...] = b
        orf[2][...] = (a * _sigmoid(a) * b).astype(BF16)

    h_spec = pl.BlockSpec((tm, d), lambda i, j, k: (i, 0))
    w_spec = pl.BlockSpec((None, d, fs), lambda i, j, k: (j, 0, 0))
    o_spec = pl.BlockSpec((None, tm, fs), lambda i, j, k: (j, i, 0))
    return _gemm(name, (m // tm, nd, 1), [(hb, h_spec, wg, w_spec), (hb, h_spec, wu, w_spec)], [0, 1],
                 [(tm, fs)] * 2, NN, [],
                 [(_sds((nd, m, fs), F32), o_spec), (_sds((nd, m, fs), F32), o_spec), (_sds((nd, m, fs), BF16), o_spec)], epi,
                 carried)


def _residual_ln(name, a, a_stacked, w, h_in, g, beta, scale, carried=None):
    d = w.shape[-1]
    m = h_in.shape[0]
    tm = _tile(m, LN_ROW_TILE)

    def epi(acc, er, orf):
        r = DN_ALPHA * er[0][...] + scale * acc[0]
        h = _ln_rows(r, er[1][...], er[2][...])
        orf[0][...] = r
        orf[1][...] = h
        orf[2][...] = h.astype(BF16)

    once = pl.Buffered(1)
    if a_stacked:
        a_spec = pl.BlockSpec((a.shape[0], tm, a.shape[2]), lambda i, k: (0, i, 0))
        w_spec = pl.BlockSpec(w.shape, lambda i, k: (0, 0, 0), pipeline_mode=once)
    else:
        a_spec = pl.BlockSpec((tm, a.shape[1]), lambda i, k: (i, 0))
        w_spec = pl.BlockSpec(w.shape, lambda i, k: (0, 0), pipeline_mode=once)
    row = pl.BlockSpec((tm, d), lambda i, k: (i, 0))
    vec = pl.BlockSpec((1, d), lambda i, k: (0, 0))
    return _gemm(name, (m // tm, 1), [(a, a_spec, w, w_spec)], [0], [(tm, d)], NN,
                 [(h_in, row), (g, vec), (beta, vec)],
                 [(_sds((m, d), F32), row), (_sds((m, d), F32), row), (_sds((m, d), BF16), row)], epi, carried)


def _in_proj(hb, w_in, carried=None):
    m, d = hb.shape
    nd, _, cs = w_in.shape
    tm = _tile(m, ROW_TILE)

    def epi(acc, er, orf):
        orf[0][...] = acc[0]

    res = _gemm("in_proj", (m // tm, nd, 1),
                [(hb, pl.BlockSpec((tm, d), lambda i, j, k: (i, 0)), w_in, pl.BlockSpec((None, d, cs), lambda i, j, k: (j, 0, 0)))],
                [0], [(tm, cs)], NN, [], [(_sds((m, nd * cs), F32), pl.BlockSpec((tm, cs), lambda i, j, k: (i, j)))], epi,
                carried)
    return res[0] if carried is None else (res[0][0], res[1])


def _proj_merge(o_hg, o_sb, p_hg, p_sb, proj, b_gate, gate_col):
    m, w = o_hg.shape
    d = p_hg.shape[1]
    tm = _tile(m, ROW_TILE)
    tn = _tile(d, COL_TILE, 128)
    nn = d // tn
    c0 = gate_col // tn

    def epi(acc, er, orf):
        u_hg, u_sb = acc
        g_hg = _sigmoid(er[0][...] + er[2][...])
        g_sb = _sigmoid(er[1][...] + er[3][...])
        orf[0][...] = u_hg
        orf[1][...] = u_sb
        orf[2][...] = (g_hg * u_hg + g_sb * u_sb).astype(BF16)

    o_spec = pl.BlockSpec((tm, w), lambda i, j, k: (i, 0))
    p_spec = pl.BlockSpec((w, tn), lambda i, j, k: (0, j))
    out = pl.BlockSpec((tm, tn), lambda i, j, k: (i, j))
    return _gemm("proj_merge", (m // tm, nn, 1), [(o_hg, o_spec, p_hg, p_spec), (o_sb, o_spec, p_sb, p_spec)], [0, 1],
                 [(tm, tn)] * 2, NN,
                 [(proj, pl.BlockSpec((tm, tn), lambda i, j, k: (i, c0 + j))),
                  (proj, pl.BlockSpec((tm, tn), lambda i, j, k: (i, c0 + nn + j))),
                  (b_gate, pl.BlockSpec((1, tn), lambda i, j, k: (0, j))),
                  (b_gate, pl.BlockSpec((1, tn), lambda i, j, k: (0, nn + j)))],
                 [(_sds((m, d), F32), out), (_sds((m, d), F32), out), (_sds((m, d), BF16), out)], epi)


def _ln_bwd(name, r, g, out_scale, dy=None, beta=None, target=None, first_row=0):
    m, d = r.shape
    tm = _tile(m, LN_ROW_TILE if target is None else BLOCK)
    with_loss = target is not None
    skip = first_row // tm if with_loss else 0
    assert not with_loss or first_row % tm == 0

    def body(*refs):
        if with_loss:
            r_ref, g_ref, b_ref, t_ref, dr_ref, drb_ref, dg_ref, db_ref, loss_ref = refs
        else:
            r_ref, g_ref, dy_ref, dr_ref, drb_ref, dg_ref, db_ref = refs
        i = pl.program_id(0)
        x = r_ref[...]
        mu = jnp.mean(x, axis=-1, keepdims=True)
        xc = x - mu
        var = jnp.mean(xc * xc, axis=-1, keepdims=True)
        rstd = lax.rsqrt(var + LN_EPS)
        xhat = xc * rstd
        gv = g_ref[...]
        if with_loss:
            err = xhat * gv + b_ref[...] - t_ref[...]
            live = (i >= skip).astype(F32)
            dyv = err * (live / d)
            part = 0.5 * live * jnp.sum(jnp.sum(err * err, axis=-1, keepdims=True), axis=0, keepdims=True) / d
        else:
            dyv = dy_ref[...]
        dxh = dyv * gv
        m1 = jnp.mean(dxh, axis=-1, keepdims=True)
        m2 = jnp.mean(dxh * xhat, axis=-1, keepdims=True)
        dr = rstd * (dxh - m1 - xhat * m2)
        dr_ref[...] = dr
        drb_ref[...] = (out_scale * dr).astype(BF16)

        @pl.when(i == 0)
        def _():
            dg_ref[...] = jnp.zeros_like(dg_ref)
            db_ref[...] = jnp.zeros_like(db_ref)
            if with_loss:
                loss_ref[...] = jnp.zeros_like(loss_ref)

        dg_ref[...] += jnp.sum(dyv * xhat, axis=0, keepdims=True)
        db_ref[...] += jnp.sum(dyv, axis=0, keepdims=True)
        if with_loss:
            loss_ref[...] += jnp.broadcast_to(part, loss_ref.shape)

    row = pl.BlockSpec((tm, d), lambda i: (i, 0))
    vec = pl.BlockSpec((1, d), lambda i: (0, 0))
    out_shape = [_sds((m, d), F32), _sds((m, d), BF16), _sds((1, d), F32), _sds((1, d), F32)]
    out_specs = [row, row, vec, vec]
    if with_loss:
        operands = [r, g, beta, target]
        in_specs = [row, vec, vec, pl.BlockSpec((tm, d), lambda i: (jnp.maximum(i - skip, 0), 0))]
        out_shape.append(_sds((1, BLOCK), F32))
        out_specs.append(pl.BlockSpec((1, BLOCK), lambda i: (0, 0)))
    else:
        operands = [r, g, dy]
        in_specs = [row, vec, row]
    return pl.pallas_call(body, name=name, grid=(m // tm,), in_specs=in_specs, out_specs=out_specs, out_shape=out_shape,
                          compiler_params=_params(1))(*operands)


def _ffn_bwd(tag, drb, dr, hb, a, b, s, wg, wu, wd, exchange=None):
    m, d = drb.shape
    nd, _, fs = wg.shape
    tm = _tile(m, ROW_TILE)

    def epi_ds(acc, er, orf):
        ds = acc[0]
        av, bv = er[0][...], er[1][...]
        sg = _sigmoid(av)
        orf[0][...] = (ds * bv * sg * (1.0 + av * (1.0 - sg))).astype(BF16)
        orf[1][...] = (ds * av * sg).astype(BF16)

    st = pl.BlockSpec((None, tm, fs), lambda i, j, k: (j, i, 0))
    da, db = _gemm(tag + "_ds", (m // tm, nd, 1),
                   [(drb, pl.BlockSpec((tm, d), lambda i, j, k: (i, 0)), wd, pl.BlockSpec((None, fs, d), lambda i, j, k: (j, 0, 0)))],
                   [0], [(tm, fs)], NT, [(a, st), (b, st)],
                   [(_sds((nd, m, fs), BF16), st), (_sds((nd, m, fs), BF16), st)], epi_ds)

    def epi_w(acc, er, orf):
        for o, v in zip(orf, acc):
            o[...] = v.astype(BF16)

    tr = _tile(m, GRAD_ROW_TILE)
    nkm = m // tr
    dwd = _gemm(tag + "_dwd", (nd, nkm),
                [(s, pl.BlockSpec((None, tr, fs), lambda j, k: (j, k, 0)), drb, pl.BlockSpec((tr, d), lambda j, k: (k, 0)))],
                [0], [(fs, d)], TN, [], [(_sds((nd, fs, d), BF16), pl.BlockSpec((None, fs, d), lambda j, k: (j, 0, 0)))], epi_w)[0]
    h_spec = pl.BlockSpec((tr, d), lambda j, k: (k, 0))
    g_spec = pl.BlockSpec((None, tr, fs), lambda j, k: (j, k, 0))
    w_out = pl.BlockSpec((None, d, fs), lambda j, k: (j, 0, 0))
    dwgu = _gemm(tag + "_dwgu", (nd, nkm), [(hb, h_spec, da, g_spec), (hb, h_spec, db, g_spec)], [0, 1],
                 [(d, fs)] * 2, TN, [], [(_sds((nd, d, fs), BF16), w_out), (_sds((nd, d, fs), BF16), w_out)], epi_w,
                 exchange([dwd]) if exchange else None)
    if exchange:
        (dwg, dwu), (dwd,) = dwgu
    else:
        dwg, dwu = dwgu

    def epi_dh(acc, er, orf):
        orf[0][...] = DN_ALPHA * er[0][...] + acc[0]

    gk = pl.BlockSpec((None, tm, fs), lambda i, k: (k, i, 0))
    wk = pl.BlockSpec((None, d, fs), lambda i, k: (k, 0, 0))
    row = pl.BlockSpec((tm, d), lambda i, k: (i, 0))
    dh = _gemm(tag + "_dh", (m // tm, nd), [(da, gk, wg, wk), (db, gk, wu, wk)], [0, 0], [(tm, d)], NT,
               [(dr, row)], [(_sds((m, d), F32), row)], epi_dh, exchange([dwg, dwu]) if exchange else None)
    if exchange:
        (dh,), (dwg, dwu) = dh
    else:
        dh = dh[0]
    return dh, dwg, dwu, dwd


def _merge_bwd(dmixb, w_out2, proj, b_gate, u_hg, u_sb, gate_col):
    m, d = dmixb.shape
    ds = _tile(d, COL_TILE, 128)
    nd = d // ds
    tm = _tile(m, ROW_TILE)
    c0 = gate_col // ds

    def epi(acc, er, orf):
        i = pl.program_id(1)
        dy = acc[0]
        g_hg = _sigmoid(er[0][...] + er[2][...])
        g_sb = _sigmoid(er[1][...] + er[3][...])
        orf[0][...] = (dy * g_hg).astype(BF16)
        orf[1][...] = (dy * g_sb).astype(BF16)
        dz_hg = dy * er[4][...] * g_hg * (1.0 - g_hg)
        dz_sb = dy * er[5][...] * g_sb * (1.0 - g_sb)
        orf[2][...] = dz_hg.astype(BF16)
        orf[3][...] = dz_sb.astype(BF16)

        @pl.when(i == 0)
        def _():
            orf[4][...] = jnp.zeros_like(orf[4])
            orf[5][...] = jnp.zeros_like(orf[5])

        orf[4][...] += jnp.sum(dz_hg, axis=0, keepdims=True)
        orf[5][...] += jnp.sum(dz_sb, axis=0, keepdims=True)

    tile = pl.BlockSpec((tm, ds), lambda j, i, k: (i, j))
    vec = pl.BlockSpec((1, ds), lambda j, i, k: (0, j))
    du_hg, du_sb, dz_hg, dz_sb, db_hg, db_sb = _gemm(
        "merge_bwd", (nd, m // tm, 1),
        [(dmixb, pl.BlockSpec((tm, d), lambda j, i, k: (i, 0)), w_out2, pl.BlockSpec((ds, d), lambda j, i, k: (j, 0)))],
        [0], [(tm, ds)], NT,
        [(proj, pl.BlockSpec((tm, ds), lambda j, i, k: (i, c0 + j))),
         (proj, pl.BlockSpec((tm, ds), lambda j, i, k: (i, c0 + nd + j))),
         (b_gate, vec), (b_gate, pl.BlockSpec((1, ds), lambda j, i, k: (0, nd + j))),
         (u_hg, tile), (u_sb, tile)],
        [(_sds((m, d), BF16), tile), (_sds((m, d), BF16), tile), (_sds((m, d), BF16), tile), (_sds((m, d), BF16), tile),
         (_sds((1, d), F32), vec), (_sds((1, d), F32), vec)], epi)
    return du_hg, du_sb, dz_hg, dz_sb, jnp.concatenate([db_hg, db_sb], axis=1)


def _grad_w(name, x, dy, nd_out):
    m, kx = x.shape
    n = dy.shape[1]
    ns = n // nd_out
    tm = _tile(m, GRAD_ROW_TILE)

    def epi(acc, er, orf):
        orf[0][...] = acc[0].astype(BF16)

    return _gemm(name, (nd_out, m // tm),
                 [(x, pl.BlockSpec((tm, kx), lambda j, k: (k, 0)), dy, pl.BlockSpec((tm, ns), lambda j, k: (k, j)))],
                 [0], [(kx, ns)], TN, [], [(_sds((nd_out, kx, ns), BF16), pl.BlockSpec((None, kx, ns), lambda j, k: (j, 0, 0)))], epi)[0]


def _grad_in_whole(name, dy, w2):
    m, n = dy.shape
    kx = w2.shape[0]
    tm = _tile(m, ROW_TILE)

    def epi(acc, er, orf):
        orf[0][...] = acc[0]

    return _gemm(name, (m // tm, 1),
                 [(dy, pl.BlockSpec((tm, n), lambda i, k: (i, 0)),
                   w2, pl.BlockSpec((kx, n), lambda i, k: (0, 0), pipeline_mode=pl.Buffered(1)))],
                 [0], [(tm, kx)], NT, [], [(_sds((m, kx), F32), pl.BlockSpec((tm, kx), lambda i, k: (i, 0)))], epi)[0]


def _grad_in(name, dy, w, add=None, carried=None):
    m = dy.shape[0]
    nd, kx, ns = w.shape
    tm = _tile(m, ROW_TILE)

    def epi(acc, er, orf):
        orf[0][...] = acc[0] if add is None else DN_ALPHA * er[0][...] + acc[0]

    row = pl.BlockSpec((tm, kx), lambda i, k: (i, 0))
    res = _gemm(name, (m // tm, nd),
                [(dy, pl.BlockSpec((tm, ns), lambda i, k: (i, k)), w, pl.BlockSpec((None, kx, ns), lambda i, k: (k, 0, 0)))],
                [0], [(tm, kx)], NT, [] if add is None else [(add, row)], [(_sds((m, kx), F32), row)], epi, carried)
    return res[0] if carried is None else (res[0][0], res[1])


def _tri(n, kind):
    r = lax.broadcasted_iota(jnp.int32, (n, n), 0)
    c = lax.broadcasted_iota(jnp.int32, (n, n), 1)
    return {"le": c <= r, "ge": c >= r, "gt": r > c, "lt": r < c}[kind]


def _dot_f32(a, b, dims=NN):
    return lax.dot_general(a, b, dims, preferred_element_type=F32, precision=lax.Precision.HIGHEST)


def _hgrn_gates(i, hq, hf, logits):
    lg = logits
    mx = jnp.maximum(lg[0:1], lg[1:2])
    e0 = jnp.exp(lg[0:1] - mx)
    lb = e0 / (e0 + jnp.exp(lg[1:2] - mx))
    sig = _sigmoid(hf)
    f = lb + (1.0 - lb) * sig
    valid = (i * BLOCK + lax.broadcasted_iota(jnp.int32, hf.shape, 0)) >= PAD
    g = jnp.where(valid, jnp.log(f), 0.0)
    k = jnp.where(valid, 1.0 - f, 0.0)
    sq = _sigmoid(hq)
    return hq * sq, k, g, sig, f, lb, valid, sq


def _pair_mask():
    s_i = lax.broadcasted_iota(jnp.int32, (CHUNK, CHUNK, 1), 0)
    t_i = lax.broadcasted_iota(jnp.int32, (CHUNK, CHUNK, 1), 1)
    return t_i >= s_i


def _heads_per_step(n_heads, want):
    return max(h for h in range(1, want + 1) if n_heads % h == 0)


def _hgrn_fwd(proj, logits, gn, n_heads, carried=None):
    m = proj.shape[0]
    nb = m // BLOCK
    w = n_heads * HEAD
    cpb = BLOCK // CHUNK
    hps = _heads_per_step(n_heads, 4)
    wide = hps * HEAD

    def body(hq_ref, hf_ref, hi_ref, hog_ref, lg_ref, gn_ref, o_ref, ohg_ref, st_all_ref, st_ref, q_s, k_s, v_s, b_s):
        i = pl.program_id(1)

        @pl.when(i == 0)
        def _():
            st_ref[...] = jnp.zeros_like(st_ref)

        q, k, g, _, _, _, _, _ = _hgrn_gates(i, hq_ref[...], hf_ref[...], lg_ref[...])
        q_s[...] = q
        k_s[...] = k
        v_s[...] = hi_ref[...]
        b_s[...] = _dot_f32(_tri(BLOCK, "le").astype(F32), g)
        causal = _pair_mask()

        def chunk(c, carry):
            sl = pl.ds(pl.multiple_of(c * CHUNK, CHUNK), CHUNK)
            prev = pl.ds(pl.multiple_of(jnp.maximum(c - 1, 0) * CHUNK, CHUNK), CHUNK)
            first = (c > 0).astype(F32)
            for hd in range(hps):
                cols = slice(hd * HEAD, (hd + 1) * HEAD)
                b = b_s[sl, cols] - b_s[prev, cols][CHUNK - 1:CHUNK, :] * first
                qc, kc, vc = q_s[sl, cols], k_s[sl, cols], v_s[sl, cols]
                st = st_ref[hd]
                st_all_ref[hd, c] = st.astype(BF16)
                o = lax.dot_general((qc * jnp.exp(b)).astype(BF16), st.astype(BF16), NT, preferred_element_type=F32)
                e = jnp.exp(jnp.minimum(b[None, :, :] - b[:, None, :], 0.0))
                p = jnp.sum(qc[None, :, :] * e * kc[:, None, :], axis=-1, keepdims=True)
                o_ref[sl, cols] = o + jnp.sum(jnp.where(causal, p, 0.0) * vc[:, None, :], axis=0)
                blast = b[CHUNK - 1:CHUNK, :]
                kd = kc * jnp.exp(blast - b)
                st_ref[hd] = st * jnp.exp(blast) + lax.dot_general(vc.astype(BF16), kd.astype(BF16), TN,
                                                                   preferred_element_type=F32)
            return carry

        lax.fori_loop(0, cpb, chunk, 0)
        for hd in range(hps):
            cols = slice(hd * HEAD, (hd + 1) * HEAD)
            o = o_ref[:, cols]
            n = o * lax.rsqrt(jnp.mean(o * o, axis=-1, keepdims=True) + RMS_EPS)
            hog = hog_ref[:, cols]
            ohg_ref[:, cols] = (n * gn_ref[:, cols] * hog * _sigmoid(hog)).astype(BF16)

    def col(group):
        return pl.BlockSpec((BLOCK, wide), lambda h, i: (i, group * (n_heads // hps) + h))

    vec = pl.BlockSpec((1, wide), lambda h, i: (0, h))
    tile = pl.BlockSpec((BLOCK, wide), lambda h, i: (i, h))
    return _pallas(
        "hgrn_fwd", body, (n_heads // hps, nb),
        [col(0), col(1), col(2), col(3), pl.BlockSpec((2, wide), lambda h, i: (0, h)), vec],
        [tile, tile, pl.BlockSpec((hps, cpb, HEAD, HEAD), lambda h, i: (h, i, 0, 0))],
        [_sds((m, w), F32), _sds((m, w), BF16), _sds((n_heads, m // CHUNK, HEAD, HEAD), BF16)],
        [pltpu.VMEM((hps, HEAD, HEAD), F32)] + [pltpu.VMEM((BLOCK, wide), F32)] * 4,
        (proj, proj, proj, proj, logits, gn), carried)


def _hgrn_bwd(proj, logits, gn, o_raw, do_hg, states, n_heads, carried=None):
    m = proj.shape[0]
    nb = m // BLOCK
    w = n_heads * HEAD
    cpb = BLOCK // CHUNK
    last_state = m // CHUNK - 1
    hps = _heads_per_step(n_heads, 2)
    wide = hps * HEAD

    def body(hq_ref, hf_ref, hi_ref, hog_ref, lg_ref, gn_ref, o_ref, do_ref, st_all_ref, st_next_ref,
             dhq_ref, dhf_ref, dhi_ref, dhog_ref, dgn_ref, dlb_ref,
             dst_ref, q_s, k_s, v_s, b_s, do_s, dq_s, dk_s, dv_s, ex_s):
        step = pl.program_id(1)
        i = nb - 1 - step

        @pl.when(step == 0)
        def _():
            dst_ref[...] = jnp.zeros_like(dst_ref)
            dgn_ref[...] = jnp.zeros_like(dgn_ref)
            dlb_ref[...] = jnp.zeros_like(dlb_ref)

        hq = hq_ref[...]
        q, k, g, sig, f, lb, valid, sq = _hgrn_gates(i, hq, hf_ref[...], lg_ref[...])
        q_s[...] = q
        k_s[...] = k
        v_s[...] = hi_ref[...]
        b_s[...] = _dot_f32(_tri(BLOCK, "le").astype(F32), g)

        hog = hog_ref[...]
        sg = _sigmoid(hog)
        sil = hog * sg
        gnv = gn_ref[...]
        dh = do_ref[...]
        dn = dh * gnv * sil
        for hd in range(hps):
            cols = slice(hd * HEAD, (hd + 1) * HEAD)
            o = o_ref[:, cols]
            rs = lax.rsqrt(jnp.mean(o * o, axis=-1, keepdims=True) + RMS_EPS)
            n = o * rs
            ex_s[:, cols] = n
            do_s[:, cols] = rs * (dn[:, cols] - n * jnp.mean(dn[:, cols] * n, axis=-1, keepdims=True))
        n = ex_s[...]
        dhog_ref[...] = (dh * n * gnv * sg * (1.0 + hog * (1.0 - sg))).astype(BF16)
        dgn_ref[...] += jnp.sum(dh * n * sil, axis=0, keepdims=True)
        causal = _pair_mask()

        def chunk(t, st_ends):
            c = cpb - 1 - t
            sl = pl.ds(pl.multiple_of(c * CHUNK, CHUNK), CHUNK)
            prev = pl.ds(pl.multiple_of(jnp.maximum(c - 1, 0) * CHUNK, CHUNK), CHUNK)
            first = (c > 0).astype(F32)
            starts = []
            for hd in range(hps):
                cols = slice(hd * HEAD, (hd + 1) * HEAD)
                b = b_s[sl, cols] - b_s[prev, cols][CHUNK - 1:CHUNK, :] * first
                qc, kc, vc, doc = q_s[sl, cols], k_s[sl, cols], v_s[sl, cols], do_s[sl, cols]
                eb = jnp.exp(b)
                blast = b[CHUNK - 1:CHUNK, :]
                ek = jnp.exp(blast - b)
                dst = dst_ref[hd]
                dstb = dst.astype(BF16)
                docb = doc.astype(BF16)
                st = st_all_ref[hd, c]
                starts.append(st)
                ex_s[sl, cols] = jnp.broadcast_to(jnp.sum(st_ends[hd].astype(F32) * dst, axis=0, keepdims=True),
                                                  (CHUNK, HEAD))
                dq = lax.dot_general(docb, st, NN, preferred_element_type=F32) * eb
                dk = lax.dot_general(vc.astype(BF16), dstb, NN, preferred_element_type=F32) * ek
                dv = lax.dot_general((kc * ek).astype(BF16), dstb, NT, preferred_element_type=F32)
                em = jnp.where(causal, jnp.exp(jnp.minimum(b[None, :, :] - b[:, None, :], 0.0)), 0.0)
                dp = jnp.sum(doc[None, :, :] * vc[:, None, :], axis=-1, keepdims=True)
                qe = qc[None, :, :] * em
                p = jnp.sum(qe * kc[:, None, :], axis=-1, keepdims=True)
                dq_s[sl, cols] = dq + jnp.sum(dp * em * kc[:, None, :], axis=0)
                dk_s[sl, cols] = dk + jnp.sum(dp * qe, axis=1)
                dv_s[sl, cols] = dv + jnp.sum(p * doc[None, :, :], axis=1)
                dst_ref[hd] = dst * jnp.exp(blast) + lax.dot_general(docb, (qc * eb).astype(BF16), TN,
                                                                     preferred_element_type=F32)
            return tuple(starts)

        lax.fori_loop(0, cpb, chunk, tuple(st_next_ref[hd, 0] for hd in range(hps)))
        dq, dk = dq_s[...], dk_s[...]
        r_i = lax.broadcasted_iota(jnp.int32, (BLOCK, BLOCK), 0)
        c_i = lax.broadcasted_iota(jnp.int32, (BLOCK, BLOCK), 1)
        within = ((c_i >= r_i) & (c_i // CHUNK == r_i // CHUNK)).astype(F32)
        rc = _dot_f32(within, q * dq - k * dk) + ex_s[...]
        df =jnp.where(valid, rc / f - dk, 0.0)
        dhf_ref[...] = (df * (1.0 - lb) * sig * (1.0 - sig)).astype(BF16)
        dlb_ref[...] += jnp.sum(df * (1.0 - sig), axis=0, keepdims=True)
        dhq_ref[...] = (dq * sq * (1.0 + hq * (1.0 - sq))).astype(BF16)
        dhi_ref[...] = dv_s[...].astype(BF16)

    def col(group):
        return pl.BlockSpec((BLOCK, wide), lambda h, s: (nb - 1 - s, group * (n_heads // hps) + h))

    vec = pl.BlockSpec((1, wide), lambda h, s: (0, h))
    tile = pl.BlockSpec((BLOCK, wide), lambda h, s: (nb - 1 - s, h))
    nxt = pl.BlockSpec((hps, 1, HEAD, HEAD), lambda h, s: (h, jnp.minimum((nb - s) * cpb, last_state), 0, 0))
    return _pallas(
        "hgrn_bwd", body, (n_heads // hps, nb),
        [col(0), col(1), col(2), col(3), pl.BlockSpec((2, wide), lambda h, s: (0, h)), vec, tile, tile,
         pl.BlockSpec((hps, cpb, HEAD, HEAD), lambda h, s: (h, nb - 1 - s, 0, 0)), nxt],
        [tile, tile, tile, tile, vec, vec],
        [_sds((m, w), BF16)] * 4 + [_sds((1, w), F32)] * 2,
        [pltpu.VMEM((hps, HEAD, HEAD), F32)] + [pltpu.VMEM((BLOCK, wide), F32)] * 9,
        (proj, proj, proj, proj, logits, gn, o_raw, do_hg, states, states), carried)


def _split_dot(x, t):
    hi = x.astype(BF16)
    lo = (x - hi.astype(F32)).astype(BF16)
    return jnp.dot(hi, t, preferred_element_type=F32) + jnp.dot(lo, t, preferred_element_type=F32)


def _window_scan(x, tri, after):
    blocks = [x[:, u * BLOCK:(u + 1) * BLOCK] for u in range(SB_UNROLL)]
    inner = _split_dot(jnp.concatenate(blocks, axis=0), tri)
    sums = [jnp.sum(b, axis=-1, keepdims=True) for b in blocks]
    out = []
    for u in range(SB_UNROLL):
        piece = inner[u * BLOCK:(u + 1) * BLOCK, :]
        for other in (sums[u + 1:] if after else sums[:u]):
            piece = piece + other
        out.append(piece)
    total = sums[0]
    for other in sums[1:]:
        total = total + other
    return jnp.concatenate(out, axis=1), total


def _sb_window(ref, j_left, cols):
    parts = [ref[pl.ds(pl.multiple_of(jnp.maximum(j_left + u, 0) * BLOCK, BLOCK), BLOCK), cols]
             for u in range(SB_UNROLL)]
    return jnp.concatenate(parts, axis=0)


def _sb_scores(q, kw, i, j_left, scale):
    z = lax.dot_general(q, kw, NT, preferred_element_type=F32) * scale
    lp = jnp.log(1.0 + jnp.exp(-jnp.abs(z)))
    lbeta = jnp.minimum(z, 0.0) - lp
    qpos = i * BLOCK + lax.broadcasted_iota(jnp.int32, z.shape, 0)
    kpos = j_left * BLOCK + lax.broadcasted_iota(jnp.int32, z.shape, 1)
    mask = (kpos < qpos) & (kpos >= PAD)
    l1m = jnp.where(mask, lbeta - z, 0.0)
    return lbeta, l1m, mask


SB_DEAD = -104.0
SB_UNROLL = 3


def _sb_fwd(proj, n_heads, carried=None):
    m = proj.shape[0]
    nb = m // BLOCK
    w = n_heads * HEAD
    scale = 1.0 / math.sqrt(HEAD)
    hps = _heads_per_step(n_heads, 2)
    wide = hps * HEAD
    heads = [slice(hd * HEAD, (hd + 1) * HEAD) for hd in range(hps)]

    def body(q_ref, k_ref, v_ref, o_ref, start_ref, count_ref):
        h, i = pl.program_id(0), pl.program_id(1)
        tsuf = _tri(BLOCK, "gt").astype(BF16)

        def live(carry):
            t, _, runs = carry
            top = jnp.max(runs[0])
            for run in runs[1:]:
                top = jnp.maximum(top, jnp.max(run))
            return (t <= i) & (top > SB_DEAD)

        def step(carry):
            t, accs, runs = carry
            j_left = i - t - (SB_UNROLL - 1)
            new_accs, new_runs = [], []
            for hd, cols in enumerate(heads):
                start_ref[hd] = jnp.broadcast_to(runs[hd], (BLOCK, HEAD))
                lbeta, l1m, mask = _sb_scores(q_ref[:, cols], _sb_window(k_ref, j_left, cols), i, j_left, scale)
                later, total = _window_scan(l1m, tsuf, True)
                wgt = jnp.where(mask, jnp.exp(lbeta + later + runs[hd]), 0.0)
                new_accs.append(accs[hd] + jnp.dot(wgt.astype(BF16), _sb_window(v_ref, j_left, cols),
                                                   preferred_element_type=F32))
                new_runs.append(runs[hd] + total)
            return t + SB_UNROLL, tuple(new_accs), tuple(new_runs)

        t, accs, _ = lax.while_loop(live, step, (jnp.int32(0), tuple(jnp.zeros((BLOCK, HEAD), F32) for _ in heads),
                                                 tuple(jnp.zeros((BLOCK, 1), F32) for _ in heads)))
        for hd, cols in enumerate(heads):
            o_ref[:, cols] = accs[hd].astype(BF16)
        count_ref[h, i] = t.astype(F32)

    def whole(group):
        return pl.BlockSpec((m, wide), lambda h, i: (0, group * (n_heads // hps) + h), pipeline_mode=pl.Buffered(1))

    return _pallas(
        "sb_fwd", body, (n_heads // hps, nb),
        [pl.BlockSpec((BLOCK, wide), lambda h, i: (i, h)), whole(1), whole(2)],
        [pl.BlockSpec((BLOCK, wide), lambda h, i: (i, h)), pl.BlockSpec((hps, BLOCK, HEAD), lambda h, i: (h, i, 0)),
         pl.BlockSpec(memory_space=pltpu.SMEM)],
        [_sds((m, w), BF16), _sds((n_heads, m, HEAD), F32), _sds((n_heads // hps, nb), F32)],
        [], (proj, proj, proj), carried)


def _sb_bwd(proj, do, start, count, n_heads):
    m = proj.shape[0]
    nb = m // BLOCK
    w = n_heads * HEAD
    scale = 1.0 / math.sqrt(HEAD)
    hps = _heads_per_step(n_heads, 2)
    wide = hps * HEAD
    heads = [slice(hd * HEAD, (hd + 1) * HEAD) for hd in range(hps)]

    def body(q_ref, k_ref, v_ref, do_ref, start_ref, count_ref, dq_ref, dk_ref, dv_ref, dk_s, dv_s):
        h, i = pl.program_id(0), pl.program_id(1)

        @pl.when(i == 0)
        def _():
            dk_s[...] = jnp.zeros_like(dk_s)
            dv_s[...] = jnp.zeros_like(dv_s)

        count = count_ref[h, i].astype(jnp.int32)
        first = i + 1 - count
        tsuf = _tri(BLOCK, "gt").astype(BF16)
        tpre = _tri(BLOCK, "lt").astype(BF16)

        def step(t, carry):
            dqs, rights, psums = carry
            j_left = first + t * SB_UNROLL
            out = []
            for hd, cols in enumerate(heads):
                q = q_ref[:, cols]
                dob = do_ref[:, cols].astype(BF16)
                kw = _sb_window(k_ref, j_left, cols)
                vw = _sb_window(v_ref, j_left, cols)
                lbeta, l1m, mask = _sb_scores(q, kw, i, j_left, scale)
                later, total = _window_scan(l1m, tsuf, True)
                right = jnp.where(t == 0, rights[hd], rights[hd] - total)
                a = jnp.where(mask, jnp.exp(lbeta + later + right), 0.0)
                p = a * lax.dot_general(dob, vw, NT, preferred_element_type=F32)
                earlier, p_total = _window_scan(p, tpre, False)
                below = psums[hd] + earlier
                beta = jnp.exp(lbeta)
                dz = (jnp.where(mask, p * (1.0 - beta) - below * beta, 0.0) * scale).astype(BF16)
                dq = dqs[hd] + jnp.dot(dz, kw, preferred_element_type=F32)
                dkw = lax.dot_general(dz, q, TN, preferred_element_type=F32)
                dvw = lax.dot_general(a.astype(BF16), dob, TN, preferred_element_type=F32)
                for u in range(SB_UNROLL):
                    rows = pl.ds(pl.multiple_of(jnp.maximum(j_left + u, 0) * BLOCK, BLOCK), BLOCK)
                    dk_s[rows, cols] += dkw[u * BLOCK:(u + 1) * BLOCK, :]
                    dv_s[rows, cols] += dvw[u * BLOCK:(u + 1) * BLOCK, :]
                out.append((dq, right, psums[hd] + p_total))
            return tuple(o[0] for o in out), tuple(o[1] for o in out), tuple(o[2] for o in out)

        dqs, _, _ = lax.fori_loop(0, count // SB_UNROLL, step,
                                  (tuple(jnp.zeros((BLOCK, HEAD), F32) for _ in heads),
                                   tuple(start_ref[hd, :, 0:1] for hd in range(hps)),
                                   tuple(jnp.zeros((BLOCK, 1), F32) for _ in heads)))
        for hd, cols in enumerate(heads):
            dq_ref[:, cols] = dqs[hd].astype(BF16)

        @pl.when(i == nb - 1)
        def _():
            dk_ref[...] = dk_s[...].astype(BF16)
            dv_ref[...] = dv_s[...].astype(BF16)

    def whole(group):
        return pl.BlockSpec((m, wide), lambda h, i: (0, group * (n_heads // hps) + h), pipeline_mode=pl.Buffered(1))

    tile = pl.BlockSpec((BLOCK, wide), lambda h, i: (i, h))
    col = pl.BlockSpec((m, wide), lambda h, i: (0, h))
    return pl.pallas_call(
        body, name="sb_bwd", grid=(n_heads // hps, nb),
        in_specs=[tile, whole(1), whole(2), tile, pl.BlockSpec((hps, BLOCK, HEAD), lambda h, i: (h, i, 0)),
                  pl.BlockSpec(memory_space=pltpu.SMEM)],
        out_specs=[tile, col, col],
        out_shape=[_sds((m, w), BF16)] * 3,
        scratch_shapes=[pltpu.VMEM((m, wide), F32)] * 2,
        compiler_params=_params(2),
    )(proj, proj, proj, do, start, count)


def _grad_w_rows(name, x, dy, nd_out):
    m, kx = x.shape
    n = dy.shape[1]
    ks = kx // nd_out
    tm = _tile(m, GRAD_ROW_TILE)

    def epi(acc, er, orf):
        orf[0][...] = acc[0].astype(BF16)

    return _gemm(name, (nd_out, m // tm),
                 [(x, pl.BlockSpec((tm, ks), lambda j, k: (k, j)), dy, pl.BlockSpec((tm, n), lambda j, k: (k, 0)))],
                 [0], [(ks, n)], TN, [], [(_sds((nd_out, ks, n), BF16), pl.BlockSpec((None, ks, n), lambda j, k: (j, 0, 0)))], epi)[0]


def _local_step(x, target, meta, vec, wts, shards=None):
    d = x.shape[1]
    width = vec["hg_norm_g"].shape[1]
    n_heads = width // HEAD
    gate_col = 7 * width
    h0 = jnp.concatenate([jnp.zeros((PAD, d), F32), meta, x], axis=0)
    h0b = h0.astype(BF16)
    wts = dict(wts)
    exchange = None if shards is None else _exchange_carried

    if shards is None:
        a1, b1, s1 = _ffn_up("ffn1_up", h0b, wts["ffn1_w_gate"], wts["ffn1_w_up"])
        r1, h1, h1b = _residual_ln("ffn1_down", s1, True, wts["ffn1_w_down"], h0, vec["ln1_g"], vec["ln1_b"], 0.5)
        proj = _in_proj(h1b, wts["w_in"])
        qkv = proj[:, 4 * width:7 * width].astype(BF16)
        o_raw, o_hg, states = _hgrn_fwd(proj, vec["hg_lb_logits"], vec["hg_norm_g"], n_heads)
        o_sb, sb_start, sb_count = _sb_fwd(qkv, n_heads)
    else:
        half = shards["w_in"].shape[0] // 2
        (a1, b1, s1), (wts["ffn1_w_down"], w_in_top) = _ffn_up(
            "ffn1_up", h0b, wts["ffn1_w_gate"], wts["ffn1_w_up"],
            _gather_carried([shards["ffn1_w_down"], shards["w_in"][:half]]))
        (r1, h1, h1b), (w_in_bottom,) = _residual_ln("ffn1_down", s1, True, wts["ffn1_w_down"], h0, vec["ln1_g"],
                                                     vec["ln1_b"], 0.5, _gather_carried([shards["w_in"][half:]]))
        wts["w_in"] = jnp.concatenate([w_in_top, w_in_bottom], axis=1)
        with_proj = ("w_proj_hg", "w_proj_sb", "w_out", "ffn2_w_gate")
        proj, got = _in_proj(h1b, wts["w_in"], _gather_carried([shards[k] for k in with_proj]))
        wts.update(zip(with_proj, got))
        qkv = proj[:, 4 * width:7 * width].astype(BF16)
        (o_raw, o_hg, states), (wts["ffn2_w_up"],) = _hgrn_fwd(proj, vec["hg_lb_logits"], vec["hg_norm_g"], n_heads,
                                                               _gather_carried([shards["ffn2_w_up"]]))
        (o_sb, sb_start, sb_count), (wts["ffn2_w_down"],) = _sb_fwd(qkv, n_heads,
                                                                    _gather_carried([shards["ffn2_w_down"]]))
    nd = wts["w_in"].shape[0]
    w_out = wts["w_out"]
    p_hg2 = wts["w_proj_hg"].transpose(1, 0, 2).reshape(width, d)
    p_sb2 = wts["w_proj_sb"].transpose(1, 0, 2).reshape(width, d)
    u_hg, u_sb, y = _proj_merge(o_hg, o_sb, p_hg2, p_sb2, proj, vec["b_gate"], gate_col)
    r2, h2, h2b = _residual_ln("out_proj", y, False, w_out.reshape(d, d), h1, vec["ln2_g"], vec["ln2_b"], 1.0)
    a2, b2, s2 = _ffn_up("ffn2_up", h2b,wts["ffn2_w_gate"], wts["ffn2_w_up"])
    r3, _, _ = _residual_ln("ffn2_down", s2, True, wts["ffn2_w_down"], h2, vec["ln3_g"], vec["ln3_b"], 0.5)

    dr3, dr3b, dg3, db3, loss = _ln_bwd("ln3_bwd", r3, vec["ln3_g"], 0.5, beta=vec["ln3_b"], target=target, first_row=BLOCK)
    dh2, dwg2, dwu2, dwd2 = _ffn_bwd("ffn2", dr3b, dr3, h2b, a2, b2, s2, wts["ffn2_w_gate"], wts["ffn2_w_up"],
                                     wts["ffn2_w_down"], exchange)
    dr2, dr2b, dg2, db2 = _ln_bwd("ln2_bwd", r2, vec["ln2_g"], 1.0, dy=dh2)
    du_hg, du_sb, dz_hg, dz_sb, dbg = _merge_bwd(dr2b, w_out.reshape(d, d), proj, vec["b_gate"], u_hg, u_sb, gate_col)
    dw_out = _grad_w_rows("dw_out", y, dr2b, nd)
    dp_hg = _grad_w("dp_hg", o_hg, du_hg, nd)
    dp_sb = _grad_w("dp_sb", o_sb, du_sb, nd)
    do_hg = _grad_in_whole("do_hg", du_hg, p_hg2)
    do_sb = _grad_in_whole("do_sb", du_sb, p_sb2)
    hg = _hgrn_bwd(proj, vec["hg_lb_logits"], vec["hg_norm_g"], o_raw, do_hg, states, n_heads,
                   exchange([dw_out, dp_hg, dp_sb]) if exchange else None)
    if exchange:
        hg, (dw_out, dp_hg, dp_sb) = hg
    dhq, dhf, dhi, dhog, dgn, dlb = hg
    dsq, dsk, dsv = _sb_bwd(qkv, do_sb, sb_start, sb_count, n_heads)
    dproj = jnp.concatenate([dhq, dhf, dhi, dhog, dsq, dsk, dsv, dz_hg, dz_sb], axis=1)
    dw_in = _grad_w("dw_in", h1b, dproj, nd)
    dh1 = _grad_in("dh1", dproj, wts["w_in"], add=dr2, carried=exchange([dw_in]) if exchange else None)
    if exchange:
        dh1, (dw_in,) = dh1
    dr1, dr1b, dg1, db1 = _ln_bwd("ln1_bwd", r1, vec["ln1_g"], 0.5, dy=dh1)
    dh0, dwg1, dwu1, dwd1 = _ffn_bwd("ffn1", dr1b, dr1, h0b, a1, b1, s1, wts["ffn1_w_gate"], wts["ffn1_w_up"],
                                     wts["ffn1_w_down"], exchange)

    small = {"ln1_g": dg1, "ln1_b": db1, "ln2_g": dg2, "ln2_b": db2, "ln3_g": dg3, "ln3_b": db3,
             "b_gate": dbg, "hg_lb": dlb, "hg_norm_g": dgn}
    big = {"ffn1_w_gate": dwg1, "ffn1_w_up": dwu1, "ffn1_w_down": dwd1, "w_in": dw_in, "w_proj_hg": dp_hg,
           "w_proj_sb": dp_sb, "w_out": dw_out, "ffn2_w_gate": dwg2, "ffn2_w_up": dwu2, "ffn2_w_down": dwd2}
    return loss, dh0[BLOCK:], dh0[PAD:BLOCK], small, big


def _position():
    return lax.axis_index("x"), lax.axis_index("y"), lax.axis_index("c")


def _slot(px, py, pc):
    return 4 * px + 2 * py + pc


def _all_gather(shards):
    n = len(shards)

    def body(*refs):
        ins, outs = refs[:n], refs[n:2 * n]
        send_sems, recv_sems, local_sems = refs[2 * n:]
        x, y, c = _position()
        me, sibling = (x, y, c), (x, y, 1 - c)
        chips = [(1 - x, y), (x, 1 - y), (1 - x, 1 - y)]

        def copy(a, k, block, to, src=None):
            dst = outs[a].at[_slot(*block)]
            return pltpu.make_async_remote_copy(src_ref=dst if src is None else src, dst_ref=dst,
                                                send_sem=send_sems.at[a, k], recv_sem=recv_sems.at[a, k],
                                                device_id=to, device_id_type=MESH)

        mine = [pltpu.make_async_copy(ins[a], outs[a].at[_slot(*me)], local_sems.at[a]) for a in range(n)]
        for cp in mine:
            cp.start()
        first = []
        for a in range(n):
            first.append(copy(a, 0, me, sibling, src=ins[a]))
            first += [copy(a, 1 + j, me, (*chip, c), src=ins[a]) for j, chip in enumerate(chips)]
        for cp in first:
            cp.start()
        passed = []
        for j, chip in enumerate(chips):
            for a in range(n):
                copy(a, 1 + j, (*chip, c), me).wait_recv()
                cp = copy(a, 4 + j, (*chip, c), sibling)
                cp.start()
                passed.append(cp)
        for a in range(n):
            copy(a, 0, sibling, me).wait_recv()
        for j, chip in enumerate(chips):
            for a in range(n):
                copy(a, 4 + j, (*chip, 1 - c), me).wait_recv()
        for cp in first + passed:
            cp.wait_send()
        for cp in mine:
            cp.wait()

    return pl.pallas_call(
        body, name="all_gather", out_shape=[_sds((N_DEV,) + s.shape, s.dtype) for s in shards],
        in_specs=[ANY] * n, out_specs=[ANY] * n,
        scratch_shapes=[pltpu.SemaphoreType.DMA((n, 7)), pltpu.SemaphoreType.DMA((n, 7)), pltpu.SemaphoreType.DMA((n,))],
    )(*shards)


def _exchange_carried(grads):
    return _direct_copies(grads, [_sds(g.shape, g.dtype) for g in grads], lambda ref, slot: ref.at[slot])


def _gather_carried(shards):
    return _direct_copies(shards, [_sds((N_DEV,) + s.shape, s.dtype) for s in shards], lambda ref, slot: ref)


def _direct_copies(arrays, outs, block_for):
    n = len(arrays)

    def plan(ins, results, sems, arriving):
        send_sems, recv_sems, local_sems = sems
        x, y, c = _position()
        mine = _slot(x, y, c)
        peers = [(1 - x if k & 4 else x, 1 - y if k & 2 else y, 1 - c if k & 1 else c) for k in range(1, N_DEV)]
        own = [pltpu.make_async_copy(block_for(ins[a], mine), results[a].at[mine], local_sems.at[a]) for a in range(n)]
        remote = [pltpu.make_async_remote_copy(
            src_ref=block_for(ins[a], mine if arriving else _slot(*peer)),
            dst_ref=results[a].at[_slot(*peer) if arriving else mine],
            send_sem=send_sems.at[a, k], recv_sem=recv_sems.at[a, k], device_id=peer, device_id_type=MESH)
            for a in range(n) for k, peer in enumerate(peers)]
        return own, remote

    def start(ins, results, sems):
        own, sent = plan(ins, results, sems, False)
        for cp in own + sent:
            cp.start()

    def finish(ins, results, sems):
        _, landed = plan(ins, results, sems, True)
        for cp in landed:
            cp.wait_recv()
        own, sent = plan(ins, results, sems, False)
        for cp in sent:
            cp.wait_send()
        for cp in own:
            cp.wait()

    sems = [pltpu.SemaphoreType.DMA((n, 7)), pltpu.SemaphoreType.DMA((n, 7)), pltpu.SemaphoreType.DMA((n,))]
    return _Carried(list(arrays), outs, sems, start, finish)


def _all_reduce_rows(v):
    rows = v.shape[0]

    def body(v_ref, out_ref, buf, send_sems, recv_sems):
        x, y, c = _position()
        me, sibling = (x, y, c), (x, y, 1 - c)
        chips = [(1 - x, y), (x, 1 - y), (1 - x, 1 - y)]

        def copy(k, block, to, src=None):
            dst = buf.at[_slot(*block)]
            return pltpu.make_async_remote_copy(src_ref=dst if src is None else src, dst_ref=dst,
                                                send_sem=send_sems.at[k], recv_sem=recv_sems.at[k],
                                                device_id=to, device_id_type=MESH)

        first = [copy(0, me, sibling, src=v_ref)] + [copy(1 + j, me, (*chip, c), src=v_ref) for j, chip in enumerate(chips)]
        for cp in first:
            cp.start()
        buf[_slot(*me)] = v_ref[...]
        passed = [copy(4 + j, (*chip, c), sibling) for j, chip in enumerate(chips)]
        for j, chip in enumerate(chips):
            copy(1 + j, (*chip, c), me).wait_recv()
            passed[j].start()
        copy(0, sibling, me).wait_recv()
        for j, chip in enumerate(chips):
            copy(4 + j, (*chip, 1 - c), me).wait_recv()
        for cp in first + passed:
            cp.wait_send()
        total = buf[0]
        for s in range(1, N_DEV):
            total = total + buf[s]
        out_ref[...] = total

    vmem = pl.BlockSpec(memory_space=pltpu.VMEM)
    return pl.pallas_call(
        body, name="small_all_reduce", out_shape=_sds(v.shape, F32), in_specs=[vmem], out_specs=vmem,
        scratch_shapes=[pltpu.VMEM((N_DEV, rows, 128), F32), pltpu.SemaphoreType.DMA((7,)), pltpu.SemaphoreType.DMA((7,))],
    )(v)


def _adamw(name, w, m, v, contrib):
    r, c = w.shape
    n = contrib.shape[0]
    tr = _tile(r, 256)

    def body(w_ref, m_ref, v_ref, c_ref, g_out, d_out, m_out, v_out):
        g = c_ref[0].astype(F32)
        for s in range(1, n):
            g = g + c_ref[s].astype(F32)
        m2 = ADAM_B1 * m_ref[...] + (1.0 - ADAM_B1) * g
        v2 = ADAM_B2 * v_ref[...] + (1.0 - ADAM_B2) * (g * g)
        m_hat = m2 / (1.0 - ADAM_B1 ** ADAM_STEP)
        v_hat = v2 / (1.0 - ADAM_B2 ** ADAM_STEP)
        g_out[...] = g
        d_out[...] = -ADAM_LR * (m_hat / (jnp.sqrt(v_hat) + ADAM_EPS) + ADAM_WD * w_ref[...])
        m_out[...] = m2
        v_out[...] = v2

    tile = pl.BlockSpec((tr, c), lambda i: (i, 0))
    return pl.pallas_call(
        body, name=name, grid=(r // tr,), in_specs=[tile, tile, tile, pl.BlockSpec((n, tr, c), lambda i: (0, i, 0))],
        out_specs=[tile] * 4, out_shape=[_sds((r, c), F32)] * 4, compiler_params=_params(1),
    )(w, m, v, contrib)


def _lb_logits_grad(logits, dlb):
    def body(lg_ref, d_ref, out_ref):
        lg = lg_ref[...]
        mx = jnp.maximum(lg[0:1], lg[1:2])
        e0 = jnp.exp(lg[0:1] - mx)
        p0 = e0 / (e0 + jnp.exp(lg[1:2] - mx))
        g0 = d_ref[...] * p0 * (1.0 - p0)
        out_ref[0:1, :] = g0
        out_ref[1:2, :] = -g0

    return pl.pallas_call(body, name="lb_logits_grad", out_shape=_sds(logits.shape, F32))(logits, dlb)


BIG = ("ffn1_w_gate", "ffn1_w_up", "ffn1_w_down", "w_in", "w_proj_hg", "w_proj_sb", "w_out",
       "ffn2_w_gate", "ffn2_w_up", "ffn2_w_down")
VECTORS = ("ln1_g", "ln1_b", "b_gate", "hg_lb_logits", "hg_norm_g", "ln2_g", "ln2_b", "ln3_g", "ln3_b")
WEIGHTS = ("meta", "ln1_g", "ln1_b", "ffn1_w_gate", "ffn1_w_up", "ffn1_w_down", "w_in", "b_gate", "hg_lb_logits",
           "hg_norm_g", "w_proj_hg", "w_proj_sb", "w_out", "ln2_g", "ln2_b", "ffn2_w_gate", "ffn2_w_up",
           "ffn2_w_down", "ln3_g", "ln3_b")


def kernel(x, meta, ln1_g, ln1_b, ffn1_w_gate, ffn1_w_up, ffn1_w_down, w_in, b_gate, hg_lb_logits, hg_norm_g, w_proj_hg, w_proj_sb, w_out, ln2_g, ln2_b, ffn2_w_gate, ffn2_w_up, ffn2_w_down, ln3_g, ln3_b, loss_target, m_meta, m_ln1_g, m_ln1_b, m_ffn1_w_gate, m_ffn1_w_up, m_ffn1_w_down, m_w_in, m_b_gate, m_hg_lb_logits, m_hg_norm_g, m_w_proj_hg, m_w_proj_sb, m_w_out, m_ln2_g, m_ln2_b, m_ffn2_w_gate, m_ffn2_w_up, m_ffn2_w_down, m_ln3_g, m_ln3_b, v_meta, v_ln1_g, v_ln1_b, v_ffn1_w_gate, v_ffn1_w_up, v_ffn1_w_down, v_w_in, v_b_gate, v_hg_lb_logits, v_hg_norm_g, v_w_proj_hg, v_w_proj_sb, v_w_out, v_ln2_g, v_ln2_b, v_ffn2_w_gate, v_ffn2_w_up, v_ffn2_w_down, v_ln3_g, v_ln3_b):
    given = dict(locals())
    d = x.shape[-1]
    ds = meta.shape[1]

    shards = {k: given[k][0].astype(BF16) for k in BIG}
    first = ("ffn1_w_gate", "ffn1_w_up")
    gathered = _all_gather([meta] + [shards.pop(k) for k in first])
    meta_full = gathered[0].transpose(1, 0, 2).reshape(N_META, d)
    vec = {k: given[k] for k in VECTORS}
    loss, grad_x, dmeta, small, received = _local_step(x[0], loss_target[0], meta_full, vec, dict(zip(first, gathered[1:])),
                                                       shards)

    order =("ln1_g", "ln1_b", "ln2_g", "ln2_b", "ln3_g", "ln3_b", "b_gate", "hg_lb", "hg_norm_g")
    parts = [small[k].reshape(-1, 128) for k in order] + [dmeta.reshape(-1, 128), jnp.broadcast_to(loss, (8, 128))]
    total = _all_reduce_rows(jnp.concatenate(parts, axis=0))
    reduced, row = {}, 0
    for k, p in zip(order + ("meta", "loss"), parts):
        reduced[k] = total[row:row + p.shape[0]]
        row += p.shape[0]
    loss_out = reduced["loss"][0, 0]
    me = _slot(*_position())
    dmeta_mine = lax.dynamic_slice(reduced["meta"].reshape(N_META, d), (0, me * ds), (N_META, ds))
    dlogits = _lb_logits_grad(hg_lb_logits, reduced["hg_lb"].reshape(1, -1))

    grads, deltas, new_m, new_v = {}, {}, {}, {}
    for k in WEIGHTS:
        w = given[k]
        lead = w.shape[:-2]
        w2, m2, v2 = (a.reshape(a.shape[-2:]) for a in (w, given["m_" + k], given["v_" + k]))
        if k in BIG:
            contrib = received[k]
        elif k == "meta":
            contrib = dmeta_mine[None]
        elif k == "hg_lb_logits":
            contrib = dlogits[None]
        else:
            contrib = reduced[k].reshape((1,) + w2.shape)
        out = _adamw("adamw_" + k, w2, m2, v2, contrib)
        grads[k], deltas[k], new_m[k], new_v[k] = (o.reshape(lead + o.shape) for o in out)
    return (loss_out, grad_x[None], *[grads[k] for k in WEIGHTS], *[deltas[k] for k in WEIGHTS],
            *[new_m[k] for k in WEIGHTS], *[new_v[k] for k in WEIGHTS])
```

```python
import functools
import math

import jax
import jax.numpy as jnp
from jax import lax
from jax.experimental import pallas as pl
from jax.experimental.pallas import tpu as pltpu

F32 = jnp.float32
BF16 = jnp.bfloat16
MESH = pl.DeviceIdType.MESH

N_DEV = 8
N_META = 16
BLOCK = 128
PAD = BLOCK - N_META
HEAD = 128
CHUNK = 16
LN_EPS = 1e-5
RMS_EPS = 1e-6
DN_ALPHA = 2.0 ** 0.25
ADAM_LR, ADAM_B1, ADAM_B2, ADAM_EPS, ADAM_WD, ADAM_STEP = 0.001, 0.9, 0.999, 1e-08, 0.01, 10

VMEM_LIMIT_V7X = 60 * 1024 * 1024
ROW_TILE = 640
LN_ROW_TILE = 320
COL_TILE = 512
GRAD_ROW_TILE = 1664

NN = (((1,), (0,)), ((), ()))
NT = (((1,), (1,)), ((), ()))
TN = (((0,), (0,)), ((), ()))


def _tile(n, pref, mult=16):
    best = None
    for t in range(mult, min(n, pref) + 1, mult):
        if n % t == 0:
            best = t
    return n if best is None else best


def _params(n_axes):
    return pltpu.CompilerParams(dimension_semantics=("arbitrary",) * n_axes, vmem_limit_bytes=VMEM_LIMIT_V7X)


def _sigmoid(x):
    return 1.0 / (1.0 + jnp.exp(-x))


class _Carried:
    def __init__(self, ins, outs, sems, start, finish):
        self.ins, self.outs, self.sems, self.start, self.finish = ins, outs, sems, start, finish


ANY = pl.BlockSpec(memory_space=pl.ANY)


def _pallas(name, body, grid, in_specs, out_specs, out_shape, scratch, operands, carried=None):
    if carried is None:
        return pl.pallas_call(body, name=name, grid=grid, in_specs=in_specs, out_specs=out_specs, out_shape=out_shape,
                              scratch_shapes=scratch, compiler_params=_params(len(grid)))(*operands)
    n_in, n_out, n_scr = len(in_specs), len(out_specs), len(scratch)
    c_in, c_out = len(carried.ins), len(carried.outs)

    def wrapped(*refs):
        ins, rest = refs[:n_in], refs[n_in:]
        c_ins, rest = rest[:c_in], rest[c_in:]
        outs, rest = rest[:n_out], rest[n_out:]
        c_outs, rest = rest[:c_out], rest[c_out:]
        scr, c_sems = rest[:n_scr], rest[n_scr:]
        first = last = None
        for axis, size in enumerate(grid):
            at0, at_end = pl.program_id(axis) == 0, pl.program_id(axis) == size - 1
            first = at0 if first is None else first & at0
            last = at_end if last is None else last & at_end

        @pl.when(first)
        def _():
            carried.start(c_ins, c_outs, c_sems)

        body(*ins, *outs, *scr)

        @pl.when(last)
        def _():
            carried.finish(c_ins, c_outs, c_sems)

    res = pl.pallas_call(
        wrapped, name=name, grid=grid, in_specs=list(in_specs) + [ANY] * c_in, out_specs=list(out_specs) + [ANY] * c_out,
        out_shape=list(out_shape) + list(carried.outs), scratch_shapes=list(scratch) + list(carried.sems),
        compiler_params=_params(len(grid)),
    )(*operands, *carried.ins)
    return res[:n_out], res[n_out:]


def _gemm(name, grid, pairs, acc_of, acc_shapes, dims, extras, outs, epilogue, carried=None):
    n_extra, n_out = len(extras), len(outs)
    nk = grid[-1]
    k_axis = len(grid) - 1
    operands, in_specs, where = [], [], {}
    for a, a_spec, b, b_spec in pairs:
        for arr, spec in ((a, a_spec), (b, b_spec)):
            if (id(arr), id(spec)) not in where:
                where[(id(arr), id(spec))] = len(operands)
                operands.append(arr)
                in_specs.append(spec)
    n_mat = len(operands)
    slots = [(where[(id(a), id(a_spec))], where[(id(b), id(b_spec))]) for a, a_spec, b, b_spec in pairs]

    def body(*refs):
        er = refs[n_mat:n_mat + n_extra]
        orf = refs[n_mat + n_extra:n_mat + n_extra + n_out]
        accs = refs[n_mat + n_extra + n_out:]

        def part(p):
            a_ref, b_ref = refs[slots[p][0]], refs[slots[p][1]]
            if len(a_ref.shape) == 2:
                return lax.dot_general(a_ref[...].astype(BF16), b_ref[...].astype(BF16), dims, preferred_element_type=F32)
            total = None
            for s in range(a_ref.shape[0]):
                d = lax.dot_general(a_ref[s].astype(BF16), b_ref[s].astype(BF16), dims, preferred_element_type=F32)
                total = d if total is None else total + d
            return total

        if nk == 1:
            vals = [None] * len(acc_shapes)
            for p in range(len(pairs)):
                d = part(p)
                vals[acc_of[p]] = d if vals[acc_of[p]] is None else vals[acc_of[p]] + d
            epilogue(vals, er, orf)
        else:
            k = pl.program_id(k_axis)

            @pl.when(k == 0)
            def _():
                for acc in accs:
                    acc[...] = jnp.zeros_like(acc)

            for p in range(len(pairs)):
                accs[acc_of[p]][...] += part(p)

            @pl.when(k == nk - 1)
            def _():
                epilogue([acc[...] for acc in accs], er, orf)

    for e, e_spec in extras:
        operands.append(e)
        in_specs.append(e_spec)
    scratch = [] if nk == 1 else [pltpu.VMEM(s, F32) for s in acc_shapes]
    return _pallas(name, body, grid, in_specs, [s for _, s in outs], [o for o, _ in outs], scratch, operands, carried)


def _sds(shape, dtype):
    return jax.ShapeDtypeStruct(shape, dtype)


def _ln_rows(r, g, b):
    mu = jnp.mean(r, axis=-1, keepdims=True)
    xc = r - mu
    var = jnp.mean(xc * xc, axis=-1, keepdims=True)
    return xc * lax.rsqrt(var + LN_EPS) * g + b


def _ffn_up(name, hb, wg, wu, carried=None):
    m, d = hb.shape
    nd, _, fs = wg.shape
    tm = _tile(m, ROW_TILE)

    def epi(acc, er, orf):
        a, b = acc
        orf[0][...] = a
        orf[1][...] = b
        orf[2][...] = (a * _sigmoid(a) * b).astype(BF16)

    h_spec = pl.BlockSpec((tm, d), lambda i, j, k: (i, 0))
    w_spec = pl.BlockSpec((None, d, fs), lambda i, j, k: (j, 0, 0))
    o_spec = pl.BlockSpec((None, tm, fs), lambda i, j, k: (j, i, 0))
    return _gemm(name, (m // tm, nd, 1), [(hb, h_spec, wg, w_spec), (hb, h_spec, wu, w_spec)], [0, 1],
                 [(tm, fs)] * 2, NN, [],
                 [(_sds((nd, m, fs), F32), o_spec), (_sds((nd, m, fs), F32), o_spec), (_sds((nd, m, fs), BF16), o_spec)], epi,
                 carried)


def _residual_ln(name, a, a_stacked, w, h_in, g, beta, scale, carried=None):
    d = w.shape[-1]
    m = h_in.shape[0]
    tm = _tile(m, LN_ROW_TILE)

    def epi(acc, er, orf):
        r = DN_ALPHA * er[0][...] + scale * acc[0]
        h = _ln_rows(r, er[1][...], er[2][...])
        orf[0][...] = r
        orf[1][...] = h
        orf[2][...] = h.astype(BF16)

    once = pl.Buffered(1)
    if a_stacked:
        a_spec = pl.BlockSpec((a.shape[0], tm, a.shape[2]), lambda i, k: (0, i, 0))
        w_spec = pl.BlockSpec(w.shape, lambda i, k: (0, 0, 0), pipeline_mode=once)
    else:
        a_spec = pl.BlockSpec((tm, a.shape[1]), lambda i, k: (i, 0))
        w_spec = pl.BlockSpec(w.shape, lambda i, k: (0, 0), pipeline_mode=once)
    row = pl.BlockSpec((tm, d), lambda i, k: (i, 0))
    vec = pl.BlockSpec((1, d), lambda i, k: (0, 0))
    return _gemm(name, (m // tm, 1), [(a, a_spec, w, w_spec)], [0], [(tm, d)], NN,
                 [(h_in, row), (g, vec), (beta, vec)],
                 [(_sds((m, d), F32), row), (_sds((m, d), F32), row), (_sds((m, d), BF16), row)], epi, carried)


def _in_proj(hb, w_in, carried=None):
    m, d = hb.shape
    nd, _, cs = w_in.shape
    tm = _tile(m, ROW_TILE)

    def epi(acc, er, orf):
        orf[0][...] = acc[0]

    res = _gemm("in_proj", (m // tm, nd, 1),
                [(hb, pl.BlockSpec((tm, d), lambda i, j, k: (i, 0)), w_in, pl.BlockSpec((None, d, cs), lambda i, j, k: (j, 0, 0)))],
                [0], [(tm, cs)], NN, [], [(_sds((m, nd * cs), F32), pl.BlockSpec((tm, cs), lambda i, j, k: (i, j)))], epi,
                carried)
    return res[0] if carried is None else (res[0][0], res[1])


def _proj_merge(o_hg, o_sb, p_hg, p_sb, proj, b_gate, gate_col):
    m, w = o_hg.shape
    d = p_hg.shape[1]
    tm = _tile(m, ROW_TILE)
    tn = _tile(d, COL_TILE, 128)
    nn = d // tn
    c0 = gate_col // tn

    def epi(acc, er, orf):
        u_hg, u_sb = acc
        g_hg = _sigmoid(er[0][...] + er[2][...])
        g_sb = _sigmoid(er[1][...] + er[3][...])
        orf[0][...] = u_hg
        orf[1][...] = u_sb
        orf[2][...] = (g_hg * u_hg + g_sb * u_sb).astype(BF16)

    o_spec = pl.BlockSpec((tm, w), lambda i, j, k: (i, 0))
    p_spec = pl.BlockSpec((w, tn), lambda i, j, k: (0, j))
    out = pl.BlockSpec((tm, tn), lambda i, j, k: (i, j))
    return _gemm("proj_merge", (m // tm, nn, 1), [(o_hg, o_spec, p_hg, p_spec), (o_sb, o_spec, p_sb, p_spec)], [0, 1],
                 [(tm, tn)] * 2, NN,
                 [(proj, pl.BlockSpec((tm, tn), lambda i, j, k: (i, c0 + j))),
                  (proj, pl.BlockSpec((tm, tn), lambda i, j, k: (i, c0 + nn + j))),
                  (b_gate, pl.BlockSpec((1, tn), lambda i, j, k: (0, j))),
                  (b_gate, pl.BlockSpec((1, tn), lambda i, j, k: (0, nn + j)))],
                 [(_sds((m, d), F32), out), (_sds((m, d), F32), out), (_sds((m, d), BF16), out)], epi)


def _ln_bwd(name, r, g, out_scale, dy=None, beta=None, target=None, first_row=0):
    m, d = r.shape
    tm = _tile(m, LN_ROW_TILE if target is None else BLOCK)
    with_loss = target is not None
    skip = first_row // tm if with_loss else 0
    assert not with_loss or first_row % tm == 0

    def body(*refs):
        if with_loss:
            r_ref, g_ref, b_ref, t_ref, dr_ref, drb_ref, dg_ref, db_ref, loss_ref = refs
        else:
            r_ref, g_ref, dy_ref, dr_ref, drb_ref, dg_ref, db_ref = refs
        i = pl.program_id(0)
        x = r_ref[...]
        mu = jnp.mean(x, axis=-1, keepdims=True)
        xc = x - mu
        var = jnp.mean(xc * xc, axis=-1, keepdims=True)
        rstd = lax.rsqrt(var + LN_EPS)
        xhat = xc * rstd
        gv = g_ref[...]
        if with_loss:
            err = xhat * gv + b_ref[...] - t_ref[...]
            live = (i >= skip).astype(F32)
            dyv = err * (live / d)
            part = 0.5 * live * jnp.sum(jnp.sum(err * err, axis=-1, keepdims=True), axis=0, keepdims=True) / d
        else:
            dyv = dy_ref[...]
        dxh = dyv * gv
        m1 = jnp.mean(dxh, axis=-1, keepdims=True)
        m2 = jnp.mean(dxh * xhat, axis=-1, keepdims=True)
        dr = rstd * (dxh - m1 - xhat * m2)
        dr_ref[...] = dr
        drb_ref[...] = (out_scale * dr).astype(BF16)

        @pl.when(i == 0)
        def _():
            dg_ref[...] = jnp.zeros_like(dg_ref)
            db_ref[...] = jnp.zeros_like(db_ref)
            if with_loss:
                loss_ref[...] = jnp.zeros_like(loss_ref)

        dg_ref[...] += jnp.sum(dyv * xhat, axis=0, keepdims=True)
        db_ref[...] += jnp.sum(dyv, axis=0, keepdims=True)
        if with_loss:
            loss_ref[...] += jnp.broadcast_to(part, loss_ref.shape)

    row = pl.BlockSpec((tm, d), lambda i: (i, 0))
    vec = pl.BlockSpec((1, d), lambda i: (0, 0))
    out_shape = [_sds((m, d), F32), _sds((m, d), BF16), _sds((1, d), F32), _sds((1, d), F32)]
    out_specs = [row, row, vec, vec]
    if with_loss:
        operands = [r, g, beta, target]
        in_specs = [row, vec, vec, pl.BlockSpec((tm, d), lambda i: (jnp.maximum(i - skip, 0), 0))]
        out_shape.append(_sds((1, BLOCK), F32))
        out_specs.append(pl.BlockSpec((1, BLOCK), lambda i: (0, 0)))
    else:
        operands = [r, g, dy]
        in_specs = [row, vec, row]
    return pl.pallas_call(body, name=name, grid=(m // tm,), in_specs=in_specs, out_specs=out_specs, out_shape=out_shape,
                          compiler_params=_params(1))(*operands)


def _ffn_bwd(tag, drb, dr, hb, a, b, s, wg, wu, wd, exchange=None):
    m, d = drb.shape
    nd, _, fs = wg.shape
    tm = _tile(m, ROW_TILE)

    pair = 2 if nd % 2 == 0 else 1

    def ds_body(drb_ref, wd_ref, a_ref, b_ref, da_ref, db_ref):
        x = drb_ref[...]
        for blk in range(pair):
            ds = lax.dot_general(x, wd_ref[blk], NT, preferred_element_type=F32)
            av, bv = a_ref[blk], b_ref[blk]
            sg = _sigmoid(av)
            da_ref[blk] = (ds * bv * sg * (1.0 + av * (1.0 - sg))).astype(BF16)
            db_ref[blk] = (ds * av * sg).astype(BF16)

    st = pl.BlockSpec((pair, tm, fs), lambda i, j: (j, i, 0))
    da, db = _pallas(tag + "_ds", ds_body, (m // tm, nd // pair),
                     [pl.BlockSpec((tm, d), lambda i, j: (i, 0)), pl.BlockSpec((pair, fs, d), lambda i, j: (j, 0, 0)), st, st],
                     [st, st], [_sds((nd, m, fs), BF16)] * 2, [], (drb, wd, a, b))

    def epi_w(acc, er, orf):
        orf[0][...] = acc[0].astype(BF16)

    tr = _tile(m, GRAD_ROW_TILE)
    nkm = m // tr
    dwd = _gemm(tag + "_dwd", (nd, nkm),
                [(s, pl.BlockSpec((None, tr, fs), lambda j, k: (j, k, 0)), drb, pl.BlockSpec((tr, d), lambda j, k: (k, 0)))],
                [0], [(fs, d)], TN, [], [(_sds((nd, fs, d), BF16), pl.BlockSpec((None, fs, d), lambda j, k: (j, 0, 0)))], epi_w)[0]
    h_spec = pl.BlockSpec((tr, d), lambda j, k: (k, 0))
    g_spec = pl.BlockSpec((None, tr, fs), lambda j, k: (j, k, 0))
    w_out = pl.BlockSpec((None, d, fs), lambda j, k: (j, 0, 0))
    dwg = _gemm(tag + "_dwg", (nd, nkm), [(hb, h_spec, da, g_spec)], [0], [(d, fs)], TN, [],
                [(_sds((nd, d, fs), BF16), w_out)], epi_w, exchange([dwd]) if exchange else None)
    if exchange:
        (dwg,), (dwd,) = dwg
    else:
        dwg = dwg[0]
    dwu = _gemm(tag + "_dwu", (nd, nkm), [(hb, h_spec, db, g_spec)], [0], [(d, fs)], TN, [],
                [(_sds((nd, d, fs), BF16), w_out)], epi_w, exchange([dwg]) if exchange else None)
    if exchange:
        (dwu,), (dwg,) = dwu
    else:
        dwu = dwu[0]

    def epi_dh(acc, er, orf):
        orf[0][...] = DN_ALPHA * er[0][...] + acc[0]

    gk = pl.BlockSpec((None, tm, fs), lambda i, k: (k, i, 0))
    wk = pl.BlockSpec((None, d, fs), lambda i, k: (k, 0, 0))
    row = pl.BlockSpec((tm, d), lambda i, k: (i, 0))
    dh = _gemm(tag + "_dh", (m // tm, nd), [(da, gk, wg, wk), (db, gk, wu, wk)], [0, 0], [(tm, d)], NT,
               [(dr, row)], [(_sds((m, d), F32), row)], epi_dh, exchange([dwu]) if exchange else None)
    if exchange:
        (dh,), (dwu,) = dh
    else:
        dh = dh[0]
    return dh, dwg, dwu, dwd


def _merge_bwd(dmixb, w_out2, proj, b_gate, u_hg, u_sb, gate_col):
    m, d = dmixb.shape
    ds = _tile(d, COL_TILE, 128)
    nd = d // ds
    tm = _tile(m, ROW_TILE)
    c0 = gate_col // ds

    def epi(acc, er, orf):
        i = pl.program_id(1)
        dy = acc[0]
        g_hg = _sigmoid(er[0][...] + er[2][...])
        g_sb = _sigmoid(er[1][...] + er[3][...])
        orf[0][...] = (dy * g_hg).astype(BF16)
        orf[1][...] = (dy * g_sb).astype(BF16)
        dz_hg = dy * er[4][...] * g_hg * (1.0 - g_hg)
        dz_sb = dy * er[5][...] * g_sb * (1.0 - g_sb)
        orf[2][...] = dz_hg.astype(BF16)
        orf[3][...] = dz_sb.astype(BF16)

        @pl.when(i == 0)
        def _():
            orf[4][...] = jnp.zeros_like(orf[4])
            orf[5][...] = jnp.zeros_like(orf[5])

        orf[4][...] += jnp.sum(dz_hg, axis=0, keepdims=True)
        orf[5][...] += jnp.sum(dz_sb, axis=0, keepdims=True)

    tile = pl.BlockSpec((tm, ds), lambda j, i, k: (i, j))
    vec = pl.BlockSpec((1, ds), lambda j, i, k: (0, j))
    du_hg, du_sb, dz_hg, dz_sb, db_hg, db_sb = _gemm(
        "merge_bwd", (nd, m // tm, 1),
        [(dmixb, pl.BlockSpec((tm, d), lambda j, i, k: (i, 0)), w_out2, pl.BlockSpec((ds, d), lambda j, i, k: (j, 0)))],
        [0], [(tm, ds)], NT,
        [(proj, pl.BlockSpec((tm, ds), lambda j, i, k: (i, c0 + j))),
         (proj, pl.BlockSpec((tm, ds), lambda j, i, k: (i, c0 + nd + j))),
         (b_gate, vec), (b_gate, pl.BlockSpec((1, ds), lambda j, i, k: (0, nd + j))),
         (u_hg, tile), (u_sb, tile)],
        [(_sds((m, d), BF16), tile), (_sds((m, d), BF16), tile), (_sds((m, d), BF16), tile), (_sds((m, d), BF16), tile),
         (_sds((1, d), F32), vec), (_sds((1, d), F32), vec)], epi)
    return du_hg, du_sb, dz_hg, dz_sb, jnp.concatenate([db_hg, db_sb], axis=1)


def _grad_w(name, x, dy, nd_out):
    m, kx = x.shape
    n = dy.shape[1]
    ns = n // nd_out
    tm = _tile(m, GRAD_ROW_TILE)

    def epi(acc, er, orf):
        orf[0][...] = acc[0].astype(BF16)

    return _gemm(name, (nd_out, m // tm),
                 [(x, pl.BlockSpec((tm, kx), lambda j, k: (k, 0)), dy, pl.BlockSpec((tm, ns), lambda j, k: (k, j)))],
                 [0], [(kx, ns)], TN, [], [(_sds((nd_out, kx, ns), BF16), pl.BlockSpec((None, kx, ns), lambda j, k: (j, 0, 0)))], epi)[0]


def _grad_in_whole(name, dy, w2):
    m, n = dy.shape
    kx = w2.shape[0]
    tm = _tile(m, ROW_TILE)

    def epi(acc, er, orf):
        orf[0][...] = acc[0]

    return _gemm(name, (m // tm, 1),
                 [(dy, pl.BlockSpec((tm, n), lambda i, k: (i, 0)),
                   w2, pl.BlockSpec((kx, n), lambda i, k: (0, 0), pipeline_mode=pl.Buffered(1)))],
                 [0], [(tm, kx)], NT, [], [(_sds((m, kx), F32), pl.BlockSpec((tm, kx), lambda i, k: (i, 0)))], epi)[0]


def _grad_in(name, dy, w, add=None, carried=None):
    m = dy.shape[0]
    nd, kx, ns = w.shape
    tm = _tile(m, ROW_TILE)

    def epi(acc, er, orf):
        orf[0][...] = acc[0] if add is None else DN_ALPHA * er[0][...] + acc[0]

    row = pl.BlockSpec((tm, kx), lambda i, k: (i, 0))
    res = _gemm(name, (m // tm, nd),
                [(dy, pl.BlockSpec((tm, ns), lambda i, k: (i, k)), w, pl.BlockSpec((None, kx, ns), lambda i, k: (k, 0, 0)))],
                [0], [(tm, kx)], NT, [] if add is None else [(add, row)], [(_sds((m, kx), F32), row)], epi, carried)
    return res[0] if carried is None else (res[0][0], res[1])


def _tri(n, kind):
    r = lax.broadcasted_iota(jnp.int32, (n, n), 0)
    c = lax.broadcasted_iota(jnp.int32, (n, n), 1)
    return {"le": c <= r, "ge": c >= r, "gt": r > c, "lt": r < c}[kind]


def _dot_f32(a, b, dims=NN):
    return lax.dot_general(a, b, dims, preferred_element_type=F32, precision=lax.Precision.HIGHEST)


def _hgrn_gates(i, hq, hf, logits):
    lg = logits
    mx = jnp.maximum(lg[0:1], lg[1:2])
    e0 = jnp.exp(lg[0:1] - mx)
    lb = e0 / (e0 + jnp.exp(lg[1:2] - mx))
    sig = _sigmoid(hf)
    f = lb + (1.0 - lb) * sig
    valid = (i * BLOCK + lax.broadcasted_iota(jnp.int32, hf.shape, 0)) >= PAD
    g = jnp.where(valid, jnp.log(f), 0.0)
    k = jnp.where(valid, 1.0 - f, 0.0)
    sq = _sigmoid(hq)
    return hq * sq, k, g, sig, f, lb, valid, sq


PAIR_OFF = -1e30


def _pair_mask():
    s_i = lax.broadcasted_iota(jnp.int32, (CHUNK, CHUNK, 1), 0)
    t_i = lax.broadcasted_iota(jnp.int32, (CHUNK, CHUNK, 1), 1)
    return t_i >= s_i


def _heads_per_step(n_heads, want):
    return max(h for h in range(1, want + 1) if n_heads % h == 0)


def _hgrn_fwd(proj, logits, gn, n_heads, carried=None):
    m = proj.shape[0]
    nb = m // BLOCK
    w = n_heads * HEAD
    cpb = BLOCK // CHUNK
    hps = _heads_per_step(n_heads, 4)
    wide = hps * HEAD

    def body(hq_ref, hf_ref, hi_ref, hog_ref, lg_ref, gn_ref, o_ref, ohg_ref, st_all_ref, st_ref, q_s, k_s, v_s, b_s):
        i = pl.program_id(1)

        @pl.when(i == 0)
        def _():
            st_ref[...] = jnp.zeros_like(st_ref)

        q, k, g, _, _, _, _, _ = _hgrn_gates(i, hq_ref[...], hf_ref[...], lg_ref[...])
        q_s[...] = q
        k_s[...] = k
        v_s[...] = hi_ref[...]
        b_s[...] = _dot_f32(_tri(BLOCK, "le").astype(F32), g)
        causal = _pair_mask()

        def chunk(c, carry):
            sl = pl.ds(pl.multiple_of(c * CHUNK, CHUNK), CHUNK)
            prev = pl.ds(pl.multiple_of(jnp.maximum(c - 1, 0) * CHUNK, CHUNK), CHUNK)
            first = (c > 0).astype(F32)
            for hd in range(hps):
                cols = slice(hd * HEAD, (hd + 1) * HEAD)
                b = b_s[sl, cols] - b_s[prev, cols][CHUNK - 1:CHUNK, :] * first
                qc, kc, vc = q_s[sl, cols], k_s[sl, cols], v_s[sl, cols]
                st = st_ref[hd]
                st_all_ref[hd, c] = st.astype(BF16)
                o = lax.dot_general((qc * jnp.exp(b)).astype(BF16), st.astype(BF16), NT, preferred_element_type=F32)
                e = jnp.exp(jnp.minimum(b[None, :, :] - b[:, None, :], 0.0))
                p = jnp.sum(qc[None, :, :] * e * kc[:, None, :], axis=-1, keepdims=True)
                o_ref[sl, cols] = o + jnp.sum(jnp.where(causal, p, 0.0) * vc[:, None, :], axis=0)
                blast = b[CHUNK - 1:CHUNK, :]
                kd = kc * jnp.exp(blast - b)
                st_ref[hd] = st * jnp.exp(blast) + lax.dot_general(vc.astype(BF16), kd.astype(BF16), TN,
                                                                   preferred_element_type=F32)
            return carry

        lax.fori_loop(0, cpb, chunk, 0)
        for hd in range(hps):
            cols = slice(hd * HEAD, (hd + 1) * HEAD)
            o = o_ref[:, cols]
            n = o * lax.rsqrt(jnp.mean(o * o, axis=-1, keepdims=True) + RMS_EPS)
            hog = hog_ref[:, cols]
            ohg_ref[:, cols] = (n * gn_ref[:, cols] * hog * _sigmoid(hog)).astype(BF16)

    def col(group):
        return pl.BlockSpec((BLOCK, wide), lambda h, i: (i, group * (n_heads // hps) + h))

    vec = pl.BlockSpec((1, wide), lambda h, i: (0, h))
    tile = pl.BlockSpec((BLOCK, wide), lambda h, i: (i, h))
    return _pallas(
        "hgrn_fwd", body, (n_heads // hps, nb),
        [col(0), col(1), col(2), col(3), pl.BlockSpec((2, wide), lambda h, i: (0, h)), vec],
        [tile, tile, pl.BlockSpec((hps, cpb, HEAD, HEAD), lambda h, i: (h, i, 0, 0))],
        [_sds((m, w), F32), _sds((m, w), BF16), _sds((n_heads, m // CHUNK, HEAD, HEAD), BF16)],
        [pltpu.VMEM((hps, HEAD, HEAD), F32)] + [pltpu.VMEM((BLOCK, wide), F32)] * 4,
        (proj, proj, proj, proj, logits, gn), carried)


def _hgrn_bwd(proj, logits, gn, o_raw, do_hg, states, n_heads, carried=None):
    m = proj.shape[0]
    nb = m // BLOCK
    w = n_heads * HEAD
    cpb = BLOCK // CHUNK
    last_state = m // CHUNK - 1
    hps = _heads_per_step(n_heads, 2)
    wide = hps * HEAD

    def body(hq_ref, hf_ref, hi_ref, hog_ref, lg_ref, gn_ref, o_ref, do_ref, st_all_ref, st_next_ref,
             dhq_ref, dhf_ref, dhi_ref, dhog_ref, dgn_ref, dlb_ref,
             dst_ref, q_s, k_s, v_s, b_s, do_s, dq_s, dk_s, dv_s, ex_s):
        step = pl.program_id(1)
        i = nb - 1 - step

        @pl.when(step == 0)
        def _():
            dst_ref[...] = jnp.zeros_like(dst_ref)
            dgn_ref[...] = jnp.zeros_like(dgn_ref)
            dlb_ref[...] = jnp.zeros_like(dlb_ref)

        hq = hq_ref[...]
        q, k, g, sig, f, lb, valid, sq = _hgrn_gates(i, hq, hf_ref[...], lg_ref[...])
        q_s[...] = q
        k_s[...] = k
        v_s[...] = hi_ref[...]
        b_s[...] = _dot_f32(_tri(BLOCK, "le").astype(F32), g)

        hog = hog_ref[...]
        sg = _sigmoid(hog)
        sil = hog * sg
        gnv = gn_ref[...]
        dh = do_ref[...]
        dn = dh * gnv * sil
        for hd in range(hps):
            cols = slice(hd * HEAD, (hd + 1) * HEAD)
            o = o_ref[:, cols]
            rs = lax.rsqrt(jnp.mean(o * o, axis=-1, keepdims=True) + RMS_EPS)
            n = o * rs
            ex_s[:, cols] = n
            do_s[:, cols] = rs * (dn[:, cols] - n * jnp.mean(dn[:, cols] * n, axis=-1, keepdims=True))
        n = ex_s[...]
        dhog_ref[...] = (dh * n * gnv * sg * (1.0 + hog * (1.0 - sg))).astype(BF16)
        dgn_ref[...] += jnp.sum(dh * n * sil, axis=0, keepdims=True)
        causal = _pair_mask()

        def chunk(t, st_ends):
            c = cpb - 1 - t
            sl = pl.ds(pl.multiple_of(c * CHUNK, CHUNK), CHUNK)
            prev = pl.ds(pl.multiple_of(jnp.maximum(c - 1, 0) * CHUNK, CHUNK), CHUNK)
            first = (c > 0).astype(F32)
            starts = []
            for hd in range(hps):
                cols = slice(hd * HEAD, (hd + 1) * HEAD)
                b = b_s[sl, cols] - b_s[prev, cols][CHUNK - 1:CHUNK, :] * first
                qc, kc, vc, doc = q_s[sl, cols], k_s[sl, cols], v_s[sl, cols], do_s[sl, cols]
                eb = jnp.exp(b)
                blast = b[CHUNK - 1:CHUNK, :]
                ek = jnp.exp(blast - b)
                dst = dst_ref[hd]
                dstb = dst.astype(BF16)
                docb = doc.astype(BF16)
                st = st_all_ref[hd, c]
                starts.append(st)
                ex_s[sl, cols] = jnp.broadcast_to(jnp.sum(st_ends[hd].astype(F32) * dst, axis=0, keepdims=True),
                                                  (CHUNK, HEAD))
                dq = lax.dot_general(docb, st, NN, preferred_element_type=F32) * eb
                dk = lax.dot_general(vc.astype(BF16), dstb, NN, preferred_element_type=F32) * ek
                dv = lax.dot_general((kc * ek).astype(BF16), dstb, NT, preferred_element_type=F32)
                em = jnp.exp(jnp.where(causal, b[None, :, :] - b[:, None, :], PAIR_OFF))
                dp = jnp.sum(doc[None, :, :] * vc[:, None, :], axis=-1, keepdims=True)
                qe = qc[None, :, :] * em
                p = jnp.sum(qe * kc[:, None, :], axis=-1, keepdims=True)
                dq_s[sl, cols] = dq + jnp.sum(dp * em * kc[:, None, :], axis=0)
                dk_s[sl, cols] = dk + jnp.sum(dp * qe, axis=1)
                dv_s[sl, cols] = dv + jnp.sum(p * doc[None, :, :], axis=1)
                dst_ref[hd] = dst * jnp.exp(blast) + lax.dot_general(docb, (qc * eb).astype(BF16), TN,
                                                                     preferred_element_type=F32)
            return tuple(starts)

        lax.fori_loop(0, cpb, chunk, tuple(st_next_ref[hd, 0] for hd in range(hps)))
        dq, dk = dq_s[...], dk_s[...]
        r_i = lax.broadcasted_iota(jnp.int32, (BLOCK, BLOCK), 0)
        c_i = lax.broadcasted_iota(jnp.int32, (BLOCK, BLOCK), 1)
        within = ((c_i >= r_i) & (c_i // CHUNK == r_i // CHUNK)).astype(F32)
        rc = _dot_f32(within, q * dq - k * dk) + ex_s[...]
        df =jnp.where(valid, rc / f - dk, 0.0)
        dhf_ref[...] = (df * (1.0 - lb) * sig * (1.0 - sig)).astype(BF16)
        dlb_ref[...] += jnp.sum(df * (1.0 - sig), axis=0, keepdims=True)
        dhq_ref[...] = (dq * sq * (1.0 + hq * (1.0 - sq))).astype(BF16)
        dhi_ref[...] = dv_s[...].astype(BF16)

    def col(group):
        return pl.BlockSpec((BLOCK, wide), lambda h, s: (nb - 1 - s, group * (n_heads // hps) + h))

    vec = pl.BlockSpec((1, wide), lambda h, s: (0, h))
    tile = pl.BlockSpec((BLOCK, wide), lambda h, s: (nb - 1 - s, h))
    nxt = pl.BlockSpec((hps, 1, HEAD, HEAD), lambda h, s: (h, jnp.minimum((nb - s) * cpb, last_state), 0, 0))
    return _pallas(
        "hgrn_bwd", body, (n_heads // hps, nb),
        [col(0), col(1), col(2), col(3), pl.BlockSpec((2, wide), lambda h, s: (0, h)), vec, tile, tile,
         pl.BlockSpec((hps, cpb, HEAD, HEAD), lambda h, s: (h, nb - 1 - s, 0, 0)), nxt],
        [tile, tile, tile, tile, vec, vec],
        [_sds((m, w), BF16)] * 4 + [_sds((1, w), F32)] * 2,
        [pltpu.VMEM((hps, HEAD, HEAD), F32)] + [pltpu.VMEM((BLOCK, wide), F32)] * 9,
        (proj, proj, proj, proj, logits, gn, o_raw, do_hg, states, states), carried)


def _split_dot(x, t):
    hi = x.astype(BF16)
    lo = (x - hi.astype(F32)).astype(BF16)
    return jnp.dot(hi, t, preferred_element_type=F32) + jnp.dot(lo, t, preferred_element_type=F32)


def _window_scan(x, tri, after):
    blocks = [x[:, u * BLOCK:(u + 1) * BLOCK] for u in range(SB_UNROLL)]
    inner = _split_dot(jnp.concatenate(blocks, axis=0), tri)
    sums = [jnp.sum(b, axis=-1, keepdims=True) for b in blocks]
    out = []
    for u in range(SB_UNROLL):
        piece = inner[u * BLOCK:(u + 1) * BLOCK, :]
        for other in (sums[u + 1:] if after else sums[:u]):
            piece = piece + other
        out.append(piece)
    total = sums[0]
    for other in sums[1:]:
        total = total + other
    return jnp.concatenate(out, axis=1), total


def _sb_window(ref, j_left, cols):
    parts = [ref[pl.ds(pl.multiple_of(jnp.maximum(j_left + u, 0) * BLOCK, BLOCK), BLOCK), cols]
             for u in range(SB_UNROLL)]
    return jnp.concatenate(parts, axis=0)


def _sb_scores(q, kw, i, j_left, scale):
    z = lax.dot_general(q, kw, NT, preferred_element_type=F32) * scale
    lp = jnp.log(1.0 + jnp.exp(-jnp.abs(z)))
    lbeta = jnp.minimum(z, 0.0) - lp
    qpos = i * BLOCK + lax.broadcasted_iota(jnp.int32, z.shape, 0)
    kpos = j_left * BLOCK + lax.broadcasted_iota(jnp.int32, z.shape, 1)
    mask = (kpos < qpos) & (kpos >= PAD)
    l1m = jnp.where(mask, lbeta - z, 0.0)
    return lbeta, l1m, mask


SB_DEAD = -104.0
SB_UNROLL = 3


def _sb_fwd(proj, n_heads, carried=None):
    m = proj.shape[0]
    nb = m // BLOCK
    w = n_heads * HEAD
    scale = 1.0 / math.sqrt(HEAD)
    hps = _heads_per_step(n_heads, 2)
    wide = hps * HEAD
    heads = [slice(hd * HEAD, (hd + 1) * HEAD) for hd in range(hps)]

    def body(q_ref, k_ref, v_ref, o_ref, start_ref, count_ref):
        h, i = pl.program_id(0), pl.program_id(1)
        tsuf = _tri(BLOCK, "gt").astype(BF16)

        def live(carry):
            t, _, runs = carry
            top = jnp.max(runs[0])
            for run in runs[1:]:
                top = jnp.maximum(top, jnp.max(run))
            return (t <= i) & (top > SB_DEAD)

        def step(carry):
            t, accs, runs = carry
            j_left = i - t - (SB_UNROLL - 1)
            new_accs, new_runs = [], []
            for hd, cols in enumerate(heads):
                start_ref[hd] = jnp.broadcast_to(runs[hd], (BLOCK, HEAD))
                lbeta, l1m, mask = _sb_scores(q_ref[:, cols], _sb_window(k_ref, j_left, cols), i, j_left, scale)
                later, total = _window_scan(l1m, tsuf, True)
                wgt = jnp.where(mask, jnp.exp(lbeta + later + runs[hd]), 0.0)
                new_accs.append(accs[hd] + jnp.dot(wgt.astype(BF16), _sb_window(v_ref, j_left, cols),
                                                   preferred_element_type=F32))
                new_runs.append(runs[hd] + total)
            return t + SB_UNROLL, tuple(new_accs), tuple(new_runs)

        t, accs, _ = lax.while_loop(live, step, (jnp.int32(0), tuple(jnp.zeros((BLOCK, HEAD), F32) for _ in heads),
                                                 tuple(jnp.zeros((BLOCK, 1), F32) for _ in heads)))
        for hd, cols in enumerate(heads):
            o_ref[:, cols] = accs[hd].astype(BF16)
        count_ref[h, i] = t.astype(F32)

    def whole(group):
        return pl.BlockSpec((m, wide), lambda h, i: (0, group * (n_heads // hps) + h), pipeline_mode=pl.Buffered(1))

    return _pallas(
        "sb_fwd", body, (n_heads // hps, nb),
        [pl.BlockSpec((BLOCK, wide), lambda h, i: (i, h)), whole(1), whole(2)],
        [pl.BlockSpec((BLOCK, wide), lambda h, i: (i, h)), pl.BlockSpec((hps, BLOCK, HEAD), lambda h, i: (h, i, 0)),
         pl.BlockSpec(memory_space=pltpu.SMEM)],
        [_sds((m, w), BF16), _sds((n_heads, m, HEAD), F32), _sds((n_heads // hps, nb), F32)],
        [], (proj, proj, proj), carried)


def _sb_bwd(proj, do, start, count, n_heads):
    m = proj.shape[0]
    nb = m // BLOCK
    w = n_heads * HEAD
    scale = 1.0 / math.sqrt(HEAD)
    hps = _heads_per_step(n_heads, 2)
    wide = hps * HEAD
    heads = [slice(hd * HEAD, (hd + 1) * HEAD) for hd in range(hps)]

    def body(q_ref, k_ref, v_ref, do_ref, start_ref, count_ref, dq_ref, dk_ref, dv_ref, dk_s, dv_s):
        h, i = pl.program_id(0), pl.program_id(1)

        @pl.when(i == 0)
        def _():
            dk_s[...] = jnp.zeros_like(dk_s)
            dv_s[...] = jnp.zeros_like(dv_s)

        count = count_ref[h, i].astype(jnp.int32)
        first = i + 1 - count
        tsuf = _tri(BLOCK, "gt").astype(BF16)
        tpre = _tri(BLOCK, "lt").astype(BF16)

        def step(t, carry):
            dqs, rights, psums = carry
            j_left = first + t * SB_UNROLL
            out = []
            for hd, cols in enumerate(heads):
                q = q_ref[:, cols]
                dob = do_ref[:, cols].astype(BF16)
                kw = _sb_window(k_ref, j_left, cols)
                vw = _sb_window(v_ref, j_left, cols)
                lbeta, l1m, mask = _sb_scores(q, kw, i, j_left, scale)
                later, total = _window_scan(l1m, tsuf, True)
                right = jnp.where(t == 0, rights[hd], rights[hd] - total)
                a = jnp.where(mask, jnp.exp(lbeta + later + right), 0.0)
                p = a * lax.dot_general(dob, vw, NT, preferred_element_type=F32)
                earlier, p_total = _window_scan(p, tpre, False)
                below = psums[hd] + earlier
                beta = jnp.exp(lbeta)
                dz = (jnp.where(mask, p * (1.0 - beta) - below * beta, 0.0) * scale).astype(BF16)
                dq = dqs[hd] + jnp.dot(dz, kw, preferred_element_type=F32)
                dkw = lax.dot_general(dz, q, TN, preferred_element_type=F32)
                dvw = lax.dot_general(a.astype(BF16), dob, TN, preferred_element_type=F32)
                for u in range(SB_UNROLL):
                    rows = pl.ds(pl.multiple_of(jnp.maximum(j_left + u, 0) * BLOCK, BLOCK), BLOCK)
                    dk_s[rows, cols] += dkw[u * BLOCK:(u + 1) * BLOCK, :]
                    dv_s[rows, cols] += dvw[u * BLOCK:(u + 1) * BLOCK, :]
                out.append((dq, right, psums[hd] + p_total))
            return tuple(o[0] for o in out), tuple(o[1] for o in out), tuple(o[2] for o in out)

        dqs, _, _ = lax.fori_loop(0, count // SB_UNROLL, step,
                                  (tuple(jnp.zeros((BLOCK, HEAD), F32) for _ in heads),
                                   tuple(start_ref[hd, :, 0:1] for hd in range(hps)),
                                   tuple(jnp.zeros((BLOCK, 1), F32) for _ in heads)))
        for hd, cols in enumerate(heads):
            dq_ref[:, cols] = dqs[hd].astype(BF16)

        @pl.when(i == nb - 1)
        def _():
            dk_ref[...] = dk_s[...].astype(BF16)
            dv_ref[...] = dv_s[...].astype(BF16)

    def whole(group):
        return pl.BlockSpec((m, wide), lambda h, i: (0, group * (n_heads // hps) + h), pipeline_mode=pl.Buffered(1))

    tile = pl.BlockSpec((BLOCK, wide), lambda h, i: (i, h))
    col = pl.BlockSpec((m, wide), lambda h, i: (0, h))
    return pl.pallas_call(
        body, name="sb_bwd", grid=(n_heads // hps, nb),
        in_specs=[tile, whole(1), whole(2), tile, pl.BlockSpec((hps, BLOCK, HEAD), lambda h, i: (h, i, 0)),
                  pl.BlockSpec(memory_space=pltpu.SMEM)],
        out_specs=[tile, col, col],
        out_shape=[_sds((m, w), BF16)] * 3,
        scratch_shapes=[pltpu.VMEM((m, wide), F32)] * 2,
        compiler_params=_params(2),
    )(proj, proj, proj, do, start, count)


def _grad_w_rows(name, x, dy, nd_out):
    m, kx = x.shape
    n = dy.shape[1]
    ks = kx // nd_out
    tm = _tile(m, GRAD_ROW_TILE)

    def epi(acc, er, orf):
        orf[0][...] = acc[0].astype(BF16)

    return _gemm(name, (nd_out, m // tm),
                 [(x, pl.BlockSpec((tm, ks), lambda j, k: (k, j)), dy, pl.BlockSpec((tm, n), lambda j, k: (k, 0)))],
                 [0], [(ks, n)], TN, [], [(_sds((nd_out, ks, n), BF16), pl.BlockSpec((None, ks, n), lambda j, k: (j, 0, 0)))], epi)[0]


def _local_step(x, target, meta, vec, wts, shards=None):
    d = x.shape[1]
    width = vec["hg_norm_g"].shape[1]
    n_heads = width // HEAD
    gate_col = 7 * width
    h0 = jnp.concatenate([jnp.zeros((PAD, d), F32), meta, x], axis=0)
    h0b = h0.astype(BF16)
    wts = dict(wts)
    exchange = None if shards is None else _exchange_carried

    if shards is None:
        a1, b1, s1 = _ffn_up("ffn1_up", h0b, wts["ffn1_w_gate"], wts["ffn1_w_up"])
        r1, h1, h1b = _residual_ln("ffn1_down", s1, True, wts["ffn1_w_down"], h0, vec["ln1_g"], vec["ln1_b"], 0.5)
        proj = _in_proj(h1b, wts["w_in"])
        qkv = proj[:, 4 * width:7 * width].astype(BF16)
        o_raw, o_hg, states = _hgrn_fwd(proj, vec["hg_lb_logits"], vec["hg_norm_g"], n_heads)
        o_sb, sb_start, sb_count = _sb_fwd(qkv, n_heads)
    else:
        half = shards["w_in"].shape[0] // 2
        (a1, b1, s1), (wts["ffn1_w_down"], w_in_top) = _ffn_up(
            "ffn1_up", h0b, wts["ffn1_w_gate"], wts["ffn1_w_up"],
            _gather_carried([shards["ffn1_w_down"], shards["w_in"][:half]]))
        (r1, h1, h1b), (w_in_bottom,) = _residual_ln("ffn1_down", s1, True, wts["ffn1_w_down"], h0, vec["ln1_g"],
                                                     vec["ln1_b"], 0.5, _gather_carried([shards["w_in"][half:]]))
        wts["w_in"] = jnp.concatenate([w_in_top, w_in_bottom], axis=1)
        with_proj = ("w_proj_hg", "w_proj_sb", "w_out", "ffn2_w_gate")
        proj, got = _in_proj(h1b, wts["w_in"], _gather_carried([shards[k] for k in with_proj]))
        wts.update(zip(with_proj, got))
        qkv = proj[:, 4 * width:7 * width].astype(BF16)
        (o_raw, o_hg, states), (wts["ffn2_w_up"],) = _hgrn_fwd(proj, vec["hg_lb_logits"], vec["hg_norm_g"], n_heads,
                                                               _gather_carried([shards["ffn2_w_up"]]))
        (o_sb, sb_start, sb_count), (wts["ffn2_w_down"],) = _sb_fwd(qkv, n_heads,
                                                                    _gather_carried([shards["ffn2_w_down"]]))
    nd = wts["w_in"].shape[0]
    w_out = wts["w_out"]
    p_hg2 = wts["w_proj_hg"].transpose(1, 0, 2).reshape(width, d)
    p_sb2 = wts["w_proj_sb"].transpose(1, 0, 2).reshape(width, d)
    u_hg, u_sb, y = _proj_merge(o_hg, o_sb, p_hg2, p_sb2, proj, vec["b_gate"], gate_col)
    r2, h2, h2b = _residual_ln("out_proj", y, False, w_out.reshape(d, d), h1, vec["ln2_g"], vec["ln2_b"], 1.0)
    a2, b2, s2 = _ffn_up("ffn2_up", h2b,wts["ffn2_w_gate"], wts["ffn2_w_up"])
    r3, _, _ = _residual_ln("ffn2_down", s2, True, wts["ffn2_w_down"], h2, vec["ln3_g"], vec["ln3_b"], 0.5)

    dr3, dr3b, dg3, db3, loss = _ln_bwd("ln3_bwd", r3, vec["ln3_g"], 0.5, beta=vec["ln3_b"], target=target, first_row=BLOCK)
    dh2, dwg2, dwu2, dwd2 = _ffn_bwd("ffn2", dr3b, dr3, h2b, a2, b2, s2, wts["ffn2_w_gate"], wts["ffn2_w_up"],
                                     wts["ffn2_w_down"], exchange)
    dr2, dr2b, dg2, db2 = _ln_bwd("ln2_bwd", r2, vec["ln2_g"], 1.0, dy=dh2)
    du_hg, du_sb, dz_hg, dz_sb, dbg = _merge_bwd(dr2b, w_out.reshape(d, d), proj, vec["b_gate"], u_hg, u_sb, gate_col)
    dw_out = _grad_w_rows("dw_out", y, dr2b, nd)
    dp_hg = _grad_w("dp_hg", o_hg, du_hg, nd)
    dp_sb = _grad_w("dp_sb", o_sb, du_sb, nd)
    do_hg = _grad_in_whole("do_hg", du_hg, p_hg2)
    do_sb = _grad_in_whole("do_sb", du_sb, p_sb2)
    hg = _hgrn_bwd(proj, vec["hg_lb_logits"], vec["hg_norm_g"], o_raw, do_hg, states, n_heads,
                   exchange([dw_out, dp_hg, dp_sb]) if exchange else None)
    if exchange:
        hg, (dw_out, dp_hg, dp_sb) = hg
    dhq, dhf, dhi, dhog, dgn, dlb = hg
    dsq, dsk, dsv = _sb_bwd(qkv, do_sb, sb_start, sb_count, n_heads)
    dproj = jnp.concatenate([dhq, dhf, dhi, dhog, dsq, dsk, dsv, dz_hg, dz_sb], axis=1)
    dw_in = _grad_w("dw_in", h1b, dproj, nd)
    dh1 = _grad_in("dh1", dproj, wts["w_in"], add=dr2, carried=exchange([dw_in]) if exchange else None)
    if exchange:
        dh1, (dw_in,) = dh1
    dr1, dr1b, dg1, db1 = _ln_bwd("ln1_bwd", r1, vec["ln1_g"], 0.5, dy=dh1)
    dh0, dwg1, dwu1, dwd1 = _ffn_bwd("ffn1", dr1b, dr1, h0b, a1, b1, s1, wts["ffn1_w_gate"], wts["ffn1_w_up"],
                                     wts["ffn1_w_down"], exchange)

    small = {"ln1_g": dg1, "ln1_b": db1, "ln2_g": dg2, "ln2_b": db2, "ln3_g": dg3, "ln3_b": db3,
             "b_gate": dbg, "hg_lb": dlb, "hg_norm_g": dgn}
    big = {"ffn1_w_gate": dwg1, "ffn1_w_up": dwu1, "ffn1_w_down": dwd1, "w_in": dw_in, "w_proj_hg": dp_hg,
           "w_proj_sb": dp_sb, "w_out": dw_out, "ffn2_w_gate": dwg2, "ffn2_w_up": dwu2, "ffn2_w_down": dwd2}
    return loss, dh0[BLOCK:], dh0[PAD:BLOCK], small, big


def _position():
    return lax.axis_index("x"), lax.axis_index("y"), lax.axis_index("c")


def _slot(px, py, pc):
    return 4 * px + 2 * py + pc


def _all_gather(shards):
    n = len(shards)

    def body(*refs):
        ins, outs = refs[:n], refs[n:2 * n]
        send_sems, recv_sems, local_sems = refs[2 * n:]
        x, y, c = _position()
        me, sibling = (x, y, c), (x, y, 1 - c)
        chips = [(1 - x, y), (x, 1 - y), (1 - x, 1 - y)]

        def copy(a, k, block, to, src=None):
            dst = outs[a].at[_slot(*block)]
            return pltpu.make_async_remote_copy(src_ref=dst if src is None else src, dst_ref=dst,
                                                send_sem=send_sems.at[a, k], recv_sem=recv_sems.at[a, k],
                                                device_id=to, device_id_type=MESH)

        mine = [pltpu.make_async_copy(ins[a], outs[a].at[_slot(*me)], local_sems.at[a]) for a in range(n)]
        for cp in mine:
            cp.start()
        first = []
        for a in range(n):
            first.append(copy(a, 0, me, sibling, src=ins[a]))
            first += [copy(a, 1 + j, me, (*chip, c), src=ins[a]) for j, chip in enumerate(chips)]
        for cp in first:
            cp.start()
        passed = []
        for j, chip in enumerate(chips):
            for a in range(n):
                copy(a, 1 + j, (*chip, c), me).wait_recv()
                cp = copy(a, 4 + j, (*chip, c), sibling)
                cp.start()
                passed.append(cp)
        for a in range(n):
            copy(a, 0, sibling, me).wait_recv()
        for j, chip in enumerate(chips):
            for a in range(n):
                copy(a, 4 + j, (*chip, 1 - c), me).wait_recv()
        for cp in first + passed:
            cp.wait_send()
        for cp in mine:
            cp.wait()

    return pl.pallas_call(
        body, name="all_gather", out_shape=[_sds((N_DEV,) + s.shape, s.dtype) for s in shards],
        in_specs=[ANY] * n, out_specs=[ANY] * n,
        scratch_shapes=[pltpu.SemaphoreType.DMA((n, 7)), pltpu.SemaphoreType.DMA((n, 7)), pltpu.SemaphoreType.DMA((n,))],
    )(*shards)


def _exchange_carried(grads):
    return _direct_copies(grads, [_sds(g.shape, g.dtype) for g in grads], lambda ref, slot: ref.at[slot])


def _gather_carried(shards):
    return _direct_copies(shards, [_sds((N_DEV,) + s.shape, s.dtype) for s in shards], lambda ref, slot: ref)


def _direct_copies(arrays, outs, block_for):
    n = len(arrays)

    def plan(ins, results, sems, arriving):
        send_sems, recv_sems, local_sems = sems
        x, y, c = _position()
        mine = _slot(x, y, c)
        peers = [(1 - x if k & 4 else x, 1 - y if k & 2 else y, 1 - c if k & 1 else c) for k in range(1, N_DEV)]
        own = [pltpu.make_async_copy(block_for(ins[a], mine), results[a].at[mine], local_sems.at[a]) for a in range(n)]
        remote = [pltpu.make_async_remote_copy(
            src_ref=block_for(ins[a], mine if arriving else _slot(*peer)),
            dst_ref=results[a].at[_slot(*peer) if arriving else mine],
            send_sem=send_sems.at[a, k], recv_sem=recv_sems.at[a, k], device_id=peer, device_id_type=MESH)
            for a in range(n) for k, peer in enumerate(peers)]
        return own, remote

    def start(ins, results, sems):
        own, sent = plan(ins, results, sems, False)
        for cp in own + sent:
            cp.start()

    def finish(ins, results, sems):
        _, landed = plan(ins, results, sems, True)
        for cp in landed:
            cp.wait_recv()
        own, sent = plan(ins, results, sems, False)
        for cp in sent:
            cp.wait_send()
        for cp in own:
            cp.wait()

    sems = [pltpu.SemaphoreType.DMA((n, 7)), pltpu.SemaphoreType.DMA((n, 7)), pltpu.SemaphoreType.DMA((n,))]
    return _Carried(list(arrays), outs, sems, start, finish)


def _all_reduce_rows(v):
    rows = v.shape[0]

    def body(v_ref, out_ref, buf, send_sems, recv_sems):
        x, y, c = _position()
        me, sibling = (x, y, c), (x, y, 1 - c)
        chips = [(1 - x, y), (x, 1 - y), (1 - x, 1 - y)]

        def copy(k, block, to, src=None):
            dst = buf.at[_slot(*block)]
            return pltpu.make_async_remote_copy(src_ref=dst if src is None else src, dst_ref=dst,
                                                send_sem=send_sems.at[k], recv_sem=recv_sems.at[k],
                                                device_id=to, device_id_type=MESH)

        first = [copy(0, me, sibling, src=v_ref)] + [copy(1 + j, me, (*chip, c), src=v_ref) for j, chip in enumerate(chips)]
        for cp in first:
            cp.start()
        buf[_slot(*me)] = v_ref[...]
        passed = [copy(4 + j, (*chip, c), sibling) for j, chip in enumerate(chips)]
        for j, chip in enumerate(chips):
            copy(1 + j, (*chip, c), me).wait_recv()
            passed[j].start()
        copy(0, sibling, me).wait_recv()
        for j, chip in enumerate(chips):
            copy(4 + j, (*chip, 1 - c), me).wait_recv()
        for cp in first + passed:
            cp.wait_send()
        total = buf[0]
        for s in range(1, N_DEV):
            total = total + buf[s]
        out_ref[...] = total

    vmem = pl.BlockSpec(memory_space=pltpu.VMEM)
    return pl.pallas_call(
        body, name="small_all_reduce", out_shape=_sds(v.shape, F32), in_specs=[vmem], out_specs=vmem,
        scratch_shapes=[pltpu.VMEM((N_DEV, rows, 128), F32), pltpu.SemaphoreType.DMA((7,)), pltpu.SemaphoreType.DMA((7,))],
    )(v)


def _adamw(name, w, m, v, contrib):
    r, c = w.shape
    n = contrib.shape[0]
    tr = _tile(r, 256)

    def body(w_ref, m_ref, v_ref, c_ref, g_out, d_out, m_out, v_out):
        g = c_ref[0].astype(F32)
        for s in range(1, n):
            g = g + c_ref[s].astype(F32)
        m2 = ADAM_B1 * m_ref[...] + (1.0 - ADAM_B1) * g
        v2 = ADAM_B2 * v_ref[...] + (1.0 - ADAM_B2) * (g * g)
        m_hat = m2 / (1.0 - ADAM_B1 ** ADAM_STEP)
        v_hat = v2 / (1.0 - ADAM_B2 ** ADAM_STEP)
        g_out[...] = g
        d_out[...] = -ADAM_LR * (m_hat / (jnp.sqrt(v_hat) + ADAM_EPS) + ADAM_WD * w_ref[...])
        m_out[...] = m2
        v_out[...] = v2

    tile = pl.BlockSpec((tr, c), lambda i: (i, 0))
    return pl.pallas_call(
        body, name=name, grid=(r // tr,), in_specs=[tile, tile, tile, pl.BlockSpec((n, tr, c), lambda i: (0, i, 0))],
        out_specs=[tile] * 4, out_shape=[_sds((r, c), F32)] * 4, compiler_params=_params(1),
    )(w, m, v, contrib)


def _lb_logits_grad(logits, dlb):
    def body(lg_ref, d_ref, out_ref):
        lg = lg_ref[...]
        mx = jnp.maximum(lg[0:1], lg[1:2])
        e0 = jnp.exp(lg[0:1] - mx)
        p0 = e0 / (e0 + jnp.exp(lg[1:2] - mx))
        g0 = d_ref[...] * p0 * (1.0 - p0)
        out_ref[0:1, :] = g0
        out_ref[1:2, :] = -g0

    return pl.pallas_call(body, name="lb_logits_grad", out_shape=_sds(logits.shape, F32))(logits, dlb)


BIG = ("ffn1_w_gate", "ffn1_w_up", "ffn1_w_down", "w_in", "w_proj_hg", "w_proj_sb", "w_out",
       "ffn2_w_gate", "ffn2_w_up", "ffn2_w_down")
VECTORS = ("ln1_g", "ln1_b", "b_gate", "hg_lb_logits", "hg_norm_g", "ln2_g", "ln2_b", "ln3_g", "ln3_b")
WEIGHTS = ("meta", "ln1_g", "ln1_b", "ffn1_w_gate", "ffn1_w_up", "ffn1_w_down", "w_in", "b_gate", "hg_lb_logits",
           "hg_norm_g", "w_proj_hg", "w_proj_sb", "w_out", "ln2_g", "ln2_b", "ffn2_w_gate", "ffn2_w_up",
           "ffn2_w_down", "ln3_g", "ln3_b")


def kernel(x, meta, ln1_g, ln1_b, ffn1_w_gate, ffn1_w_up, ffn1_w_down, w_in, b_gate, hg_lb_logits, hg_norm_g, w_proj_hg, w_proj_sb, w_out, ln2_g, ln2_b, ffn2_w_gate, ffn2_w_up, ffn2_w_down, ln3_g, ln3_b, loss_target, m_meta, m_ln1_g, m_ln1_b, m_ffn1_w_gate, m_ffn1_w_up, m_ffn1_w_down, m_w_in, m_b_gate, m_hg_lb_logits, m_hg_norm_g, m_w_proj_hg, m_w_proj_sb, m_w_out, m_ln2_g, m_ln2_b, m_ffn2_w_gate, m_ffn2_w_up, m_ffn2_w_down, m_ln3_g, m_ln3_b, v_meta, v_ln1_g, v_ln1_b, v_ffn1_w_gate, v_ffn1_w_up, v_ffn1_w_down, v_w_in, v_b_gate, v_hg_lb_logits, v_hg_norm_g, v_w_proj_hg, v_w_proj_sb, v_w_out, v_ln2_g, v_ln2_b, v_ffn2_w_gate, v_ffn2_w_up, v_ffn2_w_down, v_ln3_g, v_ln3_b):
    given = dict(locals())
    d = x.shape[-1]
    ds = meta.shape[1]

    shards = {k: given[k][0].astype(BF16) for k in BIG}
    first = ("ffn1_w_gate", "ffn1_w_up")
    gathered = _all_gather([meta] + [shards.pop(k) for k in first])
    meta_full = gathered[0].transpose(1, 0, 2).reshape(N_META, d)
    vec = {k: given[k] for k in VECTORS}
    loss, grad_x, dmeta, small, received = _local_step(x[0], loss_target[0], meta_full, vec, dict(zip(first, gathered[1:])),
                                                       shards)

    order =("ln1_g", "ln1_b", "ln2_g", "ln2_b", "ln3_g", "ln3_b", "b_gate", "hg_lb", "hg_norm_g")
    parts = [small[k].reshape(-1, 128) for k in order] + [dmeta.reshape(-1, 128), jnp.broadcast_to(loss, (8, 128))]
    total = _all_reduce_rows(jnp.concatenate(parts, axis=0))
    reduced, row = {}, 0
    for k, p in zip(order + ("meta", "loss"), parts):
        reduced[k] = total[row:row + p.shape[0]]
        row += p.shape[0]
    loss_out = reduced["loss"][0, 0]
    me = _slot(*_position())
    dmeta_mine = lax.dynamic_slice(reduced["meta"].reshape(N_META, d), (0, me * ds), (N_META, ds))
    dlogits = _lb_logits_grad(hg_lb_logits, reduced["hg_lb"].reshape(1, -1))

    grads, deltas, new_m, new_v = {}, {}, {}, {}
    for k in WEIGHTS:
        w = given[k]
        lead = w.shape[:-2]
        w2, m2, v2 = (a.reshape(a.shape[-2:]) for a in (w, given["m_" + k], given["v_" + k]))
        if k in BIG:
            contrib = received[k]
        elif k == "meta":
            contrib = dmeta_mine[None]
        elif k == "hg_lb_logits":
            contrib = dlogits[None]
        else:
            contrib = reduced[k].reshape((1,) + w2.shape)
        out = _adamw("adamw_" + k, w2, m2, v2, contrib)
        grads[k], deltas[k], new_m[k], new_v[k] = (o.reshape(lead + o.shape) for o in out)
    return (loss_out, grad_x[None], *[grads[k] for k in WEIGHTS], *[deltas[k] for k in WEIGHTS],
            *[new_m[k] for k in WEIGHTS], *[new_v[k] for k in WEIGHTS])
```

```python
import functools
import math

import jax
import jax.numpy as jnp
from jax import lax
from jax.experimental import pallas as pl
from jax.experimental.pallas import tpu as pltpu

F32 = jnp.float32
BF16 = jnp.bfloat16
MESH = pl.DeviceIdType.MESH

N_DEV = 8
N_META = 16
BLOCK = 128
PAD = BLOCK - N_META
HEAD = 128
CHUNK = 16
LN_EPS = 1e-5
RMS_EPS = 1e-6
DN_ALPHA = 2.0 ** 0.25
ADAM_LR, ADAM_B1, ADAM_B2, ADAM_EPS, ADAM_WD, ADAM_STEP = 0.001, 0.9, 0.999, 1e-08, 0.01, 10

VMEM_LIMIT_V7X = 60 * 1024 * 1024
ROW_TILE = 640
LN_ROW_TILE = 320
COL_TILE = 512
GRAD_ROW_TILE = 1664

NN = (((1,), (0,)), ((), ()))
NT = (((1,), (1,)), ((), ()))
TN = (((0,), (0,)), ((), ()))


def _tile(n, pref, mult=16):
    best = None
    for t in range(mult, min(n, pref) + 1, mult):
        if n % t == 0:
            best = t
    return n if best is None else best


def _params(n_axes):
    return pltpu.CompilerParams(dimension_semantics=("arbitrary",) * n_axes, vmem_limit_bytes=VMEM_LIMIT_V7X)


def _sigmoid(x):
    return 1.0 / (1.0 + jnp.exp(-x))


class _Carried:
    def __init__(self, ins, outs, sems, start, finish):
        self.ins, self.outs, self.sems, self.start, self.finish = ins, outs, sems, start, finish


ANY = pl.BlockSpec(memory_space=pl.ANY)


def _pallas(name, body, grid, in_specs, out_specs, out_shape, scratch, operands, carried=None):
    if carried is None:
        return pl.pallas_call(body, name=name, grid=grid, in_specs=in_specs, out_specs=out_specs, out_shape=out_shape,
                              scratch_shapes=scratch, compiler_params=_params(len(grid)))(*operands)
    n_in, n_out, n_scr = len(in_specs), len(out_specs), len(scratch)
    c_in, c_out = len(carried.ins), len(carried.outs)

    def wrapped(*refs):
        ins, rest = refs[:n_in], refs[n_in:]
        c_ins, rest = rest[:c_in], rest[c_in:]
        outs, rest = rest[:n_out], rest[n_out:]
        c_outs, rest = rest[:c_out], rest[c_out:]
        scr, c_sems = rest[:n_scr], rest[n_scr:]
        first = last = None
        for axis, size in enumerate(grid):
            at0, at_end = pl.program_id(axis) == 0, pl.program_id(axis) == size - 1
            first = at0 if first is None else first & at0
            last = at_end if last is None else last & at_end

        @pl.when(first)
        def _():
            carried.start(c_ins, c_outs, c_sems)

        body(*ins, *outs, *scr)

        @pl.when(last)
        def _():
            carried.finish(c_ins, c_outs, c_sems)

    res = pl.pallas_call(
        wrapped, name=name, grid=grid, in_specs=list(in_specs) + [ANY] * c_in, out_specs=list(out_specs) + [ANY] * c_out,
        out_shape=list(out_shape) + list(carried.outs), scratch_shapes=list(scratch) + list(carried.sems),
        compiler_params=_params(len(grid)),
    )(*operands, *carried.ins)
    return res[:n_out], res[n_out:]


def _gemm(name, grid, pairs, acc_of, acc_shapes, dims, extras, outs, epilogue, carried=None):
    n_extra, n_out = len(extras), len(outs)
    nk = grid[-1]
    k_axis = len(grid) - 1
    operands, in_specs, where = [], [], {}
    for a, a_spec, b, b_spec in pairs:
        for arr, spec in ((a, a_spec), (b, b_spec)):
            if (id(arr), id(spec)) not in where:
                where[(id(arr), id(spec))] = len(operands)
                operands.append(arr)
                in_specs.append(spec)
    n_mat = len(operands)
    slots = [(where[(id(a), id(a_spec))], where[(id(b), id(b_spec))]) for a, a_spec, b, b_spec in pairs]

    def body(*refs):
        er = refs[n_mat:n_mat + n_extra]
        orf = refs[n_mat + n_extra:n_mat + n_extra + n_out]
        accs = refs[n_mat + n_extra + n_out:]

        def part(p):
            a_ref, b_ref = refs[slots[p][0]], refs[slots[p][1]]
            if len(a_ref.shape) == 2:
                return lax.dot_general(a_ref[...].astype(BF16), b_ref[...].astype(BF16), dims, preferred_element_type=F32)
            total = None
            for s in range(a_ref.shape[0]):
                d = lax.dot_general(a_ref[s].astype(BF16), b_ref[s].astype(BF16), dims, preferred_element_type=F32)
                total = d if total is None else total + d
            return total

        if nk == 1:
            vals = [None] * len(acc_shapes)
            for p in range(len(pairs)):
                d = part(p)
                vals[acc_of[p]] = d if vals[acc_of[p]] is None else vals[acc_of[p]] + d
            epilogue(vals, er, orf)
        else:
            k = pl.program_id(k_axis)

            @pl.when(k == 0)
            def _():
                for acc in accs:
                    acc[...] = jnp.zeros_like(acc)

            for p in range(len(pairs)):
                accs[acc_of[p]][...] += part(p)

            @pl.when(k == nk - 1)
            def _():
                epilogue([acc[...] for acc in accs], er, orf)

    for e, e_spec in extras:
        operands.append(e)
        in_specs.append(e_spec)
    scratch = [] if nk == 1 else [pltpu.VMEM(s, F32) for s in acc_shapes]
    return _pallas(name, body, grid, in_specs, [s for _, s in outs], [o for o, _ in outs], scratch, operands, carried)


def _sds(shape, dtype):
    return jax.ShapeDtypeStruct(shape, dtype)


def _ln_rows(r, g, b):
    mu = jnp.mean(r, axis=-1, keepdims=True)
    xc = r - mu
    var = jnp.mean(xc * xc, axis=-1, keepdims=True)
    return xc * lax.rsqrt(var + LN_EPS) * g + b


def _ffn_up(name, hb, wg, wu, carried=None):
    m, d = hb.shape
    nd, _, fs = wg.shape
    tm = _tile(m, ROW_TILE)

    def epi(acc, er, orf):
        a, b = acc
        sg = _sigmoid(a)
        silu = a * sg
        orf[0][...] = b * sg * (1.0 + a * (1.0 - sg))
        orf[1][...] = silu
        orf[2][...] = (silu * b).astype(BF16)

    h_spec = pl.BlockSpec((tm, d), lambda i, j, k: (i, 0))
    w_spec = pl.BlockSpec((None, d, fs), lambda i, j, k: (j, 0, 0))
    o_spec = pl.BlockSpec((None, tm, fs), lambda i, j, k: (j, i, 0))
    return _gemm(name, (m // tm, nd, 1), [(hb, h_spec, wg, w_spec), (hb, h_spec, wu, w_spec)], [0, 1],
                 [(tm, fs)] * 2, NN, [],
                 [(_sds((nd, m, fs), F32), o_spec), (_sds((nd, m, fs), F32), o_spec), (_sds((nd, m, fs), BF16), o_spec)], epi,
                 carried)


def _residual_ln(name, a, a_stacked, w, h_in, g, beta, scale, carried=None):
    d = w.shape[-1]
    m = h_in.shape[0]
    tm = _tile(m, LN_ROW_TILE)

    def epi(acc, er, orf):
        r = DN_ALPHA * er[0][...] + scale * acc[0]
        h = _ln_rows(r, er[1][...], er[2][...])
        orf[0][...] = r
        orf[1][...] = h
        orf[2][...] = h.astype(BF16)

    once = pl.Buffered(1)
    if a_stacked:
        a_spec = pl.BlockSpec((a.shape[0], tm, a.shape[2]), lambda i, k: (0, i, 0))
        w_spec = pl.BlockSpec(w.shape, lambda i, k: (0, 0, 0), pipeline_mode=once)
    else:
        a_spec = pl.BlockSpec((tm, a.shape[1]), lambda i, k: (i, 0))
        w_spec = pl.BlockSpec(w.shape, lambda i, k: (0, 0), pipeline_mode=once)
    row = pl.BlockSpec((tm, d), lambda i, k: (i, 0))
    vec = pl.BlockSpec((1, d), lambda i, k: (0, 0))
    return _gemm(name, (m // tm, 1), [(a, a_spec, w, w_spec)], [0], [(tm, d)], NN,
                 [(h_in, row), (g, vec), (beta, vec)],
                 [(_sds((m, d), F32), row), (_sds((m, d), F32), row), (_sds((m, d), BF16), row)], epi, carried)


def _in_proj(hb, w_in, carried=None):
    m, d = hb.shape
    nd, _, cs = w_in.shape
    tm = _tile(m, ROW_TILE)

    def epi(acc, er, orf):
        orf[0][...] = acc[0]
        orf[1][...] = acc[0].astype(BF16)

    out = pl.BlockSpec((tm, cs), lambda i, j, k: (i, j))
    res = _gemm("in_proj", (m // tm, nd, 1),
                [(hb, pl.BlockSpec((tm, d), lambda i, j, k: (i, 0)), w_in, pl.BlockSpec((None, d, cs), lambda i, j, k: (j, 0, 0)))],
                [0], [(tm, cs)], NN, [], [(_sds((m, nd * cs), F32), out), (_sds((m, nd * cs), BF16), out)], epi, carried)
    return tuple(res) if carried is None else (tuple(res[0]), res[1])


def _proj_merge(o_hg, o_sb, p_hg, p_sb, proj, b_gate, gate_col):
    m, w = o_hg.shape
    d = p_hg.shape[1]
    tm = _tile(m, ROW_TILE)
    tn = _tile(d, COL_TILE, 128)
    nn = d // tn
    c0 = gate_col // tn

    def epi(acc, er, orf):
        u_hg, u_sb = acc
        g_hg = _sigmoid(er[0][...] + er[2][...])
        g_sb = _sigmoid(er[1][...] + er[3][...])
        orf[0][...] = u_hg
        orf[1][...] = u_sb
        orf[2][...] = (g_hg * u_hg + g_sb * u_sb).astype(BF16)

    o_spec = pl.BlockSpec((tm, w), lambda i, j, k: (i, 0))
    p_spec = pl.BlockSpec((w, tn), lambda i, j, k: (0, j))
    out = pl.BlockSpec((tm, tn), lambda i, j, k: (i, j))
    return _gemm("proj_merge", (m // tm, nn, 1), [(o_hg, o_spec, p_hg, p_spec), (o_sb, o_spec, p_sb, p_spec)], [0, 1],
                 [(tm, tn)] * 2, NN,
                 [(proj, pl.BlockSpec((tm, tn), lambda i, j, k: (i, c0 + j))),
                  (proj, pl.BlockSpec((tm, tn), lambda i, j, k: (i, c0 + nn + j))),
                  (b_gate, pl.BlockSpec((1, tn), lambda i, j, k: (0, j))),
                  (b_gate, pl.BlockSpec((1, tn), lambda i, j, k: (0, nn + j)))],
                 [(_sds((m, d), F32), out), (_sds((m, d), F32), out), (_sds((m, d), BF16), out)], epi)


def _ln_bwd(name, r, g, out_scale, dy=None, beta=None, target=None, first_row=0):
    m, d = r.shape
    tm = _tile(m, LN_ROW_TILE if target is None else BLOCK)
    with_loss = target is not None
    skip = first_row // tm if with_loss else 0
    assert not with_loss or first_row % tm == 0

    def body(*refs):
        if with_loss:
            r_ref, g_ref, b_ref, t_ref, dr_ref, drb_ref, dg_ref, db_ref, loss_ref = refs
        else:
            r_ref, g_ref, dy_ref, dr_ref, drb_ref, dg_ref, db_ref = refs
        i = pl.program_id(0)
        x = r_ref[...]
        mu = jnp.mean(x, axis=-1, keepdims=True)
        xc = x - mu
        var = jnp.mean(xc * xc, axis=-1, keepdims=True)
        rstd = lax.rsqrt(var + LN_EPS)
        xhat = xc * rstd
        gv = g_ref[...]
        if with_loss:
            err = xhat * gv + b_ref[...] - t_ref[...]
            live = (i >= skip).astype(F32)
            dyv = err * (live / d)
            part = 0.5 * live * jnp.sum(jnp.sum(err * err, axis=-1, keepdims=True), axis=0, keepdims=True) / d
        else:
            dyv = dy_ref[...]
        dxh = dyv * gv
        m1 = jnp.mean(dxh, axis=-1, keepdims=True)
        m2 = jnp.mean(dxh * xhat, axis=-1, keepdims=True)
        dr = rstd * (dxh - m1 - xhat * m2)
        dr_ref[...] = dr
        drb_ref[...] = (out_scale * dr).astype(BF16)

        @pl.when(i == 0)
        def _():
            dg_ref[...] = jnp.zeros_like(dg_ref)
            db_ref[...] = jnp.zeros_like(db_ref)
            if with_loss:
                loss_ref[...] = jnp.zeros_like(loss_ref)

        dg_ref[...] += jnp.sum(dyv * xhat, axis=0, keepdims=True)
        db_ref[...] += jnp.sum(dyv, axis=0, keepdims=True)
        if with_loss:
            loss_ref[...] += jnp.broadcast_to(part, loss_ref.shape)

    row = pl.BlockSpec((tm, d), lambda i: (i, 0))
    vec = pl.BlockSpec((1, d), lambda i: (0, 0))
    out_shape = [_sds((m, d), F32), _sds((m, d), BF16), _sds((1, d), F32), _sds((1, d), F32)]
    out_specs = [row, row, vec, vec]
    if with_loss:
        operands = [r, g, beta, target]
        in_specs = [row, vec, vec, pl.BlockSpec((tm, d), lambda i: (jnp.maximum(i - skip, 0), 0))]
        out_shape.append(_sds((1, BLOCK), F32))
        out_specs.append(pl.BlockSpec((1, BLOCK), lambda i: (0, 0)))
    else:
        operands = [r, g, dy]
        in_specs = [row, vec, row]
    return pl.pallas_call(body, name=name, grid=(m // tm,), in_specs=in_specs, out_specs=out_specs, out_shape=out_shape,
                          compiler_params=_params(1))(*operands)


def _ffn_bwd(tag, drb, dr, hb, a, b, s, wg, wu, wd, exchange=None):
    m, d = drb.shape
    nd, _, fs = wg.shape
    tm = _tile(m, ROW_TILE)

    pair = 2 if nd % 2 == 0 else 1

    def ds_body(drb_ref, wd_ref, a_ref, b_ref, da_ref, db_ref):
        x = drb_ref[...]
        for blk in range(pair):
            ds = lax.dot_general(x, wd_ref[blk], NT, preferred_element_type=F32)
            da_ref[blk] = (ds * a_ref[blk]).astype(BF16)
            db_ref[blk] = (ds * b_ref[blk]).astype(BF16)

    st = pl.BlockSpec((pair, tm, fs), lambda i, j: (j, i, 0))
    da, db = _pallas(tag + "_ds", ds_body, (m // tm, nd // pair),
                     [pl.BlockSpec((tm, d), lambda i, j: (i, 0)), pl.BlockSpec((pair, fs, d), lambda i, j: (j, 0, 0)), st, st],
                     [st, st], [_sds((nd, m, fs), BF16)] * 2, [], (drb, wd, a, b))

    def epi_w(acc, er, orf):
        orf[0][...] = acc[0].astype(BF16)

    tr = _tile(m, GRAD_ROW_TILE)
    nkm = m // tr
    dwd = _gemm(tag + "_dwd", (nd, nkm),
                [(s, pl.BlockSpec((None, tr, fs), lambda j, k: (j, k, 0)), drb, pl.BlockSpec((tr, d), lambda j, k: (k, 0)))],
                [0], [(fs, d)], TN, [], [(_sds((nd, fs, d), BF16), pl.BlockSpec((None, fs, d), lambda j, k: (j, 0, 0)))], epi_w)[0]
    h_spec = pl.BlockSpec((tr, d), lambda j, k: (k, 0))
    g_spec = pl.BlockSpec((None, tr, fs), lambda j, k: (j, k, 0))
    w_out = pl.BlockSpec((None, d, fs), lambda j, k: (j, 0, 0))
    dwg = _gemm(tag + "_dwg", (nd, nkm), [(hb, h_spec, da, g_spec)], [0], [(d, fs)], TN, [],
                [(_sds((nd, d, fs), BF16), w_out)], epi_w, exchange([dwd]) if exchange else None)
    if exchange:
        (dwg,), (dwd,) = dwg
    else:
        dwg = dwg[0]
    dwu = _gemm(tag + "_dwu", (nd, nkm), [(hb, h_spec, db, g_spec)], [0], [(d, fs)], TN, [],
                [(_sds((nd, d, fs), BF16), w_out)], epi_w, exchange([dwg]) if exchange else None)
    if exchange:
        (dwu,), (dwg,) = dwu
    else:
        dwu = dwu[0]

    def epi_dh(acc, er, orf):
        orf[0][...] = DN_ALPHA * er[0][...] + acc[0]

    gk = pl.BlockSpec((None, tm, fs), lambda i, k: (k, i, 0))
    wk = pl.BlockSpec((None, d, fs), lambda i, k: (k, 0, 0))
    row = pl.BlockSpec((tm, d), lambda i, k: (i, 0))
    dh = _gemm(tag + "_dh", (m // tm, nd), [(da, gk, wg, wk), (db, gk, wu, wk)], [0, 0], [(tm, d)], NT,
               [(dr, row)], [(_sds((m, d), F32), row)], epi_dh, exchange([dwu]) if exchange else None)
    if exchange:
        (dh,), (dwu,) = dh
    else:
        dh = dh[0]
    return dh, dwg, dwu, dwd


def _merge_bwd(dmixb, w_out2, proj, b_gate, u_hg, u_sb, gate_col):
    m, d = dmixb.shape
    ds = _tile(d, COL_TILE, 128)
    nd = d // ds
    tm = _tile(m, ROW_TILE)
    c0 = gate_col // ds

    def epi(acc, er, orf):
        i = pl.program_id(1)
        dy = acc[0]
        g_hg = _sigmoid(er[0][...] + er[2][...])
        g_sb = _sigmoid(er[1][...] + er[3][...])
        orf[0][...] = (dy * g_hg).astype(BF16)
        orf[1][...] = (dy * g_sb).astype(BF16)
        dz_hg = dy * er[4][...] * g_hg * (1.0 - g_hg)
        dz_sb = dy * er[5][...] * g_sb * (1.0 - g_sb)
        orf[2][...] = dz_hg.astype(BF16)
        orf[3][...] = dz_sb.astype(BF16)

        @pl.when(i == 0)
        def _():
            orf[4][...] = jnp.zeros_like(orf[4])
            orf[5][...] = jnp.zeros_like(orf[5])

        orf[4][...] += jnp.sum(dz_hg, axis=0, keepdims=True)
        orf[5][...] += jnp.sum(dz_sb, axis=0, keepdims=True)

    tile = pl.BlockSpec((tm, ds), lambda j, i, k: (i, j))
    vec = pl.BlockSpec((1, ds), lambda j, i, k: (0, j))
    du_hg, du_sb, dz_hg, dz_sb, db_hg, db_sb = _gemm(
        "merge_bwd", (nd, m // tm, 1),
        [(dmixb, pl.BlockSpec((tm, d), lambda j, i, k: (i, 0)), w_out2, pl.BlockSpec((ds, d), lambda j, i, k: (j, 0)))],
        [0], [(tm, ds)], NT,
        [(proj, pl.BlockSpec((tm, ds), lambda j, i, k: (i, c0 + j))),
         (proj, pl.BlockSpec((tm, ds), lambda j, i, k: (i, c0 + nd + j))),
         (b_gate, vec), (b_gate, pl.BlockSpec((1, ds), lambda j, i, k: (0, nd + j))),
         (u_hg, tile), (u_sb, tile)],
        [(_sds((m, d), BF16), tile), (_sds((m, d), BF16), tile), (_sds((m, d), BF16), tile), (_sds((m, d), BF16), tile),
         (_sds((1, d), F32), vec), (_sds((1, d), F32), vec)], epi)
    return du_hg, du_sb, dz_hg, dz_sb, jnp.concatenate([db_hg, db_sb], axis=1)


def _grad_w(name, x, dy, nd_out):
    m, kx = x.shape
    n = dy.shape[1]
    ns = n // nd_out
    tm = _tile(m, GRAD_ROW_TILE)

    def epi(acc, er, orf):
        orf[0][...] = acc[0].astype(BF16)

    return _gemm(name, (nd_out, m // tm),
                 [(x, pl.BlockSpec((tm, kx), lambda j, k: (k, 0)), dy, pl.BlockSpec((tm, ns), lambda j, k: (k, j)))],
                 [0], [(kx, ns)], TN, [], [(_sds((nd_out, kx, ns), BF16), pl.BlockSpec((None, kx, ns), lambda j, k: (j, 0, 0)))], epi)[0]


def _grad_in_whole(name, dy, w2):
    m, n = dy.shape
    kx = w2.shape[0]
    tm = _tile(m, ROW_TILE)

    def epi(acc, er, orf):
        orf[0][...] = acc[0]

    return _gemm(name, (m // tm, 1),
                 [(dy, pl.BlockSpec((tm, n), lambda i, k: (i, 0)),
                   w2, pl.BlockSpec((kx, n), lambda i, k: (0, 0), pipeline_mode=pl.Buffered(1)))],
                 [0], [(tm, kx)], NT, [], [(_sds((m, kx), F32), pl.BlockSpec((tm, kx), lambda i, k: (i, 0)))], epi)[0]


def _grad_in(name, dy, w, add=None, carried=None):
    m = dy.shape[0]
    nd, kx, ns = w.shape
    tm = _tile(m, ROW_TILE)

    def epi(acc, er, orf):
        orf[0][...] = acc[0] if add is None else DN_ALPHA * er[0][...] + acc[0]

    row = pl.BlockSpec((tm, kx), lambda i, k: (i, 0))
    res = _gemm(name, (m // tm, nd),
                [(dy, pl.BlockSpec((tm, ns), lambda i, k: (i, k)), w, pl.BlockSpec((None, kx, ns), lambda i, k: (k, 0, 0)))],
                [0], [(tm, kx)], NT, [] if add is None else [(add, row)], [(_sds((m, kx), F32), row)], epi, carried)
    return res[0] if carried is None else (res[0][0], res[1])


def _tri(n, kind):
    r = lax.broadcasted_iota(jnp.int32, (n, n), 0)
    c = lax.broadcasted_iota(jnp.int32, (n, n), 1)
    return {"le": c <= r, "ge": c >= r, "gt": r > c, "lt": r < c}[kind]


def _dot_f32(a, b, dims=NN):
    return lax.dot_general(a, b, dims, preferred_element_type=F32, precision=lax.Precision.HIGHEST)


def _hgrn_gates(i, hq, hf, logits):
    lg = logits
    mx = jnp.maximum(lg[0:1], lg[1:2])
    e0 = jnp.exp(lg[0:1] - mx)
    lb = e0 / (e0 + jnp.exp(lg[1:2] - mx))
    sig = _sigmoid(hf)
    f = lb + (1.0 - lb) * sig
    valid = (i * BLOCK + lax.broadcasted_iota(jnp.int32, hf.shape, 0)) >= PAD
    g = jnp.where(valid, jnp.log(f), 0.0)
    k = jnp.where(valid, 1.0 - f, 0.0)
    sq = _sigmoid(hq)
    return hq * sq, k, g, sig, f, lb, valid, sq


PAIR_OFF = -1e30


def _pair_mask():
    s_i = lax.broadcasted_iota(jnp.int32, (CHUNK, CHUNK, 1), 0)
    t_i = lax.broadcasted_iota(jnp.int32, (CHUNK, CHUNK, 1), 1)
    return t_i >= s_i


def _heads_per_step(n_heads, want):
    return max(h for h in range(1, want + 1) if n_heads % h == 0)


def _hgrn_fwd(proj, logits, gn, n_heads, carried=None):
    m = proj.shape[0]
    nb = m // BLOCK
    w = n_heads * HEAD
    cpb = BLOCK // CHUNK
    hps = _heads_per_step(n_heads, 4)
    wide = hps * HEAD

    def body(hq_ref, hf_ref, hi_ref, hog_ref, lg_ref, gn_ref, o_ref, ohg_ref, st_all_ref, st_ref, q_s, k_s, v_s, b_s):
        i = pl.program_id(1)

        @pl.when(i == 0)
        def _():
            st_ref[...] = jnp.zeros_like(st_ref)

        q, k, g, _, _, _, _, _ = _hgrn_gates(i, hq_ref[...], hf_ref[...], lg_ref[...])
        q_s[...] = q
        k_s[...] = k
        v_s[...] = hi_ref[...]
        b_s[...] = _dot_f32(_tri(BLOCK, "le").astype(F32), g)
        causal = _pair_mask()

        def chunk(c, carry):
            sl = pl.ds(pl.multiple_of(c * CHUNK, CHUNK), CHUNK)
            prev = pl.ds(pl.multiple_of(jnp.maximum(c - 1, 0) * CHUNK, CHUNK), CHUNK)
            first = (c > 0).astype(F32)
            for hd in range(hps):
                cols = slice(hd * HEAD, (hd + 1) * HEAD)
                b = b_s[sl, cols] - b_s[prev, cols][CHUNK - 1:CHUNK, :] * first
                qc, kc, vc = q_s[sl, cols], k_s[sl, cols], v_s[sl, cols]
                st = st_ref[hd]
                st_all_ref[hd, c] = st.astype(BF16)
                o = lax.dot_general((qc * jnp.exp(b)).astype(BF16), st.astype(BF16), NT, preferred_element_type=F32)
                e = jnp.exp(jnp.minimum(b[None, :, :] - b[:, None, :], 0.0))
                p = jnp.sum(qc[None, :, :] * e * kc[:, None, :], axis=-1, keepdims=True)
                o_ref[sl, cols] = o + jnp.sum(jnp.where(causal, p, 0.0) * vc[:, None, :], axis=0)
                blast = b[CHUNK - 1:CHUNK, :]
                kd = kc * jnp.exp(blast - b)
                st_ref[hd] = st * jnp.exp(blast) + lax.dot_general(vc.astype(BF16), kd.astype(BF16), TN,
                                                                   preferred_element_type=F32)
            return carry

        lax.fori_loop(0, cpb, chunk, 0)
        for hd in range(hps):
            cols = slice(hd * HEAD, (hd + 1) * HEAD)
            o = o_ref[:, cols]
            n = o * lax.rsqrt(jnp.mean(o * o, axis=-1, keepdims=True) + RMS_EPS)
            hog = hog_ref[:, cols]
            ohg_ref[:, cols] = (n * gn_ref[:, cols] * hog * _sigmoid(hog)).astype(BF16)

    def col(group):
        return pl.BlockSpec((BLOCK, wide), lambda h, i: (i, group * (n_heads // hps) + h))

    vec = pl.BlockSpec((1, wide), lambda h, i: (0, h))
    tile = pl.BlockSpec((BLOCK, wide), lambda h, i: (i, h))
    return _pallas(
        "hgrn_fwd", body, (n_heads // hps, nb),
        [col(0), col(1), col(2), col(3), pl.BlockSpec((2, wide), lambda h, i: (0, h)), vec],
        [tile, tile, pl.BlockSpec((hps, cpb, HEAD, HEAD), lambda h, i: (h, i, 0, 0))],
        [_sds((m, w), F32), _sds((m, w), BF16), _sds((n_heads, m // CHUNK, HEAD, HEAD), BF16)],
        [pltpu.VMEM((hps, HEAD, HEAD), F32)] + [pltpu.VMEM((BLOCK, wide), F32)] * 4,
        (proj, proj, proj, proj, logits, gn), carried)


def _hgrn_bwd(proj, logits, gn, o_raw, do_hg, states, n_heads, carried=None):
    m = proj.shape[0]
    nb = m // BLOCK
    w = n_heads * HEAD
    cpb = BLOCK // CHUNK
    last_state = m // CHUNK - 1
    hps = _heads_per_step(n_heads, 2)
    wide = hps * HEAD

    def body(hq_ref, hf_ref, hi_ref, hog_ref, lg_ref, gn_ref, o_ref, do_ref, st_all_ref, st_next_ref,
             dhq_ref, dhf_ref, dhi_ref, dhog_ref, dgn_ref, dlb_ref,
             dst_ref, q_s, k_s, v_s, b_s, do_s, dq_s, dk_s, dv_s, ex_s):
        step = pl.program_id(1)
        i = nb - 1 - step

        @pl.when(step == 0)
        def _():
            dst_ref[...] = jnp.zeros_like(dst_ref)
            dgn_ref[...] = jnp.zeros_like(dgn_ref)
            dlb_ref[...] = jnp.zeros_like(dlb_ref)

        hq = hq_ref[...]
        q, k, g, sig, f, lb, valid, sq = _hgrn_gates(i, hq, hf_ref[...], lg_ref[...])
        q_s[...] = q
        k_s[...] = k
        v_s[...] = hi_ref[...]
        b_s[...] = _dot_f32(_tri(BLOCK, "le").astype(F32), g)

        hog = hog_ref[...]
        sg = _sigmoid(hog)
        sil = hog * sg
        gnv = gn_ref[...]
        dh = do_ref[...]
        dn = dh * gnv * sil
        for hd in range(hps):
            cols = slice(hd * HEAD, (hd + 1) * HEAD)
            o = o_ref[:, cols]
            rs = lax.rsqrt(jnp.mean(o * o, axis=-1, keepdims=True) + RMS_EPS)
            n = o * rs
            ex_s[:, cols] = n
            do_s[:, cols] = rs * (dn[:, cols] - n * jnp.mean(dn[:, cols] * n, axis=-1, keepdims=True))
        n = ex_s[...]
        dhog_ref[...] = (dh * n * gnv * sg * (1.0 + hog * (1.0 - sg))).astype(BF16)
        dgn_ref[...] += jnp.sum(dh * n * sil, axis=0, keepdims=True)
        causal = _pair_mask()

        def chunk(t, st_ends):
            c = cpb - 1 - t
            sl = pl.ds(pl.multiple_of(c * CHUNK, CHUNK), CHUNK)
            prev = pl.ds(pl.multiple_of(jnp.maximum(c - 1, 0) * CHUNK, CHUNK), CHUNK)
            first = (c > 0).astype(F32)
            starts = []
            for hd in range(hps):
                cols = slice(hd * HEAD, (hd + 1) * HEAD)
                b = b_s[sl, cols] - b_s[prev, cols][CHUNK - 1:CHUNK, :] * first
                qc, kc, vc, doc = q_s[sl, cols], k_s[sl, cols], v_s[sl, cols], do_s[sl, cols]
                eb = jnp.exp(b)
                blast = b[CHUNK - 1:CHUNK, :]
                ek = jnp.exp(blast - b)
                dst = dst_ref[hd]
                dstb = dst.astype(BF16)
                docb = doc.astype(BF16)
                st = st_all_ref[hd, c]
                starts.append(st)
                ex_s[sl, cols] = jnp.broadcast_to(jnp.sum(st_ends[hd].astype(F32) * dst, axis=0, keepdims=True),
                                                  (CHUNK, HEAD))
                dq = lax.dot_general(docb, st, NN, preferred_element_type=F32) * eb
                dk = lax.dot_general(vc.astype(BF16), dstb, NN, preferred_element_type=F32) * ek
                dv = lax.dot_general((kc * ek).astype(BF16), dstb, NT, preferred_element_type=F32)
                em = jnp.exp(jnp.where(causal, b[None, :, :] - b[:, None, :], PAIR_OFF))
                dp = jnp.sum(doc[None, :, :] * vc[:, None, :], axis=-1, keepdims=True)
                qe = qc[None, :, :] * em
                p = jnp.sum(qe * kc[:, None, :], axis=-1, keepdims=True)
                dq_s[sl, cols] = dq + jnp.sum(dp * em * kc[:, None, :], axis=0)
                dk_s[sl, cols] = dk + jnp.sum(dp * qe, axis=1)
                dv_s[sl, cols] = dv + jnp.sum(p * doc[None, :, :], axis=1)
                dst_ref[hd] = dst * jnp.exp(blast) + lax.dot_general(docb, (qc * eb).astype(BF16), TN,
                                                                     preferred_element_type=F32)
            return tuple(starts)

        lax.fori_loop(0, cpb, chunk, tuple(st_next_ref[hd, 0] for hd in range(hps)))
        dq, dk = dq_s[...], dk_s[...]
        r_i = lax.broadcasted_iota(jnp.int32, (BLOCK, BLOCK), 0)
        c_i = lax.broadcasted_iota(jnp.int32, (BLOCK, BLOCK), 1)
        within = ((c_i >= r_i) & (c_i // CHUNK == r_i // CHUNK)).astype(F32)
        rc = _dot_f32(within, q * dq - k * dk) + ex_s[...]
        df =jnp.where(valid, rc / f - dk, 0.0)
        dhf_ref[...] = (df * (1.0 - lb) * sig * (1.0 - sig)).astype(BF16)
        dlb_ref[...] += jnp.sum(df * (1.0 - sig), axis=0, keepdims=True)
        dhq_ref[...] = (dq * sq * (1.0 + hq * (1.0 - sq))).astype(BF16)
        dhi_ref[...] = dv_s[...].astype(BF16)

    def col(group):
        return pl.BlockSpec((BLOCK, wide), lambda h, s: (nb - 1 - s, group * (n_heads // hps) + h))

    vec = pl.BlockSpec((1, wide), lambda h, s: (0, h))
    tile = pl.BlockSpec((BLOCK, wide), lambda h, s: (nb - 1 - s, h))
    nxt = pl.BlockSpec((hps, 1, HEAD, HEAD), lambda h, s: (h, jnp.minimum((nb - s) * cpb, last_state), 0, 0))
    return _pallas(
        "hgrn_bwd", body, (n_heads // hps, nb),
        [col(0), col(1), col(2), col(3), pl.BlockSpec((2, wide), lambda h, s: (0, h)), vec, tile, tile,
         pl.BlockSpec((hps, cpb, HEAD, HEAD), lambda h, s: (h, nb - 1 - s, 0, 0)), nxt],
        [tile, tile, tile, tile, vec, vec],
        [_sds((m, w), BF16)] * 4 + [_sds((1, w), F32)] * 2,
        [pltpu.VMEM((hps, HEAD, HEAD), F32)] + [pltpu.VMEM((BLOCK, wide), F32)] * 9,
        (proj, proj, proj, proj, logits, gn, o_raw, do_hg, states, states), carried)


def _split_dot(x, t):
    hi = x.astype(BF16)
    lo = (x - hi.astype(F32)).astype(BF16)
    return jnp.dot(hi, t, preferred_element_type=F32) + jnp.dot(lo, t, preferred_element_type=F32)


def _window_scan(x, tri, after):
    blocks = [x[:, u * BLOCK:(u + 1) * BLOCK] for u in range(SB_UNROLL)]
    inner = _split_dot(jnp.concatenate(blocks, axis=0), tri)
    sums = [jnp.sum(b, axis=-1, keepdims=True) for b in blocks]
    out = []
    for u in range(SB_UNROLL):
        piece = inner[u * BLOCK:(u + 1) * BLOCK, :]
        for other in (sums[u + 1:] if after else sums[:u]):
            piece = piece + other
        out.append(piece)
    total = sums[0]
    for other in sums[1:]:
        total = total + other
    return jnp.concatenate(out, axis=1), total


def _sb_window(ref, j_left, cols):
    parts = [ref[pl.ds(pl.multiple_of(jnp.maximum(j_left + u, 0) * BLOCK, BLOCK), BLOCK), cols]
             for u in range(SB_UNROLL)]
    return jnp.concatenate(parts, axis=0)


def _sb_scores(q, kw, i, j_left, scale):
    z = lax.dot_general(q, kw, NT, preferred_element_type=F32) * scale
    lp = jnp.log(1.0 + jnp.exp(-jnp.abs(z)))
    lbeta = jnp.minimum(z, 0.0) - lp
    qpos = i * BLOCK + lax.broadcasted_iota(jnp.int32, z.shape, 0)
    kpos = j_left * BLOCK + lax.broadcasted_iota(jnp.int32, z.shape, 1)
    mask = (kpos < qpos) & (kpos >= PAD)
    l1m = jnp.where(mask, lbeta - z, 0.0)
    return lbeta, l1m, mask


SB_DEAD = -104.0
SB_UNROLL = 3


def _sb_fwd(proj, n_heads, group0, carried=None):
    m = proj.shape[0]
    nb = m // BLOCK
    w = n_heads * HEAD
    scale = 1.0 / math.sqrt(HEAD)
    hps = _heads_per_step(n_heads, 2)
    wide = hps * HEAD
    heads = [slice(hd * HEAD, (hd + 1) * HEAD) for hd in range(hps)]

    def body(q_ref, k_ref, v_ref, o_ref, start_ref, count_ref):
        h, i = pl.program_id(0), pl.program_id(1)
        tsuf = _tri(BLOCK, "gt").astype(BF16)

        def live(carry):
            t, _, runs = carry
            top = jnp.max(runs[0])
            for run in runs[1:]:
                top = jnp.maximum(top, jnp.max(run))
            return (t <= i) & (top > SB_DEAD)

        def step(carry):
            t, accs, runs = carry
            j_left = i - t - (SB_UNROLL - 1)
            new_accs, new_runs = [], []
            for hd, cols in enumerate(heads):
                start_ref[hd] = jnp.broadcast_to(runs[hd], (BLOCK, HEAD))
                lbeta, l1m, mask = _sb_scores(q_ref[:, cols], _sb_window(k_ref, j_left, cols), i, j_left, scale)
                later, total = _window_scan(l1m, tsuf, True)
                wgt = jnp.where(mask, jnp.exp(lbeta + later + runs[hd]), 0.0)
                new_accs.append(accs[hd] + jnp.dot(wgt.astype(BF16), _sb_window(v_ref, j_left, cols),
                                                   preferred_element_type=F32))
                new_runs.append(runs[hd] + total)
            return t + SB_UNROLL, tuple(new_accs), tuple(new_runs)

        t, accs, _ = lax.while_loop(live, step, (jnp.int32(0), tuple(jnp.zeros((BLOCK, HEAD), F32) for _ in heads),
                                                 tuple(jnp.zeros((BLOCK, 1), F32) for _ in heads)))
        for hd, cols in enumerate(heads):
            o_ref[:, cols] = accs[hd].astype(BF16)
        count_ref[h, i] = t.astype(F32)

    def whole(group):
        return pl.BlockSpec((m, wide), lambda h, i: (0, group * (n_heads // hps) + h), pipeline_mode=pl.Buffered(1))

    return _pallas(
        "sb_fwd", body, (n_heads // hps, nb),
        [pl.BlockSpec((BLOCK, wide), lambda h, i: (i, group0 * (n_heads // hps) + h)), whole(group0 + 1), whole(group0 + 2)],
        [pl.BlockSpec((BLOCK, wide), lambda h, i: (i, h)), pl.BlockSpec((hps, BLOCK, HEAD), lambda h, i: (h, i, 0)),
         pl.BlockSpec(memory_space=pltpu.SMEM)],
        [_sds((m, w), BF16), _sds((n_heads, m, HEAD), F32), _sds((n_heads // hps, nb), F32)],
        [], (proj, proj, proj), carried)


def _sb_bwd(proj, do, start, count, n_heads, group0):
    m = proj.shape[0]
    nb = m // BLOCK
    w = n_heads * HEAD
    scale = 1.0 / math.sqrt(HEAD)
    hps = _heads_per_step(n_heads, 2)
    wide = hps * HEAD
    heads = [slice(hd * HEAD, (hd + 1) * HEAD) for hd in range(hps)]

    def body(q_ref, k_ref, v_ref, do_ref, start_ref, count_ref, dq_ref, dk_ref, dv_ref, dk_s, dv_s):
        h, i = pl.program_id(0), pl.program_id(1)

        @pl.when(i == 0)
        def _():
            dk_s[...] = jnp.zeros_like(dk_s)
            dv_s[...] = jnp.zeros_like(dv_s)

        count = count_ref[h, i].astype(jnp.int32)
        first = i + 1 - count
        tsuf = _tri(BLOCK, "gt").astype(BF16)
        tpre = _tri(BLOCK, "lt").astype(BF16)

        def step(t, carry):
            dqs, rights, psums = carry
            j_left = first + t * SB_UNROLL
            out = []
            for hd, cols in enumerate(heads):
                q = q_ref[:, cols]
                dob = do_ref[:, cols].astype(BF16)
                kw = _sb_window(k_ref, j_left, cols)
                vw = _sb_window(v_ref, j_left, cols)
                lbeta, l1m, mask = _sb_scores(q, kw, i, j_left, scale)
                later, total = _window_scan(l1m, tsuf, True)
                right = jnp.where(t == 0, rights[hd], rights[hd] - total)
                a = jnp.where(mask, jnp.exp(lbeta + later + right), 0.0)
                p = a * lax.dot_general(dob, vw, NT, preferred_element_type=F32)
                earlier, p_total = _window_scan(p, tpre, False)
                below = psums[hd] + earlier
                beta = jnp.exp(lbeta)
                dz = (jnp.where(mask, p * (1.0 - beta) - below * beta, 0.0) * scale).astype(BF16)
                dq = dqs[hd] + jnp.dot(dz, kw, preferred_element_type=F32)
                dkw = lax.dot_general(dz, q, TN, preferred_element_type=F32)
                dvw = lax.dot_general(a.astype(BF16), dob, TN, preferred_element_type=F32)
                for u in range(SB_UNROLL):
                    rows = pl.ds(pl.multiple_of(jnp.maximum(j_left + u, 0) * BLOCK, BLOCK), BLOCK)
                    dk_s[rows, cols] += dkw[u * BLOCK:(u + 1) * BLOCK, :]
                    dv_s[rows, cols] += dvw[u * BLOCK:(u + 1) * BLOCK, :]
                out.append((dq, right, psums[hd] + p_total))
            return tuple(o[0] for o in out), tuple(o[1] for o in out), tuple(o[2] for o in out)

        dqs, _, _ = lax.fori_loop(0, count // SB_UNROLL, step,
                                  (tuple(jnp.zeros((BLOCK, HEAD), F32) for _ in heads),
                                   tuple(start_ref[hd, :, 0:1] for hd in range(hps)),
                                   tuple(jnp.zeros((BLOCK, 1), F32) for _ in heads)))
        for hd, cols in enumerate(heads):
            dq_ref[:, cols] = dqs[hd].astype(BF16)

        @pl.when(i == nb - 1)
        def _():
            dk_ref[...] = dk_s[...].astype(BF16)
            dv_ref[...] = dv_s[...].astype(BF16)

    def whole(group):
        return pl.BlockSpec((m, wide), lambda h, i: (0, group * (n_heads // hps) + h), pipeline_mode=pl.Buffered(1))

    tile = pl.BlockSpec((BLOCK, wide), lambda h, i: (i, h))
    col = pl.BlockSpec((m, wide), lambda h, i: (0, h))
    return pl.pallas_call(
        body, name="sb_bwd", grid=(n_heads // hps, nb),
        in_specs=[pl.BlockSpec((BLOCK, wide), lambda h, i: (i, group0 * (n_heads // hps) + h)), whole(group0 + 1),
                  whole(group0 + 2), tile, pl.BlockSpec((hps, BLOCK, HEAD), lambda h, i: (h, i, 0)),
                  pl.BlockSpec(memory_space=pltpu.SMEM)],
        out_specs=[tile, col, col],
        out_shape=[_sds((m, w), BF16)] * 3,
        scratch_shapes=[pltpu.VMEM((m, wide), F32)] * 2,
        compiler_params=_params(2),
    )(proj, proj, proj, do, start, count)


def _grad_w_rows(name, x, dy, nd_out):
    m, kx = x.shape
    n = dy.shape[1]
    ks = kx // nd_out
    tm = _tile(m, GRAD_ROW_TILE)

    def epi(acc, er, orf):
        orf[0][...] = acc[0].astype(BF16)

    return _gemm(name, (nd_out, m // tm),
                 [(x, pl.BlockSpec((tm, ks), lambda j, k: (k, j)), dy, pl.BlockSpec((tm, n), lambda j, k: (k, 0)))],
                 [0], [(ks, n)], TN, [], [(_sds((nd_out, ks, n), BF16), pl.BlockSpec((None, ks, n), lambda j, k: (j, 0, 0)))], epi)[0]


def _local_step(x, target, meta, vec, wts, shards=None):
    d = x.shape[1]
    width = vec["hg_norm_g"].shape[1]
    n_heads = width // HEAD
    gate_col = 7 * width
    h0 = jnp.concatenate([jnp.zeros((PAD, d), F32), meta, x], axis=0)
    h0b = h0.astype(BF16)
    wts = dict(wts)
    exchange = None if shards is None else _exchange_carried

    if shards is None:
        a1, b1, s1 = _ffn_up("ffn1_up", h0b, wts["ffn1_w_gate"], wts["ffn1_w_up"])
        r1, h1, h1b = _residual_ln("ffn1_down", s1, True, wts["ffn1_w_down"], h0, vec["ln1_g"], vec["ln1_b"], 0.5)
        proj, projb = _in_proj(h1b, wts["w_in"])
        o_raw, o_hg, states = _hgrn_fwd(proj, vec["hg_lb_logits"], vec["hg_norm_g"], n_heads)
        o_sb, sb_start, sb_count = _sb_fwd(projb, n_heads, 4)
    else:
        half = shards["w_in"].shape[0] // 2
        (a1, b1, s1), (wts["ffn1_w_down"], w_in_top) = _ffn_up(
            "ffn1_up", h0b, wts["ffn1_w_gate"], wts["ffn1_w_up"],
            _gather_carried([shards["ffn1_w_down"], shards["w_in"][:half]]))
        (r1, h1, h1b), (w_in_bottom,) = _residual_ln("ffn1_down", s1, True, wts["ffn1_w_down"], h0, vec["ln1_g"],
                                                     vec["ln1_b"], 0.5, _gather_carried([shards["w_in"][half:]]))
        wts["w_in"] = jnp.concatenate([w_in_top, w_in_bottom], axis=1)
        with_proj = ("w_proj_hg", "w_proj_sb", "w_out", "ffn2_w_gate")
        (proj, projb), got = _in_proj(h1b, wts["w_in"], _gather_carried([shards[k] for k in with_proj]))
        wts.update(zip(with_proj, got))
        (o_raw, o_hg, states), (wts["ffn2_w_up"],) = _hgrn_fwd(proj, vec["hg_lb_logits"], vec["hg_norm_g"], n_heads,
                                                               _gather_carried([shards["ffn2_w_up"]]))
        (o_sb, sb_start, sb_count), (wts["ffn2_w_down"],) = _sb_fwd(projb, n_heads, 4,
                                                                    _gather_carried([shards["ffn2_w_down"]]))
    nd = wts["w_in"].shape[0]
    w_out = wts["w_out"]
    p_hg2 = wts["w_proj_hg"].transpose(1, 0, 2).reshape(width, d)
    p_sb2 = wts["w_proj_sb"].transpose(1, 0, 2).reshape(width, d)
    u_hg, u_sb, y = _proj_merge(o_hg, o_sb, p_hg2, p_sb2, proj, vec["b_gate"], gate_col)
    r2, h2, h2b = _residual_ln("out_proj", y, False, w_out.reshape(d, d), h1, vec["ln2_g"], vec["ln2_b"], 1.0)
    a2, b2, s2 = _ffn_up("ffn2_up", h2b,wts["ffn2_w_gate"], wts["ffn2_w_up"])
    r3, _, _ = _residual_ln("ffn2_down", s2, True, wts["ffn2_w_down"], h2, vec["ln3_g"], vec["ln3_b"], 0.5)

    dr3, dr3b, dg3, db3, loss = _ln_bwd("ln3_bwd", r3, vec["ln3_g"], 0.5, beta=vec["ln3_b"], target=target, first_row=BLOCK)
    dh2, dwg2, dwu2, dwd2 = _ffn_bwd("ffn2", dr3b, dr3, h2b, a2, b2, s2, wts["ffn2_w_gate"], wts["ffn2_w_up"],
                                     wts["ffn2_w_down"], exchange)
    dr2, dr2b, dg2, db2 = _ln_bwd("ln2_bwd", r2, vec["ln2_g"], 1.0, dy=dh2)
    du_hg, du_sb, dz_hg, dz_sb, dbg = _merge_bwd(dr2b, w_out.reshape(d, d), proj, vec["b_gate"], u_hg, u_sb, gate_col)
    dw_out = _grad_w_rows("dw_out", y, dr2b, nd)
    dp_hg = _grad_w("dp_hg", o_hg, du_hg, nd)
    dp_sb = _grad_w("dp_sb", o_sb, du_sb, nd)
    do_hg = _grad_in_whole("do_hg", du_hg, p_hg2)
    do_sb = _grad_in_whole("do_sb", du_sb, p_sb2)
    hg = _hgrn_bwd(proj, vec["hg_lb_logits"], vec["hg_norm_g"], o_raw, do_hg, states, n_heads,
                   exchange([dw_out, dp_hg, dp_sb]) if exchange else None)
    if exchange:
        hg, (dw_out, dp_hg, dp_sb) = hg
    dhq, dhf, dhi, dhog, dgn, dlb = hg
    dsq, dsk, dsv = _sb_bwd(projb, do_sb, sb_start, sb_count, n_heads, 4)
    dproj = jnp.concatenate([dhq, dhf, dhi, dhog, dsq, dsk, dsv, dz_hg, dz_sb], axis=1)
    dw_in = _grad_w("dw_in", h1b, dproj, nd)
    dh1 = _grad_in("dh1", dproj, wts["w_in"], add=dr2, carried=exchange([dw_in]) if exchange else None)
    if exchange:
        dh1, (dw_in,) = dh1
    dr1, dr1b, dg1, db1 = _ln_bwd("ln1_bwd", r1, vec["ln1_g"], 0.5, dy=dh1)
    dh0, dwg1, dwu1, dwd1 = _ffn_bwd("ffn1", dr1b, dr1, h0b, a1, b1, s1, wts["ffn1_w_gate"], wts["ffn1_w_up"],
                                     wts["ffn1_w_down"], exchange)

    small = {"ln1_g": dg1, "ln1_b": db1, "ln2_g": dg2, "ln2_b": db2, "ln3_g": dg3, "ln3_b": db3,
             "b_gate": dbg, "hg_lb": dlb, "hg_norm_g": dgn}
    big = {"ffn1_w_gate": dwg1, "ffn1_w_up": dwu1, "ffn1_w_down": dwd1, "w_in": dw_in, "w_proj_hg": dp_hg,
           "w_proj_sb": dp_sb, "w_out": dw_out, "ffn2_w_gate": dwg2, "ffn2_w_up": dwu2, "ffn2_w_down": dwd2}
    return loss, dh0[BLOCK:], dh0[PAD:BLOCK], small, big


def _position():
    return lax.axis_index("x"), lax.axis_index("y"), lax.axis_index("c")


def _slot(px, py, pc):
    return 4 * px + 2 * py + pc


def _all_gather(shards):
    n = len(shards)

    def body(*refs):
        ins, outs = refs[:n], refs[n:2 * n]
        send_sems, recv_sems, local_sems = refs[2 * n:]
        x, y, c = _position()
        me, sibling = (x, y, c), (x, y, 1 - c)
        chips = [(1 - x, y), (x, 1 - y), (1 - x, 1 - y)]

        def copy(a, k, block, to, src=None):
            dst = outs[a].at[_slot(*block)]
            return pltpu.make_async_remote_copy(src_ref=dst if src is None else src, dst_ref=dst,
                                                send_sem=send_sems.at[a, k], recv_sem=recv_sems.at[a, k],
                                                device_id=to, device_id_type=MESH)

        mine = [pltpu.make_async_copy(ins[a], outs[a].at[_slot(*me)], local_sems.at[a]) for a in range(n)]
        for cp in mine:
            cp.start()
        first = []
        for a in range(n):
            first.append(copy(a, 0, me, sibling, src=ins[a]))
            first += [copy(a, 1 + j, me, (*chip, c), src=ins[a]) for j, chip in enumerate(chips)]
        for cp in first:
            cp.start()
        passed = []
        for j, chip in enumerate(chips):
            for a in range(n):
                copy(a, 1 + j, (*chip, c), me).wait_recv()
                cp = copy(a, 4 + j, (*chip, c), sibling)
                cp.start()
                passed.append(cp)
        for a in range(n):
            copy(a, 0, sibling, me).wait_recv()
        for j, chip in enumerate(chips):
            for a in range(n):
                copy(a, 4 + j, (*chip, 1 - c), me).wait_recv()
        for cp in first + passed:
            cp.wait_send()
        for cp in mine:
            cp.wait()

    return pl.pallas_call(
        body, name="all_gather", out_shape=[_sds((N_DEV,) + s.shape, s.dtype) for s in shards],
        in_specs=[ANY] * n, out_specs=[ANY] * n,
        scratch_shapes=[pltpu.SemaphoreType.DMA((n, 7)), pltpu.SemaphoreType.DMA((n, 7)), pltpu.SemaphoreType.DMA((n,))],
    )(*shards)


def _exchange_carried(grads):
    return _direct_copies(grads, [_sds(g.shape, g.dtype) for g in grads], lambda ref, slot: ref.at[slot])


def _gather_carried(shards):
    return _direct_copies(shards, [_sds((N_DEV,) + s.shape, s.dtype) for s in shards], lambda ref, slot: ref)


def _direct_copies(arrays, outs, block_for):
    n = len(arrays)

    def plan(ins, results, sems, arriving):
        send_sems, recv_sems, local_sems = sems
        x, y, c = _position()
        mine = _slot(x, y, c)
        peers = [(1 - x if k & 4 else x, 1 - y if k & 2 else y, 1 - c if k & 1 else c) for k in range(1, N_DEV)]
        own = [pltpu.make_async_copy(block_for(ins[a], mine), results[a].at[mine], local_sems.at[a]) for a in range(n)]
        remote = [pltpu.make_async_remote_copy(
            src_ref=block_for(ins[a], mine if arriving else _slot(*peer)),
            dst_ref=results[a].at[_slot(*peer) if arriving else mine],
            send_sem=send_sems.at[a, k], recv_sem=recv_sems.at[a, k], device_id=peer, device_id_type=MESH)
            for a in range(n) for k, peer in enumerate(peers)]
        return own, remote

    def start(ins, results, sems):
        own, sent = plan(ins, results, sems, False)
        for cp in own + sent:
            cp.start()

    def finish(ins, results, sems):
        _, landed = plan(ins, results, sems, True)
        for cp in landed:
            cp.wait_recv()
        own, sent = plan(ins, results, sems, False)
        for cp in sent:
            cp.wait_send()
        for cp in own:
            cp.wait()

    sems = [pltpu.SemaphoreType.DMA((n, 7)), pltpu.SemaphoreType.DMA((n, 7)), pltpu.SemaphoreType.DMA((n,))]
    return _Carried(list(arrays), outs, sems, start, finish)


def _all_reduce_rows(v):
    rows = v.shape[0]

    def body(v_ref, out_ref, buf, send_sems, recv_sems):
        x, y, c = _position()
        me, sibling = (x, y, c), (x, y, 1 - c)
        chips = [(1 - x, y), (x, 1 - y), (1 - x, 1 - y)]

        def copy(k, block, to, src=None):
            dst = buf.at[_slot(*block)]
            return pltpu.make_async_remote_copy(src_ref=dst if src is None else src, dst_ref=dst,
                                                send_sem=send_sems.at[k], recv_sem=recv_sems.at[k],
                                                device_id=to, device_id_type=MESH)

        first = [copy(0, me, sibling, src=v_ref)] + [copy(1 + j, me, (*chip, c), src=v_ref) for j, chip in enumerate(chips)]
        for cp in first:
            cp.start()
        buf[_slot(*me)] = v_ref[...]
        passed = [copy(4 + j, (*chip, c), sibling) for j, chip in enumerate(chips)]
        for j, chip in enumerate(chips):
            copy(1 + j, (*chip, c), me).wait_recv()
            passed[j].start()
        copy(0, sibling, me).wait_recv()
        for j, chip in enumerate(chips):
            copy(4 + j, (*chip, 1 - c), me).wait_recv()
        for cp in first + passed:
            cp.wait_send()
        total = buf[0]
        for s in range(1, N_DEV):
            total = total + buf[s]
        out_ref[...] = total

    vmem = pl.BlockSpec(memory_space=pltpu.VMEM)
    return pl.pallas_call(
        body, name="small_all_reduce", out_shape=_sds(v.shape, F32), in_specs=[vmem], out_specs=vmem,
        scratch_shapes=[pltpu.VMEM((N_DEV, rows, 128), F32), pltpu.SemaphoreType.DMA((7,)), pltpu.SemaphoreType.DMA((7,))],
    )(v)


def _adamw(name, w, m, v, contrib):
    r, c = w.shape
    n = contrib.shape[0]
    tr = _tile(r, 256)

    def body(w_ref, m_ref, v_ref, c_ref, g_out, d_out, m_out, v_out):
        g = c_ref[0].astype(F32)
        for s in range(1, n):
            g = g + c_ref[s].astype(F32)
        m2 = ADAM_B1 * m_ref[...] + (1.0 - ADAM_B1) * g
        v2 = ADAM_B2 * v_ref[...] + (1.0 - ADAM_B2) * (g * g)
        m_hat = m2 / (1.0 - ADAM_B1 ** ADAM_STEP)
        v_hat = v2 / (1.0 - ADAM_B2 ** ADAM_STEP)
        g_out[...] = g
        d_out[...] = -ADAM_LR * (m_hat / (jnp.sqrt(v_hat) + ADAM_EPS) + ADAM_WD * w_ref[...])
        m_out[...] = m2
        v_out[...] = v2

    tile = pl.BlockSpec((tr, c), lambda i: (i, 0))
    return pl.pallas_call(
        body, name=name, grid=(r // tr,), in_specs=[tile, tile, tile, pl.BlockSpec((n, tr, c), lambda i: (0, i, 0))],
        out_specs=[tile] * 4, out_shape=[_sds((r, c), F32)] * 4, compiler_params=_params(1),
    )(w, m, v, contrib)


def _lb_logits_grad(logits, dlb):
    def body(lg_ref, d_ref, out_ref):
        lg = lg_ref[...]
        mx = jnp.maximum(lg[0:1], lg[1:2])
        e0 = jnp.exp(lg[0:1] - mx)
        p0 = e0 / (e0 + jnp.exp(lg[1:2] - mx))
        g0 = d_ref[...] * p0 * (1.0 - p0)
        out_ref[0:1, :] = g0
        out_ref[1:2, :] = -g0

    return pl.pallas_call(body, name="lb_logits_grad", out_shape=_sds(logits.shape, F32))(logits, dlb)


BIG = ("ffn1_w_gate", "ffn1_w_up", "ffn1_w_down", "w_in", "w_proj_hg", "w_proj_sb", "w_out",
       "ffn2_w_gate", "ffn2_w_up", "ffn2_w_down")
VECTORS = ("ln1_g", "ln1_b", "b_gate", "hg_lb_logits", "hg_norm_g", "ln2_g", "ln2_b", "ln3_g", "ln3_b")
WEIGHTS = ("meta", "ln1_g", "ln1_b", "ffn1_w_gate", "ffn1_w_up", "ffn1_w_down", "w_in", "b_gate", "hg_lb_logits",
           "hg_norm_g", "w_proj_hg", "w_proj_sb", "w_out", "ln2_g", "ln2_b", "ffn2_w_gate", "ffn2_w_up",
           "ffn2_w_down", "ln3_g", "ln3_b")


def kernel(x, meta, ln1_g, ln1_b, ffn1_w_gate, ffn1_w_up, ffn1_w_down, w_in, b_gate, hg_lb_logits, hg_norm_g, w_proj_hg, w_proj_sb, w_out, ln2_g, ln2_b, ffn2_w_gate, ffn2_w_up, ffn2_w_down, ln3_g, ln3_b, loss_target, m_meta, m_ln1_g, m_ln1_b, m_ffn1_w_gate, m_ffn1_w_up, m_ffn1_w_down, m_w_in, m_b_gate, m_hg_lb_logits, m_hg_norm_g, m_w_proj_hg, m_w_proj_sb, m_w_out, m_ln2_g, m_ln2_b, m_ffn2_w_gate, m_ffn2_w_up, m_ffn2_w_down, m_ln3_g, m_ln3_b, v_meta, v_ln1_g, v_ln1_b, v_ffn1_w_gate, v_ffn1_w_up, v_ffn1_w_down, v_w_in, v_b_gate, v_hg_lb_logits, v_hg_norm_g, v_w_proj_hg, v_w_proj_sb, v_w_out, v_ln2_g, v_ln2_b, v_ffn2_w_gate, v_ffn2_w_up, v_ffn2_w_down, v_ln3_g, v_ln3_b):
    given = dict(locals())
    d = x.shape[-1]
    ds = meta.shape[1]

    shards = {k: given[k][0].astype(BF16) for k in BIG}
    first = ("ffn1_w_gate", "ffn1_w_up")
    gathered = _all_gather([meta] + [shards.pop(k) for k in first])
    meta_full = gathered[0].transpose(1, 0, 2).reshape(N_META, d)
    vec = {k: given[k] for k in VECTORS}
    loss, grad_x, dmeta, small, received = _local_step(x[0], loss_target[0], meta_full, vec, dict(zip(first, gathered[1:])),
                                                       shards)

    order =("ln1_g", "ln1_b", "ln2_g", "ln2_b", "ln3_g", "ln3_b", "b_gate", "hg_lb", "hg_norm_g")
    parts = [small[k].reshape(-1, 128) for k in order] + [dmeta.reshape(-1, 128), jnp.broadcast_to(loss, (8, 128))]
    total = _all_reduce_rows(jnp.concatenate(parts, axis=0))
    reduced, row = {}, 0
    for k, p in zip(order + ("meta", "loss"), parts):
        reduced[k] = total[row:row + p.shape[0]]
        row += p.shape[0]
    loss_out = reduced["loss"][0, 0]
    me = _slot(*_position())
    dmeta_mine = lax.dynamic_slice(reduced["meta"].reshape(N_META, d), (0, me * ds), (N_META, ds))
    dlogits = _lb_logits_grad(hg_lb_logits, reduced["hg_lb"].reshape(1, -1))

    grads, deltas, new_m, new_v = {}, {}, {}, {}
    for k in WEIGHTS:
        w = given[k]
        lead = w.shape[:-2]
        w2, m2, v2 = (a.reshape(a.shape[-2:]) for a in (w, given["m_" + k], given["v_" + k]))
        if k in BIG:
            contrib = received[k]
        elif k == "meta":
            contrib = dmeta_mine[None]
        elif k == "hg_lb_logits":
            contrib = dlogits[None]
        else:
            contrib = reduced[k].reshape((1,) + w2.shape)
        out = _adamw("adamw_" + k, w2, m2, v2, contrib)
        grads[k], deltas[k], new_m[k], new_v[k] = (o.reshape(lead + o.shape) for o in out)
    return (loss_out, grad_x[None], *[grads[k] for k in WEIGHTS], *[deltas[k] for k in WEIGHTS],
            *[new_m[k] for k in WEIGHTS], *[new_v[k] for k in WEIGHTS])
```

```python
import functools
import math

import jax
import jax.numpy as jnp
from jax import lax
from jax.experimental import pallas as pl
from jax.experimental.pallas import tpu as pltpu

F32 = jnp.float32
BF16 = jnp.bfloat16
MESH = pl.DeviceIdType.MESH

N_DEV = 8
N_META = 16
BLOCK = 128
PAD = BLOCK - N_META
HEAD = 128
CHUNK = 16
LN_EPS = 1e-5
RMS_EPS = 1e-6
DN_ALPHA = 2.0 ** 0.25
ADAM_LR, ADAM_B1, ADAM_B2, ADAM_EPS, ADAM_WD, ADAM_STEP = 0.001, 0.9, 0.999, 1e-08, 0.01, 10

VMEM_LIMIT_V7X = 60 * 1024 * 1024
ROW_TILE = 640
LN_ROW_TILE = 320
COL_TILE = 512
GRAD_ROW_TILE = 1664

NN = (((1,), (0,)), ((), ()))
NT = (((1,), (1,)), ((), ()))
TN = (((0,), (0,)), ((), ()))


def _tile(n, pref, mult=16):
    best = None
    for t in range(mult, min(n, pref) + 1, mult):
        if n % t == 0:
            best = t
    return n if best is None else best


def _params(n_axes):
    return pltpu.CompilerParams(dimension_semantics=("arbitrary",) * n_axes, vmem_limit_bytes=VMEM_LIMIT_V7X)


def _sigmoid(x):
    return 1.0 / (1.0 + jnp.exp(-x))


class _Carried:
    def __init__(self, ins, outs, sems, start, finish):
        self.ins, self.outs, self.sems, self.start, self.finish = ins, outs, sems, start, finish


ANY = pl.BlockSpec(memory_space=pl.ANY)


def _pallas(name, body, grid, in_specs, out_specs, out_shape, scratch, operands, carried=None):
    if carried is None:
        return pl.pallas_call(body, name=name, grid=grid, in_specs=in_specs, out_specs=out_specs, out_shape=out_shape,
                              scratch_shapes=scratch, compiler_params=_params(len(grid)))(*operands)
    n_in, n_out, n_scr = len(in_specs), len(out_specs), len(scratch)
    c_in, c_out = len(carried.ins), len(carried.outs)

    def wrapped(*refs):
        ins, rest = refs[:n_in], refs[n_in:]
        c_ins, rest = rest[:c_in], rest[c_in:]
        outs, rest = rest[:n_out], rest[n_out:]
        c_outs, rest = rest[:c_out], rest[c_out:]
        scr, c_sems = rest[:n_scr], rest[n_scr:]
        first = last = None
        for axis, size in enumerate(grid):
            at0, at_end = pl.program_id(axis) == 0, pl.program_id(axis) == size - 1
            first = at0 if first is None else first & at0
            last = at_end if last is None else last & at_end

        @pl.when(first)
        def _():
            carried.start(c_ins, c_outs, c_sems)

        body(*ins, *outs, *scr)

        @pl.when(last)
        def _():
            carried.finish(c_ins, c_outs, c_sems)

    res = pl.pallas_call(
        wrapped, name=name, grid=grid, in_specs=list(in_specs) + [ANY] * c_in, out_specs=list(out_specs) + [ANY] * c_out,
        out_shape=list(out_shape) + list(carried.outs), scratch_shapes=list(scratch) + list(carried.sems),
        compiler_params=_params(len(grid)),
    )(*operands, *carried.ins)
    return res[:n_out], res[n_out:]


def _gemm(name, grid, pairs, acc_of, acc_shapes, dims, extras, outs, epilogue, carried=None):
    n_extra, n_out = len(extras), len(outs)
    nk = grid[-1]
    k_axis = len(grid) - 1
    operands, in_specs, where = [], [], {}
    for a, a_spec, b, b_spec in pairs:
        for arr, spec in ((a, a_spec), (b, b_spec)):
            if (id(arr), id(spec)) not in where:
                where[(id(arr), id(spec))] = len(operands)
                operands.append(arr)
                in_specs.append(spec)
    n_mat = len(operands)
    slots = [(where[(id(a), id(a_spec))], where[(id(b), id(b_spec))]) for a, a_spec, b, b_spec in pairs]

    def body(*refs):
        er = refs[n_mat:n_mat + n_extra]
        orf = refs[n_mat + n_extra:n_mat + n_extra + n_out]
        accs = refs[n_mat + n_extra + n_out:]

        def part(p):
            a_ref, b_ref = refs[slots[p][0]], refs[slots[p][1]]
            if len(a_ref.shape) == 2:
                return lax.dot_general(a_ref[...].astype(BF16), b_ref[...].astype(BF16), dims, preferred_element_type=F32)
            total = None
            for s in range(a_ref.shape[0]):
                d = lax.dot_general(a_ref[s].astype(BF16), b_ref[s].astype(BF16), dims, preferred_element_type=F32)
                total = d if total is None else total + d
            return total

        if nk == 1:
            vals = [None] * len(acc_shapes)
            for p in range(len(pairs)):
                d = part(p)
                vals[acc_of[p]] = d if vals[acc_of[p]] is None else vals[acc_of[p]] + d
            epilogue(vals, er, orf)
        else:
            k = pl.program_id(k_axis)

            @pl.when(k == 0)
            def _():
                for acc in accs:
                    acc[...] = jnp.zeros_like(acc)

            for p in range(len(pairs)):
                accs[acc_of[p]][...] += part(p)

            @pl.when(k == nk - 1)
            def _():
                epilogue([acc[...] for acc in accs], er, orf)

    for e, e_spec in extras:
        operands.append(e)
        in_specs.append(e_spec)
    scratch = [] if nk == 1 else [pltpu.VMEM(s, F32) for s in acc_shapes]
    return _pallas(name, body, grid, in_specs, [s for _, s in outs], [o for o, _ in outs], scratch, operands, carried)


def _sds(shape, dtype):
    return jax.ShapeDtypeStruct(shape, dtype)


def _ln_rows(r, g, b):
    mu = jnp.mean(r, axis=-1, keepdims=True)
    xc = r - mu
    var = jnp.mean(xc * xc, axis=-1, keepdims=True)
    return xc * lax.rsqrt(var + LN_EPS) * g + b


def _ffn_up(name, hb, wg, wu, carried=None):
    m, d = hb.shape
    nd, _, fs = wg.shape
    tm = _tile(m, ROW_TILE)

    def epi(acc, er, orf):
        a, b = acc
        sg = _sigmoid(a)
        silu = a * sg
        orf[0][...] = b * sg * (1.0 + a * (1.0 - sg))
        orf[1][...] = silu
        orf[2][...] = (silu * b).astype(BF16)

    h_spec = pl.BlockSpec((tm, d), lambda i, j, k: (i, 0))
    w_spec = pl.BlockSpec((None, d, fs), lambda i, j, k: (j, 0, 0))
    o_spec = pl.BlockSpec((None, tm, fs), lambda i, j, k: (j, i, 0))
    return _gemm(name, (m // tm, nd, 1), [(hb, h_spec, wg, w_spec), (hb, h_spec, wu, w_spec)], [0, 1],
                 [(tm, fs)] * 2, NN, [],
                 [(_sds((nd, m, fs), F32), o_spec), (_sds((nd, m, fs), F32), o_spec), (_sds((nd, m, fs), BF16), o_spec)], epi,
                 carried)


def _residual_ln(name, a, a_stacked, w, h_in, g, beta, scale, carried=None):
    d = w.shape[-1]
    m = h_in.shape[0]
    tm = _tile(m, LN_ROW_TILE)

    def epi(acc, er, orf):
        r = DN_ALPHA * er[0][...] + scale * acc[0]
        h = _ln_rows(r, er[1][...], er[2][...])
        orf[0][...] = r
        orf[1][...] = h
        orf[2][...] = h.astype(BF16)

    once = pl.Buffered(1)
    if a_stacked:
        a_spec = pl.BlockSpec((a.shape[0], tm, a.shape[2]), lambda i, k: (0, i, 0))
        w_spec = pl.BlockSpec(w.shape, lambda i, k: (0, 0, 0), pipeline_mode=once)
    else:
        a_spec = pl.BlockSpec((tm, a.shape[1]), lambda i, k: (i, 0))
        w_spec = pl.BlockSpec(w.shape, lambda i, k: (0, 0), pipeline_mode=once)
    row = pl.BlockSpec((tm, d), lambda i, k: (i, 0))
    vec = pl.BlockSpec((1, d), lambda i, k: (0, 0))
    return _gemm(name, (m // tm, 1), [(a, a_spec, w, w_spec)], [0], [(tm, d)], NN,
                 [(h_in, row), (g, vec), (beta, vec)],
                 [(_sds((m, d), F32), row), (_sds((m, d), F32), row), (_sds((m, d), BF16), row)], epi, carried)


def _in_proj(hb, w_in, carried=None):
    m, d = hb.shape
    nd, _, cs = w_in.shape
    tm = _tile(m, ROW_TILE)

    def epi(acc, er, orf):
        orf[0][...] = acc[0]
        orf[1][...] = acc[0].astype(BF16)

    out = pl.BlockSpec((tm, cs), lambda i, j, k: (i, j))
    res = _gemm("in_proj", (m // tm, nd, 1),
                [(hb, pl.BlockSpec((tm, d), lambda i, j, k: (i, 0)), w_in, pl.BlockSpec((None, d, cs), lambda i, j, k: (j, 0, 0)))],
                [0], [(tm, cs)], NN, [], [(_sds((m, nd * cs), F32), out), (_sds((m, nd * cs), BF16), out)], epi, carried)
    return tuple(res) if carried is None else (tuple(res[0]), res[1])


def _proj_merge(o_hg, o_sb, p_hg, p_sb, proj, b_gate, gate_col):
    m, w = o_hg.shape
    d = p_hg.shape[1]
    tm = _tile(m, ROW_TILE)
    tn = _tile(d, COL_TILE, 128)
    nn = d // tn
    c0 = gate_col // tn

    def epi(acc, er, orf):
        u_hg, u_sb = acc
        g_hg = _sigmoid(er[0][...] + er[2][...])
        g_sb = _sigmoid(er[1][...] + er[3][...])
        orf[0][...] = u_hg
        orf[1][...] = u_sb
        orf[2][...] = (g_hg * u_hg + g_sb * u_sb).astype(BF16)

    o_spec = pl.BlockSpec((tm, w), lambda i, j, k: (i, 0))
    p_spec = pl.BlockSpec((w, tn), lambda i, j, k: (0, j))
    out = pl.BlockSpec((tm, tn), lambda i, j, k: (i, j))
    return _gemm("proj_merge", (m // tm, nn, 1), [(o_hg, o_spec, p_hg, p_spec), (o_sb, o_spec, p_sb, p_spec)], [0, 1],
                 [(tm, tn)] * 2, NN,
                 [(proj, pl.BlockSpec((tm, tn), lambda i, j, k: (i, c0 + j))),
                  (proj, pl.BlockSpec((tm, tn), lambda i, j, k: (i, c0 + nn + j))),
                  (b_gate, pl.BlockSpec((1, tn), lambda i, j, k: (0, j))),
                  (b_gate, pl.BlockSpec((1, tn), lambda i, j, k: (0, nn + j)))],
                 [(_sds((m, d), F32), out), (_sds((m, d), F32), out), (_sds((m, d), BF16), out)], epi)


def _ln_bwd(name, r, g, out_scale, dy=None, beta=None, target=None, first_row=0):
    m, d = r.shape
    tm = _tile(m, LN_ROW_TILE if target is None else BLOCK)
    with_loss = target is not None
    skip = first_row // tm if with_loss else 0
    assert not with_loss or first_row % tm == 0

    def body(*refs):
        if with_loss:
            r_ref, g_ref, b_ref, t_ref, dr_ref, drb_ref, dg_ref, db_ref, loss_ref = refs
        else:
            r_ref, g_ref, dy_ref, dr_ref, drb_ref, dg_ref, db_ref = refs
        i = pl.program_id(0)
        x = r_ref[...]
        mu = jnp.mean(x, axis=-1, keepdims=True)
        xc = x - mu
        var = jnp.mean(xc * xc, axis=-1, keepdims=True)
        rstd = lax.rsqrt(var + LN_EPS)
        xhat = xc * rstd
        gv = g_ref[...]
        if with_loss:
            err = xhat * gv + b_ref[...] - t_ref[...]
            live = (i >= skip).astype(F32)
            dyv = err * (live / d)
            part = 0.5 * live * jnp.sum(jnp.sum(err * err, axis=-1, keepdims=True), axis=0, keepdims=True) / d
        else:
            dyv = dy_ref[...]
        dxh = dyv * gv
        m1 = jnp.mean(dxh, axis=-1, keepdims=True)
        m2 = jnp.mean(dxh * xhat, axis=-1, keepdims=True)
        dr = rstd * (dxh - m1 - xhat * m2)
        dr_ref[...] = dr
        drb_ref[...] = (out_scale * dr).astype(BF16)

        @pl.when(i == 0)
        def _():
            dg_ref[...] = jnp.zeros_like(dg_ref)
            db_ref[...] = jnp.zeros_like(db_ref)
            if with_loss:
                loss_ref[...] = jnp.zeros_like(loss_ref)

        dg_ref[...] += jnp.sum(dyv * xhat, axis=0, keepdims=True)
        db_ref[...] += jnp.sum(dyv, axis=0, keepdims=True)
        if with_loss:
            loss_ref[...] += jnp.broadcast_to(part, loss_ref.shape)

    row = pl.BlockSpec((tm, d), lambda i: (i, 0))
    vec = pl.BlockSpec((1, d), lambda i: (0, 0))
    out_shape = [_sds((m, d), F32), _sds((m, d), BF16), _sds((1, d), F32), _sds((1, d), F32)]
    out_specs = [row, row, vec, vec]
    if with_loss:
        operands = [r, g, beta, target]
        in_specs = [row, vec, vec, pl.BlockSpec((tm, d), lambda i: (jnp.maximum(i - skip, 0), 0))]
        out_shape.append(_sds((1, BLOCK), F32))
        out_specs.append(pl.BlockSpec((1, BLOCK), lambda i: (0, 0)))
    else:
        operands = [r, g, dy]
        in_specs = [row, vec, row]
    return pl.pallas_call(body, name=name, grid=(m // tm,), in_specs=in_specs, out_specs=out_specs, out_shape=out_shape,
                          compiler_params=_params(1))(*operands)


def _ffn_bwd(tag, drb, dr, hb, a, b, s, wg, wu, wd, exchange=None):
    m, d = drb.shape
    nd, _, fs = wg.shape
    tm = _tile(m, ROW_TILE)

    pair = 2 if nd % 2 == 0 else 1

    def ds_body(drb_ref, wd_ref, a_ref, b_ref, da_ref, db_ref):
        x = drb_ref[...]
        for blk in range(pair):
            ds = lax.dot_general(x, wd_ref[blk], NT, preferred_element_type=F32)
            da_ref[blk] = (ds * a_ref[blk]).astype(BF16)
            db_ref[blk] = (ds * b_ref[blk]).astype(BF16)

    st = pl.BlockSpec((pair, tm, fs), lambda i, j: (j, i, 0))
    da, db = _pallas(tag + "_ds", ds_body, (m // tm, nd // pair),
                     [pl.BlockSpec((tm, d), lambda i, j: (i, 0)), pl.BlockSpec((pair, fs, d), lambda i, j: (j, 0, 0)), st, st],
                     [st, st], [_sds((nd, m, fs), BF16)] * 2, [], (drb, wd, a, b))

    def epi_w(acc, er, orf):
        orf[0][...] = acc[0].astype(BF16)

    tr = _tile(m, GRAD_ROW_TILE)
    nkm = m // tr
    dwd = _gemm(tag + "_dwd", (nd, nkm),
                [(s, pl.BlockSpec((None, tr, fs), lambda j, k: (j, k, 0)), drb, pl.BlockSpec((tr, d), lambda j, k: (k, 0)))],
                [0], [(fs, d)], TN, [], [(_sds((nd, fs, d), BF16), pl.BlockSpec((None, fs, d), lambda j, k: (j, 0, 0)))], epi_w)[0]
    h_spec = pl.BlockSpec((tr, d), lambda j, k: (k, 0))
    g_spec = pl.BlockSpec((None, tr, fs), lambda j, k: (j, k, 0))
    w_out = pl.BlockSpec((None, d, fs), lambda j, k: (j, 0, 0))
    dwg = _gemm(tag + "_dwg", (nd, nkm), [(hb, h_spec, da, g_spec)], [0], [(d, fs)], TN, [],
                [(_sds((nd, d, fs), BF16), w_out)], epi_w, exchange([dwd]) if exchange else None)
    if exchange:
        (dwg,), (dwd,) = dwg
    else:
        dwg = dwg[0]
    dwu = _gemm(tag + "_dwu", (nd, nkm), [(hb, h_spec, db, g_spec)], [0], [(d, fs)], TN, [],
                [(_sds((nd, d, fs), BF16), w_out)], epi_w, exchange([dwg]) if exchange else None)
    if exchange:
        (dwu,), (dwg,) = dwu
    else:
        dwu = dwu[0]

    def epi_dh(acc, er, orf):
        orf[0][...] = DN_ALPHA * er[0][...] + acc[0]

    gk = pl.BlockSpec((None, tm, fs), lambda i, k: (k, i, 0))
    wk = pl.BlockSpec((None, d, fs), lambda i, k: (k, 0, 0))
    row = pl.BlockSpec((tm, d), lambda i, k: (i, 0))
    dh = _gemm(tag + "_dh", (m // tm, nd), [(da, gk, wg, wk), (db, gk, wu, wk)], [0, 0], [(tm, d)], NT,
               [(dr, row)], [(_sds((m, d), F32), row)], epi_dh, exchange([dwu]) if exchange else None)
    if exchange:
        (dh,), (dwu,) = dh
    else:
        dh = dh[0]
    return dh, dwg, dwu, dwd


def _merge_bwd(dmixb, w_out2, proj, b_gate, u_hg, u_sb, gate_col):
    m, d = dmixb.shape
    ds = _tile(d, COL_TILE, 128)
    nd = d // ds
    tm = _tile(m, ROW_TILE)
    c0 = gate_col // ds

    def epi(acc, er, orf):
        i = pl.program_id(1)
        dy = acc[0]
        g_hg = _sigmoid(er[0][...] + er[2][...])
        g_sb = _sigmoid(er[1][...] + er[3][...])
        orf[0][...] = (dy * g_hg).astype(BF16)
        orf[1][...] = (dy * g_sb).astype(BF16)
        dz_hg = dy * er[4][...] * g_hg * (1.0 - g_hg)
        dz_sb = dy * er[5][...] * g_sb * (1.0 - g_sb)
        orf[2][...] = dz_hg.astype(BF16)
        orf[3][...] = dz_sb.astype(BF16)

        @pl.when(i == 0)
        def _():
            orf[4][...] = jnp.zeros_like(orf[4])
            orf[5][...] = jnp.zeros_like(orf[5])

        orf[4][...] += jnp.sum(dz_hg, axis=0, keepdims=True)
        orf[5][...] += jnp.sum(dz_sb, axis=0, keepdims=True)

    tile = pl.BlockSpec((tm, ds), lambda j, i, k: (i, j))
    vec = pl.BlockSpec((1, ds), lambda j, i, k: (0, j))
    du_hg, du_sb, dz_hg, dz_sb, db_hg, db_sb = _gemm(
        "merge_bwd", (nd, m // tm, 1),
        [(dmixb, pl.BlockSpec((tm, d), lambda j, i, k: (i, 0)), w_out2, pl.BlockSpec((ds, d), lambda j, i, k: (j, 0)))],
        [0], [(tm, ds)], NT,
        [(proj, pl.BlockSpec((tm, ds), lambda j, i, k: (i, c0 + j))),
         (proj, pl.BlockSpec((tm, ds), lambda j, i, k: (i, c0 + nd + j))),
         (b_gate, vec), (b_gate, pl.BlockSpec((1, ds), lambda j, i, k: (0, nd + j))),
         (u_hg, tile), (u_sb, tile)],
        [(_sds((m, d), BF16), tile), (_sds((m, d), BF16), tile), (_sds((m, d), BF16), tile), (_sds((m, d), BF16), tile),
         (_sds((1, d), F32), vec), (_sds((1, d), F32), vec)], epi)
    return du_hg, du_sb, dz_hg, dz_sb, jnp.concatenate([db_hg, db_sb], axis=1)


def _grad_w(name, x, dy, nd_out):
    m, kx = x.shape
    n = dy.shape[1]
    ns = n // nd_out
    tm = _tile(m, GRAD_ROW_TILE)

    def epi(acc, er, orf):
        orf[0][...] = acc[0].astype(BF16)

    return _gemm(name, (nd_out, m // tm),
                 [(x, pl.BlockSpec((tm, kx), lambda j, k: (k, 0)), dy, pl.BlockSpec((tm, ns), lambda j, k: (k, j)))],
                 [0], [(kx, ns)], TN, [], [(_sds((nd_out, kx, ns), BF16), pl.BlockSpec((None, kx, ns), lambda j, k: (j, 0, 0)))], epi)[0]


def _grad_in_whole(name, dy, w2):
    m, n = dy.shape
    kx = w2.shape[0]
    tm = _tile(m, ROW_TILE)

    def epi(acc, er, orf):
        orf[0][...] = acc[0]

    return _gemm(name, (m // tm, 1),
                 [(dy, pl.BlockSpec((tm, n), lambda i, k: (i, 0)),
                   w2, pl.BlockSpec((kx, n), lambda i, k: (0, 0), pipeline_mode=pl.Buffered(1)))],
                 [0], [(tm, kx)], NT, [], [(_sds((m, kx), F32), pl.BlockSpec((tm, kx), lambda i, k: (i, 0)))], epi)[0]


def _grad_in(name, dy, w, add=None, carried=None):
    m = dy.shape[0]
    nd, kx, ns = w.shape
    tm = _tile(m, ROW_TILE)

    def epi(acc, er, orf):
        orf[0][...] = acc[0] if add is None else DN_ALPHA * er[0][...] + acc[0]

    row = pl.BlockSpec((tm, kx), lambda i, k: (i, 0))
    res = _gemm(name, (m // tm, nd),
                [(dy, pl.BlockSpec((tm, ns), lambda i, k: (i, k)), w, pl.BlockSpec((None, kx, ns), lambda i, k: (k, 0, 0)))],
                [0], [(tm, kx)], NT, [] if add is None else [(add, row)], [(_sds((m, kx), F32), row)], epi, carried)
    return res[0] if carried is None else (res[0][0], res[1])


def _tri(n, kind):
    r = lax.broadcasted_iota(jnp.int32, (n, n), 0)
    c = lax.broadcasted_iota(jnp.int32, (n, n), 1)
    return {"le": c <= r, "ge": c >= r, "gt": r > c, "lt": r < c}[kind]


def _dot_f32(a, b, dims=NN):
    return lax.dot_general(a, b, dims, preferred_element_type=F32, precision=lax.Precision.HIGHEST)


def _hgrn_gates(i, hq, hf, logits):
    lg = logits
    mx = jnp.maximum(lg[0:1], lg[1:2])
    e0 = jnp.exp(lg[0:1] - mx)
    lb = e0 / (e0 + jnp.exp(lg[1:2] - mx))
    sig = _sigmoid(hf)
    f = lb + (1.0 - lb) * sig
    valid = (i * BLOCK + lax.broadcasted_iota(jnp.int32, hf.shape, 0)) >= PAD
    g = jnp.where(valid, jnp.log(f), 0.0)
    k = jnp.where(valid, 1.0 - f, 0.0)
    sq = _sigmoid(hq)
    return hq * sq, k, g, sig, f, lb, valid, sq


PAIR_OFF = -1e30


HALF = CHUNK // 2


def _pair_mask(n_t):
    s_i = lax.broadcasted_iota(jnp.int32, (HALF, n_t, 1), 0)
    t_i = lax.broadcasted_iota(jnp.int32, (HALF, n_t, 1), 1)
    return t_i >= s_i


def _pair_groups(b):
    out = []
    for s_sl, t_sl in ((slice(0, HALF), slice(0, CHUNK)), (slice(HALF, CHUNK), slice(HALF, CHUNK))):
        bt, bs = b[t_sl], b[s_sl]
        e = jnp.exp(jnp.where(_pair_mask(bt.shape[0]), bt[None, :, :] - bs[:, None, :], PAIR_OFF))
        out.append((s_sl, t_sl, e))
    return out


def _join_t(first, second):
    return jnp.concatenate([first[:HALF], first[HALF:] + second], axis=0)


def _heads_per_step(n_heads, want):
    return max(h for h in range(1, want + 1) if n_heads % h == 0)


def _hgrn_fwd(proj, logits, gn, n_heads, carried=None):
    m = proj.shape[0]
    nb = m // BLOCK
    w = n_heads * HEAD
    cpb = BLOCK // CHUNK
    hps = _heads_per_step(n_heads, 8)
    wide = hps * HEAD

    def body(hq_ref, hf_ref, hi_ref, hog_ref, lg_ref, gn_ref, o_ref, ohg_ref, st_all_ref, st_ref, q_s, k_s, v_s, b_s):
        i = pl.program_id(1)

        @pl.when(i == 0)
        def _():
            st_ref[...] = jnp.zeros_like(st_ref)

        q, k, g, _, _, _, _, _ = _hgrn_gates(i, hq_ref[...], hf_ref[...], lg_ref[...])
        q_s[...] = q
        k_s[...] = k
        v_s[...] = hi_ref[...]
        b_s[...] = _dot_f32(_tri(BLOCK, "le").astype(F32), g)

        def chunk(c, carry):
            sl = pl.ds(pl.multiple_of(c * CHUNK, CHUNK), CHUNK)
            prev = pl.ds(pl.multiple_of(jnp.maximum(c - 1, 0) * CHUNK, CHUNK), CHUNK)
            first = (c > 0).astype(F32)
            for hd in range(hps):
                cols = slice(hd * HEAD, (hd + 1) * HEAD)
                b = b_s[sl, cols] - b_s[prev, cols][CHUNK - 1:CHUNK, :] * first
                qc, kc, vc = q_s[sl, cols], k_s[sl, cols], v_s[sl, cols]
                st = st_ref[hd]
                st_all_ref[hd, c] = st.astype(BF16)
                o = lax.dot_general((qc * jnp.exp(b)).astype(BF16), st.astype(BF16), NT, preferred_element_type=F32)
                within = []
                for s_sl, t_sl, e in _pair_groups(b):
                    p = jnp.sum(qc[t_sl][None, :, :] * e * kc[s_sl][:, None, :], axis=-1, keepdims=True)
                    within.append(jnp.sum(p * vc[s_sl][:, None, :], axis=0))
                o_ref[sl, cols] = o + _join_t(*within)
                blast = b[CHUNK - 1:CHUNK, :]
                kd = kc * jnp.exp(blast - b)
                st_ref[hd] = st * jnp.exp(blast) + lax.dot_general(vc.astype(BF16), kd.astype(BF16), TN,
                                                                   preferred_element_type=F32)
            return carry

        lax.fori_loop(0, cpb, chunk, 0)
        for hd in range(hps):
            cols = slice(hd * HEAD, (hd + 1) * HEAD)
            o = o_ref[:, cols]
            n = o * lax.rsqrt(jnp.mean(o * o, axis=-1, keepdims=True) + RMS_EPS)
            hog = hog_ref[:, cols]
            ohg_ref[:, cols] = (n * gn_ref[:, cols] * hog * _sigmoid(hog)).astype(BF16)

    def col(group):
        return pl.BlockSpec((BLOCK, wide), lambda h, i: (i, group * (n_heads // hps) + h))

    vec = pl.BlockSpec((1, wide), lambda h, i: (0, h))
    tile = pl.BlockSpec((BLOCK, wide), lambda h, i: (i, h))
    return _pallas(
        "hgrn_fwd", body, (n_heads // hps, nb),
        [col(0), col(1), col(2), col(3), pl.BlockSpec((2, wide), lambda h, i: (0, h)), vec],
        [tile, tile, pl.BlockSpec((hps, cpb, HEAD, HEAD), lambda h, i: (h, i, 0, 0))],
        [_sds((m, w), F32), _sds((m, w), BF16), _sds((n_heads, m // CHUNK, HEAD, HEAD), BF16)],
        [pltpu.VMEM((hps, HEAD, HEAD), F32)] + [pltpu.VMEM((BLOCK, wide), F32)] * 4,
        (proj, proj, proj, proj, logits, gn), carried)


def _hgrn_bwd(proj, logits, gn, o_raw, do_hg, states, n_heads, carried=None):
    m = proj.shape[0]
    nb = m // BLOCK
    w = n_heads * HEAD
    cpb = BLOCK // CHUNK
    last_state = m // CHUNK - 1
    hps = _heads_per_step(n_heads, 4)
    wide = hps * HEAD

    def body(hq_ref, hf_ref, hi_ref, hog_ref, lg_ref, gn_ref, o_ref, do_ref, st_all_ref, st_next_ref,
             dhq_ref, dhf_ref, dhi_ref, dhog_ref, dgn_ref, dlb_ref,
             dst_ref, q_s, k_s, v_s, b_s, do_s, dq_s, dk_s, dv_s, ex_s):
        step = pl.program_id(1)
        i = nb - 1 - step

        @pl.when(step == 0)
        def _():
            dst_ref[...] = jnp.zeros_like(dst_ref)
            dgn_ref[...] = jnp.zeros_like(dgn_ref)
            dlb_ref[...] = jnp.zeros_like(dlb_ref)

        hq = hq_ref[...]
        q, k, g, sig, f, lb, valid, sq = _hgrn_gates(i, hq, hf_ref[...], lg_ref[...])
        q_s[...] = q
        k_s[...] = k
        v_s[...] = hi_ref[...]
        b_s[...] = _dot_f32(_tri(BLOCK, "le").astype(F32), g)

        hog = hog_ref[...]
        sg = _sigmoid(hog)
        sil = hog * sg
        gnv = gn_ref[...]
        dh = do_ref[...]
        dn = dh * gnv * sil
        for hd in range(hps):
            cols = slice(hd * HEAD, (hd + 1) * HEAD)
            o = o_ref[:, cols]
            rs = lax.rsqrt(jnp.mean(o * o, axis=-1, keepdims=True) + RMS_EPS)
            n = o * rs
            ex_s[:, cols] = n
            do_s[:, cols] = rs * (dn[:, cols] - n * jnp.mean(dn[:, cols] * n, axis=-1, keepdims=True))
        n = ex_s[...]
        dhog_ref[...] = (dh * n * gnv * sg * (1.0 + hog * (1.0 - sg))).astype(BF16)
        dgn_ref[...] += jnp.sum(dh * n * sil, axis=0, keepdims=True)

        def chunk(t, st_ends):
            c = cpb - 1 - t
            sl = pl.ds(pl.multiple_of(c * CHUNK, CHUNK), CHUNK)
            prev = pl.ds(pl.multiple_of(jnp.maximum(c - 1, 0) * CHUNK, CHUNK), CHUNK)
            first = (c > 0).astype(F32)
            starts = []
            for hd in range(hps):
                cols = slice(hd * HEAD, (hd + 1) * HEAD)
                b = b_s[sl, cols] - b_s[prev, cols][CHUNK - 1:CHUNK, :] * first
                qc, kc, vc, doc = q_s[sl, cols], k_s[sl, cols], v_s[sl, cols], do_s[sl, cols]
                eb = jnp.exp(b)
                blast = b[CHUNK - 1:CHUNK, :]
                ek = jnp.exp(blast - b)
                dst = dst_ref[hd]
                dstb = dst.astype(BF16)
                docb = doc.astype(BF16)
                st = st_all_ref[hd, c]
                starts.append(st)
                ex_s[sl, cols] = jnp.broadcast_to(jnp.sum(st_ends[hd].astype(F32) * dst, axis=0, keepdims=True),
                                                  (CHUNK, HEAD))
                dq = lax.dot_general(docb, st, NN, preferred_element_type=F32) * eb
                dk = lax.dot_general(vc.astype(BF16), dstb, NN, preferred_element_type=F32) * ek
                dv = lax.dot_general((kc * ek).astype(BF16), dstb, NT, preferred_element_type=F32)
                dq_in, dk_in, dv_in = [], [], []
                for s_sl, t_sl, em in _pair_groups(b):
                    ks, vs = kc[s_sl][:, None, :], vc[s_sl][:, None, :]
                    qt, dot = qc[t_sl][None, :, :], doc[t_sl][None, :, :]
                    dp = jnp.sum(dot * vs, axis=-1, keepdims=True)
                    qe = qt * em
                    p = jnp.sum(qe * ks, axis=-1, keepdims=True)
                    dq_in.append(jnp.sum(dp * em * ks, axis=0))
                    dk_in.append(jnp.sum(dp * qe, axis=1))
                    dv_in.append(jnp.sum(p * dot, axis=1))
                dq_s[sl, cols] = dq + _join_t(*dq_in)
                dk_s[sl, cols] = dk + jnp.concatenate(dk_in, axis=0)
                dv_s[sl, cols] = dv + jnp.concatenate(dv_in, axis=0)
                dst_ref[hd] = dst * jnp.exp(blast) + lax.dot_general(docb, (qc * eb).astype(BF16), TN,
                                                                     preferred_element_type=F32)
            return tuple(starts)

        lax.fori_loop(0, cpb, chunk, tuple(st_next_ref[hd, 0] for hd in range(hps)))
        dq, dk = dq_s[...], dk_s[...]
        r_i = lax.broadcasted_iota(jnp.int32, (BLOCK, BLOCK), 0)
        c_i = lax.broadcasted_iota(jnp.int32, (BLOCK, BLOCK), 1)
        within = ((c_i >= r_i) & (c_i // CHUNK == r_i // CHUNK)).astype(F32)
        rc = _dot_f32(within, q * dq - k * dk) + ex_s[...]
        df =jnp.where(valid, rc / f - dk, 0.0)
        dhf_ref[...] = (df * (1.0 - lb) * sig * (1.0 - sig)).astype(BF16)
        dlb_ref[...] += jnp.sum(df * (1.0 - sig), axis=0, keepdims=True)
        dhq_ref[...] = (dq * sq * (1.0 + hq * (1.0 - sq))).astype(BF16)
        dhi_ref[...] = dv_s[...].astype(BF16)

    def col(group):
        return pl.BlockSpec((BLOCK, wide), lambda h, s: (nb - 1 - s, group * (n_heads // hps) + h))

    vec = pl.BlockSpec((1, wide), lambda h, s: (0, h))
    tile = pl.BlockSpec((BLOCK, wide), lambda h, s: (nb - 1 - s, h))
    nxt = pl.BlockSpec((hps, 1, HEAD, HEAD), lambda h, s: (h, jnp.minimum((nb - s) * cpb, last_state), 0, 0))
    return _pallas(
        "hgrn_bwd", body, (n_heads // hps, nb),
        [col(0), col(1), col(2), col(3), pl.BlockSpec((2, wide), lambda h, s: (0, h)), vec, tile, tile,
         pl.BlockSpec((hps, cpb, HEAD, HEAD), lambda h, s: (h, nb - 1 - s, 0, 0)), nxt],
        [tile, tile, tile, tile, vec, vec],
        [_sds((m, w), BF16)] * 4 + [_sds((1, w), F32)] * 2,
        [pltpu.VMEM((hps, HEAD, HEAD), F32)] + [pltpu.VMEM((BLOCK, wide), F32)] * 9,
        (proj, proj, proj, proj, logits, gn, o_raw, do_hg, states, states), carried)


def _split_dot(x, t):
    hi = x.astype(BF16)
    lo = (x - hi.astype(F32)).astype(BF16)
    return jnp.dot(hi, t, preferred_element_type=F32) + jnp.dot(lo, t, preferred_element_type=F32)


def _window_scan(x, tri, after):
    blocks = [x[:, u * BLOCK:(u + 1) * BLOCK] for u in range(SB_UNROLL)]
    inner = _split_dot(jnp.concatenate(blocks, axis=0), tri)
    sums = [jnp.sum(b, axis=-1, keepdims=True) for b in blocks]
    out = []
    for u in range(SB_UNROLL):
        piece = inner[u * BLOCK:(u + 1) * BLOCK, :]
        for other in (sums[u + 1:] if after else sums[:u]):
            piece = piece + other
        out.append(piece)
    total = sums[0]
    for other in sums[1:]:
        total = total + other
    return jnp.concatenate(out, axis=1), total


def _sb_window(ref, j_left, cols):
    parts = [ref[pl.ds(pl.multiple_of(jnp.maximum(j_left + u, 0) * BLOCK, BLOCK), BLOCK), cols]
             for u in range(SB_UNROLL)]
    return jnp.concatenate(parts, axis=0)


def _sb_scores(q, kw, i, j_left, scale):
    z = lax.dot_general(q, kw, NT, preferred_element_type=F32) * scale
    lp = jnp.log(1.0 + jnp.exp(-jnp.abs(z)))
    lbeta = jnp.minimum(z, 0.0) - lp
    qpos = i * BLOCK + lax.broadcasted_iota(jnp.int32, z.shape, 0)
    kpos = j_left * BLOCK + lax.broadcasted_iota(jnp.int32, z.shape, 1)
    mask = (kpos < qpos) & (kpos >= PAD)
    l1m = jnp.where(mask, lbeta - z, 0.0)
    return lbeta, l1m, mask


SB_DEAD = -104.0
SB_UNROLL = 3


def _sb_fwd(proj, n_heads, group0, carried=None):
    m = proj.shape[0]
    nb = m // BLOCK
    w = n_heads * HEAD
    scale = 1.0 / math.sqrt(HEAD)
    hps = _heads_per_step(n_heads, 2)
    wide = hps * HEAD
    heads = [slice(hd * HEAD, (hd + 1) * HEAD) for hd in range(hps)]

    def body(q_ref, k_ref, v_ref, o_ref, start_ref, count_ref):
        h, i = pl.program_id(0), pl.program_id(1)
        tsuf = _tri(BLOCK, "gt").astype(BF16)

        def live(carry):
            t, _, runs = carry
            top = jnp.max(runs[0])
            for run in runs[1:]:
                top = jnp.maximum(top, jnp.max(run))
            return (t <= i) & (top > SB_DEAD)

        def step(carry):
            t, accs, runs = carry
            j_left = i - t - (SB_UNROLL - 1)
            new_accs, new_runs = [], []
            for hd, cols in enumerate(heads):
                start_ref[hd] = jnp.broadcast_to(runs[hd], (BLOCK, HEAD))
                lbeta, l1m, mask = _sb_scores(q_ref[:, cols], _sb_window(k_ref, j_left, cols), i, j_left, scale)
                later, total = _window_scan(l1m, tsuf, True)
                wgt = jnp.where(mask, jnp.exp(lbeta + later + runs[hd]), 0.0)
                new_accs.append(accs[hd] + jnp.dot(wgt.astype(BF16), _sb_window(v_ref, j_left, cols),
                                                   preferred_element_type=F32))
                new_runs.append(runs[hd] + total)
            return t + SB_UNROLL, tuple(new_accs), tuple(new_runs)

        t, accs, _ = lax.while_loop(live, step, (jnp.int32(0), tuple(jnp.zeros((BLOCK, HEAD), F32) for _ in heads),
                                                 tuple(jnp.zeros((BLOCK, 1), F32) for _ in heads)))
        for hd, cols in enumerate(heads):
            o_ref[:, cols] = accs[hd].astype(BF16)
        count_ref[h, i] = t.astype(F32)

    def whole(group):
        return pl.BlockSpec((m, wide), lambda h, i: (0, group * (n_heads // hps) + h), pipeline_mode=pl.Buffered(1))

    return _pallas(
        "sb_fwd", body, (n_heads // hps, nb),
        [pl.BlockSpec((BLOCK, wide), lambda h, i: (i, group0 * (n_heads // hps) + h)), whole(group0 + 1), whole(group0 + 2)],
        [pl.BlockSpec((BLOCK, wide), lambda h, i: (i, h)), pl.BlockSpec((hps, BLOCK, HEAD), lambda h, i: (h, i, 0)),
         pl.BlockSpec(memory_space=pltpu.SMEM)],
        [_sds((m, w), BF16), _sds((n_heads, m, HEAD), F32), _sds((n_heads // hps, nb), F32)],
        [], (proj, proj, proj), carried)


def _sb_bwd(proj, do, start, count, n_heads, group0):
    m = proj.shape[0]
    nb = m // BLOCK
    w = n_heads * HEAD
    scale = 1.0 / math.sqrt(HEAD)
    hps = _heads_per_step(n_heads, 2)
    wide = hps * HEAD
    heads = [slice(hd * HEAD, (hd + 1) * HEAD) for hd in range(hps)]

    def body(q_ref, k_ref, v_ref, do_ref, start_ref, count_ref, dq_ref, dk_ref, dv_ref, dk_s, dv_s):
        h, i = pl.program_id(0), pl.program_id(1)

        @pl.when(i == 0)
        def _():
            dk_s[...] = jnp.zeros_like(dk_s)
            dv_s[...] = jnp.zeros_like(dv_s)

        count = count_ref[h, i].astype(jnp.int32)
        first = i + 1 - count
        tsuf = _tri(BLOCK, "gt").astype(BF16)
        tpre = _tri(BLOCK, "lt").astype(BF16)

        def step(t, carry):
            dqs, rights, psums = carry
            j_left = first + t * SB_UNROLL
            out = []
            for hd, cols in enumerate(heads):
                q = q_ref[:, cols]
                dob = do_ref[:, cols].astype(BF16)
                kw = _sb_window(k_ref, j_left, cols)
                vw = _sb_window(v_ref, j_left, cols)
                lbeta, l1m, mask = _sb_scores(q, kw, i, j_left, scale)
                later, total = _window_scan(l1m, tsuf, True)
                right = jnp.where(t == 0, rights[hd], rights[hd] - total)
                a = jnp.where(mask, jnp.exp(lbeta + later + right), 0.0)
                p = a * lax.dot_general(dob, vw, NT, preferred_element_type=F32)
                earlier, p_total = _window_scan(p, tpre, False)
                below = psums[hd] + earlier
                beta = jnp.exp(lbeta)
                dz = (jnp.where(mask, p * (1.0 - beta) - below * beta, 0.0) * scale).astype(BF16)
                dq = dqs[hd] + jnp.dot(dz, kw, preferred_element_type=F32)
                dkw = lax.dot_general(dz, q, TN, preferred_element_type=F32)
                dvw = lax.dot_general(a.astype(BF16), dob, TN, preferred_element_type=F32)
                for u in range(SB_UNROLL):
                    rows = pl.ds(pl.multiple_of(jnp.maximum(j_left + u, 0) * BLOCK, BLOCK), BLOCK)
                    dk_s[rows, cols] += dkw[u * BLOCK:(u + 1) * BLOCK, :]
                    dv_s[rows, cols] += dvw[u * BLOCK:(u + 1) * BLOCK, :]
                out.append((dq, right, psums[hd] + p_total))
            return tuple(o[0] for o in out), tuple(o[1] for o in out), tuple(o[2] for o in out)

        dqs, _, _ = lax.fori_loop(0, count // SB_UNROLL, step,
                                  (tuple(jnp.zeros((BLOCK, HEAD), F32) for _ in heads),
                                   tuple(start_ref[hd, :, 0:1] for hd in range(hps)),
                                   tuple(jnp.zeros((BLOCK, 1), F32) for _ in heads)))
        for hd, cols in enumerate(heads):
            dq_ref[:, cols] = dqs[hd].astype(BF16)

        @pl.when(i == nb - 1)
        def _():
            dk_ref[...] = dk_s[...].astype(BF16)
            dv_ref[...] = dv_s[...].astype(BF16)

    def whole(group):
        return pl.BlockSpec((m, wide), lambda h, i: (0, group * (n_heads // hps) + h), pipeline_mode=pl.Buffered(1))

    tile = pl.BlockSpec((BLOCK, wide), lambda h, i: (i, h))
    col = pl.BlockSpec((m, wide), lambda h, i: (0, h))
    return pl.pallas_call(
        body, name="sb_bwd", grid=(n_heads // hps, nb),
        in_specs=[pl.BlockSpec((BLOCK, wide), lambda h, i: (i, group0 * (n_heads // hps) + h)), whole(group0 + 1),
                  whole(group0 + 2), tile, pl.BlockSpec((hps, BLOCK, HEAD), lambda h, i: (h, i, 0)),
                  pl.BlockSpec(memory_space=pltpu.SMEM)],
        out_specs=[tile, col, col],
        out_shape=[_sds((m, w), BF16)] * 3,
        scratch_shapes=[pltpu.VMEM((m, wide), F32)] * 2,
        compiler_params=_params(2),
    )(proj, proj, proj, do, start, count)


def _grad_w_rows(name, x, dy, nd_out):
    m, kx = x.shape
    n = dy.shape[1]
    ks = kx // nd_out
    tm = _tile(m, GRAD_ROW_TILE)

    def epi(acc, er, orf):
        orf[0][...] = acc[0].astype(BF16)

    return _gemm(name, (nd_out, m // tm),
                 [(x, pl.BlockSpec((tm, ks), lambda j, k: (k, j)), dy, pl.BlockSpec((tm, n), lambda j, k: (k, 0)))],
                 [0], [(ks, n)], TN, [], [(_sds((nd_out, ks, n), BF16), pl.BlockSpec((None, ks, n), lambda j, k: (j, 0, 0)))], epi)[0]


def _local_step(x, target, meta, vec, wts, shards=None):
    d = x.shape[1]
    width = vec["hg_norm_g"].shape[1]
    n_heads = width // HEAD
    gate_col = 7 * width
    h0 = jnp.concatenate([jnp.zeros((PAD, d), F32), meta, x], axis=0)
    h0b = h0.astype(BF16)
    wts = dict(wts)
    exchange = None if shards is None else _exchange_carried

    if shards is None:
        a1, b1, s1 = _ffn_up("ffn1_up", h0b, wts["ffn1_w_gate"], wts["ffn1_w_up"])
        r1, h1, h1b = _residual_ln("ffn1_down", s1, True, wts["ffn1_w_down"], h0, vec["ln1_g"], vec["ln1_b"], 0.5)
        proj, projb = _in_proj(h1b, wts["w_in"])
        o_raw, o_hg, states = _hgrn_fwd(proj, vec["hg_lb_logits"], vec["hg_norm_g"], n_heads)
        o_sb, sb_start, sb_count = _sb_fwd(projb, n_heads, 4)
    else:
        half = shards["w_in"].shape[0] // 2
        (a1, b1, s1), (wts["ffn1_w_down"], w_in_top) = _ffn_up(
            "ffn1_up", h0b, wts["ffn1_w_gate"], wts["ffn1_w_up"],
            _gather_carried([shards["ffn1_w_down"], shards["w_in"][:half]]))
        (r1, h1, h1b), (w_in_bottom,) = _residual_ln("ffn1_down", s1, True, wts["ffn1_w_down"], h0, vec["ln1_g"],
                                                     vec["ln1_b"], 0.5, _gather_carried([shards["w_in"][half:]]))
        wts["w_in"] = jnp.concatenate([w_in_top, w_in_bottom], axis=1)
        with_proj = ("w_proj_hg", "w_proj_sb", "w_out", "ffn2_w_gate")
        (proj, projb), got = _in_proj(h1b, wts["w_in"], _gather_carried([shards[k] for k in with_proj]))
        wts.update(zip(with_proj, got))
        (o_raw, o_hg, states), (wts["ffn2_w_up"],) = _hgrn_fwd(proj, vec["hg_lb_logits"], vec["hg_norm_g"], n_heads,
                                                               _gather_carried([shards["ffn2_w_up"]]))
        (o_sb, sb_start, sb_count), (wts["ffn2_w_down"],) = _sb_fwd(projb, n_heads, 4,
                                                                    _gather_carried([shards["ffn2_w_down"]]))
    nd = wts["w_in"].shape[0]
    w_out = wts["w_out"]
    p_hg2 = wts["w_proj_hg"].transpose(1, 0, 2).reshape(width, d)
    p_sb2 = wts["w_proj_sb"].transpose(1, 0, 2).reshape(width, d)
    u_hg, u_sb, y = _proj_merge(o_hg, o_sb, p_hg2, p_sb2, proj, vec["b_gate"], gate_col)
    r2, h2, h2b = _residual_ln("out_proj", y, False, w_out.reshape(d, d), h1, vec["ln2_g"], vec["ln2_b"], 1.0)
    a2, b2, s2 = _ffn_up("ffn2_up", h2b,wts["ffn2_w_gate"], wts["ffn2_w_up"])
    r3, _, _ = _residual_ln("ffn2_down", s2, True, wts["ffn2_w_down"], h2, vec["ln3_g"], vec["ln3_b"], 0.5)

    dr3, dr3b, dg3, db3, loss = _ln_bwd("ln3_bwd", r3, vec["ln3_g"], 0.5, beta=vec["ln3_b"], target=target, first_row=BLOCK)
    dh2, dwg2, dwu2, dwd2 = _ffn_bwd("ffn2", dr3b, dr3, h2b, a2, b2, s2, wts["ffn2_w_gate"], wts["ffn2_w_up"],
                                     wts["ffn2_w_down"], exchange)
    dr2, dr2b, dg2, db2 = _ln_bwd("ln2_bwd", r2, vec["ln2_g"], 1.0, dy=dh2)
    du_hg, du_sb, dz_hg, dz_sb, dbg = _merge_bwd(dr2b, w_out.reshape(d, d), proj, vec["b_gate"], u_hg, u_sb, gate_col)
    dw_out = _grad_w_rows("dw_out", y, dr2b, nd)
    dp_hg = _grad_w("dp_hg", o_hg, du_hg, nd)
    dp_sb = _grad_w("dp_sb", o_sb, du_sb, nd)
    do_hg = _grad_in_whole("do_hg", du_hg, p_hg2)
    do_sb = _grad_in_whole("do_sb", du_sb, p_sb2)
    hg = _hgrn_bwd(proj, vec["hg_lb_logits"], vec["hg_norm_g"], o_raw, do_hg, states, n_heads,
                   exchange([dw_out, dp_hg, dp_sb]) if exchange else None)
    if exchange:
        hg, (dw_out, dp_hg, dp_sb) = hg
    dhq, dhf, dhi, dhog, dgn, dlb = hg
    dsq, dsk, dsv = _sb_bwd(projb, do_sb, sb_start, sb_count, n_heads, 4)
    dproj = jnp.concatenate([dhq, dhf, dhi, dhog, dsq, dsk, dsv, dz_hg, dz_sb], axis=1)
    dw_in = _grad_w("dw_in", h1b, dproj, nd)
    dh1 = _grad_in("dh1", dproj, wts["w_in"], add=dr2, carried=exchange([dw_in]) if exchange else None)
    if exchange:
        dh1, (dw_in,) = dh1
    dr1, dr1b, dg1, db1 = _ln_bwd("ln1_bwd", r1, vec["ln1_g"], 0.5, dy=dh1)
    dh0, dwg1, dwu1, dwd1 = _ffn_bwd("ffn1", dr1b, dr1, h0b, a1, b1, s1, wts["ffn1_w_gate"], wts["ffn1_w_up"],
                                     wts["ffn1_w_down"], exchange)

    small = {"ln1_g": dg1, "ln1_b": db1, "ln2_g": dg2, "ln2_b": db2, "ln3_g": dg3, "ln3_b": db3,
             "b_gate": dbg, "hg_lb": dlb, "hg_norm_g": dgn}
    big = {"ffn1_w_gate": dwg1, "ffn1_w_up": dwu1, "ffn1_w_down": dwd1, "w_in": dw_in, "w_proj_hg": dp_hg,
           "w_proj_sb": dp_sb, "w_out": dw_out, "ffn2_w_gate": dwg2, "ffn2_w_up": dwu2, "ffn2_w_down": dwd2}
    return loss, dh0[BLOCK:], dh0[PAD:BLOCK], small, big


def _position():
    return lax.axis_index("x"), lax.axis_index("y"), lax.axis_index("c")


def _slot(px, py, pc):
    return 4 * px + 2 * py + pc


def _all_gather(shards):
    n = len(shards)

    def body(*refs):
        ins, outs = refs[:n], refs[n:2 * n]
        send_sems, recv_sems, local_sems = refs[2 * n:]
        x, y, c = _position()
        me, sibling = (x, y, c), (x, y, 1 - c)
        chips = [(1 - x, y), (x, 1 - y), (1 - x, 1 - y)]

        def copy(a, k, block, to, src=None):
            dst = outs[a].at[_slot(*block)]
            return pltpu.make_async_remote_copy(src_ref=dst if src is None else src, dst_ref=dst,
                                                send_sem=send_sems.at[a, k], recv_sem=recv_sems.at[a, k],
                                                device_id=to, device_id_type=MESH)

        mine = [pltpu.make_async_copy(ins[a], outs[a].at[_slot(*me)], local_sems.at[a]) for a in range(n)]
        for cp in mine:
            cp.start()
        first = []
        for a in range(n):
            first.append(copy(a, 0, me, sibling, src=ins[a]))
            first += [copy(a, 1 + j, me, (*chip, c), src=ins[a]) for j, chip in enumerate(chips)]
        for cp in first:
            cp.start()
        passed = []
        for j, chip in enumerate(chips):
            for a in range(n):
                copy(a, 1 + j, (*chip, c), me).wait_recv()
                cp = copy(a, 4 + j, (*chip, c), sibling)
                cp.start()
                passed.append(cp)
        for a in range(n):
            copy(a, 0, sibling, me).wait_recv()
        for j, chip in enumerate(chips):
            for a in range(n):
                copy(a, 4 + j, (*chip, 1 - c), me).wait_recv()
        for cp in first + passed:
            cp.wait_send()
        for cp in mine:
            cp.wait()

    return pl.pallas_call(
        body, name="all_gather", out_shape=[_sds((N_DEV,) + s.shape, s.dtype) for s in shards],
        in_specs=[ANY] * n, out_specs=[ANY] * n,
        scratch_shapes=[pltpu.SemaphoreType.DMA((n, 7)), pltpu.SemaphoreType.DMA((n, 7)), pltpu.SemaphoreType.DMA((n,))],
    )(*shards)


def _exchange_carried(grads):
    return _direct_copies(grads, [_sds(g.shape, g.dtype) for g in grads], lambda ref, slot: ref.at[slot])


def _gather_carried(shards):
    return _direct_copies(shards, [_sds((N_DEV,) + s.shape, s.dtype) for s in shards], lambda ref, slot: ref)


def _direct_copies(arrays, outs, block_for):
    n = len(arrays)

    def plan(ins, results, sems, arriving):
        send_sems, recv_sems, local_sems = sems
        x, y, c = _position()
        mine = _slot(x, y, c)
        peers = [(1 - x if k & 4 else x, 1 - y if k & 2 else y, 1 - c if k & 1 else c) for k in range(1, N_DEV)]
        own = [pltpu.make_async_copy(block_for(ins[a], mine), results[a].at[mine], local_sems.at[a]) for a in range(n)]
        remote = [pltpu.make_async_remote_copy(
            src_ref=block_for(ins[a], mine if arriving else _slot(*peer)),
            dst_ref=results[a].at[_slot(*peer) if arriving else mine],
            send_sem=send_sems.at[a, k], recv_sem=recv_sems.at[a, k], device_id=peer, device_id_type=MESH)
            for a in range(n) for k, peer in enumerate(peers)]
        return own, remote

    def start(ins, results, sems):
        own, sent = plan(ins, results, sems, False)
        for cp in own + sent:
            cp.start()

    def finish(ins, results, sems):
        _, landed = plan(ins, results, sems, True)
        for cp in landed:
            cp.wait_recv()
        own, sent = plan(ins, results, sems, False)
        for cp in sent:
            cp.wait_send()
        for cp in own:
            cp.wait()

    sems = [pltpu.SemaphoreType.DMA((n, 7)), pltpu.SemaphoreType.DMA((n, 7)), pltpu.SemaphoreType.DMA((n,))]
    return _Carried(list(arrays), outs, sems, start, finish)


def _all_reduce_rows(v):
    rows = v.shape[0]

    def body(v_ref, out_ref, buf, send_sems, recv_sems):
        x, y, c = _position()
        me, sibling = (x, y, c), (x, y, 1 - c)
        chips = [(1 - x, y), (x, 1 - y), (1 - x, 1 - y)]

        def copy(k, block, to, src=None):
            dst = buf.at[_slot(*block)]
            return pltpu.make_async_remote_copy(src_ref=dst if src is None else src, dst_ref=dst,
                                                send_sem=send_sems.at[k], recv_sem=recv_sems.at[k],
                                                device_id=to, device_id_type=MESH)

        first = [copy(0, me, sibling, src=v_ref)] + [copy(1 + j, me, (*chip, c), src=v_ref) for j, chip in enumerate(chips)]
        for cp in first:
            cp.start()
        buf[_slot(*me)] = v_ref[...]
        passed = [copy(4 + j, (*chip, c), sibling) for j, chip in enumerate(chips)]
        for j, chip in enumerate(chips):
            copy(1 + j, (*chip, c), me).wait_recv()
            passed[j].start()
        copy(0, sibling, me).wait_recv()
        for j, chip in enumerate(chips):
            copy(4 + j, (*chip, 1 - c), me).wait_recv()
        for cp in first + passed:
            cp.wait_send()
        total = buf[0]
        for s in range(1, N_DEV):
            total = total + buf[s]
        out_ref[...] = total

    vmem = pl.BlockSpec(memory_space=pltpu.VMEM)
    return pl.pallas_call(
        body, name="small_all_reduce", out_shape=_sds(v.shape, F32), in_specs=[vmem], out_specs=vmem,
        scratch_shapes=[pltpu.VMEM((N_DEV, rows, 128), F32), pltpu.SemaphoreType.DMA((7,)), pltpu.SemaphoreType.DMA((7,))],
    )(v)


def _adamw(name, w, m, v, contrib):
    r, c = w.shape
    n = contrib.shape[0]
    tr = _tile(r, 256)

    def body(w_ref, m_ref, v_ref, c_ref, g_out, d_out, m_out, v_out):
        g = c_ref[0].astype(F32)
        for s in range(1, n):
            g = g + c_ref[s].astype(F32)
        m2 = ADAM_B1 * m_ref[...] + (1.0 - ADAM_B1) * g
        v2 = ADAM_B2 * v_ref[...] + (1.0 - ADAM_B2) * (g * g)
        m_hat = m2 / (1.0 - ADAM_B1 ** ADAM_STEP)
        v_hat = v2 / (1.0 - ADAM_B2 ** ADAM_STEP)
        g_out[...] = g
        d_out[...] = -ADAM_LR * (m_hat / (jnp.sqrt(v_hat) + ADAM_EPS) + ADAM_WD * w_ref[...])
        m_out[...] = m2
        v_out[...] = v2

    tile = pl.BlockSpec((tr, c), lambda i: (i, 0))
    return pl.pallas_call(
        body, name=name, grid=(r // tr,), in_specs=[tile, tile, tile, pl.BlockSpec((n, tr, c), lambda i: (0, i, 0))],
        out_specs=[tile] * 4, out_shape=[_sds((r, c), F32)] * 4, compiler_params=_params(1),
    )(w, m, v, contrib)


def _lb_logits_grad(logits, dlb):
    def body(lg_ref, d_ref, out_ref):
        lg = lg_ref[...]
        mx = jnp.maximum(lg[0:1], lg[1:2])
        e0 = jnp.exp(lg[0:1] - mx)
        p0 = e0 / (e0 + jnp.exp(lg[1:2] - mx))
        g0 = d_ref[...] * p0 * (1.0 - p0)
        out_ref[0:1, :] = g0
        out_ref[1:2, :] = -g0

    return pl.pallas_call(body, name="lb_logits_grad", out_shape=_sds(logits.shape, F32))(logits, dlb)


BIG = ("ffn1_w_gate", "ffn1_w_up", "ffn1_w_down", "w_in", "w_proj_hg", "w_proj_sb", "w_out",
       "ffn2_w_gate", "ffn2_w_up", "ffn2_w_down")
VECTORS = ("ln1_g", "ln1_b", "b_gate", "hg_lb_logits", "hg_norm_g", "ln2_g", "ln2_b", "ln3_g", "ln3_b")
WEIGHTS = ("meta", "ln1_g", "ln1_b", "ffn1_w_gate", "ffn1_w_up", "ffn1_w_down", "w_in", "b_gate", "hg_lb_logits",
           "hg_norm_g", "w_proj_hg", "w_proj_sb", "w_out", "ln2_g", "ln2_b", "ffn2_w_gate", "ffn2_w_up",
           "ffn2_w_down", "ln3_g", "ln3_b")


def kernel(x, meta, ln1_g, ln1_b, ffn1_w_gate, ffn1_w_up, ffn1_w_down, w_in, b_gate, hg_lb_logits, hg_norm_g, w_proj_hg, w_proj_sb, w_out, ln2_g, ln2_b, ffn2_w_gate, ffn2_w_up, ffn2_w_down, ln3_g, ln3_b, loss_target, m_meta, m_ln1_g, m_ln1_b, m_ffn1_w_gate, m_ffn1_w_up, m_ffn1_w_down, m_w_in, m_b_gate, m_hg_lb_logits, m_hg_norm_g, m_w_proj_hg, m_w_proj_sb, m_w_out, m_ln2_g, m_ln2_b, m_ffn2_w_gate, m_ffn2_w_up, m_ffn2_w_down, m_ln3_g, m_ln3_b, v_meta, v_ln1_g, v_ln1_b, v_ffn1_w_gate, v_ffn1_w_up, v_ffn1_w_down, v_w_in, v_b_gate, v_hg_lb_logits, v_hg_norm_g, v_w_proj_hg, v_w_proj_sb, v_w_out, v_ln2_g, v_ln2_b, v_ffn2_w_gate, v_ffn2_w_up, v_ffn2_w_down, v_ln3_g, v_ln3_b):
    given = dict(locals())
    d = x.shape[-1]
    ds = meta.shape[1]

    shards = {k: given[k][0].astype(BF16) for k in BIG}
    first = ("ffn1_w_gate", "ffn1_w_up")
    gathered = _all_gather([meta] + [shards.pop(k) for k in first])
    meta_full = gathered[0].transpose(1, 0, 2).reshape(N_META, d)
    vec = {k: given[k] for k in VECTORS}
    loss, grad_x, dmeta, small, received = _local_step(x[0], loss_target[0], meta_full, vec, dict(zip(first, gathered[1:])),
                                                       shards)

    order =("ln1_g", "ln1_b", "ln2_g", "ln2_b", "ln3_g", "ln3_b", "b_gate", "hg_lb", "hg_norm_g")
    parts = [small[k].reshape(-1, 128) for k in order] + [dmeta.reshape(-1, 128), jnp.broadcast_to(loss, (8, 128))]
    total = _all_reduce_rows(jnp.concatenate(parts, axis=0))
    reduced, row = {}, 0
    for k, p in zip(order + ("meta", "loss"), parts):
        reduced[k] = total[row:row + p.shape[0]]
        row += p.shape[0]
    loss_out = reduced["loss"][0, 0]
    me = _slot(*_position())
    dmeta_mine = lax.dynamic_slice(reduced["meta"].reshape(N_META, d), (0, me * ds), (N_META, ds))
    dlogits = _lb_logits_grad(hg_lb_logits, reduced["hg_lb"].reshape(1, -1))

    grads, deltas, new_m, new_v = {}, {}, {}, {}
    for k in WEIGHTS:
        w = given[k]
        lead = w.shape[:-2]
        w2, m2, v2 = (a.reshape(a.shape[-2:]) for a in (w, given["m_" + k], given["v_" + k]))
        if k in BIG:
            contrib = received[k]
        elif k == "meta":
            contrib = dmeta_mine[None]
        elif k == "hg_lb_logits":
            contrib = dlogits[None]
        else:
            contrib = reduced[k].reshape((1,) + w2.shape)
        out = _adamw("adamw_" + k, w2, m2, v2, contrib)
        grads[k], deltas[k], new_m[k], new_v[k] = (o.reshape(lead + o.shape) for o in out)
    return (loss_out, grad_x[None], *[grads[k] for k in WEIGHTS], *[deltas[k] for k in WEIGHTS],
            *[new_m[k] for k in WEIGHTS], *[new_v[k] for k in WEIGHTS])
```

```python
import functools
import math

import jax
import jax.numpy as jnp
from jax import lax
from jax.experimental import pallas as pl
from jax.experimental.pallas import tpu as pltpu

F32 = jnp.float32
BF16 = jnp.bfloat16
MESH = pl.DeviceIdType.MESH

N_DEV = 8
N_META = 16
BLOCK = 128
PAD = BLOCK - N_META
HEAD = 128
CHUNK = 16
LN_EPS = 1e-5
RMS_EPS = 1e-6
DN_ALPHA = 2.0 ** 0.25
ADAM_LR, ADAM_B1, ADAM_B2, ADAM_EPS, ADAM_WD, ADAM_STEP = 0.001, 0.9, 0.999, 1e-08, 0.01, 10

VMEM_LIMIT_V7X = 60 * 1024 * 1024
ROW_TILE = 640
LN_ROW_TILE = 320
COL_TILE = 512
GRAD_ROW_TILE = 1664

NN = (((1,), (0,)), ((), ()))
NT = (((1,), (1,)), ((), ()))
TN = (((0,), (0,)), ((), ()))


def _tile(n, pref, mult=16):
    best = None
    for t in range(mult, min(n, pref) + 1, mult):
        if n % t == 0:
            best = t
    return n if best is None else best


def _params(n_axes):
    return pltpu.CompilerParams(dimension_semantics=("arbitrary",) * n_axes, vmem_limit_bytes=VMEM_LIMIT_V7X)


def _sigmoid(x):
    return 1.0 / (1.0 + jnp.exp(-x))


class _Carried:
    def __init__(self, ins, outs, sems, start, finish):
        self.ins, self.outs, self.sems, self.start, self.finish = ins, outs, sems, start, finish


ANY = pl.BlockSpec(memory_space=pl.ANY)


def _pallas(name, body, grid, in_specs, out_specs, out_shape, scratch, operands, carried=None):
    if carried is None:
        return pl.pallas_call(body, name=name, grid=grid, in_specs=in_specs, out_specs=out_specs, out_shape=out_shape,
                              scratch_shapes=scratch, compiler_params=_params(len(grid)))(*operands)
    n_in, n_out, n_scr = len(in_specs), len(out_specs), len(scratch)
    c_in, c_out = len(carried.ins), len(carried.outs)

    def wrapped(*refs):
        ins, rest = refs[:n_in], refs[n_in:]
        c_ins, rest = rest[:c_in], rest[c_in:]
        outs, rest = rest[:n_out], rest[n_out:]
        c_outs, rest = rest[:c_out], rest[c_out:]
        scr, c_sems = rest[:n_scr], rest[n_scr:]
        first = last = None
        for axis, size in enumerate(grid):
            at0, at_end = pl.program_id(axis) == 0, pl.program_id(axis) == size - 1
            first = at0 if first is None else first & at0
            last = at_end if last is None else last & at_end

        @pl.when(first)
        def _():
            carried.start(c_ins, c_outs, c_sems)

        body(*ins, *outs, *scr)

        @pl.when(last)
        def _():
            carried.finish(c_ins, c_outs, c_sems)

    res = pl.pallas_call(
        wrapped, name=name, grid=grid, in_specs=list(in_specs) + [ANY] * c_in, out_specs=list(out_specs) + [ANY] * c_out,
        out_shape=list(out_shape) + list(carried.outs), scratch_shapes=list(scratch) + list(carried.sems),
        compiler_params=_params(len(grid)),
    )(*operands, *carried.ins)
    return res[:n_out], res[n_out:]


def _gemm(name, grid, pairs, acc_of, acc_shapes, dims, extras, outs, epilogue, carried=None):
    n_extra, n_out = len(extras), len(outs)
    nk = grid[-1]
    k_axis = len(grid) - 1
    operands, in_specs, where = [], [], {}
    for a, a_spec, b, b_spec in pairs:
        for arr, spec in ((a, a_spec), (b, b_spec)):
            if (id(arr), id(spec)) not in where:
                where[(id(arr), id(spec))] = len(operands)
                operands.append(arr)
                in_specs.append(spec)
    n_mat = len(operands)
    slots = [(where[(id(a), id(a_spec))], where[(id(b), id(b_spec))]) for a, a_spec, b, b_spec in pairs]

    def body(*refs):
        er = refs[n_mat:n_mat + n_extra]
        orf = refs[n_mat + n_extra:n_mat + n_extra + n_out]
        accs = refs[n_mat + n_extra + n_out:]

        def part(p):
            a_ref, b_ref = refs[slots[p][0]], refs[slots[p][1]]
            if len(b_ref.shape) == 2:
                return lax.dot_general(a_ref[...].astype(BF16), b_ref[...].astype(BF16), dims, preferred_element_type=F32)
            total = None
            width = b_ref.shape[2]
            for s in range(b_ref.shape[0]):
                a = a_ref[s] if len(a_ref.shape) == 3 else a_ref[:, s * width:(s + 1) * width]
                d = lax.dot_general(a.astype(BF16), b_ref[s].astype(BF16), dims, preferred_element_type=F32)
                total = d if total is None else total + d
            return total

        if nk == 1:
            vals = [None] * len(acc_shapes)
            for p in range(len(pairs)):
                d = part(p)
                vals[acc_of[p]] = d if vals[acc_of[p]] is None else vals[acc_of[p]] + d
            epilogue(vals, er, orf)
        else:
            k = pl.program_id(k_axis)

            @pl.when(k == 0)
            def _():
                for acc in accs:
                    acc[...] = jnp.zeros_like(acc)

            for p in range(len(pairs)):
                accs[acc_of[p]][...] += part(p)

            @pl.when(k == nk - 1)
            def _():
                epilogue([acc[...] for acc in accs], er, orf)

    for e, e_spec in extras:
        operands.append(e)
        in_specs.append(e_spec)
    scratch = [] if nk == 1 else [pltpu.VMEM(s, F32) for s in acc_shapes]
    return _pallas(name, body, grid, in_specs, [s for _, s in outs], [o for o, _ in outs], scratch, operands, carried)


def _sds(shape, dtype):
    return jax.ShapeDtypeStruct(shape, dtype)


def _ln_rows(r, g, b):
    mu = jnp.mean(r, axis=-1, keepdims=True)
    xc = r - mu
    var = jnp.mean(xc * xc, axis=-1, keepdims=True)
    return xc * lax.rsqrt(var + LN_EPS) * g + b


def _ffn_up(name, hb, wg, wu, carried=None):
    m, d = hb.shape
    nd, _, fs = wg.shape
    tm = _tile(m, ROW_TILE)

    def epi(acc, er, orf):
        a, b = acc
        sg = _sigmoid(a)
        silu = a * sg
        orf[0][...] = b * sg * (1.0 + a * (1.0 - sg))
        orf[1][...] = silu
        orf[2][...] = (silu * b).astype(BF16)

    h_spec = pl.BlockSpec((tm, d), lambda i, j, k: (i, 0))
    w_spec = pl.BlockSpec((None, d, fs), lambda i, j, k: (j, 0, 0))
    o_spec = pl.BlockSpec((None, tm, fs), lambda i, j, k: (j, i, 0))
    return _gemm(name, (m // tm, nd, 1), [(hb, h_spec, wg, w_spec), (hb, h_spec, wu, w_spec)], [0, 1],
                 [(tm, fs)] * 2, NN, [],
                 [(_sds((nd, m, fs), F32), o_spec), (_sds((nd, m, fs), F32), o_spec), (_sds((nd, m, fs), BF16), o_spec)], epi,
                 carried)


def _residual_ln(name, a, a_stacked, w, h_in, g, beta, scale, carried=None):
    d = w.shape[-1]
    m = h_in.shape[0]
    tm = _tile(m, LN_ROW_TILE)

    def epi(acc, er, orf):
        r = DN_ALPHA * er[0][...] + scale * acc[0]
        h = _ln_rows(r, er[1][...], er[2][...])
        orf[0][...] = r
        orf[1][...] = h
        orf[2][...] = h.astype(BF16)

    once = pl.Buffered(1)
    if a_stacked:
        a_spec = pl.BlockSpec((a.shape[0], tm, a.shape[2]), lambda i, k: (0, i, 0))
        w_spec = pl.BlockSpec(w.shape, lambda i, k: (0, 0, 0), pipeline_mode=once)
    else:
        a_spec = pl.BlockSpec((tm, a.shape[1]), lambda i, k: (i, 0))
        w_spec = pl.BlockSpec(w.shape, lambda i, k: (0, 0), pipeline_mode=once)
    row = pl.BlockSpec((tm, d), lambda i, k: (i, 0))
    vec = pl.BlockSpec((1, d), lambda i, k: (0, 0))
    return _gemm(name, (m // tm, 1), [(a, a_spec, w, w_spec)], [0], [(tm, d)], NN,
                 [(h_in, row), (g, vec), (beta, vec)],
                 [(_sds((m, d), F32), row), (_sds((m, d), F32), row), (_sds((m, d), BF16), row)], epi, carried)


def _in_proj(hb, w_in, carried=None):
    m, d = hb.shape
    nd, _, cs = w_in.shape
    tm = _tile(m, ROW_TILE)

    def epi(acc, er, orf):
        orf[0][...] = acc[0]
        orf[1][...] = acc[0].astype(BF16)

    out = pl.BlockSpec((tm, cs), lambda i, j, k: (i, j))
    res = _gemm("in_proj", (m // tm, nd, 1),
                [(hb, pl.BlockSpec((tm, d), lambda i, j, k: (i, 0)), w_in, pl.BlockSpec((None, d, cs), lambda i, j, k: (j, 0, 0)))],
                [0], [(tm, cs)], NN, [], [(_sds((m, nd * cs), F32), out), (_sds((m, nd * cs), BF16), out)], epi, carried)
    return tuple(res) if carried is None else (tuple(res[0]), res[1])


def _proj_merge(o_hg, o_sb, p_hg, p_sb, proj, b_gate, gate_col):
    m, w = o_hg.shape
    d = p_hg.shape[1]
    tm = _tile(m, ROW_TILE)
    tn = _tile(d, COL_TILE, 128)
    nn = d // tn
    c0 = gate_col // tn

    def epi(acc, er, orf):
        u_hg, u_sb = acc
        g_hg = _sigmoid(er[0][...] + er[2][...])
        g_sb = _sigmoid(er[1][...] + er[3][...])
        orf[0][...] = u_hg
        orf[1][...] = u_sb
        orf[2][...] = (g_hg * u_hg + g_sb * u_sb).astype(BF16)

    o_spec = pl.BlockSpec((tm, w), lambda i, j, k: (i, 0))
    p_spec = pl.BlockSpec((w, tn), lambda i, j, k: (0, j))
    out = pl.BlockSpec((tm, tn), lambda i, j, k: (i, j))
    return _gemm("proj_merge", (m // tm, nn, 1), [(o_hg, o_spec, p_hg, p_spec), (o_sb, o_spec, p_sb, p_spec)], [0, 1],
                 [(tm, tn)] * 2, NN,
                 [(proj, pl.BlockSpec((tm, tn), lambda i, j, k: (i, c0 + j))),
                  (proj, pl.BlockSpec((tm, tn), lambda i, j, k: (i, c0 + nn + j))),
                  (b_gate, pl.BlockSpec((1, tn), lambda i, j, k: (0, j))),
                  (b_gate, pl.BlockSpec((1, tn), lambda i, j, k: (0, nn + j)))],
                 [(_sds((m, d), F32), out), (_sds((m, d), F32), out), (_sds((m, d), BF16), out)], epi)


def _ln_bwd(name, r, g, out_scale, dy=None, beta=None, target=None, first_row=0):
    m, d = r.shape
    tm = _tile(m, LN_ROW_TILE if target is None else BLOCK)
    with_loss = target is not None
    skip = first_row // tm if with_loss else 0
    assert not with_loss or first_row % tm == 0

    def body(*refs):
        if with_loss:
            r_ref, g_ref, b_ref, t_ref, dr_ref, drb_ref, dg_ref, db_ref, loss_ref = refs
        else:
            r_ref, g_ref, dy_ref, dr_ref, drb_ref, dg_ref, db_ref = refs
        i = pl.program_id(0)
        x = r_ref[...]
        mu = jnp.mean(x, axis=-1, keepdims=True)
        xc = x - mu
        var = jnp.mean(xc * xc, axis=-1, keepdims=True)
        rstd = lax.rsqrt(var + LN_EPS)
        xhat = xc * rstd
        gv = g_ref[...]
        if with_loss:
            err = xhat * gv + b_ref[...] - t_ref[...]
            live = (i >= skip).astype(F32)
            dyv = err * (live / d)
            part = 0.5 * live * jnp.sum(jnp.sum(err * err, axis=-1, keepdims=True), axis=0, keepdims=True) / d
        else:
            dyv = dy_ref[...]
        dxh = dyv * gv
        m1 = jnp.mean(dxh, axis=-1, keepdims=True)
        m2 = jnp.mean(dxh * xhat, axis=-1, keepdims=True)
        dr = rstd * (dxh - m1 - xhat * m2)
        dr_ref[...] = dr
        drb_ref[...] = (out_scale * dr).astype(BF16)

        @pl.when(i == 0)
        def _():
            dg_ref[...] = jnp.zeros_like(dg_ref)
            db_ref[...] = jnp.zeros_like(db_ref)
            if with_loss:
                loss_ref[...] = jnp.zeros_like(loss_ref)

        dg_ref[...] += jnp.sum(dyv * xhat, axis=0, keepdims=True)
        db_ref[...] += jnp.sum(dyv, axis=0, keepdims=True)
        if with_loss:
            loss_ref[...] += jnp.broadcast_to(part, loss_ref.shape)

    row = pl.BlockSpec((tm, d), lambda i: (i, 0))
    vec = pl.BlockSpec((1, d), lambda i: (0, 0))
    out_shape = [_sds((m, d), F32), _sds((m, d), BF16), _sds((1, d), F32), _sds((1, d), F32)]
    out_specs = [row, row, vec, vec]
    if with_loss:
        operands = [r, g, beta, target]
        in_specs = [row, vec, vec, pl.BlockSpec((tm, d), lambda i: (jnp.maximum(i - skip, 0), 0))]
        out_shape.append(_sds((1, BLOCK), F32))
        out_specs.append(pl.BlockSpec((1, BLOCK), lambda i: (0, 0)))
    else:
        operands = [r, g, dy]
        in_specs = [row, vec, row]
    return pl.pallas_call(body, name=name, grid=(m // tm,), in_specs=in_specs, out_specs=out_specs, out_shape=out_shape,
                          compiler_params=_params(1))(*operands)


def _ffn_bwd(tag, drb, dr, hb, a, b, s, wg, wu, wd, exchange=None):
    m, d = drb.shape
    nd, _, fs = wg.shape
    tm = _tile(m, ROW_TILE)

    pair = 2 if nd % 2 == 0 else 1

    def ds_body(drb_ref, wd_ref, a_ref, b_ref, da_ref, db_ref):
        x = drb_ref[...]
        for blk in range(pair):
            ds = lax.dot_general(x, wd_ref[blk], NT, preferred_element_type=F32)
            da_ref[blk] = (ds * a_ref[blk]).astype(BF16)
            db_ref[blk] = (ds * b_ref[blk]).astype(BF16)

    st = pl.BlockSpec((pair, tm, fs), lambda i, j: (j, i, 0))
    da, db = _pallas(tag + "_ds", ds_body, (m // tm, nd // pair),
                     [pl.BlockSpec((tm, d), lambda i, j: (i, 0)), pl.BlockSpec((pair, fs, d), lambda i, j: (j, 0, 0)), st, st],
                     [st, st], [_sds((nd, m, fs), BF16)] * 2, [], (drb, wd, a, b))

    def epi_w(acc, er, orf):
        orf[0][...] = acc[0].astype(BF16)

    tr = _tile(m, GRAD_ROW_TILE)
    nkm = m // tr
    dwd = _gemm(tag + "_dwd", (nd, nkm),
                [(s, pl.BlockSpec((None, tr, fs), lambda j, k: (j, k, 0)), drb, pl.BlockSpec((tr, d), lambda j, k: (k, 0)))],
                [0], [(fs, d)], TN, [], [(_sds((nd, fs, d), BF16), pl.BlockSpec((None, fs, d), lambda j, k: (j, 0, 0)))], epi_w)[0]
    h_spec = pl.BlockSpec((tr, d), lambda j, k: (k, 0))
    g_spec = pl.BlockSpec((None, tr, fs), lambda j, k: (j, k, 0))
    w_out = pl.BlockSpec((None, d, fs), lambda j, k: (j, 0, 0))
    dwg = _gemm(tag + "_dwg", (nd, nkm), [(hb, h_spec, da, g_spec)], [0], [(d, fs)], TN, [],
                [(_sds((nd, d, fs), BF16), w_out)], epi_w, exchange([dwd]) if exchange else None)
    if exchange:
        (dwg,), (dwd,) = dwg
    else:
        dwg = dwg[0]
    dwu = _gemm(tag + "_dwu", (nd, nkm), [(hb, h_spec, db, g_spec)], [0], [(d, fs)], TN, [],
                [(_sds((nd, d, fs), BF16), w_out)], epi_w, exchange([dwg]) if exchange else None)
    if exchange:
        (dwu,), (dwg,) = dwu
    else:
        dwu = dwu[0]

    def epi_dh(acc, er, orf):
        orf[0][...] = DN_ALPHA * er[0][...] + acc[0]

    gk = pl.BlockSpec((pair, tm, fs), lambda i, k: (k, i, 0))
    wk = pl.BlockSpec((pair, d, fs), lambda i, k: (k, 0, 0))
    row = pl.BlockSpec((tm, d), lambda i, k: (i, 0))
    dh = _gemm(tag + "_dh", (m // tm, nd // pair), [(da, gk, wg, wk), (db, gk, wu, wk)], [0, 0], [(tm, d)], NT,
               [(dr, row)], [(_sds((m, d), F32), row)], epi_dh, exchange([dwu]) if exchange else None)
    if exchange:
        (dh,), (dwu,) = dh
    else:
        dh = dh[0]
    return dh, dwg, dwu, dwd


def _merge_bwd(dmixb, w_out2, proj, b_gate, u_hg, u_sb, gate_col):
    m, d = dmixb.shape
    ds = _tile(d, COL_TILE, 128)
    nd = d // ds
    tm = _tile(m, ROW_TILE)
    c0 = gate_col // ds

    def epi(acc, er, orf):
        i = pl.program_id(1)
        dy = acc[0]
        g_hg = _sigmoid(er[0][...] + er[2][...])
        g_sb = _sigmoid(er[1][...] + er[3][...])
        orf[0][...] = (dy * g_hg).astype(BF16)
        orf[1][...] = (dy * g_sb).astype(BF16)
        dz_hg = dy * er[4][...] * g_hg * (1.0 - g_hg)
        dz_sb = dy * er[5][...] * g_sb * (1.0 - g_sb)
        orf[2][...] = dz_hg.astype(BF16)
        orf[3][...] = dz_sb.astype(BF16)

        @pl.when(i == 0)
        def _():
            orf[4][...] = jnp.zeros_like(orf[4])
            orf[5][...] = jnp.zeros_like(orf[5])

        orf[4][...] += jnp.sum(dz_hg, axis=0, keepdims=True)
        orf[5][...] += jnp.sum(dz_sb, axis=0, keepdims=True)

    tile = pl.BlockSpec((tm, ds), lambda j, i, k: (i, j))
    vec = pl.BlockSpec((1, ds), lambda j, i, k: (0, j))
    du_hg, du_sb, dz_hg, dz_sb, db_hg, db_sb = _gemm(
        "merge_bwd", (nd, m // tm, 1),
        [(dmixb, pl.BlockSpec((tm, d), lambda j, i, k: (i, 0)), w_out2, pl.BlockSpec((ds, d), lambda j, i, k: (j, 0)))],
        [0], [(tm, ds)], NT,
        [(proj, pl.BlockSpec((tm, ds), lambda j, i, k: (i, c0 + j))),
         (proj, pl.BlockSpec((tm, ds), lambda j, i, k: (i, c0 + nd + j))),
         (b_gate, vec), (b_gate, pl.BlockSpec((1, ds), lambda j, i, k: (0, nd + j))),
         (u_hg, tile), (u_sb, tile)],
        [(_sds((m, d), BF16), tile), (_sds((m, d), BF16), tile), (_sds((m, d), BF16), tile), (_sds((m, d), BF16), tile),
         (_sds((1, d), F32), vec), (_sds((1, d), F32), vec)], epi)
    return du_hg, du_sb, dz_hg, dz_sb, jnp.concatenate([db_hg, db_sb], axis=1)


def _grad_w(name, x, dy, nd_out):
    m, kx = x.shape
    n = dy.shape[1]
    ns = n // nd_out
    tm = _tile(m, GRAD_ROW_TILE)

    def epi(acc, er, orf):
        orf[0][...] = acc[0].astype(BF16)

    return _gemm(name, (nd_out, m // tm),
                 [(x, pl.BlockSpec((tm, kx), lambda j, k: (k, 0)), dy, pl.BlockSpec((tm, ns), lambda j, k: (k, j)))],
                 [0], [(kx, ns)], TN, [], [(_sds((nd_out, kx, ns), BF16), pl.BlockSpec((None, kx, ns), lambda j, k: (j, 0, 0)))], epi)[0]


def _grad_in_whole(name, dy, w2):
    m, n = dy.shape
    kx = w2.shape[0]
    tm = _tile(m, ROW_TILE)

    def epi(acc, er, orf):
        orf[0][...] = acc[0]

    return _gemm(name, (m // tm, 1),
                 [(dy, pl.BlockSpec((tm, n), lambda i, k: (i, 0)),
                   w2, pl.BlockSpec((kx, n), lambda i, k: (0, 0), pipeline_mode=pl.Buffered(1)))],
                 [0], [(tm, kx)], NT, [], [(_sds((m, kx), F32), pl.BlockSpec((tm, kx), lambda i, k: (i, 0)))], epi)[0]


def _grad_in(name, dy, w, add=None, carried=None):
    m = dy.shape[0]
    nd, kx, ns = w.shape
    tm = _tile(m, ROW_TILE)

    def epi(acc, er, orf):
        orf[0][...] = acc[0] if add is None else DN_ALPHA * er[0][...] + acc[0]

    pair = 2 if nd % 2 == 0 else 1
    row = pl.BlockSpec((tm, kx), lambda i, k: (i, 0))
    res = _gemm(name, (m // tm, nd // pair),
                [(dy, pl.BlockSpec((tm, pair * ns), lambda i, k: (i, k)),
                  w, pl.BlockSpec((pair, kx, ns), lambda i, k: (k, 0, 0)))],
                [0], [(tm, kx)], NT, [] if add is None else [(add, row)], [(_sds((m, kx), F32), row)], epi, carried)
    return res[0] if carried is None else (res[0][0], res[1])


def _tri(n, kind):
    r = lax.broadcasted_iota(jnp.int32, (n, n), 0)
    c = lax.broadcasted_iota(jnp.int32, (n, n), 1)
    return {"le": c <= r, "ge": c >= r, "gt": r > c, "lt": r < c}[kind]


def _dot_f32(a, b, dims=NN):
    return lax.dot_general(a, b, dims, preferred_element_type=F32, precision=lax.Precision.HIGHEST)


def _hgrn_gates(i, hq, hf, logits):
    lg = logits
    mx = jnp.maximum(lg[0:1], lg[1:2])
    e0 = jnp.exp(lg[0:1] - mx)
    lb = e0 / (e0 + jnp.exp(lg[1:2] - mx))
    sig = _sigmoid(hf)
    f = lb + (1.0 - lb) * sig
    valid = (i * BLOCK + lax.broadcasted_iota(jnp.int32, hf.shape, 0)) >= PAD
    g = jnp.where(valid, jnp.log(f), 0.0)
    k = jnp.where(valid, 1.0 - f, 0.0)
    sq = _sigmoid(hq)
    return hq * sq, k, g, sig, f, lb, valid, sq


PAIR_OFF = -1e30


HALF = CHUNK // 2


def _pair_mask(n_t):
    s_i = lax.broadcasted_iota(jnp.int32, (HALF, n_t, 1), 0)
    t_i = lax.broadcasted_iota(jnp.int32, (HALF, n_t, 1), 1)
    return t_i >= s_i


def _pair_groups(b):
    out = []
    for s_sl, t_sl in ((slice(0, HALF), slice(0, CHUNK)), (slice(HALF, CHUNK), slice(HALF, CHUNK))):
        bt, bs = b[t_sl], b[s_sl]
        e = jnp.exp(jnp.where(_pair_mask(bt.shape[0]), bt[None, :, :] - bs[:, None, :], PAIR_OFF))
        out.append((s_sl, t_sl, e))
    return out


def _join_t(first, second):
    return jnp.concatenate([first[:HALF], first[HALF:] + second], axis=0)


def _heads_per_step(n_heads, want):
    return max(h for h in range(1, want + 1) if n_heads % h == 0)


def _hgrn_fwd(proj, logits, gn, n_heads, carried=None):
    m = proj.shape[0]
    nb = m // BLOCK
    w = n_heads * HEAD
    cpb = BLOCK // CHUNK
    hps = _heads_per_step(n_heads, 8)
    wide = hps * HEAD

    def body(hq_ref, hf_ref, hi_ref, hog_ref, lg_ref, gn_ref, o_ref, ohg_ref, st_all_ref, st_ref, q_s, k_s, v_s, b_s):
        i = pl.program_id(1)

        @pl.when(i == 0)
        def _():
            st_ref[...] = jnp.zeros_like(st_ref)

        q, k, g, _, _, _, _, _ = _hgrn_gates(i, hq_ref[...], hf_ref[...], lg_ref[...])
        q_s[...] = q
        k_s[...] = k
        v_s[...] = hi_ref[...]
        b_s[...] = _dot_f32(_tri(BLOCK, "le").astype(F32), g)

        def chunk(c, carry):
            sl = pl.ds(pl.multiple_of(c * CHUNK, CHUNK), CHUNK)
            prev = pl.ds(pl.multiple_of(jnp.maximum(c - 1, 0) * CHUNK, CHUNK), CHUNK)
            first = (c > 0).astype(F32)
            for hd in range(hps):
                cols = slice(hd * HEAD, (hd + 1) * HEAD)
                b = b_s[sl, cols] - b_s[prev, cols][CHUNK - 1:CHUNK, :] * first
                qc, kc, vc = q_s[sl, cols], k_s[sl, cols], v_s[sl, cols]
                st = st_ref[hd]
                st_all_ref[hd, c] = st.astype(BF16)
                o = lax.dot_general((qc * jnp.exp(b)).astype(BF16), st.astype(BF16), NT, preferred_element_type=F32)
                within = []
                for s_sl, t_sl, e in _pair_groups(b):
                    p = jnp.sum(qc[t_sl][None, :, :] * e * kc[s_sl][:, None, :], axis=-1, keepdims=True)
                    within.append(jnp.sum(p * vc[s_sl][:, None, :], axis=0))
                o_ref[sl, cols] = o + _join_t(*within)
                blast = b[CHUNK - 1:CHUNK, :]
                kd = kc * jnp.exp(blast - b)
                st_ref[hd] = st * jnp.exp(blast) + lax.dot_general(vc.astype(BF16), kd.astype(BF16), TN,
                                                                   preferred_element_type=F32)
            return carry

        lax.fori_loop(0, cpb, chunk, 0)
        for hd in range(hps):
            cols = slice(hd * HEAD, (hd + 1) * HEAD)
            o = o_ref[:, cols]
            n = o * lax.rsqrt(jnp.mean(o * o, axis=-1, keepdims=True) + RMS_EPS)
            hog = hog_ref[:, cols]
            ohg_ref[:, cols] = (n * gn_ref[:, cols] * hog * _sigmoid(hog)).astype(BF16)

    def col(group):
        return pl.BlockSpec((BLOCK, wide), lambda h, i: (i, group * (n_heads // hps) + h))

    vec = pl.BlockSpec((1, wide), lambda h, i: (0, h))
    tile = pl.BlockSpec((BLOCK, wide), lambda h, i: (i, h))
    return _pallas(
        "hgrn_fwd", body, (n_heads // hps, nb),
        [col(0), col(1), col(2), col(3), pl.BlockSpec((2, wide), lambda h, i: (0, h)), vec],
        [tile, tile, pl.BlockSpec((hps, cpb, HEAD, HEAD), lambda h, i: (h, i, 0, 0))],
        [_sds((m, w), F32), _sds((m, w), BF16), _sds((n_heads, m // CHUNK, HEAD, HEAD), BF16)],
        [pltpu.VMEM((hps, HEAD, HEAD), F32)] + [pltpu.VMEM((BLOCK, wide), F32)] * 4,
        (proj, proj, proj, proj, logits, gn), carried)


def _hgrn_bwd(proj, logits, gn, o_raw, do_hg, states, n_heads, carried=None):
    m = proj.shape[0]
    nb = m // BLOCK
    w = n_heads * HEAD
    cpb = BLOCK // CHUNK
    last_state = m // CHUNK - 1
    hps = _heads_per_step(n_heads, 4)
    wide = hps * HEAD

    def body(hq_ref, hf_ref, hi_ref, hog_ref, lg_ref, gn_ref, o_ref, do_ref, st_all_ref, st_next_ref,
             dhq_ref, dhf_ref, dhi_ref, dhog_ref, dgn_ref, dlb_ref,
             dst_ref, q_s, k_s, v_s, b_s, do_s, dq_s, dk_s, dv_s, ex_s):
        step = pl.program_id(1)
        i = nb - 1 - step

        @pl.when(step == 0)
        def _():
            dst_ref[...] = jnp.zeros_like(dst_ref)
            dgn_ref[...] = jnp.zeros_like(dgn_ref)
            dlb_ref[...] = jnp.zeros_like(dlb_ref)

        hq = hq_ref[...]
        q, k, g, sig, f, lb, valid, sq = _hgrn_gates(i, hq, hf_ref[...], lg_ref[...])
        q_s[...] = q
        k_s[...] = k
        v_s[...] = hi_ref[...]
        b_s[...] = _dot_f32(_tri(BLOCK, "le").astype(F32), g)

        hog = hog_ref[...]
        sg = _sigmoid(hog)
        sil = hog * sg
        gnv = gn_ref[...]
        dh = do_ref[...]
        dn = dh * gnv * sil
        for hd in range(hps):
            cols = slice(hd * HEAD, (hd + 1) * HEAD)
            o = o_ref[:, cols]
            rs = lax.rsqrt(jnp.mean(o * o, axis=-1, keepdims=True) + RMS_EPS)
            n = o * rs
            ex_s[:, cols] = n
            do_s[:, cols] = rs * (dn[:, cols] - n * jnp.mean(dn[:, cols] * n, axis=-1, keepdims=True))
        n = ex_s[...]
        dhog_ref[...] = (dh * n * gnv * sg * (1.0 + hog * (1.0 - sg))).astype(BF16)
        dgn_ref[...] += jnp.sum(dh * n * sil, axis=0, keepdims=True)

        def chunk(t, st_ends):
            c = cpb - 1 - t
            sl = pl.ds(pl.multiple_of(c * CHUNK, CHUNK), CHUNK)
            prev = pl.ds(pl.multiple_of(jnp.maximum(c - 1, 0) * CHUNK, CHUNK), CHUNK)
            first = (c > 0).astype(F32)
            starts = []
            for hd in range(hps):
                cols = slice(hd * HEAD, (hd + 1) * HEAD)
                b = b_s[sl, cols] - b_s[prev, cols][CHUNK - 1:CHUNK, :] * first
                qc, kc, vc, doc = q_s[sl, cols], k_s[sl, cols], v_s[sl, cols], do_s[sl, cols]
                eb = jnp.exp(b)
                blast = b[CHUNK - 1:CHUNK, :]
                ek = jnp.exp(blast - b)
                dst = dst_ref[hd]
                dstb = dst.astype(BF16)
                docb = doc.astype(BF16)
                st = st_all_ref[hd, c]
                starts.append(st)
                ex_s[sl, cols] = jnp.broadcast_to(jnp.sum(st_ends[hd].astype(F32) * dst, axis=0, keepdims=True),
                                                  (CHUNK, HEAD))
                dq = lax.dot_general(docb, st, NN, preferred_element_type=F32) * eb
                dk = lax.dot_general(vc.astype(BF16), dstb, NN, preferred_element_type=F32) * ek
                dv = lax.dot_general((kc * ek).astype(BF16), dstb, NT, preferred_element_type=F32)
                dq_in, dk_in, dv_in = [], [], []
                for s_sl, t_sl, em in _pair_groups(b):
                    ks, vs = kc[s_sl][:, None, :], vc[s_sl][:, None, :]
                    qt, dot = qc[t_sl][None, :, :], doc[t_sl][None, :, :]
                    dp = jnp.sum(dot * vs, axis=-1, keepdims=True)
                    qe = qt * em
                    p = jnp.sum(qe * ks, axis=-1, keepdims=True)
                    dq_in.append(jnp.sum(dp * em * ks, axis=0))
                    dk_in.append(jnp.sum(dp * qe, axis=1))
                    dv_in.append(jnp.sum(p * dot, axis=1))
                dq_s[sl, cols] = dq + _join_t(*dq_in)
                dk_s[sl, cols] = dk + jnp.concatenate(dk_in, axis=0)
                dv_s[sl, cols] = dv + jnp.concatenate(dv_in, axis=0)
                dst_ref[hd] = dst * jnp.exp(blast) + lax.dot_general(docb, (qc * eb).astype(BF16), TN,
                                                                     preferred_element_type=F32)
            return tuple(starts)

        lax.fori_loop(0, cpb, chunk, tuple(st_next_ref[hd, 0] for hd in range(hps)))
        dq, dk = dq_s[...], dk_s[...]
        r_i = lax.broadcasted_iota(jnp.int32, (BLOCK, BLOCK), 0)
        c_i = lax.broadcasted_iota(jnp.int32, (BLOCK, BLOCK), 1)
        within = ((c_i >= r_i) & (c_i // CHUNK == r_i // CHUNK)).astype(F32)
        rc = _dot_f32(within, q * dq - k * dk) + ex_s[...]
        df =jnp.where(valid, rc / f - dk, 0.0)
        dhf_ref[...] = (df * (1.0 - lb) * sig * (1.0 - sig)).astype(BF16)
        dlb_ref[...] += jnp.sum(df * (1.0 - sig), axis=0, keepdims=True)
        dhq_ref[...] = (dq * sq * (1.0 + hq * (1.0 - sq))).astype(BF16)
        dhi_ref[...] = dv_s[...].astype(BF16)

    def col(group):
        return pl.BlockSpec((BLOCK, wide), lambda h, s: (nb - 1 - s, group * (n_heads // hps) + h))

    vec = pl.BlockSpec((1, wide), lambda h, s: (0, h))
    tile = pl.BlockSpec((BLOCK, wide), lambda h, s: (nb - 1 - s, h))
    nxt = pl.BlockSpec((hps, 1, HEAD, HEAD), lambda h, s: (h, jnp.minimum((nb - s) * cpb, last_state), 0, 0))
    return _pallas(
        "hgrn_bwd", body, (n_heads // hps, nb),
        [col(0), col(1), col(2), col(3), pl.BlockSpec((2, wide), lambda h, s: (0, h)), vec, tile, tile,
         pl.BlockSpec((hps, cpb, HEAD, HEAD), lambda h, s: (h, nb - 1 - s, 0, 0)), nxt],
        [tile, tile, tile, tile, vec, vec],
        [_sds((m, w), BF16)] * 4 + [_sds((1, w), F32)] * 2,
        [pltpu.VMEM((hps, HEAD, HEAD), F32)] + [pltpu.VMEM((BLOCK, wide), F32)] * 9,
        (proj, proj, proj, proj, logits, gn, o_raw, do_hg, states, states), carried)


def _split_dot(x, t):
    hi = x.astype(BF16)
    lo = (x - hi.astype(F32)).astype(BF16)
    return jnp.dot(hi, t, preferred_element_type=F32) + jnp.dot(lo, t, preferred_element_type=F32)


def _window_scan(x, tri, after):
    blocks = [x[:, u * BLOCK:(u + 1) * BLOCK] for u in range(SB_UNROLL)]
    inner = _split_dot(jnp.concatenate(blocks, axis=0), tri)
    sums = [jnp.sum(b, axis=-1, keepdims=True) for b in blocks]
    out = []
    for u in range(SB_UNROLL):
        piece = inner[u * BLOCK:(u + 1) * BLOCK, :]
        for other in (sums[u + 1:] if after else sums[:u]):
            piece = piece + other
        out.append(piece)
    total = sums[0]
    for other in sums[1:]:
        total = total + other
    return jnp.concatenate(out, axis=1), total


def _sb_window(ref, j_left, cols):
    parts = [ref[pl.ds(pl.multiple_of(jnp.maximum(j_left + u, 0) * BLOCK, BLOCK), BLOCK), cols]
             for u in range(SB_UNROLL)]
    return jnp.concatenate(parts, axis=0)


def _sb_scores(q, kw, i, j_left, scale):
    z = lax.dot_general(q, kw, NT, preferred_element_type=F32) * scale
    lp = jnp.log(1.0 + jnp.exp(-jnp.abs(z)))
    lbeta = jnp.minimum(z, 0.0) - lp
    qpos = i * BLOCK + lax.broadcasted_iota(jnp.int32, z.shape, 0)
    kpos = j_left * BLOCK + lax.broadcasted_iota(jnp.int32, z.shape, 1)
    mask = (kpos < qpos) & (kpos >= PAD)
    l1m = jnp.where(mask, lbeta - z, 0.0)
    return lbeta, l1m, mask


SB_DEAD = -104.0
SB_UNROLL = 3


def _sb_fwd(proj, n_heads, group0, carried=None):
    m = proj.shape[0]
    nb = m // BLOCK
    w = n_heads * HEAD
    scale = 1.0 / math.sqrt(HEAD)
    hps = _heads_per_step(n_heads, 2)
    wide = hps * HEAD
    heads = [slice(hd * HEAD, (hd + 1) * HEAD) for hd in range(hps)]

    def body(q_ref, k_ref, v_ref, o_ref, start_ref, count_ref):
        h, i = pl.program_id(0), pl.program_id(1)
        tsuf = _tri(BLOCK, "gt").astype(BF16)

        def live(carry):
            t, _, runs = carry
            top = jnp.max(runs[0])
            for run in runs[1:]:
                top = jnp.maximum(top, jnp.max(run))
            return (t <= i) & (top > SB_DEAD)

        def step(carry):
            t, accs, runs = carry
            j_left = i - t - (SB_UNROLL - 1)
            new_accs, new_runs = [], []
            for hd, cols in enumerate(heads):
                start_ref[hd] = jnp.broadcast_to(runs[hd], (BLOCK, HEAD))
                lbeta, l1m, mask = _sb_scores(q_ref[:, cols], _sb_window(k_ref, j_left, cols), i, j_left, scale)
                later, total = _window_scan(l1m, tsuf, True)
                wgt = jnp.where(mask, jnp.exp(lbeta + later + runs[hd]), 0.0)
                new_accs.append(accs[hd] + jnp.dot(wgt.astype(BF16), _sb_window(v_ref, j_left, cols),
                                                   preferred_element_type=F32))
                new_runs.append(runs[hd] + total)
            return t + SB_UNROLL, tuple(new_accs), tuple(new_runs)

        t, accs, _ = lax.while_loop(live, step, (jnp.int32(0), tuple(jnp.zeros((BLOCK, HEAD), F32) for _ in heads),
                                                 tuple(jnp.zeros((BLOCK, 1), F32) for _ in heads)))
        for hd, cols in enumerate(heads):
            o_ref[:, cols] = accs[hd].astype(BF16)
        count_ref[h, i] = t.astype(F32)

    def whole(group):
        return pl.BlockSpec((m, wide), lambda h, i: (0, group * (n_heads // hps) + h), pipeline_mode=pl.Buffered(1))

    return _pallas(
        "sb_fwd", body, (n_heads // hps, nb),
        [pl.BlockSpec((BLOCK, wide), lambda h, i: (i, group0 * (n_heads // hps) + h)), whole(group0 + 1), whole(group0 + 2)],
        [pl.BlockSpec((BLOCK, wide), lambda h, i: (i, h)), pl.BlockSpec((hps, BLOCK, HEAD), lambda h, i: (h, i, 0)),
         pl.BlockSpec(memory_space=pltpu.SMEM)],
        [_sds((m, w), BF16), _sds((n_heads, m, HEAD), F32), _sds((n_heads // hps, nb), F32)],
        [], (proj, proj, proj), carried)


def _sb_bwd(proj, do, start, count, n_heads, group0):
    m = proj.shape[0]
    nb = m // BLOCK
    w = n_heads * HEAD
    scale = 1.0 / math.sqrt(HEAD)
    hps = _heads_per_step(n_heads, 2)
    wide = hps * HEAD
    heads = [slice(hd * HEAD, (hd + 1) * HEAD) for hd in range(hps)]

    def body(q_ref, k_ref, v_ref, do_ref, start_ref, count_ref, dq_ref, dk_ref, dv_ref, dk_s, dv_s):
        h, i = pl.program_id(0), pl.program_id(1)

        @pl.when(i == 0)
        def _():
            dk_s[...] = jnp.zeros_like(dk_s)
            dv_s[...] = jnp.zeros_like(dv_s)

        count = count_ref[h, i].astype(jnp.int32)
        first = i + 1 - count
        tsuf = _tri(BLOCK, "gt").astype(BF16)
        tpre = _tri(BLOCK, "lt").astype(BF16)

        def step(t, carry):
            dqs, rights, psums = carry
            j_left = first + t * SB_UNROLL
            out = []
            for hd, cols in enumerate(heads):
                q = q_ref[:, cols]
                dob = do_ref[:, cols].astype(BF16)
                kw = _sb_window(k_ref, j_left, cols)
                vw = _sb_window(v_ref, j_left, cols)
                lbeta, l1m, mask = _sb_scores(q, kw, i, j_left, scale)
                later, total = _window_scan(l1m, tsuf, True)
                right = jnp.where(t == 0, rights[hd], rights[hd] - total)
                a = jnp.where(mask, jnp.exp(lbeta + later + right), 0.0)
                p = a * lax.dot_general(dob, vw, NT, preferred_element_type=F32)
                earlier, p_total = _window_scan(p, tpre, False)
                below = psums[hd] + earlier
                beta = jnp.exp(lbeta)
                dz = (jnp.where(mask, p * (1.0 - beta) - below * beta, 0.0) * scale).astype(BF16)
                dq = dqs[hd] + jnp.dot(dz, kw, preferred_element_type=F32)
                dkw = lax.dot_general(dz, q, TN, preferred_element_type=F32)
                dvw = lax.dot_general(a.astype(BF16), dob, TN, preferred_element_type=F32)
                for u in range(SB_UNROLL):
                    rows = pl.ds(pl.multiple_of(jnp.maximum(j_left + u, 0) * BLOCK, BLOCK), BLOCK)
                    dk_s[rows, cols] += dkw[u * BLOCK:(u + 1) * BLOCK, :]
                    dv_s[rows, cols] += dvw[u * BLOCK:(u + 1) * BLOCK, :]
                out.append((dq, right, psums[hd] + p_total))
            return tuple(o[0] for o in out), tuple(o[1] for o in out), tuple(o[2] for o in out)

        dqs, _, _ = lax.fori_loop(0, count // SB_UNROLL, step,
                                  (tuple(jnp.zeros((BLOCK, HEAD), F32) for _ in heads),
                                   tuple(start_ref[hd, :, 0:1] for hd in range(hps)),
                                   tuple(jnp.zeros((BLOCK, 1), F32) for _ in heads)))
        for hd, cols in enumerate(heads):
            dq_ref[:, cols] = dqs[hd].astype(BF16)

        @pl.when(i == nb - 1)
        def _():
            dk_ref[...] = dk_s[...].astype(BF16)
            dv_ref[...] = dv_s[...].astype(BF16)

    def whole(group):
        return pl.BlockSpec((m, wide), lambda h, i: (0, group * (n_heads // hps) + h), pipeline_mode=pl.Buffered(1))

    tile = pl.BlockSpec((BLOCK, wide), lambda h, i: (i, h))
    col = pl.BlockSpec((m, wide), lambda h, i: (0, h))
    return pl.pallas_call(
        body, name="sb_bwd", grid=(n_heads // hps, nb),
        in_specs=[pl.BlockSpec((BLOCK, wide), lambda h, i: (i, group0 * (n_heads // hps) + h)), whole(group0 + 1),
                  whole(group0 + 2), tile, pl.BlockSpec((hps, BLOCK, HEAD), lambda h, i: (h, i, 0)),
                  pl.BlockSpec(memory_space=pltpu.SMEM)],
        out_specs=[tile, col, col],
        out_shape=[_sds((m, w), BF16)] * 3,
        scratch_shapes=[pltpu.VMEM((m, wide), F32)] * 2,
        compiler_params=_params(2),
    )(proj, proj, proj, do, start, count)


def _grad_w_rows(name, x, dy, nd_out):
    m, kx = x.shape
    n = dy.shape[1]
    ks = kx // nd_out
    tm = _tile(m, GRAD_ROW_TILE)

    def epi(acc, er, orf):
        orf[0][...] = acc[0].astype(BF16)

    return _gemm(name, (nd_out, m // tm),
                 [(x, pl.BlockSpec((tm, ks), lambda j, k: (k, j)), dy, pl.BlockSpec((tm, n), lambda j, k: (k, 0)))],
                 [0], [(ks, n)], TN, [], [(_sds((nd_out, ks, n), BF16), pl.BlockSpec((None, ks, n), lambda j, k: (j, 0, 0)))], epi)[0]


def _local_step(x, target, meta, vec, wts, shards=None):
    d = x.shape[1]
    width = vec["hg_norm_g"].shape[1]
    n_heads = width // HEAD
    gate_col = 7 * width
    h0 = jnp.concatenate([jnp.zeros((PAD, d), F32), meta, x], axis=0)
    h0b = h0.astype(BF16)
    wts = dict(wts)
    exchange = None if shards is None else _exchange_carried

    if shards is None:
        a1, b1, s1 = _ffn_up("ffn1_up", h0b, wts["ffn1_w_gate"], wts["ffn1_w_up"])
        r1, h1, h1b = _residual_ln("ffn1_down", s1, True, wts["ffn1_w_down"], h0, vec["ln1_g"], vec["ln1_b"], 0.5)
        proj, projb = _in_proj(h1b, wts["w_in"])
        o_raw, o_hg, states = _hgrn_fwd(proj, vec["hg_lb_logits"], vec["hg_norm_g"], n_heads)
        o_sb, sb_start, sb_count = _sb_fwd(projb, n_heads, 4)
    else:
        half = shards["w_in"].shape[0] // 2
        (a1, b1, s1), (wts["ffn1_w_down"], w_in_top) = _ffn_up(
            "ffn1_up", h0b, wts["ffn1_w_gate"], wts["ffn1_w_up"],
            _gather_carried([shards["ffn1_w_down"], shards["w_in"][:half]]))
        (r1, h1, h1b), (w_in_bottom,) = _residual_ln("ffn1_down", s1, True, wts["ffn1_w_down"], h0, vec["ln1_g"],
                                                     vec["ln1_b"], 0.5, _gather_carried([shards["w_in"][half:]]))
        wts["w_in"] = jnp.concatenate([w_in_top, w_in_bottom], axis=1)
        with_proj = ("w_proj_hg", "w_proj_sb", "w_out", "ffn2_w_gate")
        (proj, projb), got = _in_proj(h1b, wts["w_in"], _gather_carried([shards[k] for k in with_proj]))
        wts.update(zip(with_proj, got))
        (o_raw, o_hg, states), (wts["ffn2_w_up"],) = _hgrn_fwd(proj, vec["hg_lb_logits"], vec["hg_norm_g"], n_heads,
                                                               _gather_carried([shards["ffn2_w_up"]]))
        (o_sb, sb_start, sb_count), (wts["ffn2_w_down"],) = _sb_fwd(projb, n_heads, 4,
                                                                    _gather_carried([shards["ffn2_w_down"]]))
    nd = wts["w_in"].shape[0]
    w_out = wts["w_out"]
    p_hg2 = wts["w_proj_hg"].transpose(1, 0, 2).reshape(width, d)
    p_sb2 = wts["w_proj_sb"].transpose(1, 0, 2).reshape(width, d)
    u_hg, u_sb, y = _proj_merge(o_hg, o_sb, p_hg2, p_sb2, proj, vec["b_gate"], gate_col)
    r2, h2, h2b = _residual_ln("out_proj", y, False, w_out.reshape(d, d), h1, vec["ln2_g"], vec["ln2_b"], 1.0)
    a2, b2, s2 = _ffn_up("ffn2_up", h2b,wts["ffn2_w_gate"], wts["ffn2_w_up"])
    r3, _, _ = _residual_ln("ffn2_down", s2, True, wts["ffn2_w_down"], h2, vec["ln3_g"], vec["ln3_b"], 0.5)

    dr3, dr3b, dg3, db3, loss = _ln_bwd("ln3_bwd", r3, vec["ln3_g"], 0.5, beta=vec["ln3_b"], target=target, first_row=BLOCK)
    dh2, dwg2, dwu2, dwd2 = _ffn_bwd("ffn2", dr3b, dr3, h2b, a2, b2, s2, wts["ffn2_w_gate"], wts["ffn2_w_up"],
                                     wts["ffn2_w_down"], exchange)
    dr2, dr2b, dg2, db2 = _ln_bwd("ln2_bwd", r2, vec["ln2_g"], 1.0, dy=dh2)
    du_hg, du_sb, dz_hg, dz_sb, dbg = _merge_bwd(dr2b, w_out.reshape(d, d), proj, vec["b_gate"], u_hg, u_sb, gate_col)
    dw_out = _grad_w_rows("dw_out", y, dr2b, nd)
    dp_hg = _grad_w("dp_hg", o_hg, du_hg, nd)
    dp_sb = _grad_w("dp_sb", o_sb, du_sb, nd)
    do_hg = _grad_in_whole("do_hg", du_hg, p_hg2)
    do_sb = _grad_in_whole("do_sb", du_sb, p_sb2)
    hg = _hgrn_bwd(proj, vec["hg_lb_logits"], vec["hg_norm_g"], o_raw, do_hg, states, n_heads,
                   exchange([dw_out, dp_hg, dp_sb]) if exchange else None)
    if exchange:
        hg, (dw_out, dp_hg, dp_sb) = hg
    dhq, dhf, dhi, dhog, dgn, dlb = hg
    dsq, dsk, dsv = _sb_bwd(projb, do_sb, sb_start, sb_count, n_heads, 4)
    dproj = jnp.concatenate([dhq, dhf, dhi, dhog, dsq, dsk, dsv, dz_hg, dz_sb], axis=1)
    dw_in = _grad_w("dw_in", h1b, dproj, nd)
    dh1 = _grad_in("dh1", dproj, wts["w_in"], add=dr2, carried=exchange([dw_in]) if exchange else None)
    if exchange:
        dh1, (dw_in,) = dh1
    dr1, dr1b, dg1, db1 = _ln_bwd("ln1_bwd", r1, vec["ln1_g"], 0.5, dy=dh1)
    dh0, dwg1, dwu1, dwd1 = _ffn_bwd("ffn1", dr1b, dr1, h0b, a1, b1, s1, wts["ffn1_w_gate"], wts["ffn1_w_up"],
                                     wts["ffn1_w_down"], exchange)

    small = {"ln1_g": dg1, "ln1_b": db1, "ln2_g": dg2, "ln2_b": db2, "ln3_g": dg3, "ln3_b": db3,
             "b_gate": dbg, "hg_lb": dlb, "hg_norm_g": dgn}
    big = {"ffn1_w_gate": dwg1, "ffn1_w_up": dwu1, "ffn1_w_down": dwd1, "w_in": dw_in, "w_proj_hg": dp_hg,
           "w_proj_sb": dp_sb, "w_out": dw_out, "ffn2_w_gate": dwg2, "ffn2_w_up": dwu2, "ffn2_w_down": dwd2}
    return loss, dh0[BLOCK:], dh0[PAD:BLOCK], small, big


def _position():
    return lax.axis_index("x"), lax.axis_index("y"), lax.axis_index("c")


def _slot(px, py, pc):
    return 4 * px + 2 * py + pc


def _all_gather(shards):
    n = len(shards)

    def body(*refs):
        ins, outs = refs[:n], refs[n:2 * n]
        send_sems, recv_sems, local_sems = refs[2 * n:]
        x, y, c = _position()
        me, sibling = (x, y, c), (x, y, 1 - c)
        chips = [(1 - x, y), (x, 1 - y), (1 - x, 1 - y)]

        def copy(a, k, block, to, src=None):
            dst = outs[a].at[_slot(*block)]
            return pltpu.make_async_remote_copy(src_ref=dst if src is None else src, dst_ref=dst,
                                                send_sem=send_sems.at[a, k], recv_sem=recv_sems.at[a, k],
                                                device_id=to, device_id_type=MESH)

        mine = [pltpu.make_async_copy(ins[a], outs[a].at[_slot(*me)], local_sems.at[a]) for a in range(n)]
        for cp in mine:
            cp.start()
        first = []
        for a in range(n):
            first.append(copy(a, 0, me, sibling, src=ins[a]))
            first += [copy(a, 1 + j, me, (*chip, c), src=ins[a]) for j, chip in enumerate(chips)]
        for cp in first:
            cp.start()
        passed = []
        for j, chip in enumerate(chips):
            for a in range(n):
                copy(a, 1 + j, (*chip, c), me).wait_recv()
                cp = copy(a, 4 + j, (*chip, c), sibling)
                cp.start()
                passed.append(cp)
        for a in range(n):
            copy(a, 0, sibling, me).wait_recv()
        for j, chip in enumerate(chips):
            for a in range(n):
                copy(a, 4 + j, (*chip, 1 - c), me).wait_recv()
        for cp in first + passed:
            cp.wait_send()
        for cp in mine:
            cp.wait()

    return pl.pallas_call(
        body, name="all_gather", out_shape=[_sds((N_DEV,) + s.shape, s.dtype) for s in shards],
        in_specs=[ANY] * n, out_specs=[ANY] * n,
        scratch_shapes=[pltpu.SemaphoreType.DMA((n, 7)), pltpu.SemaphoreType.DMA((n, 7)), pltpu.SemaphoreType.DMA((n,))],
    )(*shards)


def _exchange_carried(grads):
    return _direct_copies(grads, [_sds(g.shape, g.dtype) for g in grads], lambda ref, slot: ref.at[slot])


def _gather_carried(shards):
    return _direct_copies(shards, [_sds((N_DEV,) + s.shape, s.dtype) for s in shards], lambda ref, slot: ref)


def _direct_copies(arrays, outs, block_for):
    n = len(arrays)

    def plan(ins, results, sems, arriving):
        send_sems, recv_sems, local_sems = sems
        x, y, c = _position()
        mine = _slot(x, y, c)
        peers = [(1 - x if k & 4 else x, 1 - y if k & 2 else y, 1 - c if k & 1 else c) for k in range(1, N_DEV)]
        own = [pltpu.make_async_copy(block_for(ins[a], mine), results[a].at[mine], local_sems.at[a]) for a in range(n)]
        remote = [pltpu.make_async_remote_copy(
            src_ref=block_for(ins[a], mine if arriving else _slot(*peer)),
            dst_ref=results[a].at[_slot(*peer) if arriving else mine],
            send_sem=send_sems.at[a, k], recv_sem=recv_sems.at[a, k], device_id=peer, device_id_type=MESH)
            for a in range(n) for k, peer in enumerate(peers)]
        return own, remote

    def start(ins, results, sems):
        own, sent = plan(ins, results, sems, False)
        for cp in own + sent:
            cp.start()

    def finish(ins, results, sems):
        _, landed = plan(ins, results, sems, True)
        for cp in landed:
            cp.wait_recv()
        own, sent = plan(ins, results, sems, False)
        for cp in sent:
            cp.wait_send()
        for cp in own:
            cp.wait()

    sems = [pltpu.SemaphoreType.DMA((n, 7)), pltpu.SemaphoreType.DMA((n, 7)), pltpu.SemaphoreType.DMA((n,))]
    return _Carried(list(arrays), outs, sems, start, finish)


def _all_reduce_rows(v):
    rows = v.shape[0]

    def body(v_ref, out_ref, buf, send_sems, recv_sems):
        x, y, c = _position()
        me, sibling = (x, y, c), (x, y, 1 - c)
        chips = [(1 - x, y), (x, 1 - y), (1 - x, 1 - y)]

        def copy(k, block, to, src=None):
            dst = buf.at[_slot(*block)]
            return pltpu.make_async_remote_copy(src_ref=dst if src is None else src, dst_ref=dst,
                                                send_sem=send_sems.at[k], recv_sem=recv_sems.at[k],
                                                device_id=to, device_id_type=MESH)

        first = [copy(0, me, sibling, src=v_ref)] + [copy(1 + j, me, (*chip, c), src=v_ref) for j, chip in enumerate(chips)]
        for cp in first:
            cp.start()
        buf[_slot(*me)] = v_ref[...]
        passed = [copy(4 + j, (*chip, c), sibling) for j, chip in enumerate(chips)]
        for j, chip in enumerate(chips):
            copy(1 + j, (*chip, c), me).wait_recv()
            passed[j].start()
        copy(0, sibling, me).wait_recv()
        for j, chip in enumerate(chips):
            copy(4 + j, (*chip, 1 - c), me).wait_recv()
        for cp in first + passed:
            cp.wait_send()
        total = buf[0]
        for s in range(1, N_DEV):
            total = total + buf[s]
        out_ref[...] = total

    vmem = pl.BlockSpec(memory_space=pltpu.VMEM)
    return pl.pallas_call(
        body, name="small_all_reduce", out_shape=_sds(v.shape, F32), in_specs=[vmem], out_specs=vmem,
        scratch_shapes=[pltpu.VMEM((N_DEV, rows, 128), F32), pltpu.SemaphoreType.DMA((7,)), pltpu.SemaphoreType.DMA((7,))],
    )(v)


def _adamw(name, w, m, v, contrib):
    r, c = w.shape
    n = contrib.shape[0]
    tr = _tile(r, 256)

    def body(w_ref, m_ref, v_ref, c_ref, g_out, d_out, m_out, v_out):
        g = c_ref[0].astype(F32)
        for s in range(1, n):
            g = g + c_ref[s].astype(F32)
        m2 = ADAM_B1 * m_ref[...] + (1.0 - ADAM_B1) * g
        v2 = ADAM_B2 * v_ref[...] + (1.0 - ADAM_B2) * (g * g)
        m_hat = m2 / (1.0 - ADAM_B1 ** ADAM_STEP)
        v_hat = v2 / (1.0 - ADAM_B2 ** ADAM_STEP)
        g_out[...] = g
        d_out[...] = -ADAM_LR * (m_hat / (jnp.sqrt(v_hat) + ADAM_EPS) + ADAM_WD * w_ref[...])
        m_out[...] = m2
        v_out[...] = v2

    tile = pl.BlockSpec((tr, c), lambda i: (i, 0))
    return pl.pallas_call(
        body, name=name, grid=(r // tr,), in_specs=[tile, tile, tile, pl.BlockSpec((n, tr, c), lambda i: (0, i, 0))],
        out_specs=[tile] * 4, out_shape=[_sds((r, c), F32)] * 4, compiler_params=_params(1),
    )(w, m, v, contrib)


def _lb_logits_grad(logits, dlb):
    def body(lg_ref, d_ref, out_ref):
        lg = lg_ref[...]
        mx = jnp.maximum(lg[0:1], lg[1:2])
        e0 = jnp.exp(lg[0:1] - mx)
        p0 = e0 / (e0 + jnp.exp(lg[1:2] - mx))
        g0 = d_ref[...] * p0 * (1.0 - p0)
        out_ref[0:1, :] = g0
        out_ref[1:2, :] = -g0

    return pl.pallas_call(body, name="lb_logits_grad", out_shape=_sds(logits.shape, F32))(logits, dlb)


BIG = ("ffn1_w_gate", "ffn1_w_up", "ffn1_w_down", "w_in", "w_proj_hg", "w_proj_sb", "w_out",
       "ffn2_w_gate", "ffn2_w_up", "ffn2_w_down")
VECTORS = ("ln1_g", "ln1_b", "b_gate", "hg_lb_logits", "hg_norm_g", "ln2_g", "ln2_b", "ln3_g", "ln3_b")
WEIGHTS = ("meta", "ln1_g", "ln1_b", "ffn1_w_gate", "ffn1_w_up", "ffn1_w_down", "w_in", "b_gate", "hg_lb_logits",
           "hg_norm_g", "w_proj_hg", "w_proj_sb", "w_out", "ln2_g", "ln2_b", "ffn2_w_gate", "ffn2_w_up",
           "ffn2_w_down", "ln3_g", "ln3_b")


def kernel(x, meta, ln1_g, ln1_b, ffn1_w_gate, ffn1_w_up, ffn1_w_down, w_in, b_gate, hg_lb_logits, hg_norm_g, w_proj_hg, w_proj_sb, w_out, ln2_g, ln2_b, ffn2_w_gate, ffn2_w_up, ffn2_w_down, ln3_g, ln3_b, loss_target, m_meta, m_ln1_g, m_ln1_b, m_ffn1_w_gate, m_ffn1_w_up, m_ffn1_w_down, m_w_in, m_b_gate, m_hg_lb_logits, m_hg_norm_g, m_w_proj_hg, m_w_proj_sb, m_w_out, m_ln2_g, m_ln2_b, m_ffn2_w_gate, m_ffn2_w_up, m_ffn2_w_down, m_ln3_g, m_ln3_b, v_meta, v_ln1_g, v_ln1_b, v_ffn1_w_gate, v_ffn1_w_up, v_ffn1_w_down, v_w_in, v_b_gate, v_hg_lb_logits, v_hg_norm_g, v_w_proj_hg, v_w_proj_sb, v_w_out, v_ln2_g, v_ln2_b, v_ffn2_w_gate, v_ffn2_w_up, v_ffn2_w_down, v_ln3_g, v_ln3_b):
    given = dict(locals())
    d = x.shape[-1]
    ds = meta.shape[1]

    shards = {k: given[k][0].astype(BF16) for k in BIG}
    first = ("ffn1_w_gate", "ffn1_w_up")
    gathered = _all_gather([meta] + [shards.pop(k) for k in first])
    meta_full = gathered[0].transpose(1, 0, 2).reshape(N_META, d)
    vec = {k: given[k] for k in VECTORS}
    loss, grad_x, dmeta, small, received = _local_step(x[0], loss_target[0], meta_full, vec, dict(zip(first, gathered[1:])),
                                                       shards)

    order =("ln1_g", "ln1_b", "ln2_g", "ln2_b", "ln3_g", "ln3_b", "b_gate", "hg_lb", "hg_norm_g")
    parts = [small[k].reshape(-1, 128) for k in order] + [dmeta.reshape(-1, 128), jnp.broadcast_to(loss, (8, 128))]
    total = _all_reduce_rows(jnp.concatenate(parts, axis=0))
    reduced, row = {}, 0
    for k, p in zip(order + ("meta", "loss"), parts):
        reduced[k] = total[row:row + p.shape[0]]
        row += p.shape[0]
    loss_out = reduced["loss"][0, 0]
    me = _slot(*_position())
    dmeta_mine = lax.dynamic_slice(reduced["meta"].reshape(N_META, d), (0, me * ds), (N_META, ds))
    dlogits = _lb_logits_grad(hg_lb_logits, reduced["hg_lb"].reshape(1, -1))

    grads, deltas, new_m, new_v = {}, {}, {}, {}
    for k in WEIGHTS:
        w = given[k]
        lead = w.shape[:-2]
        w2, m2, v2 = (a.reshape(a.shape[-2:]) for a in (w, given["m_" + k], given["v_" + k]))
        if k in BIG:
            contrib = received[k]
        elif k == "meta":
            contrib = dmeta_mine[None]
        elif k == "hg_lb_logits":
            contrib = dlogits[None]
        else:
            contrib = reduced[k].reshape((1,) + w2.shape)
        out = _adamw("adamw_" + k, w2, m2, v2, contrib)
        grads[k], deltas[k], new_m[k], new_v[k] = (o.reshape(lead + o.shape) for o in out)
    return (loss_out, grad_x[None], *[grads[k] for k in WEIGHTS], *[deltas[k] for k in WEIGHTS],
            *[new_m[k] for k in WEIGHTS], *[new_v[k] for k in WEIGHTS])
```

```python
import functools
import math

import jax
import jax.numpy as jnp
from jax import lax
from jax.experimental import pallas as pl
from jax.experimental.pallas import tpu as pltpu

F32 = jnp.float32
BF16 = jnp.bfloat16
MESH = pl.DeviceIdType.MESH

N_DEV = 8
N_META = 16
BLOCK = 128
PAD = BLOCK - N_META
HEAD = 128
CHUNK = 16
LN_EPS = 1e-5
RMS_EPS = 1e-6
DN_ALPHA = 2.0 ** 0.25
ADAM_LR, ADAM_B1, ADAM_B2, ADAM_EPS, ADAM_WD, ADAM_STEP = 0.001, 0.9, 0.999, 1e-08, 0.01, 10

VMEM_LIMIT_V7X = 60 * 1024 * 1024
ROW_TILE = 640
LN_ROW_TILE = 320
COL_TILE = 512
GRAD_ROW_TILE = 1664

NN = (((1,), (0,)), ((), ()))
NT = (((1,), (1,)), ((), ()))
TN = (((0,), (0,)), ((), ()))


def _tile(n, pref, mult=16):
    best = None
    for t in range(mult, min(n, pref) + 1, mult):
        if n % t == 0:
            best = t
    return n if best is None else best


def _params(n_axes):
    return pltpu.CompilerParams(dimension_semantics=("arbitrary",) * n_axes, vmem_limit_bytes=VMEM_LIMIT_V7X)


def _sigmoid(x):
    return 1.0 / (1.0 + jnp.exp(-x))


class _Carried:
    def __init__(self, ins, outs, sems, start, finish):
        self.ins, self.outs, self.sems, self.start, self.finish = ins, outs, sems, start, finish


ANY = pl.BlockSpec(memory_space=pl.ANY)


def _pallas(name, body, grid, in_specs, out_specs, out_shape, scratch, operands, carried=None):
    if carried is None:
        return pl.pallas_call(body, name=name, grid=grid, in_specs=in_specs, out_specs=out_specs, out_shape=out_shape,
                              scratch_shapes=scratch, compiler_params=_params(len(grid)))(*operands)
    n_in, n_out, n_scr = len(in_specs), len(out_specs), len(scratch)
    c_in, c_out = len(carried.ins), len(carried.outs)

    def wrapped(*refs):
        ins, rest = refs[:n_in], refs[n_in:]
        c_ins, rest = rest[:c_in], rest[c_in:]
        outs, rest = rest[:n_out], rest[n_out:]
        c_outs, rest = rest[:c_out], rest[c_out:]
        scr, c_sems = rest[:n_scr], rest[n_scr:]
        first = last = None
        for axis, size in enumerate(grid):
            at0, at_end = pl.program_id(axis) == 0, pl.program_id(axis) == size - 1
            first = at0 if first is None else first & at0
            last = at_end if last is None else last & at_end

        @pl.when(first)
        def _():
            carried.start(c_ins, c_outs, c_sems)

        body(*ins, *outs, *scr)

        @pl.when(last)
        def _():
            carried.finish(c_ins, c_outs, c_sems)

    res = pl.pallas_call(
        wrapped, name=name, grid=grid, in_specs=list(in_specs) + [ANY] * c_in, out_specs=list(out_specs) + [ANY] * c_out,
        out_shape=list(out_shape) + list(carried.outs), scratch_shapes=list(scratch) + list(carried.sems),
        compiler_params=_params(len(grid)),
    )(*operands, *carried.ins)
    return res[:n_out], res[n_out:]


def _gemm(name, grid, pairs, acc_of, acc_shapes, dims, extras, outs, epilogue, carried=None):
    n_extra, n_out = len(extras), len(outs)
    nk = grid[-1]
    k_axis = len(grid) - 1
    operands, in_specs, where = [], [], {}
    for a, a_spec, b, b_spec in pairs:
        for arr, spec in ((a, a_spec), (b, b_spec)):
            if (id(arr), id(spec)) not in where:
                where[(id(arr), id(spec))] = len(operands)
                operands.append(arr)
                in_specs.append(spec)
    n_mat = len(operands)
    slots = [(where[(id(a), id(a_spec))], where[(id(b), id(b_spec))]) for a, a_spec, b, b_spec in pairs]

    def body(*refs):
        er = refs[n_mat:n_mat + n_extra]
        orf = refs[n_mat + n_extra:n_mat + n_extra + n_out]
        accs = refs[n_mat + n_extra + n_out:]

        def part(p):
            a_ref, b_ref = refs[slots[p][0]], refs[slots[p][1]]
            if len(b_ref.shape) == 2:
                return lax.dot_general(a_ref[...].astype(BF16), b_ref[...].astype(BF16), dims, preferred_element_type=F32)
            total = None
            width = b_ref.shape[2]
            for s in range(b_ref.shape[0]):
                a = a_ref[s] if len(a_ref.shape) == 3 else a_ref[:, s * width:(s + 1) * width]
                d = lax.dot_general(a.astype(BF16), b_ref[s].astype(BF16), dims, preferred_element_type=F32)
                total = d if total is None else total + d
            return total

        if nk == 1:
            vals = [None] * len(acc_shapes)
            for p in range(len(pairs)):
                d = part(p)
                vals[acc_of[p]] = d if vals[acc_of[p]] is None else vals[acc_of[p]] + d
            epilogue(vals, er, orf)
        else:
            k = pl.program_id(k_axis)

            @pl.when(k == 0)
            def _():
                for acc in accs:
                    acc[...] = jnp.zeros_like(acc)

            for p in range(len(pairs)):
                accs[acc_of[p]][...] += part(p)

            @pl.when(k == nk - 1)
            def _():
                epilogue([acc[...] for acc in accs], er, orf)

    for e, e_spec in extras:
        operands.append(e)
        in_specs.append(e_spec)
    scratch = [] if nk == 1 else [pltpu.VMEM(s, F32) for s in acc_shapes]
    return _pallas(name, body, grid, in_specs, [s for _, s in outs], [o for o, _ in outs], scratch, operands, carried)


def _sds(shape, dtype):
    return jax.ShapeDtypeStruct(shape, dtype)


def _ln_rows(r, g, b):
    mu = jnp.mean(r, axis=-1, keepdims=True)
    xc = r - mu
    var = jnp.mean(xc * xc, axis=-1, keepdims=True)
    return xc * lax.rsqrt(var + LN_EPS) * g + b


def _ffn_up(name, hb, wg, wu, carried=None):
    m, d = hb.shape
    nd, _, fs = wg.shape
    tm = _tile(m, ROW_TILE)

    def epi(acc, er, orf):
        a, b = acc
        sg = _sigmoid(a)
        silu = a * sg
        orf[0][...] = b * sg * (1.0 + a * (1.0 - sg))
        orf[1][...] = silu
        orf[2][...] = (silu * b).astype(BF16)

    h_spec = pl.BlockSpec((tm, d), lambda i, j, k: (i, 0))
    w_spec = pl.BlockSpec((None, d, fs), lambda i, j, k: (j, 0, 0))
    o_spec = pl.BlockSpec((None, tm, fs), lambda i, j, k: (j, i, 0))
    return _gemm(name, (m // tm, nd, 1), [(hb, h_spec, wg, w_spec), (hb, h_spec, wu, w_spec)], [0, 1],
                 [(tm, fs)] * 2, NN, [],
                 [(_sds((nd, m, fs), F32), o_spec), (_sds((nd, m, fs), F32), o_spec), (_sds((nd, m, fs), BF16), o_spec)], epi,
                 carried)


def _residual_ln(name, a, a_stacked, w, h_in, g, beta, scale, carried=None):
    d = w.shape[-1]
    m = h_in.shape[0]
    tm = _tile(m, LN_ROW_TILE)

    def epi(acc, er, orf):
        r = DN_ALPHA * er[0][...] + scale * acc[0]
        h = _ln_rows(r, er[1][...], er[2][...])
        orf[0][...] = r
        orf[1][...] = h
        orf[2][...] = h.astype(BF16)

    once = pl.Buffered(1)
    if a_stacked:
        a_spec = pl.BlockSpec((a.shape[0], tm, a.shape[2]), lambda i, k: (0, i, 0))
        w_spec = pl.BlockSpec(w.shape, lambda i, k: (0, 0, 0), pipeline_mode=once)
    else:
        a_spec = pl.BlockSpec((tm, a.shape[1]), lambda i, k: (i, 0))
        w_spec = pl.BlockSpec(w.shape, lambda i, k: (0, 0), pipeline_mode=once)
    row = pl.BlockSpec((tm, d), lambda i, k: (i, 0))
    vec = pl.BlockSpec((1, d), lambda i, k: (0, 0))
    return _gemm(name, (m // tm, 1), [(a, a_spec, w, w_spec)], [0], [(tm, d)], NN,
                 [(h_in, row), (g, vec), (beta, vec)],
                 [(_sds((m, d), F32), row), (_sds((m, d), F32), row), (_sds((m, d), BF16), row)], epi, carried)


def _in_proj(hb, w_in, carried=None):
    m, d = hb.shape
    nd, _, cs = w_in.shape
    tm = _tile(m, ROW_TILE)

    def epi(acc, er, orf):
        orf[0][...] = acc[0]
        orf[1][...] = acc[0].astype(BF16)

    out = pl.BlockSpec((tm, cs), lambda i, j, k: (i, j))
    res = _gemm("in_proj", (m // tm, nd, 1),
                [(hb, pl.BlockSpec((tm, d), lambda i, j, k: (i, 0)), w_in, pl.BlockSpec((None, d, cs), lambda i, j, k: (j, 0, 0)))],
                [0], [(tm, cs)], NN, [], [(_sds((m, nd * cs), F32), out), (_sds((m, nd * cs), BF16), out)], epi, carried)
    return tuple(res) if carried is None else (tuple(res[0]), res[1])


def _proj_merge(o_hg, o_sb, p_hg, p_sb, proj, b_gate, gate_col):
    m, w = o_hg.shape
    d = p_hg.shape[1]
    tm = _tile(m, ROW_TILE)
    tn = _tile(d, COL_TILE, 128)
    nn = d // tn
    c0 = gate_col // tn

    def epi(acc, er, orf):
        u_hg, u_sb = acc
        g_hg = _sigmoid(er[0][...] + er[2][...])
        g_sb = _sigmoid(er[1][...] + er[3][...])
        orf[0][...] = u_hg
        orf[1][...] = u_sb
        orf[2][...] = (g_hg * u_hg + g_sb * u_sb).astype(BF16)

    o_spec = pl.BlockSpec((tm, w), lambda i, j, k: (i, 0))
    p_spec = pl.BlockSpec((w, tn), lambda i, j, k: (0, j))
    out = pl.BlockSpec((tm, tn), lambda i, j, k: (i, j))
    return _gemm("proj_merge", (m // tm, nn, 1), [(o_hg, o_spec, p_hg, p_spec), (o_sb, o_spec, p_sb, p_spec)], [0, 1],
                 [(tm, tn)] * 2, NN,
                 [(proj, pl.BlockSpec((tm, tn), lambda i, j, k: (i, c0 + j))),
                  (proj, pl.BlockSpec((tm, tn), lambda i, j, k: (i, c0 + nn + j))),
                  (b_gate, pl.BlockSpec((1, tn), lambda i, j, k: (0, j))),
                  (b_gate, pl.BlockSpec((1, tn), lambda i, j, k: (0, nn + j)))],
                 [(_sds((m, d), F32), out), (_sds((m, d), F32), out), (_sds((m, d), BF16), out)], epi)


def _ln_bwd(name, r, g, out_scale, dy=None, beta=None, target=None, first_row=0):
    m, d = r.shape
    tm = _tile(m, LN_ROW_TILE if target is None else BLOCK)
    with_loss = target is not None
    skip = first_row // tm if with_loss else 0
    assert not with_loss or first_row % tm == 0

    def body(*refs):
        if with_loss:
            r_ref, g_ref, b_ref, t_ref, dr_ref, drb_ref, dg_ref, db_ref, loss_ref = refs
        else:
            r_ref, g_ref, dy_ref, dr_ref, drb_ref, dg_ref, db_ref = refs
        i = pl.program_id(0)
        x = r_ref[...]
        mu = jnp.mean(x, axis=-1, keepdims=True)
        xc = x - mu
        var = jnp.mean(xc * xc, axis=-1, keepdims=True)
        rstd = lax.rsqrt(var + LN_EPS)
        xhat = xc * rstd
        gv = g_ref[...]
        if with_loss:
            err = xhat * gv + b_ref[...] - t_ref[...]
            live = (i >= skip).astype(F32)
            dyv = err * (live / d)
            part = 0.5 * live * jnp.sum(jnp.sum(err * err, axis=-1, keepdims=True), axis=0, keepdims=True) / d
        else:
            dyv = dy_ref[...]
        dxh = dyv * gv
        m1 = jnp.mean(dxh, axis=-1, keepdims=True)
        m2 = jnp.mean(dxh * xhat, axis=-1, keepdims=True)
        dr = rstd * (dxh - m1 - xhat * m2)
        dr_ref[...] = dr
        drb_ref[...] = (out_scale * dr).astype(BF16)

        @pl.when(i == 0)
        def _():
            dg_ref[...] = jnp.zeros_like(dg_ref)
            db_ref[...] = jnp.zeros_like(db_ref)
            if with_loss:
                loss_ref[...] = jnp.zeros_like(loss_ref)

        dg_ref[...] += jnp.sum(dyv * xhat, axis=0, keepdims=True)
        db_ref[...] += jnp.sum(dyv, axis=0, keepdims=True)
        if with_loss:
            loss_ref[...] += jnp.broadcast_to(part, loss_ref.shape)

    row = pl.BlockSpec((tm, d), lambda i: (i, 0))
    vec = pl.BlockSpec((1, d), lambda i: (0, 0))
    out_shape = [_sds((m, d), F32), _sds((m, d), BF16), _sds((1, d), F32), _sds((1, d), F32)]
    out_specs = [row, row, vec, vec]
    if with_loss:
        operands = [r, g, beta, target]
        in_specs = [row, vec, vec, pl.BlockSpec((tm, d), lambda i: (jnp.maximum(i - skip, 0), 0))]
        out_shape.append(_sds((1, BLOCK), F32))
        out_specs.append(pl.BlockSpec((1, BLOCK), lambda i: (0, 0)))
    else:
        operands = [r, g, dy]
        in_specs = [row, vec, row]
    return pl.pallas_call(body, name=name, grid=(m // tm,), in_specs=in_specs, out_specs=out_specs, out_shape=out_shape,
                          compiler_params=_params(1))(*operands)


def _ffn_bwd(tag, drb, dr, hb, a, b, s, wg, wu, wd, exchange=None):
    m, d = drb.shape
    nd, _, fs = wg.shape
    tm = _tile(m, ROW_TILE)

    pair = 2 if nd % 2 == 0 else 1

    def ds_body(drb_ref, wd_ref, a_ref, b_ref, da_ref, db_ref):
        x = drb_ref[...]
        for blk in range(pair):
            ds = lax.dot_general(x, wd_ref[blk], NT, preferred_element_type=F32)
            da_ref[blk] = (ds * a_ref[blk]).astype(BF16)
            db_ref[blk] = (ds * b_ref[blk]).astype(BF16)

    st = pl.BlockSpec((pair, tm, fs), lambda i, j: (j, i, 0))
    da, db = _pallas(tag + "_ds", ds_body, (m // tm, nd // pair),
                     [pl.BlockSpec((tm, d), lambda i, j: (i, 0)), pl.BlockSpec((pair, fs, d), lambda i, j: (j, 0, 0)), st, st],
                     [st, st], [_sds((nd, m, fs), BF16)] * 2, [], (drb, wd, a, b))

    def epi_w(acc, er, orf):
        orf[0][...] = acc[0].astype(BF16)

    tr = _tile(m, GRAD_ROW_TILE)
    nkm = m // tr
    dwd = _gemm(tag + "_dwd", (nd, nkm),
                [(s, pl.BlockSpec((None, tr, fs), lambda j, k: (j, k, 0)), drb, pl.BlockSpec((tr, d), lambda j, k: (k, 0)))],
                [0], [(fs, d)], TN, [], [(_sds((nd, fs, d), BF16), pl.BlockSpec((None, fs, d), lambda j, k: (j, 0, 0)))], epi_w)[0]
    h_spec = pl.BlockSpec((tr, d), lambda j, k: (k, 0))
    g_spec = pl.BlockSpec((None, tr, fs), lambda j, k: (j, k, 0))
    w_out = pl.BlockSpec((None, d, fs), lambda j, k: (j, 0, 0))
    dwg = _gemm(tag + "_dwg", (nd, nkm), [(hb, h_spec, da, g_spec)], [0], [(d, fs)], TN, [],
                [(_sds((nd, d, fs), BF16), w_out)], epi_w, exchange([dwd]) if exchange else None)
    if exchange:
        (dwg,), (dwd,) = dwg
    else:
        dwg = dwg[0]
    dwu = _gemm(tag + "_dwu", (nd, nkm), [(hb, h_spec, db, g_spec)], [0], [(d, fs)], TN, [],
                [(_sds((nd, d, fs), BF16), w_out)], epi_w, exchange([dwg]) if exchange else None)
    if exchange:
        (dwu,), (dwg,) = dwu
    else:
        dwu = dwu[0]

    def epi_dh(acc, er, orf):
        orf[0][...] = DN_ALPHA * er[0][...] + acc[0]

    gk = pl.BlockSpec((pair, tm, fs), lambda i, k: (k, i, 0))
    wk = pl.BlockSpec((pair, d, fs), lambda i, k: (k, 0, 0))
    row = pl.BlockSpec((tm, d), lambda i, k: (i, 0))
    dh = _gemm(tag + "_dh", (m // tm, nd // pair), [(da, gk, wg, wk), (db, gk, wu, wk)], [0, 0], [(tm, d)], NT,
               [(dr, row)], [(_sds((m, d), F32), row)], epi_dh, exchange([dwu]) if exchange else None)
    if exchange:
        (dh,), (dwu,) = dh
    else:
        dh = dh[0]
    return dh, dwg, dwu, dwd


def _merge_bwd(dmixb, w_out2, proj, b_gate, u_hg, u_sb, gate_col):
    m, d = dmixb.shape
    ds = _tile(d, COL_TILE, 128)
    nd = d // ds
    tm = _tile(m, ROW_TILE)
    c0 = gate_col // ds

    def epi(acc, er, orf):
        i = pl.program_id(1)
        dy = acc[0]
        g_hg = _sigmoid(er[0][...] + er[2][...])
        g_sb = _sigmoid(er[1][...] + er[3][...])
        orf[0][...] = (dy * g_hg).astype(BF16)
        orf[1][...] = (dy * g_sb).astype(BF16)
        dz_hg = dy * er[4][...] * g_hg * (1.0 - g_hg)
        dz_sb = dy * er[5][...] * g_sb * (1.0 - g_sb)
        orf[2][...] = dz_hg.astype(BF16)
        orf[3][...] = dz_sb.astype(BF16)

        @pl.when(i == 0)
        def _():
            orf[4][...] = jnp.zeros_like(orf[4])
            orf[5][...] = jnp.zeros_like(orf[5])

        orf[4][...] += jnp.sum(dz_hg, axis=0, keepdims=True)
        orf[5][...] += jnp.sum(dz_sb, axis=0, keepdims=True)

    tile = pl.BlockSpec((tm, ds), lambda j, i, k: (i, j))
    vec = pl.BlockSpec((1, ds), lambda j, i, k: (0, j))
    du_hg, du_sb, dz_hg, dz_sb, db_hg, db_sb = _gemm(
        "merge_bwd", (nd, m // tm, 1),
        [(dmixb, pl.BlockSpec((tm, d), lambda j, i, k: (i, 0)), w_out2, pl.BlockSpec((ds, d), lambda j, i, k: (j, 0)))],
        [0], [(tm, ds)], NT,
        [(proj, pl.BlockSpec((tm, ds), lambda j, i, k: (i, c0 + j))),
         (proj, pl.BlockSpec((tm, ds), lambda j, i, k: (i, c0 + nd + j))),
         (b_gate, vec), (b_gate, pl.BlockSpec((1, ds), lambda j, i, k: (0, nd + j))),
         (u_hg, tile), (u_sb, tile)],
        [(_sds((m, d), BF16), tile), (_sds((m, d), BF16), tile), (_sds((m, d), BF16), tile), (_sds((m, d), BF16), tile),
         (_sds((1, d), F32), vec), (_sds((1, d), F32), vec)], epi)
    return du_hg, du_sb, dz_hg, dz_sb, jnp.concatenate([db_hg, db_sb], axis=1)


def _grad_w(name, x, dy, nd_out):
    m, kx = x.shape
    n = dy.shape[1]
    ns = n // nd_out
    tm = _tile(m, GRAD_ROW_TILE)

    def epi(acc, er, orf):
        orf[0][...] = acc[0].astype(BF16)

    return _gemm(name, (nd_out, m // tm),
                 [(x, pl.BlockSpec((tm, kx), lambda j, k: (k, 0)), dy, pl.BlockSpec((tm, ns), lambda j, k: (k, j)))],
                 [0], [(kx, ns)], TN, [], [(_sds((nd_out, kx, ns), BF16), pl.BlockSpec((None, kx, ns), lambda j, k: (j, 0, 0)))], epi)[0]


def _grad_in_whole(name, dy, w2):
    m, n = dy.shape
    kx = w2.shape[0]
    tm = _tile(m, ROW_TILE)

    def epi(acc, er, orf):
        orf[0][...] = acc[0]

    return _gemm(name, (m // tm, 1),
                 [(dy, pl.BlockSpec((tm, n), lambda i, k: (i, 0)),
                   w2, pl.BlockSpec((kx, n), lambda i, k: (0, 0), pipeline_mode=pl.Buffered(1)))],
                 [0], [(tm, kx)], NT, [], [(_sds((m, kx), F32), pl.BlockSpec((tm, kx), lambda i, k: (i, 0)))], epi)[0]


def _grad_in(name, dy, w, add=None, carried=None):
    m = dy.shape[0]
    nd, kx, ns = w.shape
    tm = _tile(m, ROW_TILE)

    def epi(acc, er, orf):
        orf[0][...] = acc[0] if add is None else DN_ALPHA * er[0][...] + acc[0]

    pair = 2 if nd % 2 == 0 else 1
    row = pl.BlockSpec((tm, kx), lambda i, k: (i, 0))
    res = _gemm(name, (m // tm, nd // pair),
                [(dy, pl.BlockSpec((tm, pair * ns), lambda i, k: (i, k)),
                  w, pl.BlockSpec((pair, kx, ns), lambda i, k: (k, 0, 0)))],
                [0], [(tm, kx)], NT, [] if add is None else [(add, row)], [(_sds((m, kx), F32), row)], epi, carried)
    return res[0] if carried is None else (res[0][0], res[1])


def _tri(n, kind):
    r = lax.broadcasted_iota(jnp.int32, (n, n), 0)
    c = lax.broadcasted_iota(jnp.int32, (n, n), 1)
    return {"le": c <= r, "ge": c >= r, "gt": r > c, "lt": r < c}[kind]


def _dot_f32(a, b, dims=NN):
    return lax.dot_general(a, b, dims, preferred_element_type=F32, precision=lax.Precision.HIGHEST)


def _hgrn_gates(i, hq, hf, logits):
    lg = logits
    mx = jnp.maximum(lg[0:1], lg[1:2])
    e0 = jnp.exp(lg[0:1] - mx)
    lb = e0 / (e0 + jnp.exp(lg[1:2] - mx))
    sig = _sigmoid(hf)
    f = lb + (1.0 - lb) * sig
    valid = (i * BLOCK + lax.broadcasted_iota(jnp.int32, hf.shape, 0)) >= PAD
    g = jnp.where(valid, jnp.log(f), 0.0)
    k = jnp.where(valid, 1.0 - f, 0.0)
    sq = _sigmoid(hq)
    return hq * sq, k, g, sig, f, lb, valid, sq


PAIR_OFF = -1e30


HALF = CHUNK // 2


def _pair_mask(n_t):
    s_i = lax.broadcasted_iota(jnp.int32, (HALF, n_t, 1), 0)
    t_i = lax.broadcasted_iota(jnp.int32, (HALF, n_t, 1), 1)
    return t_i >= s_i


def _pair_groups(b):
    out = []
    for s_sl, t_sl in ((slice(0, HALF), slice(0, CHUNK)), (slice(HALF, CHUNK), slice(HALF, CHUNK))):
        bt, bs = b[t_sl], b[s_sl]
        e = jnp.exp(jnp.where(_pair_mask(bt.shape[0]), bt[None, :, :] - bs[:, None, :], PAIR_OFF))
        out.append((s_sl, t_sl, e))
    return out


def _join_t(first, second):
    return jnp.concatenate([first[:HALF], first[HALF:] + second], axis=0)


def _heads_per_step(n_heads, want):
    return max(h for h in range(1, want + 1) if n_heads % h == 0)


def _hgrn_fwd(proj, logits, gn, n_heads, carried=None):
    m = proj.shape[0]
    nb = m // BLOCK
    w = n_heads * HEAD
    cpb = BLOCK // CHUNK
    hps = _heads_per_step(n_heads, 8)
    wide = hps * HEAD

    def body(hq_ref, hf_ref, hi_ref, hog_ref, lg_ref, gn_ref, o_ref, ohg_ref, st_all_ref, st_ref, q_s, k_s, v_s, b_s):
        i = pl.program_id(1)

        @pl.when(i == 0)
        def _():
            st_ref[...] = jnp.zeros_like(st_ref)

        q, k, g, _, _, _, _, _ = _hgrn_gates(i, hq_ref[...], hf_ref[...], lg_ref[...])
        q_s[...] = q
        k_s[...] = k
        v_s[...] = hi_ref[...]
        b_s[...] = _dot_f32(_tri(BLOCK, "le").astype(F32), g)

        def chunk(c, carry):
            sl = pl.ds(pl.multiple_of(c * CHUNK, CHUNK), CHUNK)
            prev = pl.ds(pl.multiple_of(jnp.maximum(c - 1, 0) * CHUNK, CHUNK), CHUNK)
            first = (c > 0).astype(F32)
            for hd in range(hps):
                cols = slice(hd * HEAD, (hd + 1) * HEAD)
                b = b_s[sl, cols] - b_s[prev, cols][CHUNK - 1:CHUNK, :] * first
                qc, kc, vc = q_s[sl, cols], k_s[sl, cols], v_s[sl, cols]
                st = st_ref[hd]
                st_all_ref[hd, c] = st.astype(BF16)
                o = lax.dot_general((qc * jnp.exp(b)).astype(BF16), st.astype(BF16), NT, preferred_element_type=F32)
                within = []
                for s_sl, t_sl, e in _pair_groups(b):
                    p = jnp.sum(qc[t_sl][None, :, :] * e * kc[s_sl][:, None, :], axis=-1, keepdims=True)
                    within.append(jnp.sum(p * vc[s_sl][:, None, :], axis=0))
                o_ref[sl, cols] = o + _join_t(*within)
                blast = b[CHUNK - 1:CHUNK, :]
                kd = kc * jnp.exp(blast - b)
                st_ref[hd] = st * jnp.exp(blast) + lax.dot_general(vc.astype(BF16), kd.astype(BF16), TN,
                                                                   preferred_element_type=F32)
            return carry

        lax.fori_loop(0, cpb, chunk, 0)
        for hd in range(hps):
            cols = slice(hd * HEAD, (hd + 1) * HEAD)
            o = o_ref[:, cols]
            n = o * lax.rsqrt(jnp.mean(o * o, axis=-1, keepdims=True) + RMS_EPS)
            hog = hog_ref[:, cols]
            ohg_ref[:, cols] = (n * gn_ref[:, cols] * hog * _sigmoid(hog)).astype(BF16)

    def col(group):
        return pl.BlockSpec((BLOCK, wide), lambda h, i: (i, group * (n_heads // hps) + h))

    vec = pl.BlockSpec((1, wide), lambda h, i: (0, h))
    tile = pl.BlockSpec((BLOCK, wide), lambda h, i: (i, h))
    return _pallas(
        "hgrn_fwd", body, (n_heads // hps, nb),
        [col(0), col(1), col(2), col(3), pl.BlockSpec((2, wide), lambda h, i: (0, h)), vec],
        [tile, tile, pl.BlockSpec((hps, cpb, HEAD, HEAD), lambda h, i: (h, i, 0, 0))],
        [_sds((m, w), F32), _sds((m, w), BF16), _sds((n_heads, m // CHUNK, HEAD, HEAD), BF16)],
        [pltpu.VMEM((hps, HEAD, HEAD), F32)] + [pltpu.VMEM((BLOCK, wide), F32)] * 4,
        (proj, proj, proj, proj, logits, gn), carried)


def _hgrn_bwd(proj, logits, gn, o_raw, do_hg, states, n_heads, carried=None):
    m = proj.shape[0]
    nb = m // BLOCK
    w = n_heads * HEAD
    cpb = BLOCK // CHUNK
    last_state = m // CHUNK - 1
    hps = _heads_per_step(n_heads, 8)
    wide = hps * HEAD

    def body(hq_ref, hf_ref, hi_ref, hog_ref, lg_ref, gn_ref, o_ref, do_ref, st_all_ref, st_next_ref,
             dhq_ref, dhf_ref, dhi_ref, dhog_ref, dgn_ref, dlb_ref,
             dst_ref, q_s, k_s, v_s, b_s, do_s, dq_s, dk_s, dv_s, ex_s):
        step = pl.program_id(1)
        i = nb - 1 - step

        @pl.when(step == 0)
        def _():
            dst_ref[...] = jnp.zeros_like(dst_ref)
            dgn_ref[...] = jnp.zeros_like(dgn_ref)
            dlb_ref[...] = jnp.zeros_like(dlb_ref)

        hq = hq_ref[...]
        q, k, g, sig, f, lb, valid, sq = _hgrn_gates(i, hq, hf_ref[...], lg_ref[...])
        q_s[...] = q
        k_s[...] = k
        v_s[...] = hi_ref[...]
        b_s[...] = _dot_f32(_tri(BLOCK, "le").astype(F32), g)

        hog = hog_ref[...]
        sg = _sigmoid(hog)
        sil = hog * sg
        gnv = gn_ref[...]
        dh = do_ref[...]
        dn = dh * gnv * sil
        for hd in range(hps):
            cols = slice(hd * HEAD, (hd + 1) * HEAD)
            o = o_ref[:, cols]
            rs = lax.rsqrt(jnp.mean(o * o, axis=-1, keepdims=True) + RMS_EPS)
            n = o * rs
            ex_s[:, cols] = n
            do_s[:, cols] = rs * (dn[:, cols] - n * jnp.mean(dn[:, cols] * n, axis=-1, keepdims=True))
        n = ex_s[...]
        dhog_ref[...] = (dh * n * gnv * sg * (1.0 + hog * (1.0 - sg))).astype(BF16)
        dgn_ref[...] += jnp.sum(dh * n * sil, axis=0, keepdims=True)

        def chunk(t, st_ends):
            c = cpb - 1 - t
            sl = pl.ds(pl.multiple_of(c * CHUNK, CHUNK), CHUNK)
            prev = pl.ds(pl.multiple_of(jnp.maximum(c - 1, 0) * CHUNK, CHUNK), CHUNK)
            first = (c > 0).astype(F32)
            starts = []
            for hd in range(hps):
                cols = slice(hd * HEAD, (hd + 1) * HEAD)
                b = b_s[sl, cols] - b_s[prev, cols][CHUNK - 1:CHUNK, :] * first
                qc, kc, vc, doc = q_s[sl, cols], k_s[sl, cols], v_s[sl, cols], do_s[sl, cols]
                eb = jnp.exp(b)
                blast = b[CHUNK - 1:CHUNK, :]
                ek = jnp.exp(blast - b)
                dst = dst_ref[hd]
                dstb = dst.astype(BF16)
                docb = doc.astype(BF16)
                st = st_all_ref[hd, c]
                starts.append(st)
                ex_s[sl, cols] = jnp.broadcast_to(jnp.sum(st_ends[hd].astype(F32) * dst, axis=0, keepdims=True),
                                                  (CHUNK, HEAD))
                dq = lax.dot_general(docb, st, NN, preferred_element_type=F32) * eb
                dk = lax.dot_general(vc.astype(BF16), dstb, NN, preferred_element_type=F32) * ek
                dv = lax.dot_general((kc * ek).astype(BF16), dstb, NT, preferred_element_type=F32)
                dq_in, dk_in, dv_in = [], [], []
                for s_sl, t_sl, em in _pair_groups(b):
                    ks, vs = kc[s_sl][:, None, :], vc[s_sl][:, None, :]
                    qt, dot = qc[t_sl][None, :, :], doc[t_sl][None, :, :]
                    dp = jnp.sum(dot * vs, axis=-1, keepdims=True)
                    qe = qt * em
                    p = jnp.sum(qe * ks, axis=-1, keepdims=True)
                    dq_in.append(jnp.sum(dp * em * ks, axis=0))
                    dk_in.append(jnp.sum(dp * qe, axis=1))
                    dv_in.append(jnp.sum(p * dot, axis=1))
                dq_s[sl, cols] = dq + _join_t(*dq_in)
                dk_s[sl, cols] = dk + jnp.concatenate(dk_in, axis=0)
                dv_s[sl, cols] = dv + jnp.concatenate(dv_in, axis=0)
                dst_ref[hd] = dst * jnp.exp(blast) + lax.dot_general(docb, (qc * eb).astype(BF16), TN,
                                                                     preferred_element_type=F32)
            return tuple(starts)

        lax.fori_loop(0, cpb, chunk, tuple(st_next_ref[hd, 0] for hd in range(hps)))
        dq, dk = dq_s[...], dk_s[...]
        r_i = lax.broadcasted_iota(jnp.int32, (BLOCK, BLOCK), 0)
        c_i = lax.broadcasted_iota(jnp.int32, (BLOCK, BLOCK), 1)
        within = ((c_i >= r_i) & (c_i // CHUNK == r_i // CHUNK)).astype(F32)
        rc = _dot_f32(within, q * dq - k * dk) + ex_s[...]
        df =jnp.where(valid, rc / f - dk, 0.0)
        dhf_ref[...] = (df * (1.0 - lb) * sig * (1.0 - sig)).astype(BF16)
        dlb_ref[...] += jnp.sum(df * (1.0 - sig), axis=0, keepdims=True)
        dhq_ref[...] = (dq * sq * (1.0 + hq * (1.0 - sq))).astype(BF16)
        dhi_ref[...] = dv_s[...].astype(BF16)

    def col(group):
        return pl.BlockSpec((BLOCK, wide), lambda h, s: (nb - 1 - s, group * (n_heads // hps) + h))

    vec = pl.BlockSpec((1, wide), lambda h, s: (0, h))
    tile = pl.BlockSpec((BLOCK, wide), lambda h, s: (nb - 1 - s, h))
    nxt = pl.BlockSpec((hps, 1, HEAD, HEAD), lambda h, s: (h, jnp.minimum((nb - s) * cpb, last_state), 0, 0))
    return _pallas(
        "hgrn_bwd", body, (n_heads // hps, nb),
        [col(0), col(1), col(2), col(3), pl.BlockSpec((2, wide), lambda h, s: (0, h)), vec, tile, tile,
         pl.BlockSpec((hps, cpb, HEAD, HEAD), lambda h, s: (h, nb - 1 - s, 0, 0)), nxt],
        [tile, tile, tile, tile, vec, vec],
        [_sds((m, w), BF16)] * 4 + [_sds((1, w), F32)] * 2,
        [pltpu.VMEM((hps, HEAD, HEAD), F32)] + [pltpu.VMEM((BLOCK, wide), F32)] * 9,
        (proj, proj, proj, proj, logits, gn, o_raw, do_hg, states, states), carried)


def _split_dot(x, t):
    hi = x.astype(BF16)
    lo = (x - hi.astype(F32)).astype(BF16)
    return jnp.dot(hi, t, preferred_element_type=F32) + jnp.dot(lo, t, preferred_element_type=F32)


def _window_scan(x, tri, after):
    blocks = [x[:, u * BLOCK:(u + 1) * BLOCK] for u in range(SB_UNROLL)]
    inner = _split_dot(jnp.concatenate(blocks, axis=0), tri)
    sums = [jnp.sum(b, axis=-1, keepdims=True) for b in blocks]
    out = []
    for u in range(SB_UNROLL):
        piece = inner[u * BLOCK:(u + 1) * BLOCK, :]
        for other in (sums[u + 1:] if after else sums[:u]):
            piece = piece + other
        out.append(piece)
    total = sums[0]
    for other in sums[1:]:
        total = total + other
    return jnp.concatenate(out, axis=1), total


def _sb_window(ref, j_left, cols):
    parts = [ref[pl.ds(pl.multiple_of(jnp.maximum(j_left + u, 0) * BLOCK, BLOCK), BLOCK), cols]
             for u in range(SB_UNROLL)]
    return jnp.concatenate(parts, axis=0)


def _sb_scores(q, kw, i, j_left, scale):
    z = lax.dot_general(q, kw, NT, preferred_element_type=F32) * scale
    lp = jnp.log(1.0 + jnp.exp(-jnp.abs(z)))
    lbeta = jnp.minimum(z, 0.0) - lp
    qpos = i * BLOCK + lax.broadcasted_iota(jnp.int32, z.shape, 0)
    kpos = j_left * BLOCK + lax.broadcasted_iota(jnp.int32, z.shape, 1)
    mask = (kpos < qpos) & (kpos >= PAD)
    l1m = jnp.where(mask, lbeta - z, 0.0)
    return lbeta, l1m, mask


SB_DEAD = -104.0
SB_UNROLL = 3


def _sb_fwd(proj, n_heads, group0, carried=None):
    m = proj.shape[0]
    nb = m // BLOCK
    w = n_heads * HEAD
    scale = 1.0 / math.sqrt(HEAD)
    hps = _heads_per_step(n_heads, 2)
    wide = hps * HEAD
    heads = [slice(hd * HEAD, (hd + 1) * HEAD) for hd in range(hps)]

    def body(q_ref, k_ref, v_ref, o_ref, start_ref, count_ref):
        h, i = pl.program_id(0), pl.program_id(1)
        tsuf = _tri(BLOCK, "gt").astype(BF16)

        def live(carry):
            t, _, runs = carry
            top = jnp.max(runs[0])
            for run in runs[1:]:
                top = jnp.maximum(top, jnp.max(run))
            return (t <= i) & (top > SB_DEAD)

        def step(carry):
            t, accs, runs = carry
            j_left = i - t - (SB_UNROLL - 1)
            new_accs, new_runs = [], []
            for hd, cols in enumerate(heads):
                start_ref[hd] = jnp.broadcast_to(runs[hd], (BLOCK, HEAD))
                lbeta, l1m, mask = _sb_scores(q_ref[:, cols], _sb_window(k_ref, j_left, cols), i, j_left, scale)
                later, total = _window_scan(l1m, tsuf, True)
                wgt = jnp.where(mask, jnp.exp(lbeta + later + runs[hd]), 0.0)
                new_accs.append(accs[hd] + jnp.dot(wgt.astype(BF16), _sb_window(v_ref, j_left, cols),
                                                   preferred_element_type=F32))
                new_runs.append(runs[hd] + total)
            return t + SB_UNROLL, tuple(new_accs), tuple(new_runs)

        t, accs, _ = lax.while_loop(live, step, (jnp.int32(0), tuple(jnp.zeros((BLOCK, HEAD), F32) for _ in heads),
                                                 tuple(jnp.zeros((BLOCK, 1), F32) for _ in heads)))
        for hd, cols in enumerate(heads):
            o_ref[:, cols] = accs[hd].astype(BF16)
        count_ref[h, i] = t.astype(F32)

    def whole(group):
        return pl.BlockSpec((m, wide), lambda h, i: (0, group * (n_heads // hps) + h), pipeline_mode=pl.Buffered(1))

    return _pallas(
        "sb_fwd", body, (n_heads // hps, nb),
        [pl.BlockSpec((BLOCK, wide), lambda h, i: (i, group0 * (n_heads // hps) + h)), whole(group0 + 1), whole(group0 + 2)],
        [pl.BlockSpec((BLOCK, wide), lambda h, i: (i, h)), pl.BlockSpec((hps, BLOCK, HEAD), lambda h, i: (h, i, 0)),
         pl.BlockSpec(memory_space=pltpu.SMEM)],
        [_sds((m, w), BF16), _sds((n_heads, m, HEAD), F32), _sds((n_heads // hps, nb), F32)],
        [], (proj, proj, proj), carried)


def _sb_bwd(proj, do, start, count, n_heads, group0):
    m = proj.shape[0]
    nb = m // BLOCK
    w = n_heads * HEAD
    scale = 1.0 / math.sqrt(HEAD)
    hps = _heads_per_step(n_heads, 2)
    wide = hps * HEAD
    heads = [slice(hd * HEAD, (hd + 1) * HEAD) for hd in range(hps)]

    def body(q_ref, k_ref, v_ref, do_ref, start_ref, count_ref, dq_ref, dk_ref, dv_ref, dk_s, dv_s):
        h, i = pl.program_id(0), pl.program_id(1)

        @pl.when(i == 0)
        def _():
            dk_s[...] = jnp.zeros_like(dk_s)
            dv_s[...] = jnp.zeros_like(dv_s)

        count = count_ref[h, i].astype(jnp.int32)
        first = i + 1 - count
        tsuf = _tri(BLOCK, "gt").astype(BF16)
        tpre = _tri(BLOCK, "lt").astype(BF16)

        def step(t, carry):
            dqs, rights, psums = carry
            j_left = first + t * SB_UNROLL
            out = []
            for hd, cols in enumerate(heads):
                q = q_ref[:, cols]
                dob = do_ref[:, cols].astype(BF16)
                kw = _sb_window(k_ref, j_left, cols)
                vw = _sb_window(v_ref, j_left, cols)
                lbeta, l1m, mask = _sb_scores(q, kw, i, j_left, scale)
                later, total = _window_scan(l1m, tsuf, True)
                right = jnp.where(t == 0, rights[hd], rights[hd] - total)
                a = jnp.where(mask, jnp.exp(lbeta + later + right), 0.0)
                p = a * lax.dot_general(dob, vw, NT, preferred_element_type=F32)
                earlier, p_total = _window_scan(p, tpre, False)
                below = psums[hd] + earlier
                beta = jnp.exp(lbeta)
                dz = (jnp.where(mask, p * (1.0 - beta) - below * beta, 0.0) * scale).astype(BF16)
                dq = dqs[hd] + jnp.dot(dz, kw, preferred_element_type=F32)
                dkw = lax.dot_general(dz, q, TN, preferred_element_type=F32)
                dvw = lax.dot_general(a.astype(BF16), dob, TN, preferred_element_type=F32)
                for u in range(SB_UNROLL):
                    rows = pl.ds(pl.multiple_of(jnp.maximum(j_left + u, 0) * BLOCK, BLOCK), BLOCK)
                    dk_s[rows, cols] += dkw[u * BLOCK:(u + 1) * BLOCK, :]
                    dv_s[rows, cols] += dvw[u * BLOCK:(u + 1) * BLOCK, :]
                out.append((dq, right, psums[hd] + p_total))
            return tuple(o[0] for o in out), tuple(o[1] for o in out), tuple(o[2] for o in out)

        dqs, _, _ = lax.fori_loop(0, count // SB_UNROLL, step,
                                  (tuple(jnp.zeros((BLOCK, HEAD), F32) for _ in heads),
                                   tuple(start_ref[hd, :, 0:1] for hd in range(hps)),
                                   tuple(jnp.zeros((BLOCK, 1), F32) for _ in heads)))
        for hd, cols in enumerate(heads):
            dq_ref[:, cols] = dqs[hd].astype(BF16)

        @pl.when(i == nb - 1)
        def _():
            dk_ref[...] = dk_s[...].astype(BF16)
            dv_ref[...] = dv_s[...].astype(BF16)

    def whole(group):
        return pl.BlockSpec((m, wide), lambda h, i: (0, group * (n_heads // hps) + h), pipeline_mode=pl.Buffered(1))

    tile = pl.BlockSpec((BLOCK, wide), lambda h, i: (i, h))
    col = pl.BlockSpec((m, wide), lambda h, i: (0, h))
    return pl.pallas_call(
        body, name="sb_bwd", grid=(n_heads // hps, nb),
        in_specs=[pl.BlockSpec((BLOCK, wide), lambda h, i: (i, group0 * (n_heads // hps) + h)), whole(group0 + 1),
                  whole(group0 + 2), tile, pl.BlockSpec((hps, BLOCK, HEAD), lambda h, i: (h, i, 0)),
                  pl.BlockSpec(memory_space=pltpu.SMEM)],
        out_specs=[tile, col, col],
        out_shape=[_sds((m, w), BF16)] * 3,
        scratch_shapes=[pltpu.VMEM((m, wide), F32)] * 2,
        compiler_params=_params(2),
    )(proj, proj, proj, do, start, count)


def _grad_w_rows(name, x, dy, nd_out):
    m, kx = x.shape
    n = dy.shape[1]
    ks = kx // nd_out
    tm = _tile(m, GRAD_ROW_TILE)

    def epi(acc, er, orf):
        orf[0][...] = acc[0].astype(BF16)

    return _gemm(name, (nd_out, m // tm),
                 [(x, pl.BlockSpec((tm, ks), lambda j, k: (k, j)), dy, pl.BlockSpec((tm, n), lambda j, k: (k, 0)))],
                 [0], [(ks, n)], TN, [], [(_sds((nd_out, ks, n), BF16), pl.BlockSpec((None, ks, n), lambda j, k: (j, 0, 0)))], epi)[0]


def _local_step(x, target, meta, vec, wts, shards=None):
    d = x.shape[1]
    width = vec["hg_norm_g"].shape[1]
    n_heads = width // HEAD
    gate_col = 7 * width
    h0 = jnp.concatenate([jnp.zeros((PAD, d), F32), meta, x], axis=0)
    h0b = h0.astype(BF16)
    wts = dict(wts)
    exchange = None if shards is None else _exchange_carried

    if shards is None:
        a1, b1, s1 = _ffn_up("ffn1_up", h0b, wts["ffn1_w_gate"], wts["ffn1_w_up"])
        r1, h1, h1b = _residual_ln("ffn1_down", s1, True, wts["ffn1_w_down"], h0, vec["ln1_g"], vec["ln1_b"], 0.5)
        proj, projb = _in_proj(h1b, wts["w_in"])
        o_raw, o_hg, states = _hgrn_fwd(proj, vec["hg_lb_logits"], vec["hg_norm_g"], n_heads)
        o_sb, sb_start, sb_count = _sb_fwd(projb, n_heads, 4)
    else:
        half = shards["w_in"].shape[0] // 2
        (a1, b1, s1), (wts["ffn1_w_down"], w_in_top) = _ffn_up(
            "ffn1_up", h0b, wts["ffn1_w_gate"], wts["ffn1_w_up"],
            _gather_carried([shards["ffn1_w_down"], shards["w_in"][:half]]))
        (r1, h1, h1b), (w_in_bottom,) = _residual_ln("ffn1_down", s1, True, wts["ffn1_w_down"], h0, vec["ln1_g"],
                                                     vec["ln1_b"], 0.5, _gather_carried([shards["w_in"][half:]]))
        wts["w_in"] = jnp.concatenate([w_in_top, w_in_bottom], axis=1)
        with_proj = ("w_proj_hg", "w_proj_sb", "w_out", "ffn2_w_gate")
        (proj, projb), got = _in_proj(h1b, wts["w_in"], _gather_carried([shards[k] for k in with_proj]))
        wts.update(zip(with_proj, got))
        (o_raw, o_hg, states), (wts["ffn2_w_up"],) = _hgrn_fwd(proj, vec["hg_lb_logits"], vec["hg_norm_g"], n_heads,
                                                               _gather_carried([shards["ffn2_w_up"]]))
        (o_sb, sb_start, sb_count), (wts["ffn2_w_down"],) = _sb_fwd(projb, n_heads, 4,
                                                                    _gather_carried([shards["ffn2_w_down"]]))
    nd = wts["w_in"].shape[0]
    w_out = wts["w_out"]
    p_hg2 = wts["w_proj_hg"].transpose(1, 0, 2).reshape(width, d)
    p_sb2 = wts["w_proj_sb"].transpose(1, 0, 2).reshape(width, d)
    u_hg, u_sb, y = _proj_merge(o_hg, o_sb, p_hg2, p_sb2, proj, vec["b_gate"], gate_col)
    r2, h2, h2b = _residual_ln("out_proj", y, False, w_out.reshape(d, d), h1, vec["ln2_g"], vec["ln2_b"], 1.0)
    a2, b2, s2 = _ffn_up("ffn2_up", h2b,wts["ffn2_w_gate"], wts["ffn2_w_up"])
    r3, _, _ = _residual_ln("ffn2_down", s2, True, wts["ffn2_w_down"], h2, vec["ln3_g"], vec["ln3_b"], 0.5)

    dr3, dr3b, dg3, db3, loss = _ln_bwd("ln3_bwd", r3, vec["ln3_g"], 0.5, beta=vec["ln3_b"], target=target, first_row=BLOCK)
    dh2, dwg2, dwu2, dwd2 = _ffn_bwd("ffn2", dr3b, dr3, h2b, a2, b2, s2, wts["ffn2_w_gate"], wts["ffn2_w_up"],
                                     wts["ffn2_w_down"], exchange)
    dr2, dr2b, dg2, db2 = _ln_bwd("ln2_bwd", r2, vec["ln2_g"], 1.0, dy=dh2)
    du_hg, du_sb, dz_hg, dz_sb, dbg = _merge_bwd(dr2b, w_out.reshape(d, d), proj, vec["b_gate"], u_hg, u_sb, gate_col)
    dw_out = _grad_w_rows("dw_out", y, dr2b, nd)
    dp_hg = _grad_w("dp_hg", o_hg, du_hg, nd)
    dp_sb = _grad_w("dp_sb", o_sb, du_sb, nd)
    do_hg = _grad_in_whole("do_hg", du_hg, p_hg2)
    do_sb = _grad_in_whole("do_sb", du_sb, p_sb2)
    hg = _hgrn_bwd(proj, vec["hg_lb_logits"], vec["hg_norm_g"], o_raw, do_hg, states, n_heads,
                   exchange([dw_out, dp_hg, dp_sb]) if exchange else None)
    if exchange:
        hg, (dw_out, dp_hg, dp_sb) = hg
    dhq, dhf, dhi, dhog, dgn, dlb = hg
    dsq, dsk, dsv = _sb_bwd(projb, do_sb, sb_start, sb_count, n_heads, 4)
    dproj = jnp.concatenate([dhq, dhf, dhi, dhog, dsq, dsk, dsv, dz_hg, dz_sb], axis=1)
    dw_in = _grad_w("dw_in", h1b, dproj, nd)
    dh1 = _grad_in("dh1", dproj, wts["w_in"], add=dr2, carried=exchange([dw_in]) if exchange else None)
    if exchange:
        dh1, (dw_in,) = dh1
    dr1, dr1b, dg1, db1 = _ln_bwd("ln1_bwd", r1, vec["ln1_g"], 0.5, dy=dh1)
    dh0, dwg1, dwu1, dwd1 = _ffn_bwd("ffn1", dr1b, dr1, h0b, a1, b1, s1, wts["ffn1_w_gate"], wts["ffn1_w_up"],
                                     wts["ffn1_w_down"], exchange)

    small = {"ln1_g": dg1, "ln1_b": db1, "ln2_g": dg2, "ln2_b": db2, "ln3_g": dg3, "ln3_b": db3,
             "b_gate": dbg, "hg_lb": dlb, "hg_norm_g": dgn}
    big = {"ffn1_w_gate": dwg1, "ffn1_w_up": dwu1, "ffn1_w_down": dwd1, "w_in": dw_in, "w_proj_hg": dp_hg,
           "w_proj_sb": dp_sb, "w_out": dw_out, "ffn2_w_gate": dwg2, "ffn2_w_up": dwu2, "ffn2_w_down": dwd2}
    return loss, dh0[BLOCK:], dh0[PAD:BLOCK], small, big


def _position():
    return lax.axis_index("x"), lax.axis_index("y"), lax.axis_index("c")


def _slot(px, py, pc):
    return 4 * px + 2 * py + pc


def _all_gather(shards):
    n = len(shards)

    def body(*refs):
        ins, outs = refs[:n], refs[n:2 * n]
        send_sems, recv_sems, local_sems = refs[2 * n:]
        x, y, c = _position()
        me, sibling = (x, y, c), (x, y, 1 - c)
        chips = [(1 - x, y), (x, 1 - y), (1 - x, 1 - y)]

        def copy(a, k, block, to, src=None):
            dst = outs[a].at[_slot(*block)]
            return pltpu.make_async_remote_copy(src_ref=dst if src is None else src, dst_ref=dst,
                                                send_sem=send_sems.at[a, k], recv_sem=recv_sems.at[a, k],
                                                device_id=to, device_id_type=MESH)

        mine = [pltpu.make_async_copy(ins[a], outs[a].at[_slot(*me)], local_sems.at[a]) for a in range(n)]
        for cp in mine:
            cp.start()
        first = []
        for a in range(n):
            first.append(copy(a, 0, me, sibling, src=ins[a]))
            first += [copy(a, 1 + j, me, (*chip, c), src=ins[a]) for j, chip in enumerate(chips)]
        for cp in first:
            cp.start()
        passed = []
        for j, chip in enumerate(chips):
            for a in range(n):
                copy(a, 1 + j, (*chip, c), me).wait_recv()
                cp = copy(a, 4 + j, (*chip, c), sibling)
                cp.start()
                passed.append(cp)
        for a in range(n):
            copy(a, 0, sibling, me).wait_recv()
        for j, chip in enumerate(chips):
            for a in range(n):
                copy(a, 4 + j, (*chip, 1 - c), me).wait_recv()
        for cp in first + passed:
            cp.wait_send()
        for cp in mine:
            cp.wait()

    return pl.pallas_call(
        body, name="all_gather", out_shape=[_sds((N_DEV,) + s.shape, s.dtype) for s in shards],
        in_specs=[ANY] * n, out_specs=[ANY] * n,
        scratch_shapes=[pltpu.SemaphoreType.DMA((n, 7)), pltpu.SemaphoreType.DMA((n, 7)), pltpu.SemaphoreType.DMA((n,))],
    )(*shards)


def _exchange_carried(grads):
    return _direct_copies(grads, [_sds(g.shape, g.dtype) for g in grads], lambda ref, slot: ref.at[slot])


def _gather_carried(shards):
    return _direct_copies(shards, [_sds((N_DEV,) + s.shape, s.dtype) for s in shards], lambda ref, slot: ref)


def _direct_copies(arrays, outs, block_for):
    n = len(arrays)

    def plan(ins, results, sems, arriving):
        send_sems, recv_sems, local_sems = sems
        x, y, c = _position()
        mine = _slot(x, y, c)
        peers = [(1 - x if k & 4 else x, 1 - y if k & 2 else y, 1 - c if k & 1 else c) for k in range(1, N_DEV)]
        own = [pltpu.make_async_copy(block_for(ins[a], mine), results[a].at[mine], local_sems.at[a]) for a in range(n)]
        remote = [pltpu.make_async_remote_copy(
            src_ref=block_for(ins[a], mine if arriving else _slot(*peer)),
            dst_ref=results[a].at[_slot(*peer) if arriving else mine],
            send_sem=send_sems.at[a, k], recv_sem=recv_sems.at[a, k], device_id=peer, device_id_type=MESH)
            for a in range(n) for k, peer in enumerate(peers)]
        return own, remote

    def start(ins, results, sems):
        own, sent = plan(ins, results, sems, False)
        for cp in own + sent:
            cp.start()

    def finish(ins, results, sems):
        _, landed = plan(ins, results, sems, True)
        for cp in landed:
            cp.wait_recv()
        own, sent = plan(ins, results, sems, False)
        for cp in sent:
            cp.wait_send()
        for cp in own:
            cp.wait()

    sems = [pltpu.SemaphoreType.DMA((n, 7)), pltpu.SemaphoreType.DMA((n, 7)), pltpu.SemaphoreType.DMA((n,))]
    return _Carried(list(arrays), outs, sems, start, finish)


def _all_reduce_rows(v):
    rows = v.shape[0]

    def body(v_ref, out_ref, buf, send_sems, recv_sems):
        x, y, c = _position()
        me, sibling = (x, y, c), (x, y, 1 - c)
        chips = [(1 - x, y), (x, 1 - y), (1 - x, 1 - y)]

        def copy(k, block, to, src=None):
            dst = buf.at[_slot(*block)]
            return pltpu.make_async_remote_copy(src_ref=dst if src is None else src, dst_ref=dst,
                                                send_sem=send_sems.at[k], recv_sem=recv_sems.at[k],
                                                device_id=to, device_id_type=MESH)

        first = [copy(0, me, sibling, src=v_ref)] + [copy(1 + j, me, (*chip, c), src=v_ref) for j, chip in enumerate(chips)]
        for cp in first:
            cp.start()
        buf[_slot(*me)] = v_ref[...]
        passed = [copy(4 + j, (*chip, c), sibling) for j, chip in enumerate(chips)]
        for j, chip in enumerate(chips):
            copy(1 + j, (*chip, c), me).wait_recv()
            passed[j].start()
        copy(0, sibling, me).wait_recv()
        for j, chip in enumerate(chips):
            copy(4 + j, (*chip, 1 - c), me).wait_recv()
        for cp in first + passed:
            cp.wait_send()
        total = buf[0]
        for s in range(1, N_DEV):
            total = total + buf[s]
        out_ref[...] = total

    vmem = pl.BlockSpec(memory_space=pltpu.VMEM)
    return pl.pallas_call(
        body, name="small_all_reduce", out_shape=_sds(v.shape, F32), in_specs=[vmem], out_specs=vmem,
        scratch_shapes=[pltpu.VMEM((N_DEV, rows, 128), F32), pltpu.SemaphoreType.DMA((7,)), pltpu.SemaphoreType.DMA((7,))],
    )(v)


def _adamw(name, w, m, v, contrib):
    r, c = w.shape
    n = contrib.shape[0]
    tr = _tile(r, 256)

    def body(w_ref, m_ref, v_ref, c_ref, g_out, d_out, m_out, v_out):
        g = c_ref[0].astype(F32)
        for s in range(1, n):
            g = g + c_ref[s].astype(F32)
        m2 = ADAM_B1 * m_ref[...] + (1.0 - ADAM_B1) * g
        v2 = ADAM_B2 * v_ref[...] + (1.0 - ADAM_B2) * (g * g)
        m_hat = m2 / (1.0 - ADAM_B1 ** ADAM_STEP)
        v_hat = v2 / (1.0 - ADAM_B2 ** ADAM_STEP)
        g_out[...] = g
        d_out[...] = -ADAM_LR * (m_hat / (jnp.sqrt(v_hat) + ADAM_EPS) + ADAM_WD * w_ref[...])
        m_out[...] = m2
        v_out[...] = v2

    tile = pl.BlockSpec((tr, c), lambda i: (i, 0))
    return pl.pallas_call(
        body, name=name, grid=(r // tr,), in_specs=[tile, tile, tile, pl.BlockSpec((n, tr, c), lambda i: (0, i, 0))],
        out_specs=[tile] * 4, out_shape=[_sds((r, c), F32)] * 4, compiler_params=_params(1),
    )(w, m, v, contrib)


def _lb_logits_grad(logits, dlb):
    def body(lg_ref, d_ref, out_ref):
        lg = lg_ref[...]
        mx = jnp.maximum(lg[0:1], lg[1:2])
        e0 = jnp.exp(lg[0:1] - mx)
        p0 = e0 / (e0 + jnp.exp(lg[1:2] - mx))
        g0 = d_ref[...] * p0 * (1.0 - p0)
        out_ref[0:1, :] = g0
        out_ref[1:2, :] = -g0

    return pl.pallas_call(body, name="lb_logits_grad", out_shape=_sds(logits.shape, F32))(logits, dlb)


BIG = ("ffn1_w_gate", "ffn1_w_up", "ffn1_w_down", "w_in", "w_proj_hg", "w_proj_sb", "w_out",
       "ffn2_w_gate", "ffn2_w_up", "ffn2_w_down")
VECTORS = ("ln1_g", "ln1_b", "b_gate", "hg_lb_logits", "hg_norm_g", "ln2_g", "ln2_b", "ln3_g", "ln3_b")
WEIGHTS = ("meta", "ln1_g", "ln1_b", "ffn1_w_gate", "ffn1_w_up", "ffn1_w_down", "w_in", "b_gate", "hg_lb_logits",
           "hg_norm_g", "w_proj_hg", "w_proj_sb", "w_out", "ln2_g", "ln2_b", "ffn2_w_gate", "ffn2_w_up",
           "ffn2_w_down", "ln3_g", "ln3_b")


def kernel(x, meta, ln1_g, ln1_b, ffn1_w_gate, ffn1_w_up, ffn1_w_down, w_in, b_gate, hg_lb_logits, hg_norm_g, w_proj_hg, w_proj_sb, w_out, ln2_g, ln2_b, ffn2_w_gate, ffn2_w_up, ffn2_w_down, ln3_g, ln3_b, loss_target, m_meta, m_ln1_g, m_ln1_b, m_ffn1_w_gate, m_ffn1_w_up, m_ffn1_w_down, m_w_in, m_b_gate, m_hg_lb_logits, m_hg_norm_g, m_w_proj_hg, m_w_proj_sb, m_w_out, m_ln2_g, m_ln2_b, m_ffn2_w_gate, m_ffn2_w_up, m_ffn2_w_down, m_ln3_g, m_ln3_b, v_meta, v_ln1_g, v_ln1_b, v_ffn1_w_gate, v_ffn1_w_up, v_ffn1_w_down, v_w_in, v_b_gate, v_hg_lb_logits, v_hg_norm_g, v_w_proj_hg, v_w_proj_sb, v_w_out, v_ln2_g, v_ln2_b, v_ffn2_w_gate, v_ffn2_w_up, v_ffn2_w_down, v_ln3_g, v_ln3_b):
    given = dict(locals())
    d = x.shape[-1]
    ds = meta.shape[1]

    shards = {k: given[k][0].astype(BF16) for k in BIG}
    first = ("ffn1_w_gate", "ffn1_w_up")
    gathered = _all_gather([meta] + [shards.pop(k) for k in first])
    meta_full = gathered[0].transpose(1, 0, 2).reshape(N_META, d)
    vec = {k: given[k] for k in VECTORS}
    loss, grad_x, dmeta, small, received = _local_step(x[0], loss_target[0], meta_full, vec, dict(zip(first, gathered[1:])),
                                                       shards)

    order =("ln1_g", "ln1_b", "ln2_g", "ln2_b", "ln3_g", "ln3_b", "b_gate", "hg_lb", "hg_norm_g")
    parts = [small[k].reshape(-1, 128) for k in order] + [dmeta.reshape(-1, 128), jnp.broadcast_to(loss, (8, 128))]
    total = _all_reduce_rows(jnp.concatenate(parts, axis=0))
    reduced, row = {}, 0
    for k, p in zip(order + ("meta", "loss"), parts):
        reduced[k] = total[row:row + p.shape[0]]
        row += p.shape[0]
    loss_out = reduced["loss"][0, 0]
    me = _slot(*_position())
    dmeta_mine = lax.dynamic_slice(reduced["meta"].reshape(N_META, d), (0, me * ds), (N_META, ds))
    dlogits = _lb_logits_grad(hg_lb_logits, reduced["hg_lb"].reshape(1, -1))

    grads, deltas, new_m, new_v = {}, {}, {}, {}
    for k in WEIGHTS:
        w = given[k]
        lead = w.shape[:-2]
        w2, m2, v2 = (a.reshape(a.shape[-2:]) for a in (w, given["m_" + k], given["v_" + k]))
        if k in BIG:
            contrib = received[k]
        elif k == "meta":
            contrib = dmeta_mine[None]
        elif k == "hg_lb_logits":
            contrib = dlogits[None]
        else:
            contrib = reduced[k].reshape((1,) + w2.shape)
        out = _adamw("adamw_" + k, w2, m2, v2, contrib)
        grads[k], deltas[k], new_m[k], new_v[k] = (o.reshape(lead + o.shape) for o in out)
    return (loss_out, grad_x[None], *[grads[k] for k in WEIGHTS], *[deltas[k] for k in WEIGHTS],
            *[new_m[k] for k in WEIGHTS], *[new_v[k] for k in WEIGHTS])
```
